```python
import jax
import jax.numpy as jnp
from jax import lax
import numpy as np

D_MODEL = 1024
BATCH = 16
SEQ = 4096
DEPTH = 4

GRID_W = 64
CTX_LEN = 256
EPS = 1e-6
F32 = jnp.float32
A_HEADS = 4
A_DK = 128
A_DV = 128
A_WIDTH = A_HEADS * A_DK
A_VWIDTH = A_HEADS * A_DV
CHUNK = 64
B_HEADS = 8
B_KV_HEADS = 2
B_HD = 64
B_GROUP = B_HEADS // B_KV_HEADS
B_WIDTH = B_HEADS * B_HD
B_KV_WIDTH = B_KV_HEADS * B_HD
Q_BLOCK = 128
ROPE_THETA = 10000.0
AXIS_DIM = B_HD // 2
N_GROUPS = 4
EXPERTS_PER_GROUP = 8
N_EXPERTS = N_GROUPS * EXPERTS_PER_GROUP
TOP_K = 2
D_EXPERT = 512
ROW_BLOCK = 128
IN_SIZES = (A_WIDTH, A_WIDTH, A_WIDTH, A_VWIDTH, A_VWIDTH, B_WIDTH, B_KV_WIDTH, B_KV_WIDTH, D_MODEL, D_MODEL)
D_IN = A_WIDTH * 3 + A_VWIDTH * 2 + B_WIDTH + 2 * B_KV_WIDTH + 2 * D_MODEL
N_MOD = 6

kernel_name = 'hybrid_hgrn2_gqa_hier_moe_dit'


def rmsnorm(x, g):
    xf = x.astype(F32)
    y = xf * lax.rsqrt(jnp.mean(xf * xf, axis=-1, keepdims=True) + EPS)
    return (y * g.astype(F32)).astype(x.dtype)


def split_in(z):
    parts, off = [], 0
    for s in IN_SIZES:
        parts.append(z[..., off:off + s])
        off += s
    return parts


def to_heads(t, n_heads):
    bsz, n, w = t.shape
    return t.reshape(bsz, n, n_heads, w // n_heads).transpose(0, 2, 1, 3)


def from_heads(t):
    bsz, h, n, d = t.shape
    return t.transpose(0, 2, 1, 3).reshape(bsz, n, h * d)


def axial_rope_tables(rows):
    t = jnp.arange(rows * GRID_W)
    r = (t // GRID_W).astype(F32)
    col = (t % GRID_W).astype(F32)
    inv = ROPE_THETA ** (-jnp.arange(0, AXIS_DIM, 2, dtype=F32) / AXIS_DIM)
    ang = jnp.stack([r[:, None] * inv, col[:, None] * inv], axis=1)
    return jnp.cos(ang), jnp.sin(ang)


def apply_rope(x, cos, sin):
    shp = x.shape
    xa = x.astype(F32).reshape(shp[:-1] + (2, 2, AXIS_DIM // 2))
    x1, x2 = xa[..., 0, :], xa[..., 1, :]
    out = jnp.stack([x1 * cos - x2 * sin, x2 * cos + x1 * sin], axis=-2)
    return out.reshape(shp).astype(x.dtype)


def hgrn_gates(zf, lb):
    zf = zf.astype(F32)
    logf = jnp.logaddexp(jnp.log(lb), jnp.log1p(-lb) + jax.nn.log_sigmoid(zf))
    k = (1.0 - lb) * jax.nn.sigmoid(-zf)
    return k, logf


def gla_final_state(k, v, logf):
    b = jnp.cumsum(logf, axis=-2)
    return jnp.einsum('bhsk,bhsv->bhkv', k * jnp.exp(b[..., -1:, :] - b), v)


def gla_chunk_scan(q, k, v, logf, s0):
    bsz, h, n, _ = q.shape
    dv = v.shape[-1]
    nc = n // CHUNK

    def chunks(t):
        return jnp.moveaxis(t.reshape(bsz, h, nc, CHUNK, t.shape[-1]), 2, 0)

    lower = jnp.tril(jnp.ones((CHUNK, CHUNK), bool))[:, :, None]

    def step(state, inp):
        qc, kc, vc, lfc = inp
        b = jnp.cumsum(lfc, axis=-2)
        o_inter = jnp.einsum('bhtk,bhkv->bhtv', qc * jnp.exp(b), state)
        diff = b[..., :, None, :] - b[..., None, :, :]
        decay = jnp.exp(jnp.where(lower, diff, -jnp.inf))
        scores = jnp.einsum('bhtk,bhsk,bhtsk->bhts', qc, kc, decay)
        o_intra = jnp.einsum('bhts,bhsv->bhtv', scores, vc)
        b_last = b[..., -1:, :]
        new_state = jnp.exp(b_last[..., 0, :])[..., None] * state + jnp.einsum('bhsk,bhsv->bhkv', kc * jnp.exp(b_last - b), vc)
        return new_state, o_inter + o_intra

    _, o = lax.scan(step, s0, (chunks(q), chunks(k), chunks(v), chunks(logf)))
    return jnp.moveaxis(o, 0, 2).reshape(bsz, h, n, dv)


def gated_head_norm(o, gz, g):
    bsz, h, n, dv = o.shape
    o = o.transpose(0, 2, 1, 3)
    o = o * lax.rsqrt(jnp.mean(o * o, axis=-1, keepdims=True) + EPS) * g.astype(F32).reshape(h, dv)
    gate = jax.nn.silu(gz.astype(F32)).reshape(bsz, n, h, dv)
    return (o * gate).reshape(bsz, n, h * dv).astype(gz.dtype)


def hgrn2_mixer(parts_lat, parts_ctx, lb_fwd, lb_bwd, norm_g, need_ctx):
    def prep(qz, ffz, fbz, iz):
        q = jax.nn.silu(to_heads(qz, A_HEADS).astype(F32)) * (A_DK ** -0.5)
        v = to_heads(iz, A_HEADS).astype(F32)
        kf, lf = hgrn_gates(to_heads(ffz, A_HEADS), lb_fwd)
        kb, lbw = hgrn_gates(to_heads(fbz, A_HEADS), lb_bwd)
        return q, v, kf, lf, kb, lbw

    def flip(t):
        return jnp.flip(t, axis=-2)

    ql, vl, kfl, lfl, kbl, lbl = prep(*parts_lat[:4])
    qc, vc, kfc, lfc, kbc, lbc = prep(*parts_ctx[:4])
    s_fwd = gla_final_state(kfc, vc, lfc)
    s_bwd = gla_final_state(flip(kbc), flip(vc), flip(lbc))
    o_lat = gla_chunk_scan(ql, kfl, vl, lfl, s_fwd) + flip(gla_chunk_scan(flip(ql), flip(kbl), flip(vl), flip(lbl), s_bwd))
    out_lat = gated_head_norm(o_lat, parts_lat[4], norm_g)
    if not need_ctx:
        return out_lat, None
    zero = jnp.zeros_like(s_fwd)
    o_ctx = gla_chunk_scan(qc, kfc, vc, lfc, zero) + flip(gla_chunk_scan(flip(qc), flip(kbc), flip(vc), flip(lbc), zero))
    return out_lat, gated_head_norm(o_ctx, parts_ctx[4], norm_g)


def attend(q, k, v):
    s = jnp.einsum('bkgqd,bkld->bkgql', q, k).astype(F32) * (B_HD ** -0.5)
    p = jax.nn.softmax(s, axis=-1).astype(v.dtype)
    return jnp.einsum('bkgql,bkld->bkgqd', p, v)


def gqa_mixer(parts_lat, parts_ctx, q_norm_g, k_norm_g, cos, sin, need_ctx):
    def prep(qz, kz, vz):
        q = rmsnorm(to_heads(qz, B_HEADS), q_norm_g)
        k = rmsnorm(to_heads(kz, B_KV_HEADS), k_norm_g)
        v = to_heads(vz, B_KV_HEADS)
        return q, k, v

    ql, kl, vl = prep(*parts_lat[5:8])
    qc, kc, vc = prep(*parts_ctx[5:8])
    ql = apply_rope(ql, cos, sin)
    kl = apply_rope(kl, cos, sin)
    k_all = jnp.concatenate([kc, kl], axis=2)
    v_all = jnp.concatenate([vc, vl], axis=2)
    bsz, _, n, _ = ql.shape
    nb = n // Q_BLOCK
    qg = jnp.moveaxis(ql.reshape(bsz, B_KV_HEADS, B_GROUP, nb, Q_BLOCK, B_HD), 3, 0)
    o = lax.map(lambda qb: attend(qb, k_all, v_all), qg)
    o = jnp.moveaxis(o, 0, 3).reshape(bsz, B_HEADS, n, B_HD)
    out_lat = from_heads(o)
    if not need_ctx:
        return out_lat, None
    l = qc.shape[2]
    oc = attend(qc.reshape(bsz, B_KV_HEADS, B_GROUP, l, B_HD), kc, vc).reshape(bsz, B_HEADS, l, B_HD)
    return out_lat, from_heads(oc)


def hier_moe(xf, w_rg, b_rg, w_re, b_re, w_gate, w_up, w_down):
    t, d = xf.shape
    glog = (xf @ w_rg).astype(F32) + b_rg.astype(F32)
    gsel = jnp.argmax(glog, axis=-1)
    p_group = jnp.take_along_axis(jax.nn.softmax(glog, axis=-1), gsel[:, None], axis=1)[:, 0]
    elog_all = ((xf @ w_re).astype(F32) + b_re.astype(F32)).reshape(t, N_GROUPS, EXPERTS_PER_GROUP)
    elog = jnp.take_along_axis(elog_all, gsel[:, None, None], axis=1)[:, 0]
    top_p, top_i = lax.top_k(jax.nn.softmax(elog, axis=-1), TOP_K)
    weights = top_p / jnp.sum(top_p, axis=-1, keepdims=True) * p_group[:, None]
    expert_id = (gsel[:, None].astype(jnp.int32) * EXPERTS_PER_GROUP + top_i.astype(jnp.int32)).reshape(-1)
    m = t * TOP_K
    n_blocks = (m + ROW_BLOCK - 1) // ROW_BLOCK + N_EXPERTS
    r = n_blocks * ROW_BLOCK
    order = jnp.argsort(expert_id)
    e_sorted = expert_id[order]
    tok_sorted = (order // TOP_K).astype(jnp.int32)
    w_sorted = weights.reshape(-1)[order]
    counts = jnp.bincount(expert_id, length=N_EXPERTS)
    padded = (counts + ROW_BLOCK - 1) // ROW_BLOCK * ROW_BLOCK
    pad_end = jnp.cumsum(padded)
    start = jnp.cumsum(counts) - counts
    dest = (pad_end - padded)[e_sorted] + jnp.arange(m) - start[e_sorted]
    row_tok = jnp.full((r,), t, jnp.int32).at[dest].set(tok_sorted)
    row_w = jnp.zeros((r,), F32).at[dest].set(w_sorted)
    block_e = jnp.minimum(jnp.searchsorted(pad_end, jnp.arange(n_blocks) * ROW_BLOCK, side='right'), N_EXPERTS - 1)
    x_pad = jnp.concatenate([xf, jnp.zeros((1, d), xf.dtype)], axis=0)

    def expert_block(args):
        toks, wts, e = args
        xb = x_pad[toks]
        hb = jax.nn.silu(xb @ w_gate[e]) * (xb @ w_up[e])
        return (hb @ w_down[e]) * wts[:, None].astype(xf.dtype)

    yb = lax.map(expert_block, (row_tok.reshape(n_blocks, ROW_BLOCK), row_w.reshape(n_blocks, ROW_BLOCK), block_e))
    y = jax.ops.segment_sum(yb.reshape(r, d), row_tok, num_segments=t + 1)
    return y[:t]


def trunk_layer(h, hc, mod_lat, mod_ctx, cos, sin, norm1_g, norm2_g, w_in, lb_fwd, lb_bwd, hgrn_norm_g,
                q_norm_g, k_norm_g, w_up_a, w_up_b, w_out, w_rg, b_rg, w_re, b_re, w_gate, w_up, w_down, need_ctx):
    sh1, sc1, g1, sh2, sc2, g2 = jnp.split(mod_lat, N_MOD, axis=-1)
    csh1, csc1, cg1, csh2, csc2, cg2 = jnp.split(mod_ctx, N_MOD, axis=-1)
    xl = rmsnorm(h, norm1_g) * (1.0 + sc1) + sh1
    xc = rmsnorm(hc, norm1_g) * (1.0 + csc1) + csh1
    pl = split_in(xl @ w_in)
    pc = split_in(xc @ w_in)
    a_l, a_c = hgrn2_mixer(pl, pc, lb_fwd, lb_bwd, hgrn_norm_g, need_ctx)
    b_l, b_c = gqa_mixer(pl, pc, q_norm_g, k_norm_g, cos, sin, need_ctx)

    def merge(parts, ya, yb):
        return (jax.nn.sigmoid(parts[8]) * (ya @ w_up_a) + jax.nn.sigmoid(parts[9]) * (yb @ w_up_b)) @ w_out

    h = h + g1 * merge(pl, a_l, b_l)
    if need_ctx:
        hc = hc + cg1 * merge(pc, a_c, b_c)
    bsz, n, d = h.shape
    xl2 = rmsnorm(h, norm2_g) * (1.0 + sc2) + sh2
    if need_ctx:
        xc2 = rmsnorm(hc, norm2_g) * (1.0 + csc2) + csh2
        tokens = jnp.concatenate([xl2.reshape(-1, d), xc2.reshape(-1, d)], axis=0)
        y = hier_moe(tokens, w_rg, b_rg, w_re, b_re, w_gate, w_up, w_down)
        h = h + g2 * y[:bsz * n].reshape(bsz, n, d)
        hc = hc + cg2 * y[bsz * n:].reshape(hc.shape)
    else:
        y = hier_moe(xl2.reshape(-1, d), w_rg, b_rg, w_re, b_re, w_gate, w_up, w_down)
        h = h + g2 * y.reshape(bsz, n, d)
    return h, hc


def setup_inputs(seed: int = 0) -> dict:
    key = jax.random.key(seed)
    ks = jax.random.split(key, 24)

    def nrm(k, shape, scale):
        return jax.random.normal(k, shape, F32) * scale

    def gain(k, shape):
        return 1.0 + 0.05 * jax.random.normal(k, shape, F32)

    return {
        'x': nrm(ks[0], (BATCH, SEQ, D_MODEL), 1.0),
        'c': nrm(ks[1], (BATCH, D_MODEL), 1.0),
        'ctx': nrm(ks[2], (BATCH, CTX_LEN, D_MODEL), 1.0),
        'c_ctx': nrm(ks[3], (D_MODEL,), 1.0),
        'w_ada': nrm(ks[4], (DEPTH, D_MODEL, N_MOD * D_MODEL), 0.3 * D_MODEL ** -0.5),
        'b_ada': nrm(ks[5], (DEPTH, N_MOD * D_MODEL), 0.02),
        'norm1_g': gain(ks[6], (DEPTH, D_MODEL)),
        'norm2_g': gain(ks[7], (DEPTH, D_MODEL)),
        'w_in': nrm(ks[8], (DEPTH, D_MODEL, D_IN), D_MODEL ** -0.5),
        'lb_logits': nrm(ks[9], (2, DEPTH, A_WIDTH), 0.5),
        'hgrn_norm_g': gain(ks[10], (DEPTH, A_VWIDTH)),
        'q_norm_g': gain(ks[11], (DEPTH, B_HD)),
        'k_norm_g': gain(ks[12], (DEPTH, B_HD)),
        'w_up_a': nrm(ks[13], (DEPTH, A_VWIDTH, D_MODEL), A_VWIDTH ** -0.5),
        'w_up_b': nrm(ks[14], (DEPTH, B_WIDTH, D_MODEL), B_WIDTH ** -0.5),
        'w_out': nrm(ks[15], (DEPTH, D_MODEL, D_MODEL), D_MODEL ** -0.5),
        'w_router_group': nrm(ks[16], (DEPTH, D_MODEL, N_GROUPS), D_MODEL ** -0.5),
        'b_router_group': nrm(ks[17], (DEPTH, N_GROUPS), 0.01),
        'w_router_expert': nrm(ks[18], (DEPTH, D_MODEL, N_EXPERTS), D_MODEL ** -0.5),
        'b_router_expert': nrm(ks[19], (DEPTH, N_EXPERTS), 0.01),
        'w_gate': nrm(ks[20], (DEPTH, N_EXPERTS, D_MODEL, D_EXPERT), D_MODEL ** -0.5),
        'w_up': nrm(ks[21], (DEPTH, N_EXPERTS, D_MODEL, D_EXPERT), D_MODEL ** -0.5),
        'w_down': nrm(ks[22], (DEPTH, N_EXPERTS, D_EXPERT, D_MODEL), D_EXPERT ** -0.5),
        'final_norm_g': gain(ks[23], (D_MODEL,)),
    }


def reference(x, c, ctx, c_ctx, w_ada, b_ada, norm1_g, norm2_g, w_in, lb_logits, hgrn_norm_g, q_norm_g,
              k_norm_g, w_up_a, w_up_b, w_out, w_router_group, b_router_group, w_router_expert,
              b_router_expert, w_gate, w_up, w_down, final_norm_g):
    rows = x.shape[1] // GRID_W
    cos, sin = axial_rope_tables(rows)
    lb = jnp.cumsum(jax.nn.softmax(lb_logits.astype(F32), axis=1), axis=1)
    lb = lb - lb[:, :1]
    c_act = jax.nn.silu(c)
    cc_act = jax.nn.silu(c_ctx)
    h, hc = x, ctx
    for l in range(DEPTH):
        mod_lat = (c_act @ w_ada[l] + b_ada[l])[:, None, :]
        mod_ctx = cc_act @ w_ada[l] + b_ada[l]
        h, hc = trunk_layer(h, hc, mod_lat, mod_ctx, cos, sin, norm1_g[l], norm2_g[l], w_in[l],
                            lb[0, l].reshape(A_HEADS, 1, A_DK), lb[1, l].reshape(A_HEADS, 1, A_DK),
                            hgrn_norm_g[l], q_norm_g[l], k_norm_g[l], w_up_a[l], w_up_b[l], w_out[l],
                            w_router_group[l], b_router_group[l], w_router_expert[l], b_router_expert[l],
                            w_gate[l], w_up[l], w_down[l], l < DEPTH - 1)
    return rmsnorm(h, final_norm_g)
```

```python
import functools

import jax
import jax.numpy as jnp
from jax import lax
from jax.experimental import pallas as pl
from jax.experimental.pallas import tpu as pltpu

F32 = jnp.float32
MXU_DTYPE = jnp.bfloat16
HIGHEST = lax.Precision.HIGHEST

EPS = 1e-6
GRID_W = 64
ROPE_THETA = 10000.0
LANES = 128

A_HEADS = 4
A_DK = 128
A_WIDTH = A_HEADS * A_DK
HGRN_CHUNK = 64
HGRN_SUB = 16
HGRN_EXP_CLAMP = 60.0
B_HEADS = 8
B_KV_HEADS = 2
B_HD = 64
B_GROUP = B_HEADS // B_KV_HEADS
B_WIDTH = B_HEADS * B_HD
B_KV_WIDTH = B_KV_HEADS * B_HD
AXIS_DIM = B_HD // 2
N_GROUPS = 4
EXPERTS_PER_GROUP = 8
N_EXPERTS = N_GROUPS * EXPERTS_PER_GROUP
TOP_K = 2
N_MOD = 6

OFF_GATES = 0
OFF_QA = 2048
OFF_FF = OFF_QA + A_WIDTH
OFF_FB = OFF_FF + A_WIDTH
OFF_IA = OFF_FB + A_WIDTH
OFF_OG = OFF_IA + A_WIDTH
OFF_QB = OFF_OG + A_WIDTH
OFF_KB = OFF_QB + B_WIDTH
OFF_VB = OFF_KB + B_KV_WIDTH
D_IN = OFF_VB + B_KV_WIDTH

VMEM_LIMIT = 52 * 1024 * 1024


def _cparams(sem):
    return pltpu.CompilerParams(dimension_semantics=sem, vmem_limit_bytes=VMEM_LIMIT)


def _sigmoid(x):
    return 1.0 / (1.0 + jnp.exp(-x))


def _ada_kernel(a_ref, w_ref, b_ref, o_ref):
    a = a_ref[...]
    a = a * _sigmoid(a)
    o_ref[0] = jnp.dot(a.astype(MXU_DTYPE), w_ref[0].astype(MXU_DTYPE), preferred_element_type=F32) + b_ref[0]


def ada_mod(act, w_ada, b_ada):
    depth, d, n = w_ada.shape
    r = act.shape[0]
    tn = 1024
    return pl.pallas_call(
        _ada_kernel,
        grid=(depth, n // tn),
        in_specs=[
            pl.BlockSpec((r, d), lambda l, j: (0, 0)),
            pl.BlockSpec((1, d, tn), lambda l, j: (l, 0, j)),
            pl.BlockSpec((1, 1, tn), lambda l, j: (l, 0, j)),
        ],
        out_specs=pl.BlockSpec((1, r, tn), lambda l, j: (l, 0, j)),
        out_shape=jax.ShapeDtypeStruct((depth, r, n), F32),
        compiler_params=_cparams(("parallel", "parallel")),
        name="ada_mod",
    )(act, w_ada, b_ada.reshape(depth, 1, n))


def _modulated_norm(x, g, sc, sh):
    ms = jnp.mean(x * x, axis=-1, keepdims=True)
    return (x * lax.rsqrt(ms + EPS) * g) * (1.0 + sc) + sh


def _inproj_kernel(h_ref, g_ref, sc_ref, sh_ref, w_ref, z_ref, *, col_chunk):
    y = _modulated_norm(h_ref[0], g_ref[...], sc_ref[0], sh_ref[0]).astype(MXU_DTYPE)
    for c in range(D_IN // col_chunk):
        cs = slice(c * col_chunk, (c + 1) * col_chunk)
        z_ref[0, :, cs] = jnp.dot(y, w_ref[:, cs], preferred_element_type=F32).astype(z_ref.dtype)


def in_proj(h, norm_g, sc, sh, w_in):
    bsz, n, d = h.shape
    tm = min(512, n)
    kern = functools.partial(_inproj_kernel, col_chunk=768)
    return pl.pallas_call(
        kern,
        grid=(bsz, n // tm),
        in_specs=[
            pl.BlockSpec((1, tm, d), lambda b, i: (b, i, 0)),
            pl.BlockSpec((1, d), lambda b, i: (0, 0)),
            pl.BlockSpec((1, 1, d), lambda b, i: (b, 0, 0)),
            pl.BlockSpec((1, 1, d), lambda b, i: (b, 0, 0)),
            pl.BlockSpec((d, D_IN), lambda b, i: (0, 0)),
        ],
        out_specs=pl.BlockSpec((1, tm, D_IN), lambda b, i: (b, i, 0)),
        out_shape=jax.ShapeDtypeStruct((bsz, n, D_IN), MXU_DTYPE),
        compiler_params=_cparams(("parallel", "parallel")),
        name="in_proj",
    )(h, norm_g.reshape(1, d), sc, sh, w_in)


def _hgrn_gates(zf, loglb, log1m, onem):
    e = jnp.exp(-jnp.abs(zf))
    logsig = jnp.minimum(zf, 0.0) - jnp.log1p(e)
    c = log1m + logsig
    logf = jnp.maximum(loglb, c) + jnp.log1p(jnp.exp(-jnp.abs(loglb - c)))
    inv = 1.0 / (1.0 + e)
    k = onem * jnp.where(zf > 0, e * inv, inv)
    return k, logf


def _hgrn_chunk(q, k, v, logf, state, cum_mat, rev):
    c = q.shape[0]
    nsub = c // HGRN_SUB
    cum = jnp.dot(cum_mat, logf, precision=HIGHEST, preferred_element_type=F32)
    tot = cum[0:1] if rev else cum[c - 1:c]
    qd = (q * jnp.exp(cum)).astype(MXU_DTYPE)
    o = lax.dot_general(qd, state.astype(MXU_DTYPE), (((1,), (1,)), ((), ())), preferred_element_type=F32)

    refs = []
    for i in range(nsub):
        if rev:
            r = cum[(i + 1) * HGRN_SUB:(i + 1) * HGRN_SUB + 1] if i < nsub - 1 else jnp.zeros_like(tot)
        else:
            r = cum[i * HGRN_SUB - 1:i * HGRN_SUB] if i > 0 else jnp.zeros_like(tot)
        refs.append(r)
    ref_rows = jnp.concatenate([jnp.broadcast_to(r, (HGRN_SUB, r.shape[1])) for r in refs], axis=0)
    qt = q * jnp.exp(cum - ref_rows)
    row_blk = lax.broadcasted_iota(jnp.int32, qt.shape, 0) // HGRN_SUB
    q_big = jnp.concatenate([jnp.where(row_blk == i, qt, 0.0) for i in range(nsub)], axis=1)
    k_big = jnp.concatenate(
        [k * jnp.exp(jnp.minimum(refs[i] - cum, HGRN_EXP_CLAMP)) for i in range(nsub)], axis=1)
    scores = lax.dot_general(q_big.astype(MXU_DTYPE), k_big.astype(MXU_DTYPE), (((1,), (1,)), ((), ())),
                             preferred_element_type=F32)
    scores = jnp.where(cum_mat > 0, scores, 0.0)
    o = o + jnp.dot(scores.astype(MXU_DTYPE), v, preferred_element_type=F32)

    kd = (k * jnp.exp(tot - cum)).astype(MXU_DTYPE)
    upd = lax.dot_general(v, kd, (((0,), (0,)), ((), ())), preferred_element_type=F32)
    return o, state * jnp.exp(tot) + upd


def _hgrn_kernel(*refs, rev, n_chunks, fuse):
    if fuse:
        q_ref, f_ref, v_ref, lbp_ref, s0_ref, ob_ref, og_ref, ng_ref, o_ref, sT_ref, state = refs
    else:
        q_ref, f_ref, v_ref, lbp_ref, s0_ref, o_ref, sT_ref, state = refs
    i = pl.program_id(2)

    @pl.when(i == 0)
    def _():
        state[...] = s0_ref[0, 0]

    loglb = lbp_ref[0, 0:1, :]
    log1m = lbp_ref[0, 1:2, :]
    onem = lbp_ref[0, 2:3, :]
    c = HGRN_CHUNK
    r_i = lax.broadcasted_iota(jnp.int32, (c, c), 0)
    c_i = lax.broadcasted_iota(jnp.int32, (c, c), 1)
    cum_mat = ((c_i >= r_i) if rev else (c_i <= r_i)).astype(F32)
    s = state[...]
    order = range(n_chunks - 1, -1, -1) if rev else range(n_chunks)
    for ci in order:
        sl = slice(ci * c, (ci + 1) * c)
        k, logf = _hgrn_gates(f_ref[0, sl, :].astype(F32), loglb, log1m, onem)
        zq = q_ref[0, sl, :].astype(F32)
        q = zq * _sigmoid(zq) * (A_DK ** -0.5)
        o, s = _hgrn_chunk(q, k, v_ref[0, sl, :], logf, s, cum_mat, rev)
        if fuse:
            o = o + ob_ref[0, sl, :]
            o = o * lax.rsqrt(jnp.mean(o * o, axis=-1, keepdims=True) + EPS) * ng_ref[0]
            og = og_ref[0, sl, :].astype(F32)
            o = o * (og * _sigmoid(og))
        o_ref[0, sl, :] = o.astype(o_ref.dtype)
    state[...] = s

    @pl.when(i == pl.num_programs(2) - 1)
    def _():
        sT_ref[0, 0] = s


def hgrn_scan(z, lbp, s0, rev, o_other=None, norm_g=None):
    bsz, n, _ = z.shape
    fuse = o_other is not None
    t = min(256, n)
    nt = n // t
    f_off = (OFF_FB if rev else OFF_FF) // LANES

    def tmap(i):
        return nt - 1 - i if rev else i

    def col(off):
        return pl.BlockSpec((1, t, LANES), lambda b, h, i: (b, tmap(i), off + h))

    in_specs = [col(OFF_QA // LANES), col(f_off), col(OFF_IA // LANES),
                pl.BlockSpec((1, 8, LANES), lambda b, h, i: (h, 0, 0)),
                pl.BlockSpec((1, 1, LANES, LANES), lambda b, h, i: (b, h, 0, 0))]
    args = [z, z, z, lbp, s0]
    if fuse:
        in_specs += [pl.BlockSpec((1, t, LANES), lambda b, h, i: (b, tmap(i), h)),
                     col(OFF_OG // LANES),
                     pl.BlockSpec((1, 1, LANES), lambda b, h, i: (h, 0, 0))]
        args += [o_other, z, norm_g.reshape(A_HEADS, 1, LANES)]
    kern = functools.partial(_hgrn_kernel, rev=rev, n_chunks=t // HGRN_CHUNK, fuse=fuse)
    return pl.pallas_call(
        kern,
        grid=(bsz, A_HEADS, nt),
        in_specs=in_specs,
        out_specs=[pl.BlockSpec((1, t, LANES), lambda b, h, i: (b, tmap(i), h)),
                   pl.BlockSpec((1, 1, LANES, LANES), lambda b, h, i: (b, h, 0, 0))],
        out_shape=[jax.ShapeDtypeStruct((bsz, n, A_WIDTH), MXU_DTYPE if fuse else F32),
                   jax.ShapeDtypeStruct((bsz, A_HEADS, LANES, LANES), F32)],
        scratch_shapes=[pltpu.VMEM((LANES, LANES), F32)],
        compiler_params=_cparams(("parallel", "parallel", "arbitrary")),
        name="hgrn_bwd" if rev else "hgrn_fwd",
    )(*args)


def _swap16(x):
    lane = lax.broadcasted_iota(jnp.int32, x.shape, 1)
    return jnp.where(lane % 32 < 16, pltpu.roll(x, LANES - 16, 1), pltpu.roll(x, 16, 1))


def _head_norm_rope(x, gsum, g, cos, sin):
    ssum = jnp.dot(x * x, gsum, precision=HIGHEST, preferred_element_type=F32)
    y = x * lax.rsqrt(ssum * (1.0 / B_HD) + EPS) * g
    if cos is not None:
        y = y * cos + _swap16(y) * sin
    return y


def _qkprep_kernel(*refs, rope):
    if rope:
        q_ref, k_ref, v_ref, qg_ref, kg_ref, cos_ref, sin_ref, qo_ref, ko_ref, vo_ref = refs
        cos, sin = cos_ref[...], sin_ref[...]
    else:
        q_ref, k_ref, v_ref, qg_ref, kg_ref, qo_ref, ko_ref, vo_ref = refs
        cos = sin = None
    r_i = lax.broadcasted_iota(jnp.int32, (LANES, LANES), 0)
    c_i = lax.broadcasted_iota(jnp.int32, (LANES, LANES), 1)
    gsum = (r_i // B_HD == c_i // B_HD).astype(F32)
    lane = lax.broadcasted_iota(jnp.int32, (q_ref.shape[1], LANES), 1)
    for cb in range(B_WIDTH // LANES):
        x = q_ref[0, :, cb * LANES:(cb + 1) * LANES].astype(F32)
        y = _head_norm_rope(x, gsum, qg_ref[...], cos, sin) * (B_HD ** -0.5)
        swapped = pltpu.roll(y, B_HD, 1)
        for half in range(2):
            head = 2 * cb + half
            kv = head // B_GROUP
            src = y if half == kv else swapped
            keep = (lane // B_HD) == kv
            qo_ref[0, head] = jnp.where(keep, src, 0.0).astype(qo_ref.dtype)
    kx = k_ref[0].astype(F32)
    ko_ref[0] = _head_norm_rope(kx, gsum, kg_ref[...], cos, sin).astype(ko_ref.dtype)
    vo_ref[0] = v_ref[0]


def qk_prep(z, q_norm_g, k_norm_g, cos_t, sin_t):
    bsz, n, _ = z.shape
    rope = cos_t is not None
    tm = min(256, n)
    qg = jnp.tile(q_norm_g.astype(F32), LANES // B_HD).reshape(1, LANES)
    kg = jnp.tile(k_norm_g.astype(F32), LANES // B_HD).reshape(1, LANES)
    in_specs = [pl.BlockSpec((1, tm, B_WIDTH), lambda b, i: (b, i, OFF_QB // B_WIDTH)),
                pl.BlockSpec((1, tm, LANES), lambda b, i: (b, i, OFF_KB // LANES)),
                pl.BlockSpec((1, tm, LANES), lambda b, i: (b, i, OFF_VB // LANES)),
                pl.BlockSpec((1, LANES), lambda b, i: (0, 0)),
                pl.BlockSpec((1, LANES), lambda b, i: (0, 0))]
    args = [z, z, z, qg, kg]
    if rope:
        in_specs += [pl.BlockSpec((tm, LANES), lambda b, i: (i, 0))] * 2
        args += [cos_t, sin_t]
    return pl.pallas_call(
        functools.partial(_qkprep_kernel, rope=rope),
        grid=(bsz, n // tm),
        in_specs=in_specs,
        out_specs=[pl.BlockSpec((1, B_HEADS, tm, LANES), lambda b, i: (b, 0, i, 0)),
                   pl.BlockSpec((1, tm, LANES), lambda b, i: (b, i, 0)),
                   pl.BlockSpec((1, tm, LANES), lambda b, i: (b, i, 0))],
        out_shape=[jax.ShapeDtypeStruct((bsz, B_HEADS, n, LANES), MXU_DTYPE),
                   jax.ShapeDtypeStruct((bsz, n, LANES), MXU_DTYPE),
                   jax.ShapeDtypeStruct((bsz, n, LANES), MXU_DTYPE)],
        compiler_params=_cparams(("parallel", "parallel")),
        name="qk_prep_rope" if rope else "qk_prep",
    )(*args)


def rope_tables(n):
    t = jnp.arange(n)
    r = (t // GRID_W).astype(F32)
    col = (t % GRID_W).astype(F32)
    inv = ROPE_THETA ** (-jnp.arange(0, AXIS_DIM, 2, dtype=F32) / AXIS_DIM)
    ang_r = r[:, None] * inv
    ang_c = col[:, None] * inv
    cos = jnp.concatenate([jnp.cos(ang_r)] * 2 + [jnp.cos(ang_c)] * 2, axis=1)
    sin = jnp.concatenate([-jnp.sin(ang_r), jnp.sin(ang_r), -jnp.sin(ang_c), jnp.sin(ang_c)], axis=1)
    return jnp.tile(cos, (1, 2)), jnp.tile(sin, (1, 2))


def _attn_kernel(q_ref, k_ref, v_ref, o_ref, acc, m_sc, *, tk):
    tq = q_ref.shape[2]
    rows = B_GROUP * tq
    nk = k_ref.shape[1] // tk
    lane = lax.broadcasted_iota(jnp.int32, (tk, LANES), 1)
    lane_o = lax.broadcasted_iota(jnp.int32, (tq, LANES), 1)
    for kv in range(B_KV_HEADS):
        q = q_ref[0, kv * B_GROUP:(kv + 1) * B_GROUP].reshape(rows, LANES)
        acc[...] = jnp.zeros_like(acc)
        m_sc[...] = jnp.full_like(m_sc, -jnp.inf)

        def body(j, carry, q=q, kv=kv):
            ks = k_ref[0, pl.ds(pl.multiple_of(j * tk, tk), tk), :]
            vs = v_ref[0, pl.ds(pl.multiple_of(j * tk, tk), tk), :]
            vs = jnp.where((lane // B_HD) == kv, vs, jnp.ones_like(vs))
            s = lax.dot_general(q, ks, (((1,), (1,)), ((), ())), preferred_element_type=F32)
            m_old = m_sc[...]
            m_new = jnp.maximum(m_old, jnp.max(s, axis=-1, keepdims=True))
            p = jnp.exp(s - m_new)
            acc[...] = acc[...] * jnp.exp(m_old - m_new) + jnp.dot(p.astype(MXU_DTYPE), vs,
                                                                    preferred_element_type=F32)
            m_sc[...] = m_new
            return carry

        lax.fori_loop(0, nk, body, 0)
        a = acc[...]
        o_full = a / pltpu.roll(a, B_HD, 1)
        o_swapped = pltpu.roll(o_full, B_HD, 1)
        for pair in range(B_GROUP // 2):
            g0, g1 = 2 * pair, 2 * pair + 1
            lo_src = o_full if kv == 0 else o_swapped
            hi_src = o_swapped if kv == 0 else o_full
            blk = jnp.where(lane_o < B_HD, lo_src[g0 * tq:(g0 + 1) * tq], hi_src[g1 * tq:(g1 + 1) * tq])
            cb = kv * (B_GROUP // 2) + pair
            o_ref[0, :, cb * LANES:(cb + 1) * LANES] = blk.astype(o_ref.dtype)


def attention(q, k, v):
    bsz, _, n, _ = q.shape
    nkeys = k.shape[1]
    tq = min(128, n)
    tk = 256 if nkeys % 256 == 0 else 128
    return pl.pallas_call(
        functools.partial(_attn_kernel, tk=tk),
        grid=(bsz, n // tq),
        in_specs=[pl.BlockSpec((1, B_HEADS, tq, LANES), lambda b, i: (b, 0, i, 0)),
                  pl.BlockSpec((1, nkeys, LANES), lambda b, i: (b, 0, 0)),
                  pl.BlockSpec((1, nkeys, LANES), lambda b, i: (b, 0, 0))],
        out_specs=pl.BlockSpec((1, tq, B_WIDTH), lambda b, i: (b, i, 0)),
        out_shape=jax.ShapeDtypeStruct((bsz, n, B_WIDTH), MXU_DTYPE),
        scratch_shapes=[pltpu.VMEM((B_GROUP * tq, LANES), F32), pltpu.VMEM((B_GROUP * tq, 1), F32)],
        compiler_params=_cparams(("parallel", "parallel")),
        name="gqa_attention",
    )(q, k, v)


def _merge_kernel(h_ref, a_ref, b_ref, zg_ref, wa_ref, wb_ref, wo_ref, g1_ref, n2_ref, sc_ref, sh_ref,
                  wr_ref, br_ref, ho_ref, x2_ref, lg_ref):
    d = h_ref.shape[2]
    ua = jnp.dot(a_ref[0], wa_ref[...], preferred_element_type=F32)
    ub = jnp.dot(b_ref[0], wb_ref[...], preferred_element_type=F32)
    ga = _sigmoid(zg_ref[0, :, 0:d].astype(F32))
    gb = _sigmoid(zg_ref[0, :, d:2 * d].astype(F32))
    mix = (ga * ua + gb * ub).astype(MXU_DTYPE)
    h_new = h_ref[0] + g1_ref[0] * jnp.dot(mix, wo_ref[...], preferred_element_type=F32)
    ho_ref[0] = h_new
    x2 = _modulated_norm(h_new, n2_ref[...], sc_ref[0], sh_ref[0])
    x2_ref[0] = x2.astype(x2_ref.dtype)
    lg_ref[0] = jnp.dot(x2, wr_ref[...], precision=HIGHEST, preferred_element_type=F32) + br_ref[...]


def merge_proj(h, a, b, z, w_up_a, w_up_b, w_out, g1, norm2_g, sc2, sh2, w_router, b_router):
    bsz, n, d = h.shape
    tm = min(512, n)
    row = lambda w: pl.BlockSpec((1, tm, w), lambda b_, i: (b_, i, 0))
    full = lambda s: pl.BlockSpec(s, lambda b_, i: (0,) * len(s))
    per_b = pl.BlockSpec((1, 1, d), lambda b_, i: (b_, 0, 0))
    return pl.pallas_call(
        _merge_kernel,
        grid=(bsz, n // tm),
        in_specs=[row(d), row(A_WIDTH), row(B_WIDTH),
                  pl.BlockSpec((1, tm, 2 * d), lambda b_, i: (b_, i, OFF_GATES)),
                  full((A_WIDTH, d)), full((B_WIDTH, d)), full((d, d)),
                  per_b, full((1, d)), per_b, per_b, full((d, LANES)), full((1, LANES))],
        out_specs=[row(d), row(d), row(LANES)],
        out_shape=[jax.ShapeDtypeStruct((bsz, n, d), F32),
                   jax.ShapeDtypeStruct((bsz, n, d), MXU_DTYPE),
                   jax.ShapeDtypeStruct((bsz, n, LANES), F32)],
        compiler_params=_cparams(("parallel", "parallel")),
        name="merge_proj",
    )(h, a, b, z, w_up_a, w_up_b, w_out, g1, norm2_g.reshape(1, d), sc2, sh2, w_router, b_router)


def _first_lane_where(cond, lane):
    return jnp.min(jnp.where(cond, lane, LANES), axis=-1, keepdims=True)


def _router_kernel(lg_ref, slab_ref, cnt_ref, base):
    @pl.when(pl.program_id(0) == 0)
    def _():
        base[...] = jnp.zeros_like(base)

    lg = lg_ref[...]
    tr = lg.shape[0]
    lane = lax.broadcasted_iota(jnp.int32, lg.shape, 1)
    neg = jnp.float32(-jnp.inf)
    is_g = lane < N_GROUPS
    gl = jnp.where(is_g, lg, neg)
    gmax = jnp.max(gl, axis=-1, keepdims=True)
    gsel = _first_lane_where(gl == gmax, lane)
    p_group = 1.0 / jnp.sum(jnp.where(is_g, jnp.exp(lg - gmax), 0.0), axis=-1, keepdims=True)
    lo = N_GROUPS + gsel * EXPERTS_PER_GROUP
    in_grp = (lane >= lo) & (lane < lo + EXPERTS_PER_GROUP)
    el = jnp.where(in_grp, lg, neg)
    m1 = jnp.max(el, axis=-1, keepdims=True)
    i1 = _first_lane_where(el == m1, lane)
    el2 = jnp.where(lane == i1, neg, el)
    m2 = jnp.max(el2, axis=-1, keepdims=True)
    i2 = _first_lane_where(el2 == m2, lane)
    zsum = jnp.sum(jnp.where(in_grp, jnp.exp(lg - m1), 0.0), axis=-1, keepdims=True)
    p1 = 1.0 / zsum
    p2 = jnp.exp(m2 - m1) / zsum
    w1 = p1 / (p1 + p2) * p_group
    w2 = p2 / (p1 + p2) * p_group
    hit1 = lane == i1
    hit2 = lane == i2
    onehot = (hit1 | hit2).astype(F32)
    r_i = lax.broadcasted_iota(jnp.int32, (tr, tr), 0)
    c_i = lax.broadcasted_iota(jnp.int32, (tr, tr), 1)
    before = (c_i < r_i).astype(MXU_DTYPE)
    prior = jnp.dot(before, onehot.astype(MXU_DTYPE), preferred_element_type=F32) + base[0:1, :]
    r1 = jnp.sum(jnp.where(hit1, prior, 0.0), axis=-1, keepdims=True)
    r2 = jnp.sum(jnp.where(hit2, prior, 0.0), axis=-1, keepdims=True)
    new_base = base[0:1, :] + jnp.sum(onehot, axis=0, keepdims=True)
    base[...] = jnp.broadcast_to(new_base, base.shape)
    cnt_ref[...] = jnp.broadcast_to(new_base, cnt_ref.shape)
    vals = [(i1 - N_GROUPS).astype(F32), (i2 - N_GROUPS).astype(F32), w1, w2, r1, r2]
    slab = jnp.zeros_like(lg)
    for idx, val in enumerate(vals):
        slab = jnp.where(lane == idx, val, slab)
    slab_ref[...] = slab


def route(logits):
    t = logits.shape[0]
    tr = next(c for c in (512, 256, 128, 64, 32, 16, 8) if t % c == 0)
    return pl.pallas_call(
        _router_kernel,
        grid=(t // tr,),
        in_specs=[pl.BlockSpec((tr, LANES), lambda i: (i, 0))],
        out_specs=[pl.BlockSpec((tr, LANES), lambda i: (i, 0)), pl.BlockSpec((8, LANES), lambda i: (0, 0))],
        out_shape=[jax.ShapeDtypeStruct((t, LANES), F32), jax.ShapeDtypeStruct((8, LANES), F32)],
        scratch_shapes=[pltpu.VMEM((8, LANES), F32)],
        compiler_params=_cparams(("arbitrary",)),
        name="moe_route",
    )(logits)


def _expert_kernel(be_ref, nv_ref, x_ref, wg_ref, wu_ref, wd_ref, o_ref):
    j = pl.program_id(0)
    nv = nv_ref[j]

    @pl.when(nv > 0)
    def _():
        rows = lax.broadcasted_iota(jnp.int32, x_ref.shape, 0)
        x = jnp.where(rows < nv, x_ref[...], jnp.zeros_like(x_ref[...]))
        hg = jnp.dot(x, wg_ref[0], preferred_element_type=F32)
        hu = jnp.dot(x, wu_ref[0], preferred_element_type=F32)
        hb = (hg * _sigmoid(hg) * hu).astype(MXU_DTYPE)
        o_ref[...] = jnp.dot(hb, wd_ref[0], preferred_element_type=F32).astype(o_ref.dtype)

    @pl.when(nv == 0)
    def _():
        o_ref[...] = jnp.zeros_like(o_ref)


def expert_ffn(x_sorted, block_e, nvalid, w_gate, w_up, w_down, tmx):
    r, d = x_sorted.shape
    de = w_gate.shape[2]
    grid_spec = pltpu.PrefetchScalarGridSpec(
        num_scalar_prefetch=2,
        grid=(r // tmx,),
        in_specs=[pl.BlockSpec((tmx, d), lambda j, be, nv: (j, 0)),
                  pl.BlockSpec((1, d, de), lambda j, be, nv: (be[j], 0, 0)),
                  pl.BlockSpec((1, d, de), lambda j, be, nv: (be[j], 0, 0)),
                  pl.BlockSpec((1, de, d), lambda j, be, nv: (be[j], 0, 0))],
        out_specs=pl.BlockSpec((tmx, d), lambda j, be, nv: (j, 0)),
    )
    return pl.pallas_call(
        _expert_kernel,
        grid_spec=grid_spec,
        out_shape=jax.ShapeDtypeStruct((r, d), MXU_DTYPE),
        compiler_params=_cparams(("arbitrary",)),
        name="expert_ffn",
    )(block_e, nvalid, x_sorted, w_gate, w_up, w_down)


def _combine_kernel(*refs, final):
    if final:
        h_ref, y_ref, slab_ref, g2_ref, fn_ref, o_ref = refs
    else:
        h_ref, y_ref, slab_ref, g2_ref, o_ref = refs
    d = h_ref.shape[2]
    w1 = slab_ref[:, 2:3]
    w2 = slab_ref[:, 3:4]
    y = y_ref[:, 0:d].astype(F32) * w1 + y_ref[:, d:2 * d].astype(F32) * w2
    h_new = h_ref[0] + g2_ref[0] * y
    if final:
        ms = jnp.mean(h_new * h_new, axis=-1, keepdims=True)
        h_new = h_new * lax.rsqrt(ms + EPS) * fn_ref[...]
    o_ref[0] = h_new


def moe_combine(h, y_pairs, slab, g2, tok_off, final_g=None):
    bsz, n, d = h.shape
    tm = min(512, n)
    nt = n // tm
    off = tok_off // tm
    final = final_g is not None
    in_specs = [pl.BlockSpec((1, tm, d), lambda b, i: (b, i, 0)),
                pl.BlockSpec((tm, 2 * d), lambda b, i: (off + b * nt + i, 0)),
                pl.BlockSpec((tm, LANES), lambda b, i: (off + b * nt + i, 0)),
                pl.BlockSpec((1, 1, d), lambda b, i: (b, 0, 0))]
    args = [h, y_pairs, slab, g2]
    if final:
        in_specs.append(pl.BlockSpec((1, d), lambda b, i: (0, 0)))
        args.append(final_g.reshape(1, d))
    return pl.pallas_call(
        functools.partial(_combine_kernel, final=final),
        grid=(bsz, nt),
        in_specs=in_specs,
        out_specs=pl.BlockSpec((1, tm, d), lambda b, i: (b, i, 0)),
        out_shape=jax.ShapeDtypeStruct((bsz, n, d), F32),
        compiler_params=_cparams(("parallel", "parallel")),
        name="moe_combine_final" if final else "moe_combine",
    )(*args)


def _dispatch_plan(slab, counts, tmx, n_blocks):
    cnt = counts[0, N_GROUPS:N_GROUPS + N_EXPERTS].astype(jnp.int32)
    padded = (cnt + tmx - 1) // tmx * tmx
    pad_end = jnp.cumsum(padded)
    pad_start = pad_end - padded
    eid = slab[:, 0:TOP_K].astype(jnp.int32)
    rank = slab[:, 4:4 + TOP_K].astype(jnp.int32)
    dest = pad_start[eid] + rank
    blk_start = jnp.arange(n_blocks, dtype=jnp.int32) * tmx
    block_e = jnp.minimum(jnp.searchsorted(pad_end, blk_start, side="right"), N_EXPERTS - 1).astype(jnp.int32)
    nvalid = jnp.clip(pad_start[block_e] + cnt[block_e] - blk_start, 0, tmx).astype(jnp.int32)
    return dest, block_e, nvalid


def _scatter_rows(x, dest, n_rows):
    tok = jnp.broadcast_to(jnp.arange(x.shape[0], dtype=jnp.int32)[:, None], dest.shape)
    row_tok = jnp.zeros((n_rows,), jnp.int32).at[dest.reshape(-1)].set(tok.reshape(-1))
    return x[row_tok]


def _gather_pairs(y_sorted, dest):
    t = dest.shape[0]
    return y_sorted[dest.reshape(-1)].reshape(t, -1)


def hier_moe(x2, logits, w_gate, w_up, w_down):
    t, d = x2.shape
    tmx = min(512, t)
    m = t * TOP_K
    n_blocks = (m + N_EXPERTS * (tmx - 1)) // tmx + 1
    slab, counts = route(logits)
    dest, block_e, nvalid = _dispatch_plan(slab, counts, tmx, n_blocks)
    x_sorted = _scatter_rows(x2, dest, n_blocks * tmx)
    y_sorted = expert_ffn(x_sorted, block_e, nvalid, w_gate, w_up, w_down, tmx)
    return _gather_pairs(y_sorted, dest), slab


def _lower_bound_rows(lb_logits):
    lb = jnp.cumsum(jax.nn.softmax(lb_logits.astype(F32), axis=1), axis=1)
    lb = lb - lb[:, :1]
    rows = jnp.stack([jnp.log(lb), jnp.log1p(-lb), 1.0 - lb], axis=2)
    rows = jnp.concatenate([rows, jnp.zeros(rows.shape[:2] + (5, A_WIDTH), F32)], axis=2)
    depth = rows.shape[1]
    return rows.reshape(2, depth, 8, A_HEADS, A_DK).transpose(0, 1, 3, 2, 4)


def _reorder_w_in(w):
    parts, off = [], 0
    for s in (A_WIDTH,) * 5 + (B_WIDTH, B_KV_WIDTH, B_KV_WIDTH, w.shape[1], w.shape[1]):
        parts.append(w[..., off:off + s])
        off += s
    return jnp.concatenate(parts[8:] + parts[:8], axis=-1)


def kernel(x, c, ctx, c_ctx, w_ada, b_ada, norm1_g, norm2_g, w_in, lb_logits, hgrn_norm_g, q_norm_g, k_norm_g,
           w_up_a, w_up_b, w_out, w_router_group, b_router_group, w_router_expert, b_router_expert, w_gate, w_up,
           w_down, final_norm_g):
    bsz, n, d = x.shape
    n_ctx = ctx.shape[1]
    depth = w_in.shape[0]
    mx = MXU_DTYPE

    w_in_r = _reorder_w_in(w_in).astype(mx)
    w_up_a_c, w_up_b_c, w_out_c = w_up_a.astype(mx), w_up_b.astype(mx), w_out.astype(mx)
    w_gate_c, w_up_c, w_down_c = w_gate.astype(mx), w_up.astype(mx), w_down.astype(mx)
    n_r = N_GROUPS + N_EXPERTS
    w_router = jnp.concatenate([w_router_group, w_router_expert, jnp.zeros((depth, d, LANES - n_r), F32)], axis=-1)
    b_router = jnp.concatenate([b_router_group, b_router_expert, jnp.zeros((depth, LANES - n_r), F32)], axis=-1)
    lbp = _lower_bound_rows(lb_logits)
    cos_t, sin_t = rope_tables(n)

    n_rows = -(-(bsz + 1) // 8) * 8
    act = jnp.concatenate([c, c_ctx[None, :], jnp.zeros((n_rows - bsz - 1, d), F32)], axis=0)
    mod = ada_mod(act, w_ada, b_ada)

    h, hc = x, ctx
    zero_state = jnp.zeros((bsz, A_HEADS, LANES, LANES), F32)
    for l in range(depth):
        need_ctx = l < depth - 1
        ml = mod[l, :bsz].reshape(bsz, 1, N_MOD, d)
        mc = jnp.broadcast_to(mod[l, bsz].reshape(1, 1, N_MOD, d), (bsz, 1, N_MOD, d))
        sh1, sc1, g1, sh2, sc2, g2 = (ml[:, :, i] for i in range(N_MOD))
        csh1, csc1, cg1, csh2, csc2, cg2 = (mc[:, :, i] for i in range(N_MOD))

        zl = in_proj(h, norm1_g[l], sc1, sh1, w_in_r[l])
        zc = in_proj(hc, norm1_g[l], csc1, csh1, w_in_r[l])
        ob_c, s_bwd = hgrn_scan(zc, lbp[1, l], zero_state, True)
        if need_ctx:
            a_c, s_fwd = hgrn_scan(zc, lbp[0, l], zero_state, False, ob_c, hgrn_norm_g[l])
        else:
            _, s_fwd = hgrn_scan(zc, lbp[0, l], zero_state, False)
        ob_l, _ = hgrn_scan(zl, lbp[1, l], s_bwd, True)
        a_l, _ = hgrn_scan(zl, lbp[0, l], s_fwd, False, ob_l, hgrn_norm_g[l])

        q_l, k_l, v_l = qk_prep(zl, q_norm_g[l], k_norm_g[l], cos_t, sin_t)
        q_c, k_c, v_c = qk_prep(zc, q_norm_g[l], k_norm_g[l], None, None)
        b_l = attention(q_l, jnp.concatenate([k_c, k_l], axis=1), jnp.concatenate([v_c, v_l], axis=1))

        h, x2_l, lg_l = merge_proj(h, a_l, b_l, zl, w_up_a_c[l], w_up_b_c[l], w_out_c[l], g1, norm2_g[l], sc2, sh2,
                                   w_router[l], b_router[l][None, :])
        x2 = x2_l.reshape(bsz * n, d)
        lg = lg_l.reshape(bsz * n, LANES)
        if need_ctx:
            b_c = attention(q_c, k_c, v_c)
            hc, x2_c, lg_c = merge_proj(hc, a_c, b_c, zc, w_up_a_c[l], w_up_b_c[l], w_out_c[l], cg1, norm2_g[l],
                                        csc2, csh2, w_router[l], b_router[l][None, :])
            x2 = jnp.concatenate([x2, x2_c.reshape(bsz * n_ctx, d)], axis=0)
            lg = jnp.concatenate([lg, lg_c.reshape(bsz * n_ctx, LANES)], axis=0)

        y_pairs, slab = hier_moe(x2, lg, w_gate_c[l], w_up_c[l], w_down_c[l])
        h = moe_combine(h, y_pairs, slab, g2, 0, None if need_ctx else final_norm_g)
        if need_ctx:
            hc = moe_combine(hc, y_pairs, slab, cg2, bsz * n)
    return h
```

```python
import functools

import jax
import jax.numpy as jnp
from jax import lax
from jax.experimental import pallas as pl
from jax.experimental.pallas import tpu as pltpu
from jax.experimental.pallas import tpu_sc as plsc

F32 = jnp.float32
MXU_DTYPE = jnp.bfloat16
HIGHEST = lax.Precision.HIGHEST

EPS = 1e-6
GRID_W = 64
ROPE_THETA = 10000.0
LANES = 128

A_HEADS = 4
A_DK = 128
A_WIDTH = A_HEADS * A_DK
HGRN_CHUNK = 64
HGRN_SUB = 16
HGRN_EXP_CLAMP = 60.0
B_HEADS = 8
B_KV_HEADS = 2
B_HD = 64
B_GROUP = B_HEADS // B_KV_HEADS
B_WIDTH = B_HEADS * B_HD
B_KV_WIDTH = B_KV_HEADS * B_HD
AXIS_DIM = B_HD // 2
N_GROUPS = 4
EXPERTS_PER_GROUP = 8
N_EXPERTS = N_GROUPS * EXPERTS_PER_GROUP
TOP_K = 2
N_MOD = 6

OFF_GATES = 0
OFF_QA = 2048
OFF_FF = OFF_QA + A_WIDTH
OFF_FB = OFF_FF + A_WIDTH
OFF_IA = OFF_FB + A_WIDTH
OFF_OG = OFF_IA + A_WIDTH
OFF_QB = OFF_OG + A_WIDTH
OFF_KB = OFF_QB + B_WIDTH
OFF_VB = OFF_KB + B_KV_WIDTH
D_IN = OFF_VB + B_KV_WIDTH

VMEM_LIMIT = 52 * 1024 * 1024


def _cparams(sem):
    return pltpu.CompilerParams(dimension_semantics=sem, vmem_limit_bytes=VMEM_LIMIT)


def _sigmoid(x):
    return 1.0 / (1.0 + jnp.exp(-x))


def _ada_kernel(a_ref, w_ref, b_ref, o_ref):
    a = a_ref[...]
    a = a * _sigmoid(a)
    o_ref[0] = jnp.dot(a.astype(MXU_DTYPE), w_ref[0].astype(MXU_DTYPE), preferred_element_type=F32) + b_ref[0]


def ada_mod(act, w_ada, b_ada):
    depth, d, n = w_ada.shape
    r = act.shape[0]
    tn = 1024
    return pl.pallas_call(
        _ada_kernel,
        grid=(depth, n // tn),
        in_specs=[
            pl.BlockSpec((r, d), lambda l, j: (0, 0)),
            pl.BlockSpec((1, d, tn), lambda l, j: (l, 0, j)),
            pl.BlockSpec((1, 1, tn), lambda l, j: (l, 0, j)),
        ],
        out_specs=pl.BlockSpec((1, r, tn), lambda l, j: (l, 0, j)),
        out_shape=jax.ShapeDtypeStruct((depth, r, n), F32),
        compiler_params=_cparams(("parallel", "parallel")),
        name="ada_mod",
    )(act, w_ada, b_ada.reshape(depth, 1, n))


def _modulated_norm(x, g, sc, sh):
    ms = jnp.mean(x * x, axis=-1, keepdims=True)
    return (x * lax.rsqrt(ms + EPS) * g) * (1.0 + sc) + sh


def _inproj_kernel(h_ref, g_ref, sc_ref, sh_ref, w_ref, z_ref, *, col_chunk):
    y = _modulated_norm(h_ref[0], g_ref[...], sc_ref[0], sh_ref[0]).astype(MXU_DTYPE)
    for c in range(D_IN // col_chunk):
        cs = slice(c * col_chunk, (c + 1) * col_chunk)
        z_ref[0, :, cs] = jnp.dot(y, w_ref[:, cs], preferred_element_type=F32).astype(z_ref.dtype)


def in_proj(h, norm_g, sc, sh, w_in):
    bsz, n, d = h.shape
    tm = min(512, n)
    kern = functools.partial(_inproj_kernel, col_chunk=768)
    return pl.pallas_call(
        kern,
        grid=(bsz, n // tm),
        in_specs=[
            pl.BlockSpec((1, tm, d), lambda b, i: (b, i, 0)),
            pl.BlockSpec((1, d), lambda b, i: (0, 0)),
            pl.BlockSpec((1, 1, d), lambda b, i: (b, 0, 0)),
            pl.BlockSpec((1, 1, d), lambda b, i: (b, 0, 0)),
            pl.BlockSpec((d, D_IN), lambda b, i: (0, 0)),
        ],
        out_specs=pl.BlockSpec((1, tm, D_IN), lambda b, i: (b, i, 0)),
        out_shape=jax.ShapeDtypeStruct((bsz, n, D_IN), MXU_DTYPE),
        compiler_params=_cparams(("parallel", "parallel")),
        name="in_proj",
    )(h, norm_g.reshape(1, d), sc, sh, w_in)


def _hgrn_gates(zf, loglb, log1m, onem):
    e = jnp.exp(-jnp.abs(zf))
    logsig = jnp.minimum(zf, 0.0) - jnp.log1p(e)
    c = log1m + logsig
    logf = jnp.maximum(loglb, c) + jnp.log1p(jnp.exp(-jnp.abs(loglb - c)))
    inv = 1.0 / (1.0 + e)
    k = onem * jnp.where(zf > 0, e * inv, inv)
    return k, logf


def _hgrn_chunk(q, k, v, logf, state, cum_mat, rev):
    c = q.shape[0]
    nsub = c // HGRN_SUB
    cum = jnp.dot(cum_mat, logf, precision=HIGHEST, preferred_element_type=F32)
    tot = cum[0:1] if rev else cum[c - 1:c]
    qd = (q * jnp.exp(cum)).astype(MXU_DTYPE)
    o = lax.dot_general(qd, state.astype(MXU_DTYPE), (((1,), (1,)), ((), ())), preferred_element_type=F32)

    refs = []
    for i in range(nsub):
        if rev:
            r = cum[(i + 1) * HGRN_SUB:(i + 1) * HGRN_SUB + 1] if i < nsub - 1 else jnp.zeros_like(tot)
        else:
            r = cum[i * HGRN_SUB - 1:i * HGRN_SUB] if i > 0 else jnp.zeros_like(tot)
        refs.append(r)
    ref_rows = jnp.concatenate([jnp.broadcast_to(r, (HGRN_SUB, r.shape[1])) for r in refs], axis=0)
    qt = q * jnp.exp(cum - ref_rows)
    row_blk = lax.broadcasted_iota(jnp.int32, qt.shape, 0) // HGRN_SUB
    q_big = jnp.concatenate([jnp.where(row_blk == i, qt, 0.0) for i in range(nsub)], axis=1)
    k_big = jnp.concatenate(
        [k * jnp.exp(jnp.minimum(refs[i] - cum, HGRN_EXP_CLAMP)) for i in range(nsub)], axis=1)
    scores = lax.dot_general(q_big.astype(MXU_DTYPE), k_big.astype(MXU_DTYPE), (((1,), (1,)), ((), ())),
                             preferred_element_type=F32)
    scores = jnp.where(cum_mat > 0, scores, 0.0)
    o = o + jnp.dot(scores.astype(MXU_DTYPE), v, preferred_element_type=F32)

    kd = (k * jnp.exp(tot - cum)).astype(MXU_DTYPE)
    upd = lax.dot_general(v, kd, (((0,), (0,)), ((), ())), preferred_element_type=F32)
    return o, state * jnp.exp(tot) + upd


def _hgrn_kernel(*refs, rev, n_chunks, fuse):
    if fuse:
        q_ref, f_ref, v_ref, lbp_ref, s0_ref, ob_ref, og_ref, ng_ref, o_ref, sT_ref, state = refs
    else:
        q_ref, f_ref, v_ref, lbp_ref, s0_ref, o_ref, sT_ref, state = refs
    i = pl.program_id(2)

    @pl.when(i == 0)
    def _():
        state[...] = s0_ref[0, 0]

    loglb = lbp_ref[0, 0:1, :]
    log1m = lbp_ref[0, 1:2, :]
    onem = lbp_ref[0, 2:3, :]
    c = HGRN_CHUNK
    r_i = lax.broadcasted_iota(jnp.int32, (c, c), 0)
    c_i = lax.broadcasted_iota(jnp.int32, (c, c), 1)
    cum_mat = ((c_i >= r_i) if rev else (c_i <= r_i)).astype(F32)
    s = state[...]
    order = range(n_chunks - 1, -1, -1) if rev else range(n_chunks)
    for ci in order:
        sl = slice(ci * c, (ci + 1) * c)
        k, logf = _hgrn_gates(f_ref[0, sl, :].astype(F32), loglb, log1m, onem)
        zq = q_ref[0, sl, :].astype(F32)
        q = zq * _sigmoid(zq) * (A_DK ** -0.5)
        o, s = _hgrn_chunk(q, k, v_ref[0, sl, :], logf, s, cum_mat, rev)
        if fuse:
            o = o + ob_ref[0, sl, :]
            o = o * lax.rsqrt(jnp.mean(o * o, axis=-1, keepdims=True) + EPS) * ng_ref[0]
            og = og_ref[0, sl, :].astype(F32)
            o = o * (og * _sigmoid(og))
        o_ref[0, sl, :] = o.astype(o_ref.dtype)
    state[...] = s

    @pl.when(i == pl.num_programs(2) - 1)
    def _():
        sT_ref[0, 0] = s


def hgrn_scan(z, lbp, s0, rev, o_other=None, norm_g=None):
    bsz, n, _ = z.shape
    fuse = o_other is not None
    t = min(256, n)
    nt = n // t
    f_off = (OFF_FB if rev else OFF_FF) // LANES

    def tmap(i):
        return nt - 1 - i if rev else i

    def col(off):
        return pl.BlockSpec((1, t, LANES), lambda b, h, i: (b, tmap(i), off + h))

    in_specs = [col(OFF_QA // LANES), col(f_off), col(OFF_IA // LANES),
                pl.BlockSpec((1, 8, LANES), lambda b, h, i: (h, 0, 0)),
                pl.BlockSpec((1, 1, LANES, LANES), lambda b, h, i: (b, h, 0, 0))]
    args = [z, z, z, lbp, s0]
    if fuse:
        in_specs += [pl.BlockSpec((1, t, LANES), lambda b, h, i: (b, tmap(i), h)),
                     col(OFF_OG // LANES),
                     pl.BlockSpec((1, 1, LANES), lambda b, h, i: (h, 0, 0))]
        args += [o_other, z, norm_g.reshape(A_HEADS, 1, LANES)]
    kern = functools.partial(_hgrn_kernel, rev=rev, n_chunks=t // HGRN_CHUNK, fuse=fuse)
    return pl.pallas_call(
        kern,
        grid=(bsz, A_HEADS, nt),
        in_specs=in_specs,
        out_specs=[pl.BlockSpec((1, t, LANES), lambda b, h, i: (b, tmap(i), h)),
                   pl.BlockSpec((1, 1, LANES, LANES), lambda b, h, i: (b, h, 0, 0))],
        out_shape=[jax.ShapeDtypeStruct((bsz, n, A_WIDTH), MXU_DTYPE if fuse else F32),
                   jax.ShapeDtypeStruct((bsz, A_HEADS, LANES, LANES), F32)],
        scratch_shapes=[pltpu.VMEM((LANES, LANES), F32)],
        compiler_params=_cparams(("parallel", "parallel", "arbitrary")),
        name="hgrn_bwd" if rev else "hgrn_fwd",
    )(*args)


def _swap16(x):
    lane = lax.broadcasted_iota(jnp.int32, x.shape, 1)
    return jnp.where(lane % 32 < 16, pltpu.roll(x, LANES - 16, 1), pltpu.roll(x, 16, 1))


def _head_norm_rope(x, gsum, g, cos, sin):
    ssum = jnp.dot(x * x, gsum, precision=HIGHEST, preferred_element_type=F32)
    y = x * lax.rsqrt(ssum * (1.0 / B_HD) + EPS) * g
    if cos is not None:
        y = y * cos + _swap16(y) * sin
    return y


def _qkprep_kernel(*refs, rope):
    if rope:
        q_ref, k_ref, v_ref, qg_ref, kg_ref, cos_ref, sin_ref, qo_ref, ko_ref, vo_ref = refs
        cos, sin = cos_ref[...], sin_ref[...]
    else:
        q_ref, k_ref, v_ref, qg_ref, kg_ref, qo_ref, ko_ref, vo_ref = refs
        cos = sin = None
    r_i = lax.broadcasted_iota(jnp.int32, (LANES, LANES), 0)
    c_i = lax.broadcasted_iota(jnp.int32, (LANES, LANES), 1)
    gsum = (r_i // B_HD == c_i // B_HD).astype(F32)
    lane = lax.broadcasted_iota(jnp.int32, (q_ref.shape[1], LANES), 1)
    for cb in range(B_WIDTH // LANES):
        x = q_ref[0, :, cb * LANES:(cb + 1) * LANES].astype(F32)
        y = _head_norm_rope(x, gsum, qg_ref[...], cos, sin) * (B_HD ** -0.5)
        swapped = pltpu.roll(y, B_HD, 1)
        for half in range(2):
            head = 2 * cb + half
            kv = head // B_GROUP
            src = y if half == kv else swapped
            keep = (lane // B_HD) == kv
            qo_ref[0, head] = jnp.where(keep, src, 0.0).astype(qo_ref.dtype)
    kx = k_ref[0].astype(F32)
    ko_ref[0] = _head_norm_rope(kx, gsum, kg_ref[...], cos, sin).astype(ko_ref.dtype)
    vo_ref[0] = v_ref[0]


def qk_prep(z, q_norm_g, k_norm_g, cos_t, sin_t):
    bsz, n, _ = z.shape
    rope = cos_t is not None
    tm = min(256, n)
    qg = jnp.tile(q_norm_g.astype(F32), LANES // B_HD).reshape(1, LANES)
    kg = jnp.tile(k_norm_g.astype(F32), LANES // B_HD).reshape(1, LANES)
    in_specs = [pl.BlockSpec((1, tm, B_WIDTH), lambda b, i: (b, i, OFF_QB // B_WIDTH)),
                pl.BlockSpec((1, tm, LANES), lambda b, i: (b, i, OFF_KB // LANES)),
                pl.BlockSpec((1, tm, LANES), lambda b, i: (b, i, OFF_VB // LANES)),
                pl.BlockSpec((1, LANES), lambda b, i: (0, 0)),
                pl.BlockSpec((1, LANES), lambda b, i: (0, 0))]
    args = [z, z, z, qg, kg]
    if rope:
        in_specs += [pl.BlockSpec((tm, LANES), lambda b, i: (i, 0))] * 2
        args += [cos_t, sin_t]
    return pl.pallas_call(
        functools.partial(_qkprep_kernel, rope=rope),
        grid=(bsz, n // tm),
        in_specs=in_specs,
        out_specs=[pl.BlockSpec((1, B_HEADS, tm, LANES), lambda b, i: (b, 0, i, 0)),
                   pl.BlockSpec((1, tm, LANES), lambda b, i: (b, i, 0)),
                   pl.BlockSpec((1, tm, LANES), lambda b, i: (b, i, 0))],
        out_shape=[jax.ShapeDtypeStruct((bsz, B_HEADS, n, LANES), MXU_DTYPE),
                   jax.ShapeDtypeStruct((bsz, n, LANES), MXU_DTYPE),
                   jax.ShapeDtypeStruct((bsz, n, LANES), MXU_DTYPE)],
        compiler_params=_cparams(("parallel", "parallel")),
        name="qk_prep_rope" if rope else "qk_prep",
    )(*args)


def rope_tables(n):
    t = jnp.arange(n)
    r = (t // GRID_W).astype(F32)
    col = (t % GRID_W).astype(F32)
    inv = ROPE_THETA ** (-jnp.arange(0, AXIS_DIM, 2, dtype=F32) / AXIS_DIM)
    ang_r = r[:, None] * inv
    ang_c = col[:, None] * inv
    cos = jnp.concatenate([jnp.cos(ang_r)] * 2 + [jnp.cos(ang_c)] * 2, axis=1)
    sin = jnp.concatenate([-jnp.sin(ang_r), jnp.sin(ang_r), -jnp.sin(ang_c), jnp.sin(ang_c)], axis=1)
    return jnp.tile(cos, (1, 2)), jnp.tile(sin, (1, 2))


def _attn_kernel(q_ref, k_ref, v_ref, o_ref):
    tq = q_ref.shape[2]
    rows = B_GROUP * tq
    lane = lax.broadcasted_iota(jnp.int32, v_ref.shape[1:], 1)
    lane_o = lax.broadcasted_iota(jnp.int32, (tq, LANES), 1)
    ks = k_ref[0]
    for kv in range(B_KV_HEADS):
        q = q_ref[0, kv * B_GROUP:(kv + 1) * B_GROUP].reshape(rows, LANES)
        vs = jnp.where((lane // B_HD) == kv, v_ref[0], jnp.ones_like(v_ref[0]))
        s = lax.dot_general(q, ks, (((1,), (1,)), ((), ())), preferred_element_type=F32)
        p = jnp.exp(s - jnp.max(s, axis=-1, keepdims=True))
        a = jnp.dot(p.astype(MXU_DTYPE), vs, preferred_element_type=F32)
        o_full = a / pltpu.roll(a, B_HD, 1)
        o_swapped = pltpu.roll(o_full, B_HD, 1)
        for pair in range(B_GROUP // 2):
            g0, g1 = 2 * pair, 2 * pair + 1
            lo_src = o_full if kv == 0 else o_swapped
            hi_src = o_swapped if kv == 0 else o_full
            blk = jnp.where(lane_o < B_HD, lo_src[g0 * tq:(g0 + 1) * tq], hi_src[g1 * tq:(g1 + 1) * tq])
            cb = kv * (B_GROUP // 2) + pair
            o_ref[0, :, cb * LANES:(cb + 1) * LANES] = blk.astype(o_ref.dtype)


def attention(q, k, v):
    bsz, _, n, _ = q.shape
    nkeys = k.shape[1]
    tq = min(128, n)
    return pl.pallas_call(
        _attn_kernel,
        grid=(bsz, n // tq),
        in_specs=[pl.BlockSpec((1, B_HEADS, tq, LANES), lambda b, i: (b, 0, i, 0)),
                  pl.BlockSpec((1, nkeys, LANES), lambda b, i: (b, 0, 0)),
                  pl.BlockSpec((1, nkeys, LANES), lambda b, i: (b, 0, 0))],
        out_specs=pl.BlockSpec((1, tq, B_WIDTH), lambda b, i: (b, i, 0)),
        out_shape=jax.ShapeDtypeStruct((bsz, n, B_WIDTH), MXU_DTYPE),
        compiler_params=_cparams(("parallel", "parallel")),
        name="gqa_attention",
    )(q, k, v)


def _pack_rows(x):
    w = x.shape[1] // 2
    bits = lax.bitcast_convert_type(x.astype(jnp.bfloat16).astype(F32), jnp.uint32)
    packed = (bits[:, :w] & jnp.uint32(0xFFFF0000)) | (bits[:, w:] >> 16)
    return lax.bitcast_convert_type(packed, jnp.int32)


def _unpack_rows(p):
    bits = lax.bitcast_convert_type(p, jnp.uint32)
    hi = lax.bitcast_convert_type(bits & jnp.uint32(0xFFFF0000), F32)
    lo = lax.bitcast_convert_type(bits << 16, F32)
    return jnp.concatenate([hi, lo], axis=1)


def _store_chunks(ref, packed):
    for j in range(ref.shape[0]):
        ref[j] = packed[:, j * LANES:(j + 1) * LANES]


def _load_chunks(ref):
    return jnp.concatenate([ref[j] for j in range(ref.shape[0])], axis=1)


def _merge_kernel(*refs, aliased):
    if aliased:
        refs = refs[:13] + refs[15:]
    (h_ref, a_ref, b_ref, zg_ref, wa_ref, wb_ref, wo_ref, g1_ref, n2_ref, sc_ref, sh_ref, wr_ref, br_ref,
     ho_ref, x2_ref, lg_ref) = refs
    d = h_ref.shape[2]
    ua = jnp.dot(a_ref[0], wa_ref[...], preferred_element_type=F32)
    ub = jnp.dot(b_ref[0], wb_ref[...], preferred_element_type=F32)
    ga = _sigmoid(zg_ref[0, :, 0:d].astype(F32))
    gb = _sigmoid(zg_ref[0, :, d:2 * d].astype(F32))
    mix = (ga * ua + gb * ub).astype(MXU_DTYPE)
    h_new = h_ref[0] + g1_ref[0] * jnp.dot(mix, wo_ref[...], preferred_element_type=F32)
    ho_ref[0] = h_new
    x2 = _modulated_norm(h_new, n2_ref[...], sc_ref[0], sh_ref[0])
    _store_chunks(x2_ref, _pack_rows(x2))
    lg_ref[...] = jnp.dot(x2, wr_ref[...], precision=HIGHEST, preferred_element_type=F32) + br_ref[...]


def merge_proj(h, a, b, z, w_up_a, w_up_b, w_out, g1, norm2_g, sc2, sh2, w_router, b_router, t_total, tok_off,
               bufs=None):
    bsz, n, d = h.shape
    tm = min(512, n)
    nt = n // tm
    off = tok_off // tm
    n_chunks = d // (2 * LANES)
    row = lambda w: pl.BlockSpec((1, tm, w), lambda b_, i: (b_, i, 0))
    full = lambda s: pl.BlockSpec(s, lambda b_, i: (0,) * len(s))
    per_b = pl.BlockSpec((1, 1, d), lambda b_, i: (b_, 0, 0))
    in_specs = [row(d), row(A_WIDTH), row(B_WIDTH),
                pl.BlockSpec((1, tm, 2 * d), lambda b_, i: (b_, i, OFF_GATES)),
                full((A_WIDTH, d)), full((B_WIDTH, d)), full((d, d)),
                per_b, full((1, d)), per_b, per_b, full((d, LANES)), full((1, LANES))]
    args = [h, a, b, z, w_up_a, w_up_b, w_out, g1, norm2_g.reshape(1, d), sc2, sh2, w_router, b_router]
    aliases = {}
    if bufs is not None:
        in_specs += [pl.BlockSpec(memory_space=pl.ANY)] * 2
        args += list(bufs)
        aliases = {13: 1, 14: 2}
    return pl.pallas_call(
        functools.partial(_merge_kernel, aliased=bufs is not None),
        grid=(bsz, nt),
        in_specs=in_specs,
        out_specs=[row(d),
                   pl.BlockSpec((n_chunks, tm, LANES), lambda b_, i: (0, off + b_ * nt + i, 0)),
                   pl.BlockSpec((tm, LANES), lambda b_, i: (off + b_ * nt + i, 0))],
        out_shape=[jax.ShapeDtypeStruct((bsz, n, d), F32),
                   jax.ShapeDtypeStruct((n_chunks, t_total, LANES), jnp.int32),
                   jax.ShapeDtypeStruct((t_total, LANES), F32)],
        input_output_aliases=aliases,
        compiler_params=_cparams(("parallel", "parallel")),
        name="merge_proj",
    )(*args)


def _first_lane_where(cond, lane):
    return jnp.min(jnp.where(cond, lane, LANES), axis=-1, keepdims=True)


def _router_kernel(lg_ref, slab_ref, cnt_ref, base):
    @pl.when(pl.program_id(0) == 0)
    def _():
        base[...] = jnp.zeros_like(base)

    lg = lg_ref[...]
    tr = lg.shape[0]
    lane = lax.broadcasted_iota(jnp.int32, lg.shape, 1)
    neg = jnp.float32(-jnp.inf)
    is_g = lane < N_GROUPS
    gl = jnp.where(is_g, lg, neg)
    gmax = jnp.max(gl, axis=-1, keepdims=True)
    gsel = _first_lane_where(gl == gmax, lane)
    p_group = 1.0 / jnp.sum(jnp.where(is_g, jnp.exp(lg - gmax), 0.0), axis=-1, keepdims=True)
    lo = N_GROUPS + gsel * EXPERTS_PER_GROUP
    in_grp = (lane >= lo) & (lane < lo + EXPERTS_PER_GROUP)
    el = jnp.where(in_grp, lg, neg)
    m1 = jnp.max(el, axis=-1, keepdims=True)
    i1 = _first_lane_where(el == m1, lane)
    el2 = jnp.where(lane == i1, neg, el)
    m2 = jnp.max(el2, axis=-1, keepdims=True)
    i2 = _first_lane_where(el2 == m2, lane)
    zsum = jnp.sum(jnp.where(in_grp, jnp.exp(lg - m1), 0.0), axis=-1, keepdims=True)
    p1 = 1.0 / zsum
    p2 = jnp.exp(m2 - m1) / zsum
    w1 = p1 / (p1 + p2) * p_group
    w2 = p2 / (p1 + p2) * p_group
    hit1 = lane == i1
    hit2 = lane == i2
    onehot = (hit1 | hit2).astype(F32)
    r_i = lax.broadcasted_iota(jnp.int32, (tr, tr), 0)
    c_i = lax.broadcasted_iota(jnp.int32, (tr, tr), 1)
    before = (c_i < r_i).astype(MXU_DTYPE)
    prior = jnp.dot(before, onehot.astype(MXU_DTYPE), preferred_element_type=F32) + base[0:1, :]
    r1 = jnp.sum(jnp.where(hit1, prior, 0.0), axis=-1, keepdims=True)
    r2 = jnp.sum(jnp.where(hit2, prior, 0.0), axis=-1, keepdims=True)
    new_base = base[0:1, :] + jnp.sum(onehot, axis=0, keepdims=True)
    base[...] = jnp.broadcast_to(new_base, base.shape)
    cnt_ref[...] = jnp.broadcast_to(new_base, cnt_ref.shape)
    vals = [(i1 - N_GROUPS).astype(F32), (i2 - N_GROUPS).astype(F32), w1, w2, r1, r2]
    slab = jnp.zeros_like(lg)
    for idx, val in enumerate(vals):
        slab = jnp.where(lane == idx, val, slab)
    slab_ref[...] = slab


def route(logits):
    t = logits.shape[0]
    tr = next(c for c in (512, 256, 128, 64, 32, 16, 8) if t % c == 0)
    return pl.pallas_call(
        _router_kernel,
        grid=(t // tr,),
        in_specs=[pl.BlockSpec((tr, LANES), lambda i: (i, 0))],
        out_specs=[pl.BlockSpec((tr, LANES), lambda i: (i, 0)), pl.BlockSpec((8, LANES), lambda i: (0, 0))],
        out_shape=[jax.ShapeDtypeStruct((t, LANES), F32), jax.ShapeDtypeStruct((8, LANES), F32)],
        scratch_shapes=[pltpu.VMEM((8, LANES), F32)],
        compiler_params=_cparams(("arbitrary",)),
        name="moe_route",
    )(logits)


def _expert_kernel(be_ref, nv_ref, x_ref, wg_ref, wu_ref, wd_ref, o_ref):
    j = pl.program_id(0)
    nv = nv_ref[j]

    @pl.when(nv > 0)
    def _():
        x = _unpack_rows(_load_chunks(x_ref))
        rows = lax.broadcasted_iota(jnp.int32, x.shape, 0)
        x = jnp.where(rows < nv, x, 0.0).astype(MXU_DTYPE)
        hg = jnp.dot(x, wg_ref[0], preferred_element_type=F32)
        hu = jnp.dot(x, wu_ref[0], preferred_element_type=F32)
        hb = (hg * _sigmoid(hg) * hu).astype(MXU_DTYPE)
        _store_chunks(o_ref, _pack_rows(jnp.dot(hb, wd_ref[0], preferred_element_type=F32)))

    @pl.when(nv == 0)
    def _():
        o_ref[...] = jnp.zeros_like(o_ref)


def expert_ffn(x_sorted, block_e, nvalid, w_gate, w_up, w_down, tmx):
    n_chunks, r, _ = x_sorted.shape
    _, d, de = w_gate.shape
    grid_spec = pltpu.PrefetchScalarGridSpec(
        num_scalar_prefetch=2,
        grid=(r // tmx,),
        in_specs=[pl.BlockSpec((n_chunks, tmx, LANES), lambda j, be, nv: (0, j, 0)),
                  pl.BlockSpec((1, d, de), lambda j, be, nv: (be[j], 0, 0)),
                  pl.BlockSpec((1, d, de), lambda j, be, nv: (be[j], 0, 0)),
                  pl.BlockSpec((1, de, d), lambda j, be, nv: (be[j], 0, 0))],
        out_specs=pl.BlockSpec((n_chunks, tmx, LANES), lambda j, be, nv: (0, j, 0)),
    )
    return pl.pallas_call(
        _expert_kernel,
        grid_spec=grid_spec,
        out_shape=jax.ShapeDtypeStruct((n_chunks, r, LANES), jnp.int32),
        compiler_params=_cparams(("arbitrary",)),
        name="expert_ffn",
    )(block_e, nvalid, x_sorted, w_gate, w_up, w_down)


def _combine_kernel(*refs, final):
    if final:
        h_ref, y1_ref, y2_ref, slab_ref, g2_ref, fn_ref, o_ref = refs
    else:
        h_ref, y1_ref, y2_ref, slab_ref, g2_ref, o_ref = refs
    w1 = slab_ref[:, 2:3]
    w2 = slab_ref[:, 3:4]
    y = _unpack_rows(_load_chunks(y1_ref)) * w1 + _unpack_rows(_load_chunks(y2_ref)) * w2
    h_new = h_ref[0] + g2_ref[0] * y
    if final:
        ms = jnp.mean(h_new * h_new, axis=-1, keepdims=True)
        h_new = h_new * lax.rsqrt(ms + EPS) * fn_ref[...]
    o_ref[0] = h_new


def moe_combine(h, y1, y2, slab, g2, tok_off, final_g=None):
    bsz, n, d = h.shape
    n_chunks = y1.shape[0]
    tm = min(512, n)
    nt = n // tm
    off = tok_off // tm
    final = final_g is not None
    yspec = pl.BlockSpec((n_chunks, tm, LANES), lambda b, i: (0, off + b * nt + i, 0))
    in_specs = [pl.BlockSpec((1, tm, d), lambda b, i: (b, i, 0)), yspec, yspec,
                pl.BlockSpec((tm, LANES), lambda b, i: (off + b * nt + i, 0)),
                pl.BlockSpec((1, 1, d), lambda b, i: (b, 0, 0))]
    args = [h, y1, y2, slab, g2]
    if final:
        in_specs.append(pl.BlockSpec((1, d), lambda b, i: (0, 0)))
        args.append(final_g.reshape(1, d))
    return pl.pallas_call(
        functools.partial(_combine_kernel, final=final),
        grid=(bsz, nt),
        in_specs=in_specs,
        out_specs=pl.BlockSpec((1, tm, d), lambda b, i: (b, i, 0)),
        out_shape=jax.ShapeDtypeStruct((bsz, n, d), F32),
        compiler_params=_cparams(("parallel", "parallel")),
        name="moe_combine_final" if final else "moe_combine",
    )(*args)


SC_WINDOW = 128


def _sc_mesh():
    return plsc.VectorSubcoreMesh(core_axis_name="core", subcore_axis_name="subcore")


def sc_gather_rows(table, idx):
    n = idx.shape[0]

    @pl.kernel(out_type=jax.ShapeDtypeStruct((n, LANES), table.dtype), mesh=_sc_mesh())
    def gather_kernel(t_hbm, i_hbm, o_hbm):
        def body(i_vmem, o_vmem):
            pltpu.sync_copy(t_hbm.at[i_vmem.at[0]], o_vmem)

        pltpu.emit_pipeline(
            body,
            grid=(n // SC_WINDOW,),
            in_specs=[pl.BlockSpec((1, SC_WINDOW), lambda i: (0, i))],
            out_specs=[pl.BlockSpec((SC_WINDOW, LANES), lambda i: (i, 0))],
            core_axis_name=("core", "subcore"),
            dimension_semantics=(pltpu.PARALLEL,),
        )(i_hbm, o_hbm)

    return gather_kernel(table, idx.reshape(1, n))


def sc_scatter_rows2(rows, idx_a, idx_b, n_out):
    n = rows.shape[0]

    @pl.kernel(out_type=jax.ShapeDtypeStruct((n_out, LANES), rows.dtype), mesh=_sc_mesh())
    def scatter_kernel(r_hbm, ia_hbm, ib_hbm, o_hbm):
        def body(r_vmem, ia_vmem, ib_vmem):
            pltpu.sync_copy(r_vmem, o_hbm.at[ia_vmem.at[0]])
            pltpu.sync_copy(r_vmem, o_hbm.at[ib_vmem.at[0]])

        pltpu.emit_pipeline(
            body,
            grid=(n // SC_WINDOW,),
            in_specs=[pl.BlockSpec((SC_WINDOW, LANES), lambda i: (i, 0)),
                      pl.BlockSpec((1, SC_WINDOW), lambda i: (0, i)),
                      pl.BlockSpec((1, SC_WINDOW), lambda i: (0, i))],
            out_specs=[],
            core_axis_name=("core", "subcore"),
            dimension_semantics=(pltpu.PARALLEL,),
        )(r_hbm, ia_hbm, ib_hbm)

    return scatter_kernel(rows, idx_a.reshape(1, n), idx_b.reshape(1, n))


def _dispatch_plan(slab, counts, tmx, n_blocks):
    cnt = counts[0, N_GROUPS:N_GROUPS + N_EXPERTS].astype(jnp.int32)
    padded = (cnt + tmx - 1) // tmx * tmx
    pad_end = jnp.cumsum(padded)
    pad_start = pad_end - padded
    eid = slab[:, 0:TOP_K].astype(jnp.int32)
    rank = slab[:, 4:4 + TOP_K].astype(jnp.int32)
    dest = pad_start[eid] + rank
    blk_start = jnp.arange(n_blocks, dtype=jnp.int32) * tmx
    block_e = jnp.minimum(jnp.searchsorted(pad_end, blk_start, side="right"), N_EXPERTS - 1).astype(jnp.int32)
    nvalid = jnp.clip(pad_start[block_e] + cnt[block_e] - blk_start, 0, tmx).astype(jnp.int32)
    return dest, block_e, nvalid


def hier_moe(x2, logits, w_gate, w_up, w_down):
    n_chunks, t, _ = x2.shape
    tmx = min(512, t)
    n_blocks = (t * TOP_K + N_EXPERTS * (tmx - 1)) // tmx + 1
    r = n_blocks * tmx
    slab, counts = route(logits)
    dest, block_e, nvalid = _dispatch_plan(slab, counts, tmx, n_blocks)
    chunk_off = (jnp.arange(n_chunks, dtype=jnp.int32) * r)[:, None]
    idx1 = (chunk_off + dest[None, :, 0]).reshape(-1)
    idx2 = (chunk_off + dest[None, :, 1]).reshape(-1)
    x_sorted = sc_scatter_rows2(x2.reshape(n_chunks * t, LANES), idx1, idx2, n_chunks * r)
    y_sorted = expert_ffn(x_sorted.reshape(n_chunks, r, LANES), block_e, nvalid, w_gate, w_up, w_down, tmx)
    y_flat = y_sorted.reshape(n_chunks * r, LANES)
    y1 = sc_gather_rows(y_flat, idx1).reshape(n_chunks, t, LANES)
    y2 = sc_gather_rows(y_flat, idx2).reshape(n_chunks, t, LANES)
    return y1, y2, slab


def _lower_bound_rows(lb_logits):
    lb = jnp.cumsum(jax.nn.softmax(lb_logits.astype(F32), axis=1), axis=1)
    lb = lb - lb[:, :1]
    rows = jnp.stack([jnp.log(lb), jnp.log1p(-lb), 1.0 - lb], axis=2)
    rows = jnp.concatenate([rows, jnp.zeros(rows.shape[:2] + (5, A_WIDTH), F32)], axis=2)
    depth = rows.shape[1]
    return rows.reshape(2, depth, 8, A_HEADS, A_DK).transpose(0, 1, 3, 2, 4)


def _reorder_w_in(w):
    parts, off = [], 0
    for s in (A_WIDTH,) * 5 + (B_WIDTH, B_KV_WIDTH, B_KV_WIDTH, w.shape[1], w.shape[1]):
        parts.append(w[..., off:off + s])
        off += s
    return jnp.concatenate(parts[8:] + parts[:8], axis=-1)


def kernel(x, c, ctx, c_ctx, w_ada, b_ada, norm1_g, norm2_g, w_in, lb_logits, hgrn_norm_g, q_norm_g, k_norm_g,
           w_up_a, w_up_b, w_out, w_router_group, b_router_group, w_router_expert, b_router_expert, w_gate, w_up,
           w_down, final_norm_g):
    bsz, n, d = x.shape
    n_ctx = ctx.shape[1]
    depth = w_in.shape[0]
    mx = MXU_DTYPE

    w_in_r = _reorder_w_in(w_in).astype(mx)
    w_up_a_c, w_up_b_c, w_out_c = w_up_a.astype(mx), w_up_b.astype(mx), w_out.astype(mx)
    w_gate_c, w_up_c, w_down_c = w_gate.astype(mx), w_up.astype(mx), w_down.astype(mx)
    n_r = N_GROUPS + N_EXPERTS
    w_router = jnp.concatenate([w_router_group, w_router_expert, jnp.zeros((depth, d, LANES - n_r), F32)], axis=-1)
    b_router = jnp.concatenate([b_router_group, b_router_expert, jnp.zeros((depth, LANES - n_r), F32)], axis=-1)
    lbp = _lower_bound_rows(lb_logits)
    cos_t, sin_t = rope_tables(n)

    n_rows = -(-(bsz + 1) // 8) * 8
    act = jnp.concatenate([c, c_ctx[None, :], jnp.zeros((n_rows - bsz - 1, d), F32)], axis=0)
    mod = ada_mod(act, w_ada, b_ada)

    h, hc = x, ctx
    zero_state = jnp.zeros((bsz, A_HEADS, LANES, LANES), F32)
    for l in range(depth):
        need_ctx = l < depth - 1
        ml = mod[l, :bsz].reshape(bsz, 1, N_MOD, d)
        mc = jnp.broadcast_to(mod[l, bsz].reshape(1, 1, N_MOD, d), (bsz, 1, N_MOD, d))
        sh1, sc1, g1, sh2, sc2, g2 = (ml[:, :, i] for i in range(N_MOD))
        csh1, csc1, cg1, csh2, csc2, cg2 = (mc[:, :, i] for i in range(N_MOD))

        zl = in_proj(h, norm1_g[l], sc1, sh1, w_in_r[l])
        zc = in_proj(hc, norm1_g[l], csc1, csh1, w_in_r[l])
        ob_c, s_bwd = hgrn_scan(zc, lbp[1, l], zero_state, True)
        if need_ctx:
            a_c, s_fwd = hgrn_scan(zc, lbp[0, l], zero_state, False, ob_c, hgrn_norm_g[l])
        else:
            _, s_fwd = hgrn_scan(zc, lbp[0, l], zero_state, False)
        ob_l, _ = hgrn_scan(zl, lbp[1, l], s_bwd, True)
        a_l, _ = hgrn_scan(zl, lbp[0, l], s_fwd, False, ob_l, hgrn_norm_g[l])

        q_l, k_l, v_l = qk_prep(zl, q_norm_g[l], k_norm_g[l], cos_t, sin_t)
        q_c, k_c, v_c = qk_prep(zc, q_norm_g[l], k_norm_g[l], None, None)
        b_l = attention(q_l, jnp.concatenate([k_c, k_l], axis=1), jnp.concatenate([v_c, v_l], axis=1))

        t_total = bsz * (n + n_ctx) if need_ctx else bsz * n
        bufs = (jnp.zeros((d // (2 * LANES), t_total, LANES), jnp.int32),
                jnp.zeros((t_total, LANES), F32)) if need_ctx else None
        h, x2, lg = merge_proj(h, a_l, b_l, zl, w_up_a_c[l], w_up_b_c[l], w_out_c[l], g1, norm2_g[l], sc2, sh2,
                               w_router[l], b_router[l][None, :], t_total, 0, bufs)
        if need_ctx:
            b_c = attention(q_c, k_c, v_c)
            hc, x2, lg = merge_proj(hc, a_c, b_c, zc, w_up_a_c[l], w_up_b_c[l], w_out_c[l], cg1, norm2_g[l],
                                    csc2, csh2, w_router[l], b_router[l][None, :], t_total, bsz * n, (x2, lg))

        y1, y2, slab = hier_moe(x2, lg, w_gate_c[l], w_up_c[l], w_down_c[l])
        h = moe_combine(h, y1, y2, slab, g2, 0, None if need_ctx else final_norm_g)
        if need_ctx:
            hc = moe_combine(hc, y1, y2, slab, cg2, bsz * n)
    return h
```

```python
import functools

import jax
import jax.numpy as jnp
from jax import lax
from jax.experimental import pallas as pl
from jax.experimental.pallas import tpu as pltpu
from jax.experimental.pallas import tpu_sc as plsc

F32 = jnp.float32
MXU_DTYPE = jnp.bfloat16
HIGHEST = lax.Precision.HIGHEST

EPS = 1e-6
GRID_W = 64
ROPE_THETA = 10000.0
LANES = 128

A_HEADS = 4
A_DK = 128
A_WIDTH = A_HEADS * A_DK
HGRN_CHUNK = 64
HGRN_SUB = 16
HGRN_EXP2_CLAMP = 86.0
B_HEADS = 8
B_KV_HEADS = 2
B_HD = 64
B_GROUP = B_HEADS // B_KV_HEADS
B_WIDTH = B_HEADS * B_HD
B_KV_WIDTH = B_KV_HEADS * B_HD
AXIS_DIM = B_HD // 2
N_GROUPS = 4
EXPERTS_PER_GROUP = 8
N_EXPERTS = N_GROUPS * EXPERTS_PER_GROUP
TOP_K = 2
N_MOD = 6

OFF_GATES = 0
OFF_QA = 2048
OFF_FF = OFF_QA + A_WIDTH
OFF_FB = OFF_FF + A_WIDTH
OFF_IA = OFF_FB + A_WIDTH
OFF_OG = OFF_IA + A_WIDTH
OFF_QB = OFF_OG + A_WIDTH
OFF_KB = OFF_QB + B_WIDTH
OFF_VB = OFF_KB + B_KV_WIDTH
D_IN = OFF_VB + B_KV_WIDTH

VMEM_LIMIT = 52 * 1024 * 1024
MERGE_SUB_ROWS = 128


def _cparams(sem):
    return pltpu.CompilerParams(dimension_semantics=sem, vmem_limit_bytes=VMEM_LIMIT)


LOG2E = 1.4426950408889634


def _sigmoid(x):
    return 0.5 * jnp.tanh(0.5 * x) + 0.5


def _nt_dot(a, b):
    return lax.dot_general(a, b, (((1,), (1,)), ((), ())), preferred_element_type=F32)


def _ada_kernel(a_ref, w_ref, b_ref, o_ref):
    a = a_ref[...]
    a = a * _sigmoid(a)
    o_ref[0] = jnp.dot(a.astype(MXU_DTYPE), w_ref[0].astype(MXU_DTYPE), preferred_element_type=F32) + b_ref[0]


def ada_mod(act, w_ada, b_ada):
    depth, d, n = w_ada.shape
    r = act.shape[0]
    tn = 1024
    return pl.pallas_call(
        _ada_kernel,
        grid=(depth, n // tn),
        in_specs=[
            pl.BlockSpec((r, d), lambda l, j: (0, 0)),
            pl.BlockSpec((1, d, tn), lambda l, j: (l, 0, j)),
            pl.BlockSpec((1, 1, tn), lambda l, j: (l, 0, j)),
        ],
        out_specs=pl.BlockSpec((1, r, tn), lambda l, j: (l, 0, j)),
        out_shape=jax.ShapeDtypeStruct((depth, r, n), F32),
        compiler_params=_cparams(("parallel", "parallel")),
        name="ada_mod",
    )(act, w_ada, b_ada.reshape(depth, 1, n))


def _modulated_norm(x, g, sc, sh):
    ms = jnp.mean(x * x, axis=-1, keepdims=True)
    return (x * lax.rsqrt(ms + EPS) * g) * (1.0 + sc) + sh


def _inproj_kernel(h_ref, g_ref, sc_ref, sh_ref, w_ref, z_ref, *, col_chunk):
    y = _modulated_norm(h_ref[0], g_ref[...], sc_ref[0], sh_ref[0]).astype(MXU_DTYPE)
    for c in range(D_IN // col_chunk):
        cs = slice(c * col_chunk, (c + 1) * col_chunk)
        z_ref[0, :, cs] = jnp.dot(y, w_ref[:, cs], preferred_element_type=F32).astype(z_ref.dtype)


def in_proj(h, norm_g, sc, sh, w_in):
    bsz, n, d = h.shape
    tm = min(512, n)
    kern = functools.partial(_inproj_kernel, col_chunk=768)
    return pl.pallas_call(
        kern,
        grid=(bsz, n // tm),
        in_specs=[
            pl.BlockSpec((1, tm, d), lambda b, i: (b, i, 0)),
            pl.BlockSpec((1, d), lambda b, i: (0, 0)),
            pl.BlockSpec((1, 1, d), lambda b, i: (b, 0, 0)),
            pl.BlockSpec((1, 1, d), lambda b, i: (b, 0, 0)),
            pl.BlockSpec((d, D_IN), lambda b, i: (0, 0)),
        ],
        out_specs=pl.BlockSpec((1, tm, D_IN), lambda b, i: (b, i, 0)),
        out_shape=jax.ShapeDtypeStruct((bsz, n, D_IN), MXU_DTYPE),
        compiler_params=_cparams(("parallel", "parallel")),
        name="in_proj",
    )(h, norm_g.reshape(1, d), sc, sh, w_in)


def _hgrn_gates(zf, loglb, log1m, onem):
    l1pe = jnp.log(1.0 + jnp.exp(-jnp.abs(zf)))
    c = log1m + (jnp.minimum(zf, 0.0) - l1pe)
    logf = jnp.maximum(loglb, c) + jnp.log(1.0 + jnp.exp(-jnp.abs(loglb - c)))
    k = onem * jnp.exp(jnp.minimum(-zf, 0.0) - l1pe)
    return k, logf * LOG2E


def _cumsum_rows(cum_mat, x):
    hi = x.astype(jnp.bfloat16)
    r1 = x - hi.astype(F32)
    mid = r1.astype(jnp.bfloat16)
    lo = (r1 - mid.astype(F32)).astype(jnp.bfloat16)
    parts = jnp.concatenate([hi, mid, lo], axis=1)
    out = jnp.dot(cum_mat, parts, preferred_element_type=F32)
    w = x.shape[1]
    return out[:, :w] + out[:, w:2 * w] + out[:, 2 * w:]


def _hgrn_kernel(*refs, rev, n_chunks, fuse):
    if fuse:
        q_ref, f_ref, v_ref, lbp_ref, s0_ref, ob_ref, og_ref, ng_ref, o_ref, sT_ref, state = refs
    else:
        q_ref, f_ref, v_ref, lbp_ref, s0_ref, o_ref, sT_ref, state = refs
    i = pl.program_id(1)

    @pl.when(i == 0)
    def _():
        state[...] = s0_ref[0]

    loglb = lbp_ref[0:1, :]
    log1m = lbp_ref[1:2, :]
    onem = lbp_ref[2:3, :]
    c = HGRN_CHUNK
    nsub = c // HGRN_SUB
    r_i = lax.broadcasted_iota(jnp.int32, (c, c), 0)
    c_i = lax.broadcasted_iota(jnp.int32, (c, c), 1)
    causal = (c_i >= r_i) if rev else (c_i <= r_i)
    cum_mat = causal.astype(jnp.bfloat16)
    row_blk = lax.broadcasted_iota(jnp.int32, (c, A_DK), 0) // HGRN_SUB
    s = [state[h] for h in range(A_HEADS)]
    order = range(n_chunks - 1, -1, -1) if rev else range(n_chunks)
    for ci in order:
        sl = slice(ci * c, (ci + 1) * c)
        k, logf = _hgrn_gates(f_ref[0, sl, :].astype(F32), loglb, log1m, onem)
        zq = q_ref[0, sl, :].astype(F32)
        q = zq * _sigmoid(zq) * (A_DK ** -0.5)
        v = v_ref[0, sl, :]
        cum = _cumsum_rows(cum_mat, logf)
        tot = cum[0:1] if rev else cum[c - 1:c]
        qd = (q * jnp.exp2(cum)).astype(MXU_DTYPE)
        kd = (k * jnp.exp2(tot - cum)).astype(MXU_DTYPE)
        decay = jnp.exp2(tot)
        refs_j = []
        for j in range(nsub):
            if rev:
                r = cum[(j + 1) * HGRN_SUB:(j + 1) * HGRN_SUB + 1] if j < nsub - 1 else jnp.zeros_like(tot)
            else:
                r = cum[j * HGRN_SUB - 1:j * HGRN_SUB] if j > 0 else jnp.zeros_like(tot)
            refs_j.append(r)
        ref_rows = jnp.concatenate([jnp.broadcast_to(r, (HGRN_SUB, r.shape[1])) for r in refs_j], axis=0)
        qt = (q * jnp.exp2(cum - ref_rows)).astype(MXU_DTYPE)
        kt = [(k * jnp.exp2(jnp.minimum(r - cum, HGRN_EXP2_CLAMP))).astype(MXU_DTYPE) for r in refs_j]
        outs = []
        for h in range(A_HEADS):
            hs = slice(h * A_DK, (h + 1) * A_DK)
            q_big = jnp.concatenate([jnp.where(row_blk == j, qt[:, hs], jnp.zeros_like(qt[:, hs]))
                                     for j in range(nsub)], axis=1)
            k_big = jnp.concatenate([kt[j][:, hs] for j in range(nsub)], axis=1)
            scores = jnp.where(causal, _nt_dot(q_big, k_big), 0.0)
            o_h = _nt_dot(qd[:, hs], s[h].astype(MXU_DTYPE))
            o_h = o_h + jnp.dot(scores.astype(MXU_DTYPE), v[:, hs], preferred_element_type=F32)
            upd = lax.dot_general(v[:, hs], kd[:, hs], (((0,), (0,)), ((), ())), preferred_element_type=F32)
            s[h] = s[h] * decay[:, hs] + upd
            if fuse:
                o_h = o_h + ob_ref[0, sl, hs]
                o_h = o_h * lax.rsqrt(jnp.mean(o_h * o_h, axis=-1, keepdims=True) + EPS)
            outs.append(o_h)
        o = jnp.concatenate(outs, axis=1)
        if fuse:
            og = og_ref[0, sl, :].astype(F32)
            o = o * ng_ref[...] * (og * _sigmoid(og))
        o_ref[0, sl, :] = o.astype(o_ref.dtype)
    for h in range(A_HEADS):
        state[h] = s[h]

    @pl.when(i == pl.num_programs(1) - 1)
    def _():
        for h in range(A_HEADS):
            sT_ref[0, h] = s[h]


def hgrn_scan(z, lbp, s0, rev, o_other=None, norm_g=None):
    bsz, n, _ = z.shape
    fuse = o_other is not None
    t = min(256, n)
    nt = n // t
    f_off = (OFF_FB if rev else OFF_FF) // A_WIDTH

    def tmap(i):
        return nt - 1 - i if rev else i

    def col(blk):
        return pl.BlockSpec((1, t, A_WIDTH), lambda b, i: (b, tmap(i), blk))

    state_spec = pl.BlockSpec((1, A_HEADS, LANES, LANES), lambda b, i: (b, 0, 0, 0))
    in_specs = [col(OFF_QA // A_WIDTH), col(f_off), col(OFF_IA // A_WIDTH),
                pl.BlockSpec((8, A_WIDTH), lambda b, i: (0, 0)), state_spec]
    args = [z, z, z, lbp, s0]
    if fuse:
        in_specs += [col(0), col(OFF_OG // A_WIDTH), pl.BlockSpec((1, A_WIDTH), lambda b, i: (0, 0))]
        args += [o_other, z, norm_g.reshape(1, A_WIDTH)]
    kern = functools.partial(_hgrn_kernel, rev=rev, n_chunks=t // HGRN_CHUNK, fuse=fuse)
    return pl.pallas_call(
        kern,
        grid=(bsz, nt),
        in_specs=in_specs,
        out_specs=[col(0), state_spec],
        out_shape=[jax.ShapeDtypeStruct((bsz, n, A_WIDTH), MXU_DTYPE if fuse else F32),
                   jax.ShapeDtypeStruct((bsz, A_HEADS, LANES, LANES), F32)],
        scratch_shapes=[pltpu.VMEM((A_HEADS, LANES, LANES), F32)],
        compiler_params=_cparams(("parallel", "arbitrary")),
        name="hgrn_bwd" if rev else "hgrn_fwd",
    )(*args)


def _swap16(x):
    lane = lax.broadcasted_iota(jnp.int32, x.shape, 1)
    return jnp.where(lane % 32 < 16, pltpu.roll(x, LANES - 16, 1), pltpu.roll(x, 16, 1))


def _head_norm_rope(x, gsum, g, cos, sin):
    ssum = jnp.dot(x * x, gsum, precision=HIGHEST, preferred_element_type=F32)
    y = x * lax.rsqrt(ssum * (1.0 / B_HD) + EPS) * g
    if cos is not None:
        y = y * cos + _swap16(y) * sin
    return y


def _qkprep_kernel(*refs, rope):
    if rope:
        q_ref, k_ref, v_ref, qg_ref, kg_ref, cos_ref, sin_ref, qo_ref, ko_ref, vo_ref = refs
        cos, sin = cos_ref[...], sin_ref[...]
    else:
        q_ref, k_ref, v_ref, qg_ref, kg_ref, qo_ref, ko_ref, vo_ref = refs
        cos = sin = None
    r_i = lax.broadcasted_iota(jnp.int32, (LANES, LANES), 0)
    c_i = lax.broadcasted_iota(jnp.int32, (LANES, LANES), 1)
    gsum = (r_i // B_HD == c_i // B_HD).astype(F32)
    lane = lax.broadcasted_iota(jnp.int32, (q_ref.shape[1], LANES), 1)
    for cb in range(B_WIDTH // LANES):
        x = q_ref[0, :, cb * LANES:(cb + 1) * LANES].astype(F32)
        y = _head_norm_rope(x, gsum, qg_ref[...], cos, sin) * (B_HD ** -0.5 * LOG2E)
        swapped = pltpu.roll(y, B_HD, 1)
        for half in range(2):
            head = 2 * cb + half
            kv = head // B_GROUP
            src = y if half == kv else swapped
            keep = (lane // B_HD) == kv
            qo_ref[0, head] = jnp.where(keep, src, 0.0).astype(qo_ref.dtype)
    kx = k_ref[0].astype(F32)
    ko_ref[0] = _head_norm_rope(kx, gsum, kg_ref[...], cos, sin).astype(ko_ref.dtype)
    vo_ref[0] = v_ref[0]


def qk_prep(z, q_norm_g, k_norm_g, cos_t, sin_t):
    bsz, n, _ = z.shape
    rope = cos_t is not None
    tm = min(256, n)
    qg = jnp.tile(q_norm_g.astype(F32), LANES // B_HD).reshape(1, LANES)
    kg = jnp.tile(k_norm_g.astype(F32), LANES // B_HD).reshape(1, LANES)
    in_specs = [pl.BlockSpec((1, tm, B_WIDTH), lambda b, i: (b, i, OFF_QB // B_WIDTH)),
                pl.BlockSpec((1, tm, LANES), lambda b, i: (b, i, OFF_KB // LANES)),
                pl.BlockSpec((1, tm, LANES), lambda b, i: (b, i, OFF_VB // LANES)),
                pl.BlockSpec((1, LANES), lambda b, i: (0, 0)),
                pl.BlockSpec((1, LANES), lambda b, i: (0, 0))]
    args = [z, z, z, qg, kg]
    if rope:
        in_specs += [pl.BlockSpec((tm, LANES), lambda b, i: (i, 0))] * 2
        args += [cos_t, sin_t]
    return pl.pallas_call(
        functools.partial(_qkprep_kernel, rope=rope),
        grid=(bsz, n // tm),
        in_specs=in_specs,
        out_specs=[pl.BlockSpec((1, B_HEADS, tm, LANES), lambda b, i: (b, 0, i, 0)),
                   pl.BlockSpec((1, tm, LANES), lambda b, i: (b, i, 0)),
                   pl.BlockSpec((1, tm, LANES), lambda b, i: (b, i, 0))],
        out_shape=[jax.ShapeDtypeStruct((bsz, B_HEADS, n, LANES), MXU_DTYPE),
                   jax.ShapeDtypeStruct((bsz, n, LANES), MXU_DTYPE),
                   jax.ShapeDtypeStruct((bsz, n, LANES), MXU_DTYPE)],
        compiler_params=_cparams(("parallel", "parallel")),
        name="qk_prep_rope" if rope else "qk_prep",
    )(*args)


def rope_tables(n):
    t = jnp.arange(n)
    r = (t // GRID_W).astype(F32)
    col = (t % GRID_W).astype(F32)
    inv = ROPE_THETA ** (-jnp.arange(0, AXIS_DIM, 2, dtype=F32) / AXIS_DIM)
    ang_r = r[:, None] * inv
    ang_c = col[:, None] * inv
    cos = jnp.concatenate([jnp.cos(ang_r)] * 2 + [jnp.cos(ang_c)] * 2, axis=1)
    sin = jnp.concatenate([-jnp.sin(ang_r), jnp.sin(ang_r), -jnp.sin(ang_c), jnp.sin(ang_c)], axis=1)
    return jnp.tile(cos, (1, 2)), jnp.tile(sin, (1, 2))


def _attn_kernel(q_ref, k_ref, v_ref, o_ref):
    tq = q_ref.shape[2]
    rows = B_GROUP * tq
    lane = lax.broadcasted_iota(jnp.int32, v_ref.shape[1:], 1)
    lane_o = lax.broadcasted_iota(jnp.int32, (tq, LANES), 1)
    s_all = _nt_dot(q_ref[0].reshape(B_HEADS * tq, LANES), k_ref[0])
    for kv in range(B_KV_HEADS):
        s = s_all[kv * rows:(kv + 1) * rows]
        vs = jnp.where((lane // B_HD) == kv, v_ref[0], jnp.ones_like(v_ref[0]))
        p = jnp.exp2(s - jnp.max(s, axis=-1, keepdims=True))
        a = jnp.dot(p.astype(MXU_DTYPE), vs, preferred_element_type=F32)
        o_full = a / pltpu.roll(a, B_HD, 1)
        o_swapped = pltpu.roll(o_full, B_HD, 1)
        for pair in range(B_GROUP // 2):
            g0, g1 = 2 * pair, 2 * pair + 1
            lo_src = o_full if kv == 0 else o_swapped
            hi_src = o_swapped if kv == 0 else o_full
            blk = jnp.where(lane_o < B_HD, lo_src[g0 * tq:(g0 + 1) * tq], hi_src[g1 * tq:(g1 + 1) * tq])
            cb = kv * (B_GROUP // 2) + pair
            o_ref[0, :, cb * LANES:(cb + 1) * LANES] = blk.astype(o_ref.dtype)


def attention(q, k, v):
    bsz, _, n, _ = q.shape
    nkeys = k.shape[1]
    tq = min(128, n)
    return pl.pallas_call(
        _attn_kernel,
        grid=(bsz, n // tq),
        in_specs=[pl.BlockSpec((1, B_HEADS, tq, LANES), lambda b, i: (b, 0, i, 0)),
                  pl.BlockSpec((1, nkeys, LANES), lambda b, i: (b, 0, 0)),
                  pl.BlockSpec((1, nkeys, LANES), lambda b, i: (b, 0, 0))],
        out_specs=pl.BlockSpec((1, tq, B_WIDTH), lambda b, i: (b, i, 0)),
        out_shape=jax.ShapeDtypeStruct((bsz, n, B_WIDTH), MXU_DTYPE),
        compiler_params=_cparams(("parallel", "parallel")),
        name="gqa_attention",
    )(q, k, v)


def _pack_rows(x):
    w = x.shape[1] // 2
    bits = lax.bitcast_convert_type(x.astype(jnp.bfloat16).astype(F32), jnp.uint32)
    packed = (bits[:, :w] & jnp.uint32(0xFFFF0000)) | (bits[:, w:] >> 16)
    return lax.bitcast_convert_type(packed, jnp.int32)


def _unpack_rows(p):
    bits = lax.bitcast_convert_type(p, jnp.uint32)
    hi = lax.bitcast_convert_type(bits & jnp.uint32(0xFFFF0000), F32)
    lo = lax.bitcast_convert_type(bits << 16, F32)
    return jnp.concatenate([hi, lo], axis=1)


def _store_chunks(ref, packed):
    for j in range(ref.shape[0]):
        ref[j] = packed[:, j * LANES:(j + 1) * LANES]


def _load_chunks(ref):
    return jnp.concatenate([ref[j] for j in range(ref.shape[0])], axis=1)


def _merge_kernel(*refs, aliased):
    if aliased:
        refs = refs[:13] + refs[15:]
    (h_ref, a_ref, b_ref, zg_ref, wa_ref, wb_ref, wo_ref, g1_ref, n2_ref, sc_ref, sh_ref, wr_ref, br_ref,
     ho_ref, x2_ref, lg_ref) = refs
    d = h_ref.shape[2]
    tm = h_ref.shape[1]
    sub = min(MERGE_SUB_ROWS, tm)
    for r in range(tm // sub):
        rs = slice(r * sub, (r + 1) * sub)
        ua = jnp.dot(a_ref[0, rs, :], wa_ref[...], preferred_element_type=F32)
        ub = jnp.dot(b_ref[0, rs, :], wb_ref[...], preferred_element_type=F32)
        ga = _sigmoid(zg_ref[0, rs, 0:d].astype(F32))
        gb = _sigmoid(zg_ref[0, rs, d:2 * d].astype(F32))
        mix = (ga * ua + gb * ub).astype(MXU_DTYPE)
        h_new = h_ref[0, rs, :] + g1_ref[0] * jnp.dot(mix, wo_ref[...], preferred_element_type=F32)
        ho_ref[0, rs, :] = h_new
        x2 = _modulated_norm(h_new, n2_ref[...], sc_ref[0], sh_ref[0])
        packed = _pack_rows(x2)
        for j in range(x2_ref.shape[0]):
            x2_ref[j, rs, :] = packed[:, j * LANES:(j + 1) * LANES]
        x_hi = x2.astype(jnp.bfloat16)
        x_lo = (x2 - x_hi.astype(F32)).astype(jnp.bfloat16)
        lg = jnp.dot(x_hi, wr_ref[0], preferred_element_type=F32)
        lg = lg + jnp.dot(x_lo, wr_ref[0], preferred_element_type=F32)
        lg = lg + jnp.dot(x_hi, wr_ref[1], preferred_element_type=F32)
        lg_ref[rs, :] = lg + br_ref[...]


def merge_proj(h, a, b, z, w_up_a, w_up_b, w_out, g1, norm2_g, sc2, sh2, w_router, b_router, t_total, tok_off,
               bufs=None):
    bsz, n, d = h.shape
    tm = min(512, n)
    nt = n // tm
    off = tok_off // tm
    n_chunks = d // (2 * LANES)
    row = lambda w: pl.BlockSpec((1, tm, w), lambda b_, i: (b_, i, 0))
    full = lambda s: pl.BlockSpec(s, lambda b_, i: (0,) * len(s))
    per_b = pl.BlockSpec((1, 1, d), lambda b_, i: (b_, 0, 0))
    in_specs = [row(d), row(A_WIDTH), row(B_WIDTH),
                pl.BlockSpec((1, tm, 2 * d), lambda b_, i: (b_, i, OFF_GATES)),
                full((A_WIDTH, d)), full((B_WIDTH, d)), full((d, d)),
                per_b, full((1, d)), per_b, per_b, full((2, d, LANES)), full((1, LANES))]
    args = [h, a, b, z, w_up_a, w_up_b, w_out, g1, norm2_g.reshape(1, d), sc2, sh2, w_router, b_router]
    aliases = {}
    if bufs is not None:
        in_specs += [pl.BlockSpec(memory_space=pl.ANY)] * 2
        args += list(bufs)
        aliases = {13: 1, 14: 2}
    return pl.pallas_call(
        functools.partial(_merge_kernel, aliased=bufs is not None),
        grid=(bsz, nt),
        in_specs=in_specs,
        out_specs=[row(d),
                   pl.BlockSpec((n_chunks, tm, LANES), lambda b_, i: (0, off + b_ * nt + i, 0)),
                   pl.BlockSpec((tm, LANES), lambda b_, i: (off + b_ * nt + i, 0))],
        out_shape=[jax.ShapeDtypeStruct((bsz, n, d), F32),
                   jax.ShapeDtypeStruct((n_chunks, t_total, LANES), jnp.int32),
                   jax.ShapeDtypeStruct((t_total, LANES), F32)],
        input_output_aliases=aliases,
        compiler_params=_cparams(("parallel", "parallel")),
        name="merge_proj",
    )(*args)


def _first_lane_where(cond, lane):
    return jnp.min(jnp.where(cond, lane, LANES), axis=-1, keepdims=True)


def _router_kernel(lg_ref, slab_ref, cnt_ref, base):
    @pl.when(pl.program_id(0) == 0)
    def _():
        base[...] = jnp.zeros_like(base)

    lg = lg_ref[...]
    tr = lg.shape[0]
    lane = lax.broadcasted_iota(jnp.int32, lg.shape, 1)
    neg = jnp.float32(-jnp.inf)
    is_g = lane < N_GROUPS
    gl = jnp.where(is_g, lg, neg)
    gmax = jnp.max(gl, axis=-1, keepdims=True)
    gsel = _first_lane_where(gl == gmax, lane)
    p_group = 1.0 / jnp.sum(jnp.where(is_g, jnp.exp(lg - gmax), 0.0), axis=-1, keepdims=True)
    lo = N_GROUPS + gsel * EXPERTS_PER_GROUP
    in_grp = (lane >= lo) & (lane < lo + EXPERTS_PER_GROUP)
    el = jnp.where(in_grp, lg, neg)
    m1 = jnp.max(el, axis=-1, keepdims=True)
    i1 = _first_lane_where(el == m1, lane)
    el2 = jnp.where(lane == i1, neg, el)
    m2 = jnp.max(el2, axis=-1, keepdims=True)
    i2 = _first_lane_where(el2 == m2, lane)
    zsum = jnp.sum(jnp.where(in_grp, jnp.exp(lg - m1), 0.0), axis=-1, keepdims=True)
    p1 = 1.0 / zsum
    p2 = jnp.exp(m2 - m1) / zsum
    w1 = p1 / (p1 + p2) * p_group
    w2 = p2 / (p1 + p2) * p_group
    hit1 = lane == i1
    hit2 = lane == i2
    onehot = (hit1 | hit2).astype(F32)
    r_i = lax.broadcasted_iota(jnp.int32, (tr, tr), 0)
    c_i = lax.broadcasted_iota(jnp.int32, (tr, tr), 1)
    before = (c_i < r_i).astype(MXU_DTYPE)
    prior = jnp.dot(before, onehot.astype(MXU_DTYPE), preferred_element_type=F32) + base[0:1, :]
    r1 = jnp.sum(jnp.where(hit1, prior, 0.0), axis=-1, keepdims=True)
    r2 = jnp.sum(jnp.where(hit2, prior, 0.0), axis=-1, keepdims=True)
    new_base = base[0:1, :] + jnp.sum(onehot, axis=0, keepdims=True)
    base[...] = jnp.broadcast_to(new_base, base.shape)
    cnt_ref[...] = jnp.broadcast_to(new_base, cnt_ref.shape)
    vals = [(i1 - N_GROUPS).astype(F32), (i2 - N_GROUPS).astype(F32), w1, w2, r1, r2]
    slab = jnp.zeros_like(lg)
    for idx, val in enumerate(vals):
        slab = jnp.where(lane == idx, val, slab)
    slab_ref[...] = slab


def route(logits):
    t = logits.shape[0]
    tr = next(c for c in (512, 256, 128, 64, 32, 16, 8) if t % c == 0)
    return pl.pallas_call(
        _router_kernel,
        grid=(t // tr,),
        in_specs=[pl.BlockSpec((tr, LANES), lambda i: (i, 0))],
        out_specs=[pl.BlockSpec((tr, LANES), lambda i: (i, 0)), pl.BlockSpec((8, LANES), lambda i: (0, 0))],
        out_shape=[jax.ShapeDtypeStruct((t, LANES), F32), jax.ShapeDtypeStruct((8, LANES), F32)],
        scratch_shapes=[pltpu.VMEM((8, LANES), F32)],
        compiler_params=_cparams(("arbitrary",)),
        name="moe_route",
    )(logits)


def _expert_kernel(be_ref, nv_ref, x_ref, wg_ref, wu_ref, wd_ref, o_ref):
    j = pl.program_id(0)
    nv = nv_ref[j]

    @pl.when(nv > 0)
    def _():
        x = _unpack_rows(_load_chunks(x_ref))
        rows = lax.broadcasted_iota(jnp.int32, x.shape, 0)
        x = jnp.where(rows < nv, x, 0.0).astype(MXU_DTYPE)
        hg = jnp.dot(x, wg_ref[0], preferred_element_type=F32)
        hu = jnp.dot(x, wu_ref[0], preferred_element_type=F32)
        hb = (hg * _sigmoid(hg) * hu).astype(MXU_DTYPE)
        _store_chunks(o_ref, _pack_rows(jnp.dot(hb, wd_ref[0], preferred_element_type=F32)))

    @pl.when(nv == 0)
    def _():
        o_ref[...] = jnp.zeros_like(o_ref)


def expert_ffn(x_sorted, block_e, nvalid, w_gate, w_up, w_down, tmx):
    n_chunks, r, _ = x_sorted.shape
    _, d, de = w_gate.shape
    grid_spec = pltpu.PrefetchScalarGridSpec(
        num_scalar_prefetch=2,
        grid=(r // tmx,),
        in_specs=[pl.BlockSpec((n_chunks, tmx, LANES), lambda j, be, nv: (0, j, 0)),
                  pl.BlockSpec((1, d, de), lambda j, be, nv: (be[j], 0, 0)),
                  pl.BlockSpec((1, d, de), lambda j, be, nv: (be[j], 0, 0)),
                  pl.BlockSpec((1, de, d), lambda j, be, nv: (be[j], 0, 0))],
        out_specs=pl.BlockSpec((n_chunks, tmx, LANES), lambda j, be, nv: (0, j, 0)),
    )
    return pl.pallas_call(
        _expert_kernel,
        grid_spec=grid_spec,
        out_shape=jax.ShapeDtypeStruct((n_chunks, r, LANES), jnp.int32),
        compiler_params=_cparams(("arbitrary",)),
        name="expert_ffn",
    )(block_e, nvalid, x_sorted, w_gate, w_up, w_down)


def _combine_kernel(*refs, final):
    if final:
        h_ref, y1_ref, y2_ref, slab_ref, g2_ref, fn_ref, o_ref = refs
    else:
        h_ref, y1_ref, y2_ref, slab_ref, g2_ref, o_ref = refs
    w1 = slab_ref[:, 2:3]
    w2 = slab_ref[:, 3:4]
    y = _unpack_rows(_load_chunks(y1_ref)) * w1 + _unpack_rows(_load_chunks(y2_ref)) * w2
    h_new = h_ref[0] + g2_ref[0] * y
    if final:
        ms = jnp.mean(h_new * h_new, axis=-1, keepdims=True)
        h_new = h_new * lax.rsqrt(ms + EPS) * fn_ref[...]
    o_ref[0] = h_new


def moe_combine(h, y1, y2, slab, g2, tok_off, final_g=None):
    bsz, n, d = h.shape
    n_chunks = y1.shape[0]
    tm = min(512, n)
    nt = n // tm
    off = tok_off // tm
    final = final_g is not None
    yspec = pl.BlockSpec((n_chunks, tm, LANES), lambda b, i: (0, off + b * nt + i, 0))
    in_specs = [pl.BlockSpec((1, tm, d), lambda b, i: (b, i, 0)), yspec, yspec,
                pl.BlockSpec((tm, LANES), lambda b, i: (off + b * nt + i, 0)),
                pl.BlockSpec((1, 1, d), lambda b, i: (b, 0, 0))]
    args = [h, y1, y2, slab, g2]
    if final:
        in_specs.append(pl.BlockSpec((1, d), lambda b, i: (0, 0)))
        args.append(final_g.reshape(1, d))
    return pl.pallas_call(
        functools.partial(_combine_kernel, final=final),
        grid=(bsz, nt),
        in_specs=in_specs,
        out_specs=pl.BlockSpec((1, tm, d), lambda b, i: (b, i, 0)),
        out_shape=jax.ShapeDtypeStruct((bsz, n, d), F32),
        compiler_params=_cparams(("parallel", "parallel")),
        name="moe_combine_final" if final else "moe_combine",
    )(*args)


SC_WINDOW = 128


def _sc_mesh():
    return plsc.VectorSubcoreMesh(core_axis_name="core", subcore_axis_name="subcore")


def sc_gather_rows(table, idx):
    n = idx.shape[0]

    @pl.kernel(out_type=jax.ShapeDtypeStruct((n, LANES), table.dtype), mesh=_sc_mesh())
    def gather_kernel(t_hbm, i_hbm, o_hbm):
        def body(i_vmem, o_vmem):
            pltpu.sync_copy(t_hbm.at[i_vmem.at[0]], o_vmem)

        pltpu.emit_pipeline(
            body,
            grid=(n // SC_WINDOW,),
            in_specs=[pl.BlockSpec((1, SC_WINDOW), lambda i: (0, i))],
            out_specs=[pl.BlockSpec((SC_WINDOW, LANES), lambda i: (i, 0))],
            core_axis_name=("core", "subcore"),
            dimension_semantics=(pltpu.PARALLEL,),
        )(i_hbm, o_hbm)

    return gather_kernel(table, idx.reshape(1, n))


def sc_scatter_rows2(rows, idx_a, idx_b, n_out):
    n = rows.shape[0]

    @pl.kernel(out_type=jax.ShapeDtypeStruct((n_out, LANES), rows.dtype), mesh=_sc_mesh())
    def scatter_kernel(r_hbm, ia_hbm, ib_hbm, o_hbm):
        def body(r_vmem, ia_vmem, ib_vmem):
            pltpu.sync_copy(r_vmem, o_hbm.at[ia_vmem.at[0]])
            pltpu.sync_copy(r_vmem, o_hbm.at[ib_vmem.at[0]])

        pltpu.emit_pipeline(
            body,
            grid=(n // SC_WINDOW,),
            in_specs=[pl.BlockSpec((SC_WINDOW, LANES), lambda i: (i, 0)),
                      pl.BlockSpec((1, SC_WINDOW), lambda i: (0, i)),
                      pl.BlockSpec((1, SC_WINDOW), lambda i: (0, i))],
            out_specs=[],
            core_axis_name=("core", "subcore"),
            dimension_semantics=(pltpu.PARALLEL,),
        )(r_hbm, ia_hbm, ib_hbm)

    return scatter_kernel(rows, idx_a.reshape(1, n), idx_b.reshape(1, n))


def _dispatch_plan(slab, counts, tmx, n_blocks):
    cnt = counts[0, N_GROUPS:N_GROUPS + N_EXPERTS].astype(jnp.int32)
    padded = (cnt + tmx - 1) // tmx * tmx
    pad_end = jnp.cumsum(padded)
    pad_start = pad_end - padded
    eid = slab[:, 0:TOP_K].astype(jnp.int32)
    rank = slab[:, 4:4 + TOP_K].astype(jnp.int32)
    dest = pad_start[eid] + rank
    blk_start = jnp.arange(n_blocks, dtype=jnp.int32) * tmx
    block_e = jnp.minimum(jnp.searchsorted(pad_end, blk_start, side="right"), N_EXPERTS - 1).astype(jnp.int32)
    nvalid = jnp.clip(pad_start[block_e] + cnt[block_e] - blk_start, 0, tmx).astype(jnp.int32)
    return dest, block_e, nvalid


def hier_moe(x2, logits, w_gate, w_up, w_down):
    n_chunks, t, _ = x2.shape
    tmx = min(512, t)
    n_blocks = (t * TOP_K + N_EXPERTS * (tmx - 1)) // tmx + 1
    r = n_blocks * tmx
    slab, counts = route(logits)
    dest, block_e, nvalid = _dispatch_plan(slab, counts, tmx, n_blocks)
    chunk_off = (jnp.arange(n_chunks, dtype=jnp.int32) * r)[:, None]
    idx1 = (chunk_off + dest[None, :, 0]).reshape(-1)
    idx2 = (chunk_off + dest[None, :, 1]).reshape(-1)
    x_sorted = sc_scatter_rows2(x2.reshape(n_chunks * t, LANES), idx1, idx2, n_chunks * r)
    y_sorted = expert_ffn(x_sorted.reshape(n_chunks, r, LANES), block_e, nvalid, w_gate, w_up, w_down, tmx)
    y_flat = y_sorted.reshape(n_chunks * r, LANES)
    y1 = sc_gather_rows(y_flat, idx1).reshape(n_chunks, t, LANES)
    y2 = sc_gather_rows(y_flat, idx2).reshape(n_chunks, t, LANES)
    return y1, y2, slab


def _lower_bound_rows(lb_logits):
    lb = jnp.cumsum(jax.nn.softmax(lb_logits.astype(F32), axis=1), axis=1)
    lb = lb - lb[:, :1]
    rows = jnp.stack([jnp.log(lb), jnp.log1p(-lb), 1.0 - lb], axis=2)
    rows = jnp.concatenate([rows, jnp.zeros(rows.shape[:2] + (5, A_WIDTH), F32)], axis=2)
    return rows


def _reorder_w_in(w):
    parts, off = [], 0
    for s in (A_WIDTH,) * 5 + (B_WIDTH, B_KV_WIDTH, B_KV_WIDTH, w.shape[1], w.shape[1]):
        parts.append(w[..., off:off + s])
        off += s
    return jnp.concatenate(parts[8:] + parts[:8], axis=-1)


def kernel(x, c, ctx, c_ctx, w_ada, b_ada, norm1_g, norm2_g, w_in, lb_logits, hgrn_norm_g, q_norm_g, k_norm_g,
           w_up_a, w_up_b, w_out, w_router_group, b_router_group, w_router_expert, b_router_expert, w_gate, w_up,
           w_down, final_norm_g):
    bsz, n, d = x.shape
    n_ctx = ctx.shape[1]
    depth = w_in.shape[0]
    mx = MXU_DTYPE

    w_in_r = _reorder_w_in(w_in).astype(mx)
    w_up_a_c, w_up_b_c, w_out_c = w_up_a.astype(mx), w_up_b.astype(mx), w_out.astype(mx)
    w_gate_c, w_up_c, w_down_c = w_gate.astype(mx), w_up.astype(mx), w_down.astype(mx)
    n_r = N_GROUPS + N_EXPERTS
    w_router = jnp.concatenate([w_router_group, w_router_expert, jnp.zeros((depth, d, LANES - n_r), F32)], axis=-1)
    w_router_hi = w_router.astype(jnp.bfloat16)
    w_router_lo = (w_router - w_router_hi.astype(F32)).astype(jnp.bfloat16)
    w_router = jnp.stack([w_router_hi, w_router_lo], axis=1)
    b_router = jnp.concatenate([b_router_group, b_router_expert, jnp.zeros((depth, LANES - n_r), F32)], axis=-1)
    lbp = _lower_bound_rows(lb_logits)
    cos_t, sin_t = rope_tables(n)

    n_rows = -(-(bsz + 1) // 8) * 8
    act = jnp.concatenate([c, c_ctx[None, :], jnp.zeros((n_rows - bsz - 1, d), F32)], axis=0)
    mod = ada_mod(act, w_ada, b_ada)

    h, hc = x, ctx
    zero_state = jnp.zeros((bsz, A_HEADS, LANES, LANES), F32)
    for l in range(depth):
        need_ctx = l < depth - 1
        ml = mod[l, :bsz].reshape(bsz, 1, N_MOD, d)
        mc = jnp.broadcast_to(mod[l, bsz].reshape(1, 1, N_MOD, d), (bsz, 1, N_MOD, d))
        sh1, sc1, g1, sh2, sc2, g2 = (ml[:, :, i] for i in range(N_MOD))
        csh1, csc1, cg1, csh2, csc2, cg2 = (mc[:, :, i] for i in range(N_MOD))

        zl = in_proj(h, norm1_g[l], sc1, sh1, w_in_r[l])
        zc = in_proj(hc, norm1_g[l], csc1, csh1, w_in_r[l])
        ob_c, s_bwd = hgrn_scan(zc, lbp[1, l], zero_state, True)
        if need_ctx:
            a_c, s_fwd = hgrn_scan(zc, lbp[0, l], zero_state, False, ob_c, hgrn_norm_g[l])
        else:
            _, s_fwd = hgrn_scan(zc, lbp[0, l], zero_state, False)
        ob_l, _ = hgrn_scan(zl, lbp[1, l], s_bwd, True)
        a_l, _ = hgrn_scan(zl, lbp[0, l], s_fwd, False, ob_l, hgrn_norm_g[l])

        q_l, k_l, v_l = qk_prep(zl, q_norm_g[l], k_norm_g[l], cos_t, sin_t)
        q_c, k_c, v_c = qk_prep(zc, q_norm_g[l], k_norm_g[l], None, None)
        b_l = attention(q_l, jnp.concatenate([k_c, k_l], axis=1), jnp.concatenate([v_c, v_l], axis=1))

        t_total = bsz * (n + n_ctx) if need_ctx else bsz * n
        bufs = (jnp.zeros((d // (2 * LANES), t_total, LANES), jnp.int32),
                jnp.zeros((t_total, LANES), F32)) if need_ctx else None
        h, x2, lg = merge_proj(h, a_l, b_l, zl, w_up_a_c[l], w_up_b_c[l], w_out_c[l], g1, norm2_g[l], sc2, sh2,
                               w_router[l], b_router[l][None, :], t_total, 0, bufs)
        if need_ctx:
            b_c = attention(q_c, k_c, v_c)
            hc, x2, lg = merge_proj(hc, a_c, b_c, zc, w_up_a_c[l], w_up_b_c[l], w_out_c[l], cg1, norm2_g[l],
                                    csc2, csh2, w_router[l], b_router[l][None, :], t_total, bsz * n, (x2, lg))

        y1, y2, slab = hier_moe(x2, lg, w_gate_c[l], w_up_c[l], w_down_c[l])
        h = moe_combine(h, y1, y2, slab, g2, 0, None if need_ctx else final_norm_g)
        if need_ctx:
            hc = moe_combine(hc, y1, y2, slab, cg2, bsz * n)
    return h
```

```python
import functools

import jax
import jax.numpy as jnp
from jax import lax
from jax.experimental import pallas as pl
from jax.experimental.pallas import tpu as pltpu
from jax.experimental.pallas import tpu_sc as plsc

F32 = jnp.float32
MXU_DTYPE = jnp.bfloat16
HIGHEST = lax.Precision.HIGHEST

EPS = 1e-6
GRID_W = 64
ROPE_THETA = 10000.0
LANES = 128

A_HEADS = 4
A_DK = 128
A_WIDTH = A_HEADS * A_DK
HGRN_CHUNK = 64
HGRN_SUB = 16
HGRN_STEP_TOKENS = 1024
HGRN_EXP2_CLAMP = 86.0
B_HEADS = 8
B_KV_HEADS = 2
B_HD = 64
B_GROUP = B_HEADS // B_KV_HEADS
B_WIDTH = B_HEADS * B_HD
B_KV_WIDTH = B_KV_HEADS * B_HD
AXIS_DIM = B_HD // 2
N_GROUPS = 4
EXPERTS_PER_GROUP = 8
N_EXPERTS = N_GROUPS * EXPERTS_PER_GROUP
TOP_K = 2
N_MOD = 6

OFF_GATES = 0
OFF_QA = 2048
OFF_FF = OFF_QA + A_WIDTH
OFF_FB = OFF_FF + A_WIDTH
OFF_IA = OFF_FB + A_WIDTH
OFF_OG = OFF_IA + A_WIDTH
OFF_QB = OFF_OG + A_WIDTH
OFF_KB = OFF_QB + B_WIDTH
OFF_VB = OFF_KB + B_KV_WIDTH
D_IN = OFF_VB + B_KV_WIDTH

VMEM_LIMIT = 52 * 1024 * 1024
MERGE_SUB_ROWS = 128


def _cparams(sem):
    return pltpu.CompilerParams(dimension_semantics=sem, vmem_limit_bytes=VMEM_LIMIT)


LOG2E = 1.4426950408889634


def _sigmoid(x):
    return 0.5 * jnp.tanh(0.5 * x) + 0.5


def _nt_dot(a, b):
    return lax.dot_general(a, b, (((1,), (1,)), ((), ())), preferred_element_type=F32)


def _ada_kernel(a_ref, w_ref, b_ref, o_ref):
    a = a_ref[...]
    a = a * _sigmoid(a)
    o_ref[0] = jnp.dot(a.astype(MXU_DTYPE), w_ref[0].astype(MXU_DTYPE), preferred_element_type=F32) + b_ref[0]


def ada_mod(act, w_ada, b_ada):
    depth, d, n = w_ada.shape
    r = act.shape[0]
    tn = 1024
    return pl.pallas_call(
        _ada_kernel,
        grid=(depth, n // tn),
        in_specs=[
            pl.BlockSpec((r, d), lambda l, j: (0, 0)),
            pl.BlockSpec((1, d, tn), lambda l, j: (l, 0, j)),
            pl.BlockSpec((1, 1, tn), lambda l, j: (l, 0, j)),
        ],
        out_specs=pl.BlockSpec((1, r, tn), lambda l, j: (l, 0, j)),
        out_shape=jax.ShapeDtypeStruct((depth, r, n), F32),
        compiler_params=_cparams(("parallel", "parallel")),
        name="ada_mod",
    )(act, w_ada, b_ada.reshape(depth, 1, n))


def _modulated_norm(x, g, sc, sh):
    ms = jnp.mean(x * x, axis=-1, keepdims=True)
    return (x * lax.rsqrt(ms + EPS) * g) * (1.0 + sc) + sh


def _inproj_kernel(h_ref, g_ref, sc_ref, sh_ref, w_ref, z_ref, *, col_chunk):
    y = _modulated_norm(h_ref[0], g_ref[...], sc_ref[0], sh_ref[0]).astype(MXU_DTYPE)
    for c in range(D_IN // col_chunk):
        cs = slice(c * col_chunk, (c + 1) * col_chunk)
        z_ref[0, :, cs] = jnp.dot(y, w_ref[:, cs], preferred_element_type=F32).astype(z_ref.dtype)


def in_proj(h, norm_g, sc, sh, w_in):
    bsz, n, d = h.shape
    tm = min(512, n)
    kern = functools.partial(_inproj_kernel, col_chunk=768)
    return pl.pallas_call(
        kern,
        grid=(bsz, n // tm),
        in_specs=[
            pl.BlockSpec((1, tm, d), lambda b, i: (b, i, 0)),
            pl.BlockSpec((1, d), lambda b, i: (0, 0)),
            pl.BlockSpec((1, 1, d), lambda b, i: (b, 0, 0)),
            pl.BlockSpec((1, 1, d), lambda b, i: (b, 0, 0)),
            pl.BlockSpec((d, D_IN), lambda b, i: (0, 0)),
        ],
        out_specs=pl.BlockSpec((1, tm, D_IN), lambda b, i: (b, i, 0)),
        out_shape=jax.ShapeDtypeStruct((bsz, n, D_IN), MXU_DTYPE),
        compiler_params=_cparams(("parallel", "parallel")),
        name="in_proj",
    )(h, norm_g.reshape(1, d), sc, sh, w_in)


def _hgrn_gates(zf, loglb, log1m, onem):
    l1pe = jnp.log(1.0 + jnp.exp(-jnp.abs(zf)))
    c = log1m + (jnp.minimum(zf, 0.0) - l1pe)
    logf = jnp.maximum(loglb, c) + jnp.log(1.0 + jnp.exp(-jnp.abs(loglb - c)))
    k = onem * jnp.exp(jnp.minimum(-zf, 0.0) - l1pe)
    return k, logf * LOG2E


def _cumsum_rows(cum_mat, x):
    hi = x.astype(jnp.bfloat16)
    r1 = x - hi.astype(F32)
    mid = r1.astype(jnp.bfloat16)
    lo = (r1 - mid.astype(F32)).astype(jnp.bfloat16)
    parts = jnp.concatenate([hi, mid, lo], axis=1)
    out = jnp.dot(cum_mat, parts, preferred_element_type=F32)
    w = x.shape[1]
    return out[:, :w] + out[:, w:2 * w] + out[:, 2 * w:]


def _hgrn_kernel(*refs, rev, n_chunks, fuse):
    if fuse:
        q_ref, f_ref, v_ref, lbp_ref, s0_ref, ob_ref, og_ref, ng_ref, o_ref, sT_ref, state = refs
    else:
        q_ref, f_ref, v_ref, lbp_ref, s0_ref, o_ref, sT_ref, state = refs
    i = pl.program_id(1)

    @pl.when(i == 0)
    def _():
        state[...] = s0_ref[0]

    loglb = lbp_ref[0:1, :]
    log1m = lbp_ref[1:2, :]
    onem = lbp_ref[2:3, :]
    c = HGRN_CHUNK
    nsub = c // HGRN_SUB
    r_i = lax.broadcasted_iota(jnp.int32, (c, c), 0)
    c_i = lax.broadcasted_iota(jnp.int32, (c, c), 1)
    causal = (c_i >= r_i) if rev else (c_i <= r_i)
    cum_mat = causal.astype(jnp.bfloat16)
    row_blk = lax.broadcasted_iota(jnp.int32, (c, A_DK), 0) // HGRN_SUB
    s = [state[h] for h in range(A_HEADS)]
    order = list(range(n_chunks - 1, -1, -1) if rev else range(n_chunks))

    def prepare(ci):
        sl = slice(ci * c, (ci + 1) * c)
        k, logf = _hgrn_gates(f_ref[0, sl, :].astype(F32), loglb, log1m, onem)
        zq = q_ref[0, sl, :].astype(F32)
        q = zq * _sigmoid(zq) * (A_DK ** -0.5)
        cum = _cumsum_rows(cum_mat, logf)
        tot = cum[0:1] if rev else cum[c - 1:c]
        qd = (q * jnp.exp2(cum)).astype(MXU_DTYPE)
        kd = (k * jnp.exp2(tot - cum)).astype(MXU_DTYPE)
        refs_j = []
        for j in range(nsub):
            if rev:
                r = cum[(j + 1) * HGRN_SUB:(j + 1) * HGRN_SUB + 1] if j < nsub - 1 else jnp.zeros_like(tot)
            else:
                r = cum[j * HGRN_SUB - 1:j * HGRN_SUB] if j > 0 else jnp.zeros_like(tot)
            refs_j.append(r)
        ref_rows = jnp.concatenate([jnp.broadcast_to(r, (HGRN_SUB, r.shape[1])) for r in refs_j], axis=0)
        qt = (q * jnp.exp2(cum - ref_rows)).astype(MXU_DTYPE)
        kt = [(k * jnp.exp2(jnp.minimum(r - cum, HGRN_EXP2_CLAMP))).astype(MXU_DTYPE) for r in refs_j]
        return sl, qd, kd, jnp.exp2(tot), qt, kt

    ready = prepare(order[0])
    for n, ci in enumerate(order):
        sl, qd, kd, decay, qt, kt = ready
        if n + 1 < len(order):
            ready = prepare(order[n + 1])
        v = v_ref[0, sl, :]
        outs = []
        for h in range(A_HEADS):
            hs = slice(h * A_DK, (h + 1) * A_DK)
            q_big = jnp.concatenate([jnp.where(row_blk == j, qt[:, hs], jnp.zeros_like(qt[:, hs]))
                                     for j in range(nsub)], axis=1)
            k_big = jnp.concatenate([kt[j][:, hs] for j in range(nsub)], axis=1)
            scores = jnp.where(causal, _nt_dot(q_big, k_big), 0.0)
            o_h = _nt_dot(qd[:, hs], s[h].astype(MXU_DTYPE))
            o_h = o_h + jnp.dot(scores.astype(MXU_DTYPE), v[:, hs], preferred_element_type=F32)
            upd = lax.dot_general(v[:, hs], kd[:, hs], (((0,), (0,)), ((), ())), preferred_element_type=F32)
            s[h] = s[h] * decay[:, hs] + upd
            if fuse:
                o_h = o_h + ob_ref[0, sl, hs]
                o_h = o_h * lax.rsqrt(jnp.mean(o_h * o_h, axis=-1, keepdims=True) + EPS)
            outs.append(o_h)
        o = jnp.concatenate(outs, axis=1)
        if fuse:
            og = og_ref[0, sl, :].astype(F32)
            o = o * ng_ref[...] * (og * _sigmoid(og))
        o_ref[0, sl, :] = o.astype(o_ref.dtype)
    for h in range(A_HEADS):
        state[h] = s[h]

    @pl.when(i == pl.num_programs(1) - 1)
    def _():
        for h in range(A_HEADS):
            sT_ref[0, h] = s[h]


def hgrn_scan(z, lbp, s0, rev, o_other=None, norm_g=None):
    bsz, n, _ = z.shape
    fuse = o_other is not None
    t = min(HGRN_STEP_TOKENS, n)
    nt = n // t
    f_off = (OFF_FB if rev else OFF_FF) // A_WIDTH

    def tmap(i):
        return nt - 1 - i if rev else i

    def col(blk):
        return pl.BlockSpec((1, t, A_WIDTH), lambda b, i: (b, tmap(i), blk))

    state_spec = pl.BlockSpec((1, A_HEADS, LANES, LANES), lambda b, i: (b, 0, 0, 0))
    in_specs = [col(OFF_QA // A_WIDTH), col(f_off), col(OFF_IA // A_WIDTH),
                pl.BlockSpec((8, A_WIDTH), lambda b, i: (0, 0)), state_spec]
    args = [z, z, z, lbp, s0]
    if fuse:
        in_specs += [col(0), col(OFF_OG // A_WIDTH), pl.BlockSpec((1, A_WIDTH), lambda b, i: (0, 0))]
        args += [o_other, z, norm_g.reshape(1, A_WIDTH)]
    kern = functools.partial(_hgrn_kernel, rev=rev, n_chunks=t // HGRN_CHUNK, fuse=fuse)
    return pl.pallas_call(
        kern,
        grid=(bsz, nt),
        in_specs=in_specs,
        out_specs=[col(0), state_spec],
        out_shape=[jax.ShapeDtypeStruct((bsz, n, A_WIDTH), MXU_DTYPE if fuse else F32),
                   jax.ShapeDtypeStruct((bsz, A_HEADS, LANES, LANES), F32)],
        scratch_shapes=[pltpu.VMEM((A_HEADS, LANES, LANES), F32)],
        compiler_params=_cparams(("parallel", "arbitrary")),
        name="hgrn_bwd" if rev else "hgrn_fwd",
    )(*args)


def _swap16(x):
    lane = lax.broadcasted_iota(jnp.int32, x.shape, 1)
    return jnp.where(lane % 32 < 16, pltpu.roll(x, LANES - 16, 1), pltpu.roll(x, 16, 1))


def _head_norm_rope(x, gsum, g, cos, sin):
    ssum = jnp.dot(x * x, gsum, precision=HIGHEST, preferred_element_type=F32)
    y = x * lax.rsqrt(ssum * (1.0 / B_HD) + EPS) * g
    if cos is not None:
        y = y * cos + _swap16(y) * sin
    return y


def _qkprep_kernel(*refs, rope):
    if rope:
        q_ref, k_ref, v_ref, qg_ref, kg_ref, cos_ref, sin_ref, qo_ref, ko_ref, vo_ref = refs
        cos, sin = cos_ref[...], sin_ref[...]
    else:
        q_ref, k_ref, v_ref, qg_ref, kg_ref, qo_ref, ko_ref, vo_ref = refs
        cos = sin = None
    r_i = lax.broadcasted_iota(jnp.int32, (LANES, LANES), 0)
    c_i = lax.broadcasted_iota(jnp.int32, (LANES, LANES), 1)
    gsum = (r_i // B_HD == c_i // B_HD).astype(F32)
    lane = lax.broadcasted_iota(jnp.int32, (q_ref.shape[1], LANES), 1)
    for cb in range(B_WIDTH // LANES):
        x = q_ref[0, :, cb * LANES:(cb + 1) * LANES].astype(F32)
        y = _head_norm_rope(x, gsum, qg_ref[...], cos, sin) * (B_HD ** -0.5 * LOG2E)
        swapped = pltpu.roll(y, B_HD, 1)
        for half in range(2):
            head = 2 * cb + half
            kv = head // B_GROUP
            src = y if half == kv else swapped
            keep = (lane // B_HD) == kv
            qo_ref[0, head] = jnp.where(keep, src, 0.0).astype(qo_ref.dtype)
    kx = k_ref[0].astype(F32)
    ko_ref[0] = _head_norm_rope(kx, gsum, kg_ref[...], cos, sin).astype(ko_ref.dtype)
    vo_ref[0] = v_ref[0]


def qk_prep(z, q_norm_g, k_norm_g, cos_t, sin_t):
    bsz, n, _ = z.shape
    rope = cos_t is not None
    tm = min(256, n)
    qg = jnp.tile(q_norm_g.astype(F32), LANES // B_HD).reshape(1, LANES)
    kg = jnp.tile(k_norm_g.astype(F32), LANES // B_HD).reshape(1, LANES)
    in_specs = [pl.BlockSpec((1, tm, B_WIDTH), lambda b, i: (b, i, OFF_QB // B_WIDTH)),
                pl.BlockSpec((1, tm, LANES), lambda b, i: (b, i, OFF_KB // LANES)),
                pl.BlockSpec((1, tm, LANES), lambda b, i: (b, i, OFF_VB // LANES)),
                pl.BlockSpec((1, LANES), lambda b, i: (0, 0)),
                pl.BlockSpec((1, LANES), lambda b, i: (0, 0))]
    args = [z, z, z, qg, kg]
    if rope:
        in_specs += [pl.BlockSpec((tm, LANES), lambda b, i: (i, 0))] * 2
        args += [cos_t, sin_t]
    return pl.pallas_call(
        functools.partial(_qkprep_kernel, rope=rope),
        grid=(bsz, n // tm),
        in_specs=in_specs,
        out_specs=[pl.BlockSpec((1, B_HEADS, tm, LANES), lambda b, i: (b, 0, i, 0)),
                   pl.BlockSpec((1, tm, LANES), lambda b, i: (b, i, 0)),
                   pl.BlockSpec((1, tm, LANES), lambda b, i: (b, i, 0))],
        out_shape=[jax.ShapeDtypeStruct((bsz, B_HEADS, n, LANES), MXU_DTYPE),
                   jax.ShapeDtypeStruct((bsz, n, LANES), MXU_DTYPE),
                   jax.ShapeDtypeStruct((bsz, n, LANES), MXU_DTYPE)],
        compiler_params=_cparams(("parallel", "parallel")),
        name="qk_prep_rope" if rope else "qk_prep",
    )(*args)


def rope_tables(n):
    t = jnp.arange(n)
    r = (t // GRID_W).astype(F32)
    col = (t % GRID_W).astype(F32)
    inv = ROPE_THETA ** (-jnp.arange(0, AXIS_DIM, 2, dtype=F32) / AXIS_DIM)
    ang_r = r[:, None] * inv
    ang_c = col[:, None] * inv
    cos = jnp.concatenate([jnp.cos(ang_r)] * 2 + [jnp.cos(ang_c)] * 2, axis=1)
    sin = jnp.concatenate([-jnp.sin(ang_r), jnp.sin(ang_r), -jnp.sin(ang_c), jnp.sin(ang_c)], axis=1)
    return jnp.tile(cos, (1, 2)), jnp.tile(sin, (1, 2))


def _attn_kernel(q_ref, k_ref, v_ref, o_ref):
    tq = q_ref.shape[2]
    rows = B_GROUP * tq
    lane = lax.broadcasted_iota(jnp.int32, v_ref.shape[1:], 1)
    lane_o = lax.broadcasted_iota(jnp.int32, (tq, LANES), 1)
    s_all = _nt_dot(q_ref[0].reshape(B_HEADS * tq, LANES), k_ref[0])
    for kv in range(B_KV_HEADS):
        s = s_all[kv * rows:(kv + 1) * rows]
        vs = jnp.where((lane // B_HD) == kv, v_ref[0], jnp.ones_like(v_ref[0]))
        p = jnp.exp2(s - jnp.max(s, axis=-1, keepdims=True))
        a = jnp.dot(p.astype(MXU_DTYPE), vs, preferred_element_type=F32)
        o_full = a / pltpu.roll(a, B_HD, 1)
        o_swapped = pltpu.roll(o_full, B_HD, 1)
        for pair in range(B_GROUP // 2):
            g0, g1 = 2 * pair, 2 * pair + 1
            lo_src = o_full if kv == 0 else o_swapped
            hi_src = o_swapped if kv == 0 else o_full
            blk = jnp.where(lane_o < B_HD, lo_src[g0 * tq:(g0 + 1) * tq], hi_src[g1 * tq:(g1 + 1) * tq])
            cb = kv * (B_GROUP // 2) + pair
            o_ref[0, :, cb * LANES:(cb + 1) * LANES] = blk.astype(o_ref.dtype)


def attention(q, k, v):
    bsz, _, n, _ = q.shape
    nkeys = k.shape[1]
    tq = min(128, n)
    return pl.pallas_call(
        _attn_kernel,
        grid=(bsz, n // tq),
        in_specs=[pl.BlockSpec((1, B_HEADS, tq, LANES), lambda b, i: (b, 0, i, 0)),
                  pl.BlockSpec((1, nkeys, LANES), lambda b, i: (b, 0, 0)),
                  pl.BlockSpec((1, nkeys, LANES), lambda b, i: (b, 0, 0))],
        out_specs=pl.BlockSpec((1, tq, B_WIDTH), lambda b, i: (b, i, 0)),
        out_shape=jax.ShapeDtypeStruct((bsz, n, B_WIDTH), MXU_DTYPE),
        compiler_params=_cparams(("parallel", "parallel")),
        name="gqa_attention",
    )(q, k, v)


def _pack_rows(x):
    w = x.shape[1] // 2
    bits = lax.bitcast_convert_type(x.astype(jnp.bfloat16).astype(F32), jnp.uint32)
    packed = (bits[:, :w] & jnp.uint32(0xFFFF0000)) | (bits[:, w:] >> 16)
    return lax.bitcast_convert_type(packed, jnp.int32)


def _unpack_rows(p):
    bits = lax.bitcast_convert_type(p, jnp.uint32)
    hi = lax.bitcast_convert_type(bits & jnp.uint32(0xFFFF0000), F32)
    lo = lax.bitcast_convert_type(bits << 16, F32)
    return jnp.concatenate([hi, lo], axis=1)


def _store_chunks(ref, packed):
    for j in range(ref.shape[0]):
        ref[j] = packed[:, j * LANES:(j + 1) * LANES]


def _load_chunks(ref):
    return jnp.concatenate([ref[j] for j in range(ref.shape[0])], axis=1)


def _merge_kernel(*refs, aliased):
    if aliased:
        refs = refs[:13] + refs[15:]
    (h_ref, a_ref, b_ref, zg_ref, wa_ref, wb_ref, wo_ref, g1_ref, n2_ref, sc_ref, sh_ref, wr_ref, br_ref,
     ho_ref, x2_ref, lg_ref) = refs
    d = h_ref.shape[2]
    tm = h_ref.shape[1]
    sub = min(MERGE_SUB_ROWS, tm)
    for r in range(tm // sub):
        rs = slice(r * sub, (r + 1) * sub)
        ua = jnp.dot(a_ref[0, rs, :], wa_ref[...], preferred_element_type=F32)
        ub = jnp.dot(b_ref[0, rs, :], wb_ref[...], preferred_element_type=F32)
        ga = _sigmoid(zg_ref[0, rs, 0:d].astype(F32))
        gb = _sigmoid(zg_ref[0, rs, d:2 * d].astype(F32))
        mix = (ga * ua + gb * ub).astype(MXU_DTYPE)
        h_new = h_ref[0, rs, :] + g1_ref[0] * jnp.dot(mix, wo_ref[...], preferred_element_type=F32)
        ho_ref[0, rs, :] = h_new
        x2 = _modulated_norm(h_new, n2_ref[...], sc_ref[0], sh_ref[0])
        packed = _pack_rows(x2)
        for j in range(x2_ref.shape[0]):
            x2_ref[j, rs, :] = packed[:, j * LANES:(j + 1) * LANES]
        x_hi = x2.astype(jnp.bfloat16)
        x_lo = (x2 - x_hi.astype(F32)).astype(jnp.bfloat16)
        lg = jnp.dot(x_hi, wr_ref[0], preferred_element_type=F32)
        lg = lg + jnp.dot(x_lo, wr_ref[0], preferred_element_type=F32)
        lg = lg + jnp.dot(x_hi, wr_ref[1], preferred_element_type=F32)
        lg_ref[rs, :] = lg + br_ref[...]


def merge_proj(h, a, b, z, w_up_a, w_up_b, w_out, g1, norm2_g, sc2, sh2, w_router, b_router, t_total, tok_off,
               bufs=None):
    bsz, n, d = h.shape
    tm = min(512, n)
    nt = n // tm
    off = tok_off // tm
    n_chunks = d // (2 * LANES)
    row = lambda w: pl.BlockSpec((1, tm, w), lambda b_, i: (b_, i, 0))
    full = lambda s: pl.BlockSpec(s, lambda b_, i: (0,) * len(s))
    per_b = pl.BlockSpec((1, 1, d), lambda b_, i: (b_, 0, 0))
    in_specs = [row(d), row(A_WIDTH), row(B_WIDTH),
                pl.BlockSpec((1, tm, 2 * d), lambda b_, i: (b_, i, OFF_GATES)),
                full((A_WIDTH, d)), full((B_WIDTH, d)), full((d, d)),
                per_b, full((1, d)), per_b, per_b, full((2, d, LANES)), full((1, LANES))]
    args = [h, a, b, z, w_up_a, w_up_b, w_out, g1, norm2_g.reshape(1, d), sc2, sh2, w_router, b_router]
    aliases = {}
    if bufs is not None:
        in_specs += [pl.BlockSpec(memory_space=pl.ANY)] * 2
        args += list(bufs)
        aliases = {13: 1, 14: 2}
    return pl.pallas_call(
        functools.partial(_merge_kernel, aliased=bufs is not None),
        grid=(bsz, nt),
        in_specs=in_specs,
        out_specs=[row(d),
                   pl.BlockSpec((n_chunks, tm, LANES), lambda b_, i: (0, off + b_ * nt + i, 0)),
                   pl.BlockSpec((tm, LANES), lambda b_, i: (off + b_ * nt + i, 0))],
        out_shape=[jax.ShapeDtypeStruct((bsz, n, d), F32),
                   jax.ShapeDtypeStruct((n_chunks, t_total, LANES), jnp.int32),
                   jax.ShapeDtypeStruct((t_total, LANES), F32)],
        input_output_aliases=aliases,
        compiler_params=_cparams(("parallel", "parallel")),
        name="merge_proj",
    )(*args)


def _first_lane_where(cond, lane):
    return jnp.min(jnp.where(cond, lane, LANES), axis=-1, keepdims=True)


def _router_kernel(lg_ref, slab_ref, slab_t_ref, cnt_ref, base):
    @pl.when(pl.program_id(0) == 0)
    def _():
        base[...] = jnp.zeros_like(base)

    lg = lg_ref[...]
    tr = lg.shape[0]
    lane = lax.broadcasted_iota(jnp.int32, lg.shape, 1)
    neg = jnp.float32(-jnp.inf)
    is_g = lane < N_GROUPS
    gl = jnp.where(is_g, lg, neg)
    gmax = jnp.max(gl, axis=-1, keepdims=True)
    gsel = _first_lane_where(gl == gmax, lane)
    p_group = 1.0 / jnp.sum(jnp.where(is_g, jnp.exp(lg - gmax), 0.0), axis=-1, keepdims=True)
    lo = N_GROUPS + gsel * EXPERTS_PER_GROUP
    in_grp = (lane >= lo) & (lane < lo + EXPERTS_PER_GROUP)
    el = jnp.where(in_grp, lg, neg)
    m1 = jnp.max(el, axis=-1, keepdims=True)
    i1 = _first_lane_where(el == m1, lane)
    el2 = jnp.where(lane == i1, neg, el)
    m2 = jnp.max(el2, axis=-1, keepdims=True)
    i2 = _first_lane_where(el2 == m2, lane)
    zsum = jnp.sum(jnp.where(in_grp, jnp.exp(lg - m1), 0.0), axis=-1, keepdims=True)
    p1 = 1.0 / zsum
    p2 = jnp.exp(m2 - m1) / zsum
    w1 = p1 / (p1 + p2) * p_group
    w2 = p2 / (p1 + p2) * p_group
    hit1 = lane == i1
    hit2 = lane == i2
    onehot = (hit1 | hit2).astype(F32)
    r_i = lax.broadcasted_iota(jnp.int32, (tr, tr), 0)
    c_i = lax.broadcasted_iota(jnp.int32, (tr, tr), 1)
    before = (c_i < r_i).astype(MXU_DTYPE)
    prior = jnp.dot(before, onehot.astype(MXU_DTYPE), preferred_element_type=F32) + base[0:1, :]
    r1 = jnp.sum(jnp.where(hit1, prior, 0.0), axis=-1, keepdims=True)
    r2 = jnp.sum(jnp.where(hit2, prior, 0.0), axis=-1, keepdims=True)
    new_base = base[0:1, :] + jnp.sum(onehot, axis=0, keepdims=True)
    base[...] = jnp.broadcast_to(new_base, base.shape)
    cnt_ref[...] = jnp.broadcast_to(new_base, cnt_ref.shape)
    vals = [(i1 - N_GROUPS).astype(F32), (i2 - N_GROUPS).astype(F32), w1, w2, r1, r2]
    slab = jnp.zeros_like(lg)
    for idx, val in enumerate(vals):
        slab = jnp.where(lane == idx, val, slab)
    slab_ref[...] = slab
    slab_t_ref[...] = slab.T[0:8, :]


def route(logits):
    t = logits.shape[0]
    tr = next(c for c in (512, 256, 128) if t % c == 0)
    return pl.pallas_call(
        _router_kernel,
        grid=(t // tr,),
        in_specs=[pl.BlockSpec((tr, LANES), lambda i: (i, 0))],
        out_specs=[pl.BlockSpec((tr, LANES), lambda i: (i, 0)), pl.BlockSpec((8, tr), lambda i: (0, i)),
                   pl.BlockSpec((8, LANES), lambda i: (0, 0))],
        out_shape=[jax.ShapeDtypeStruct((t, LANES), F32), jax.ShapeDtypeStruct((8, t), F32),
                   jax.ShapeDtypeStruct((8, LANES), F32)],
        scratch_shapes=[pltpu.VMEM((8, LANES), F32)],
        compiler_params=_cparams(("arbitrary",)),
        name="moe_route",
    )(logits)


def _expert_kernel(be_ref, nv_ref, x_ref, wg_ref, wu_ref, wd_ref, o_ref, wg_s, wu_s, wd_s):
    j = pl.program_id(0)
    nv = nv_ref[j]

    @pl.when((j == 0) | (be_ref[j] != be_ref[jnp.maximum(j - 1, 0)]))
    def _():
        wg_s[...] = wg_ref[0].astype(wg_s.dtype)
        wu_s[...] = wu_ref[0].astype(wu_s.dtype)
        wd_s[...] = wd_ref[0].astype(wd_s.dtype)

    @pl.when(nv > 0)
    def _():
        x = _unpack_rows(_load_chunks(x_ref))
        rows = lax.broadcasted_iota(jnp.int32, x.shape, 0)
        x = jnp.where(rows < nv, x, 0.0).astype(MXU_DTYPE)
        hg = jnp.dot(x, wg_s[...], preferred_element_type=F32)
        hu = jnp.dot(x, wu_s[...], preferred_element_type=F32)
        hb = (hg * _sigmoid(hg) * hu).astype(MXU_DTYPE)
        _store_chunks(o_ref, _pack_rows(jnp.dot(hb, wd_s[...], preferred_element_type=F32)))

    @pl.when(nv == 0)
    def _():
        o_ref[...] = jnp.zeros_like(o_ref)


def expert_ffn(x_sorted, block_e, nvalid, w_gate, w_up, w_down, tmx):
    n_chunks, r, _ = x_sorted.shape
    _, d, de = w_gate.shape
    grid_spec = pltpu.PrefetchScalarGridSpec(
        num_scalar_prefetch=2,
        grid=(r // tmx,),
        in_specs=[pl.BlockSpec((n_chunks, tmx, LANES), lambda j, be, nv: (0, j, 0)),
                  pl.BlockSpec((1, d, de), lambda j, be, nv: (be[j], 0, 0)),
                  pl.BlockSpec((1, d, de), lambda j, be, nv: (be[j], 0, 0)),
                  pl.BlockSpec((1, de, d), lambda j, be, nv: (be[j], 0, 0))],
        out_specs=pl.BlockSpec((n_chunks, tmx, LANES), lambda j, be, nv: (0, j, 0)),
        scratch_shapes=[pltpu.VMEM((d, de), MXU_DTYPE), pltpu.VMEM((d, de), MXU_DTYPE),
                        pltpu.VMEM((de, d), MXU_DTYPE)],
    )
    return pl.pallas_call(
        _expert_kernel,
        grid_spec=grid_spec,
        out_shape=jax.ShapeDtypeStruct((n_chunks, r, LANES), jnp.int32),
        compiler_params=_cparams(("arbitrary",)),
        name="expert_ffn",
    )(block_e, nvalid, x_sorted, w_gate, w_up, w_down)


def _combine_kernel(*refs, final):
    if final:
        h_ref, y1_ref, y2_ref, slab_ref, g2_ref, fn_ref, o_ref = refs
    else:
        h_ref, y1_ref, y2_ref, slab_ref, g2_ref, o_ref = refs
    w1 = slab_ref[:, 2:3]
    w2 = slab_ref[:, 3:4]
    y = _unpack_rows(_load_chunks(y1_ref)) * w1 + _unpack_rows(_load_chunks(y2_ref)) * w2
    h_new = h_ref[0] + g2_ref[0] * y
    if final:
        ms = jnp.mean(h_new * h_new, axis=-1, keepdims=True)
        h_new = h_new * lax.rsqrt(ms + EPS) * fn_ref[...]
    o_ref[0] = h_new


def moe_combine(h, y1, y2, slab, g2, tok_off, final_g=None):
    bsz, n, d = h.shape
    n_chunks = y1.shape[0]
    tm = min(512, n)
    nt = n // tm
    off = tok_off // tm
    final = final_g is not None
    yspec = pl.BlockSpec((n_chunks, tm, LANES), lambda b, i: (0, off + b * nt + i, 0))
    in_specs = [pl.BlockSpec((1, tm, d), lambda b, i: (b, i, 0)), yspec, yspec,
                pl.BlockSpec((tm, LANES), lambda b, i: (off + b * nt + i, 0)),
                pl.BlockSpec((1, 1, d), lambda b, i: (b, 0, 0))]
    args = [h, y1, y2, slab, g2]
    if final:
        in_specs.append(pl.BlockSpec((1, d), lambda b, i: (0, 0)))
        args.append(final_g.reshape(1, d))
    return pl.pallas_call(
        functools.partial(_combine_kernel, final=final),
        grid=(bsz, nt),
        in_specs=in_specs,
        out_specs=pl.BlockSpec((1, tm, d), lambda b, i: (b, i, 0)),
        out_shape=jax.ShapeDtypeStruct((bsz, n, d), F32),
        compiler_params=_cparams(("parallel", "parallel")),
        name="moe_combine_final" if final else "moe_combine",
    )(*args)


SC_WINDOW = 128


def _sc_mesh():
    return plsc.VectorSubcoreMesh(core_axis_name="core", subcore_axis_name="subcore")


def sc_gather_rows(table, idx):
    n = idx.shape[0]

    @pl.kernel(out_type=jax.ShapeDtypeStruct((n, LANES), table.dtype), mesh=_sc_mesh())
    def gather_kernel(t_hbm, i_hbm, o_hbm):
        def body(i_vmem, o_vmem):
            pltpu.sync_copy(t_hbm.at[i_vmem.at[0]], o_vmem)

        pltpu.emit_pipeline(
            body,
            grid=(n // SC_WINDOW,),
            in_specs=[pl.BlockSpec((1, SC_WINDOW), lambda i: (0, i))],
            out_specs=[pl.BlockSpec((SC_WINDOW, LANES), lambda i: (i, 0))],
            core_axis_name=("core", "subcore"),
            dimension_semantics=(pltpu.PARALLEL,),
            trace_scopes=False,
        )(i_hbm, o_hbm)

    return gather_kernel(table, idx.reshape(1, n))


def sc_scatter_rows2(rows, idx_a, idx_b, n_out):
    n = rows.shape[0]

    @pl.kernel(out_type=jax.ShapeDtypeStruct((n_out, LANES), rows.dtype), mesh=_sc_mesh())
    def scatter_kernel(r_hbm, ia_hbm, ib_hbm, o_hbm):
        def body(r_vmem, ia_vmem, ib_vmem):
            pltpu.sync_copy(r_vmem, o_hbm.at[ia_vmem.at[0]])
            pltpu.sync_copy(r_vmem, o_hbm.at[ib_vmem.at[0]])

        pltpu.emit_pipeline(
            body,
            grid=(n // SC_WINDOW,),
            in_specs=[pl.BlockSpec((SC_WINDOW, LANES), lambda i: (i, 0)),
                      pl.BlockSpec((1, SC_WINDOW), lambda i: (0, i)),
                      pl.BlockSpec((1, SC_WINDOW), lambda i: (0, i))],
            out_specs=[],
            core_axis_name=("core", "subcore"),
            dimension_semantics=(pltpu.PARALLEL,),
            trace_scopes=False,
        )(r_hbm, ia_hbm, ib_hbm)

    return scatter_kernel(rows, idx_a.reshape(1, n), idx_b.reshape(1, n))


def _plan_kernel(ps_ref, st_ref, ia_ref, ib_ref, *, n_rows):
    e = st_ref[0:2, :].astype(jnp.int32)
    dest = st_ref[4:6, :].astype(jnp.int32)
    for ex in range(N_EXPERTS):
        dest = dest + jnp.where(e == ex, ps_ref[ex], 0)
    for k, ref in enumerate((ia_ref, ib_ref)):
        for j in range(ref.shape[0]):
            ref[j:j + 1, :] = dest[k:k + 1, :] + j * n_rows


def dispatch_indices(slab_t, pad_start, n_chunks, n_rows):
    t = slab_t.shape[1]
    tp = next(c for c in (4096, 2048, 1024, 512, 256, 128) if t % c == 0)
    grid_spec = pltpu.PrefetchScalarGridSpec(
        num_scalar_prefetch=1,
        grid=(t // tp,),
        in_specs=[pl.BlockSpec((8, tp), lambda i, ps: (0, i))],
        out_specs=[pl.BlockSpec((n_chunks, tp), lambda i, ps: (0, i))] * 2,
    )
    return pl.pallas_call(
        functools.partial(_plan_kernel, n_rows=n_rows),
        grid_spec=grid_spec,
        out_shape=[jax.ShapeDtypeStruct((n_chunks, t), jnp.int32)] * 2,
        compiler_params=_cparams(("parallel",)),
        name="moe_plan",
    )(pad_start, slab_t)


def hier_moe(x2, logits, w_gate, w_up, w_down):
    n_chunks, t, _ = x2.shape
    tmx = min(512, t)
    n_blocks = (t * TOP_K + N_EXPERTS * (tmx - 1)) // tmx + 1
    r = n_blocks * tmx
    slab, slab_t, counts = route(logits)
    cnt = counts[0, N_GROUPS:N_GROUPS + N_EXPERTS].astype(jnp.int32)
    padded = (cnt + tmx - 1) // tmx * tmx
    pad_end = jnp.cumsum(padded)
    pad_start = pad_end - padded
    blk_start = jnp.arange(n_blocks, dtype=jnp.int32) * tmx
    block_e = jnp.minimum(jnp.sum(pad_end[None, :] <= blk_start[:, None], axis=1), N_EXPERTS - 1).astype(jnp.int32)
    nvalid = jnp.clip(pad_start[block_e] + cnt[block_e] - blk_start, 0, tmx).astype(jnp.int32)
    idx1, idx2 = dispatch_indices(slab_t, pad_start.astype(jnp.int32), n_chunks, r)
    idx1, idx2 = idx1.reshape(-1), idx2.reshape(-1)
    x_sorted = sc_scatter_rows2(x2.reshape(n_chunks * t, LANES), idx1, idx2, n_chunks * r)
    y_sorted = expert_ffn(x_sorted.reshape(n_chunks, r, LANES), block_e, nvalid, w_gate, w_up, w_down, tmx)
    y_flat = y_sorted.reshape(n_chunks * r, LANES)
    y1 = sc_gather_rows(y_flat, idx1).reshape(n_chunks, t, LANES)
    y2 = sc_gather_rows(y_flat, idx2).reshape(n_chunks, t, LANES)
    return y1, y2, slab


def _lower_bound_rows(lb_logits):
    lb = jnp.cumsum(jax.nn.softmax(lb_logits.astype(F32), axis=1), axis=1)
    lb = lb - lb[:, :1]
    rows = jnp.stack([jnp.log(lb), jnp.log1p(-lb), 1.0 - lb], axis=2)
    rows = jnp.concatenate([rows, jnp.zeros(rows.shape[:2] + (5, A_WIDTH), F32)], axis=2)
    return rows


def _reorder_w_in(w):
    parts, off = [], 0
    for s in (A_WIDTH,) * 5 + (B_WIDTH, B_KV_WIDTH, B_KV_WIDTH, w.shape[1], w.shape[1]):
        parts.append(w[..., off:off + s])
        off += s
    return jnp.concatenate(parts[8:] + parts[:8], axis=-1)


def kernel(x, c, ctx, c_ctx, w_ada, b_ada, norm1_g, norm2_g, w_in, lb_logits, hgrn_norm_g, q_norm_g, k_norm_g,
           w_up_a, w_up_b, w_out, w_router_group, b_router_group, w_router_expert, b_router_expert, w_gate, w_up,
           w_down, final_norm_g):
    bsz, n, d = x.shape
    n_ctx = ctx.shape[1]
    depth = w_in.shape[0]
    mx = MXU_DTYPE

    w_in_r = _reorder_w_in(w_in).astype(mx)
    w_up_a_c, w_up_b_c, w_out_c = w_up_a.astype(mx), w_up_b.astype(mx), w_out.astype(mx)
    n_r = N_GROUPS + N_EXPERTS
    w_router = jnp.concatenate([w_router_group, w_router_expert, jnp.zeros((depth, d, LANES - n_r), F32)], axis=-1)
    w_router_hi = w_router.astype(jnp.bfloat16)
    w_router_lo = (w_router - w_router_hi.astype(F32)).astype(jnp.bfloat16)
    w_router = jnp.stack([w_router_hi, w_router_lo], axis=1)
    b_router = jnp.concatenate([b_router_group, b_router_expert, jnp.zeros((depth, LANES - n_r), F32)], axis=-1)
    lbp = _lower_bound_rows(lb_logits)
    cos_t, sin_t = rope_tables(n)

    n_rows = -(-(bsz + 1) // 8) * 8
    act = jnp.concatenate([c, c_ctx[None, :], jnp.zeros((n_rows - bsz - 1, d), F32)], axis=0)
    mod = ada_mod(act, w_ada, b_ada)

    h, hc = x, ctx
    zero_state = jnp.zeros((bsz, A_HEADS, LANES, LANES), F32)
    for l in range(depth):
        need_ctx = l < depth - 1
        ml = mod[l, :bsz].reshape(bsz, 1, N_MOD, d)
        mc = jnp.broadcast_to(mod[l, bsz].reshape(1, 1, N_MOD, d), (bsz, 1, N_MOD, d))
        sh1, sc1, g1, sh2, sc2, g2 = (ml[:, :, i] for i in range(N_MOD))
        csh1, csc1, cg1, csh2, csc2, cg2 = (mc[:, :, i] for i in range(N_MOD))

        zl = in_proj(h, norm1_g[l], sc1, sh1, w_in_r[l])
        zc = in_proj(hc, norm1_g[l], csc1, csh1, w_in_r[l])
        ob_c, s_bwd = hgrn_scan(zc, lbp[1, l], zero_state, True)
        if need_ctx:
            a_c, s_fwd = hgrn_scan(zc, lbp[0, l], zero_state, False, ob_c, hgrn_norm_g[l])
        else:
            _, s_fwd = hgrn_scan(zc, lbp[0, l], zero_state, False)
        ob_l, _ = hgrn_scan(zl, lbp[1, l], s_bwd, True)
        a_l, _ = hgrn_scan(zl, lbp[0, l], s_fwd, False, ob_l, hgrn_norm_g[l])

        q_l, k_l, v_l = qk_prep(zl, q_norm_g[l], k_norm_g[l], cos_t, sin_t)
        q_c, k_c, v_c = qk_prep(zc, q_norm_g[l], k_norm_g[l], None, None)
        b_l = attention(q_l, jnp.concatenate([k_c, k_l], axis=1), jnp.concatenate([v_c, v_l], axis=1))

        t_total = bsz * (n + n_ctx) if need_ctx else bsz * n
        bufs = (jnp.zeros((d // (2 * LANES), t_total, LANES), jnp.int32),
                jnp.zeros((t_total, LANES), F32)) if need_ctx else None
        h, x2, lg = merge_proj(h, a_l, b_l, zl, w_up_a_c[l], w_up_b_c[l], w_out_c[l], g1, norm2_g[l], sc2, sh2,
                               w_router[l], b_router[l][None, :], t_total, 0, bufs)
        if need_ctx:
            b_c = attention(q_c, k_c, v_c)
            hc, x2, lg = merge_proj(hc, a_c, b_c, zc, w_up_a_c[l], w_up_b_c[l], w_out_c[l], cg1, norm2_g[l],
                                    csc2, csh2, w_router[l], b_router[l][None, :], t_total, bsz * n, (x2, lg))

        y1, y2, slab = hier_moe(x2, lg, w_gate[l], w_up[l], w_down[l])
        h = moe_combine(h, y1, y2, slab, g2, 0, None if need_ctx else final_norm_g)
        if need_ctx:
            hc = moe_combine(hc, y1, y2, slab, cg2, bsz * n)
    return h
```

```python
import functools

import jax
import jax.numpy as jnp
from jax import lax
from jax.experimental import pallas as pl
from jax.experimental.pallas import tpu as pltpu
from jax.experimental.pallas import tpu_sc as plsc

F32 = jnp.float32
MXU_DTYPE = jnp.bfloat16

EPS = 1e-6
GRID_W = 64
ROPE_THETA = 10000.0
LANES = 128

A_HEADS = 4
A_DK = 128
A_WIDTH = A_HEADS * A_DK
HGRN_CHUNK = 64
HGRN_SUB = 16
HGRN_STEP_TOKENS = 1024
HGRN_EXP2_CLAMP = 86.0
B_HEADS = 8
B_KV_HEADS = 2
B_HD = 64
B_GROUP = B_HEADS // B_KV_HEADS
B_WIDTH = B_HEADS * B_HD
B_KV_WIDTH = B_KV_HEADS * B_HD
AXIS_DIM = B_HD // 2
N_GROUPS = 4
EXPERTS_PER_GROUP = 8
N_EXPERTS = N_GROUPS * EXPERTS_PER_GROUP
TOP_K = 2
N_MOD = 6

OFF_GATES = 0
OFF_QA = 2048
OFF_FF = OFF_QA + A_WIDTH
OFF_FB = OFF_FF + A_WIDTH
OFF_IA = OFF_FB + A_WIDTH
OFF_OG = OFF_IA + A_WIDTH
OFF_QB = OFF_OG + A_WIDTH
OFF_KB = OFF_QB + B_WIDTH
OFF_VB = OFF_KB + B_KV_WIDTH
D_IN = OFF_VB + B_KV_WIDTH

VMEM_LIMIT = 52 * 1024 * 1024
MERGE_SUB_ROWS = 128
EXPERT_SUB_ROWS = 512


def _cparams(sem):
    return pltpu.CompilerParams(dimension_semantics=sem, vmem_limit_bytes=VMEM_LIMIT)


LOG2E = 1.4426950408889634


def _sigmoid(x):
    return 0.5 * jnp.tanh(0.5 * x) + 0.5


def _nt_dot(a, b):
    return lax.dot_general(a, b, (((1,), (1,)), ((), ())), preferred_element_type=F32)


def _ada_kernel(a_ref, w_ref, b_ref, o_ref):
    a = a_ref[...]
    a = a * _sigmoid(a)
    o_ref[0] = jnp.dot(a.astype(MXU_DTYPE), w_ref[0].astype(MXU_DTYPE), preferred_element_type=F32) + b_ref[0]


def ada_mod(act, w_ada, b_ada):
    depth, d, n = w_ada.shape
    r = act.shape[0]
    tn = 1024
    return pl.pallas_call(
        _ada_kernel,
        grid=(depth, n // tn),
        in_specs=[
            pl.BlockSpec((r, d), lambda l, j: (0, 0)),
            pl.BlockSpec((1, d, tn), lambda l, j: (l, 0, j)),
            pl.BlockSpec((1, 1, tn), lambda l, j: (l, 0, j)),
        ],
        out_specs=pl.BlockSpec((1, r, tn), lambda l, j: (l, 0, j)),
        out_shape=jax.ShapeDtypeStruct((depth, r, n), F32),
        compiler_params=_cparams(("parallel", "parallel")),
        name="ada_mod",
    )(act, w_ada, b_ada.reshape(depth, 1, n))


def _modulated_norm(x, g, sc, sh):
    ms = jnp.mean(x * x, axis=-1, keepdims=True)
    return (x * lax.rsqrt(ms + EPS) * g) * (1.0 + sc) + sh


def _split3(x):
    hi = x.astype(jnp.bfloat16)
    r1 = x - hi.astype(F32)
    mid = r1.astype(jnp.bfloat16)
    lo = (r1 - mid.astype(F32)).astype(jnp.bfloat16)
    return hi, mid, lo


def _swap16(x):
    lane = lax.broadcasted_iota(jnp.int32, x.shape, 1)
    return jnp.where(lane % 32 < 16, pltpu.roll(x, LANES - 16, 1), pltpu.roll(x, 16, 1))


def _head_norm_rope(x, gsum, g, cos, sin):
    ssum = sum(jnp.dot(t, gsum, preferred_element_type=F32) for t in _split3(x * x))
    y = x * lax.rsqrt(ssum * (1.0 / B_HD) + EPS) * g
    if cos is not None:
        y = y * cos + _swap16(y) * sin
    return y


def _inproj_kernel(*refs, col_chunk, rope):
    if rope:
        h_ref, g_ref, sc_ref, sh_ref, w_ref, qg_ref, kg_ref, cos_ref, sin_ref, z_ref, qo_ref, ko_ref, vo_ref = refs
        cos, sin = cos_ref[...], sin_ref[...]
    else:
        h_ref, g_ref, sc_ref, sh_ref, w_ref, qg_ref, kg_ref, z_ref, qo_ref, ko_ref, vo_ref = refs
        cos = sin = None
    y = _modulated_norm(h_ref[0], g_ref[...], sc_ref[0], sh_ref[0]).astype(MXU_DTYPE)
    gqa = jnp.dot(y, w_ref[:, OFF_QB:], preferred_element_type=F32)
    r_i = lax.broadcasted_iota(jnp.int32, (LANES, LANES), 0)
    c_i = lax.broadcasted_iota(jnp.int32, (LANES, LANES), 1)
    gsum = (r_i // B_HD == c_i // B_HD).astype(jnp.bfloat16)
    lane = lax.broadcasted_iota(jnp.int32, (gqa.shape[0], LANES), 1)
    for cb in range(B_WIDTH // LANES):
        x = gqa[:, cb * LANES:(cb + 1) * LANES]
        q = _head_norm_rope(x, gsum, qg_ref[...], cos, sin) * (B_HD ** -0.5 * LOG2E)
        swapped = pltpu.roll(q, B_HD, 1)
        for half in range(2):
            head = 2 * cb + half
            kv = head // B_GROUP
            src = q if half == kv else swapped
            keep = (lane // B_HD) == kv
            qo_ref[0, head] = jnp.where(keep, src, 0.0).astype(qo_ref.dtype)
    ko_ref[0] = _head_norm_rope(gqa[:, B_WIDTH:B_WIDTH + B_KV_WIDTH], gsum, kg_ref[...], cos, sin).astype(ko_ref.dtype)
    vo_ref[0] = gqa[:, B_WIDTH + B_KV_WIDTH:].astype(vo_ref.dtype)
    for c in range(z_ref.shape[2] // col_chunk):
        cs = slice(c * col_chunk, (c + 1) * col_chunk)
        z_ref[0, :, cs] = jnp.dot(y, w_ref[:, cs], preferred_element_type=F32).astype(z_ref.dtype)


def in_proj(h, norm_g, sc, sh, w_in, q_norm_g, k_norm_g, cos_t=None, sin_t=None):
    bsz, n, d = h.shape
    tm = min(512, n)
    rope = cos_t is not None
    qg = jnp.tile(q_norm_g.astype(F32), LANES // B_HD).reshape(1, LANES)
    kg = jnp.tile(k_norm_g.astype(F32), LANES // B_HD).reshape(1, LANES)
    vec = pl.BlockSpec((1, LANES), lambda b, i: (0, 0))
    in_specs = [
        pl.BlockSpec((1, tm, d), lambda b, i: (b, i, 0)),
        pl.BlockSpec((1, d), lambda b, i: (0, 0)),
        pl.BlockSpec((1, 1, d), lambda b, i: (b, 0, 0)),
        pl.BlockSpec((1, 1, d), lambda b, i: (b, 0, 0)),
        pl.BlockSpec((d, D_IN), lambda b, i: (0, 0)),
        vec, vec,
    ]
    args = [h, norm_g.reshape(1, d), sc, sh, w_in, qg, kg]
    if rope:
        in_specs += [pl.BlockSpec((tm, LANES), lambda b, i: (i, 0))] * 2
        args += [cos_t, sin_t]
    row = pl.BlockSpec((1, tm, LANES), lambda b, i: (b, i, 0))
    return pl.pallas_call(
        functools.partial(_inproj_kernel, col_chunk=768, rope=rope),
        grid=(bsz, n // tm),
        in_specs=in_specs,
        out_specs=[pl.BlockSpec((1, tm, OFF_QB), lambda b, i: (b, i, 0)),
                   pl.BlockSpec((1, B_HEADS, tm, LANES), lambda b, i: (b, 0, i, 0)), row, row],
        out_shape=[jax.ShapeDtypeStruct((bsz, n, OFF_QB), MXU_DTYPE),
                   jax.ShapeDtypeStruct((bsz, B_HEADS, n, LANES), MXU_DTYPE),
                   jax.ShapeDtypeStruct((bsz, n, LANES), MXU_DTYPE),
                   jax.ShapeDtypeStruct((bsz, n, LANES), MXU_DTYPE)],
        compiler_params=_cparams(("parallel", "parallel")),
        name="in_proj_rope" if rope else "in_proj",
    )(*args)


def rope_tables(n):
    t = jnp.arange(n)
    r = (t // GRID_W).astype(F32)
    col = (t % GRID_W).astype(F32)
    inv = ROPE_THETA ** (-jnp.arange(0, AXIS_DIM, 2, dtype=F32) / AXIS_DIM)
    ang_r = r[:, None] * inv
    ang_c = col[:, None] * inv
    cos = jnp.concatenate([jnp.cos(ang_r)] * 2 + [jnp.cos(ang_c)] * 2, axis=1)
    sin = jnp.concatenate([-jnp.sin(ang_r), jnp.sin(ang_r), -jnp.sin(ang_c), jnp.sin(ang_c)], axis=1)
    return jnp.tile(cos, (1, 2)), jnp.tile(sin, (1, 2))


def _hgrn_gates(zf, loglb, log1m, onem):
    l1pe = jnp.log(1.0 + jnp.exp(-jnp.abs(zf)))
    c = log1m + (jnp.minimum(zf, 0.0) - l1pe)
    logf = jnp.maximum(loglb, c) + jnp.log(1.0 + jnp.exp(-jnp.abs(loglb - c)))
    k = onem * jnp.exp(jnp.minimum(-zf, 0.0) - l1pe)
    return k, logf * LOG2E


def _cumsum_rows(cum_mat, x):
    out = jnp.dot(cum_mat, jnp.concatenate(_split3(x), axis=1), preferred_element_type=F32)
    w = x.shape[1]
    return out[:, :w] + out[:, w:2 * w] + out[:, 2 * w:]


def _hgrn_kernel(*refs, rev, n_chunks, fuse):
    if fuse:
        q_ref, f_ref, v_ref, lbp_ref, s0_ref, ob_ref, og_ref, ng_ref, o_ref, sT_ref, state = refs
    else:
        q_ref, f_ref, v_ref, lbp_ref, s0_ref, o_ref, sT_ref, state = refs
    i = pl.program_id(1)

    @pl.when(i == 0)
    def _():
        state[...] = s0_ref[0]

    loglb = lbp_ref[0:1, :]
    log1m = lbp_ref[1:2, :]
    onem = lbp_ref[2:3, :]
    c = HGRN_CHUNK
    nsub = c // HGRN_SUB
    r_i = lax.broadcasted_iota(jnp.int32, (c, c), 0)
    c_i = lax.broadcasted_iota(jnp.int32, (c, c), 1)
    causal = (c_i >= r_i) if rev else (c_i <= r_i)
    cum_mat = causal.astype(jnp.bfloat16)
    row_blk = lax.broadcasted_iota(jnp.int32, (c, A_DK), 0) // HGRN_SUB
    s = [state[h] for h in range(A_HEADS)]
    order = list(range(n_chunks - 1, -1, -1) if rev else range(n_chunks))

    def prepare(ci):
        sl = slice(ci * c, (ci + 1) * c)
        k, logf = _hgrn_gates(f_ref[0, sl, :].astype(F32), loglb, log1m, onem)
        zq = q_ref[0, sl, :].astype(F32)
        q = zq * _sigmoid(zq) * (A_DK ** -0.5)
        cum = _cumsum_rows(cum_mat, logf)
        tot = cum[0:1] if rev else cum[c - 1:c]
        qd = (q * jnp.exp2(cum)).astype(MXU_DTYPE)
        kd = (k * jnp.exp2(tot - cum)).astype(MXU_DTYPE)
        refs_j = []
        for j in range(nsub):
            if rev:
                r = cum[(j + 1) * HGRN_SUB:(j + 1) * HGRN_SUB + 1] if j < nsub - 1 else jnp.zeros_like(tot)
            else:
                r = cum[j * HGRN_SUB - 1:j * HGRN_SUB] if j > 0 else jnp.zeros_like(tot)
            refs_j.append(r)
        ref_rows = jnp.concatenate([jnp.broadcast_to(r, (HGRN_SUB, r.shape[1])) for r in refs_j], axis=0)
        qt = (q * jnp.exp2(cum - ref_rows)).astype(MXU_DTYPE)
        kt = [(k * jnp.exp2(jnp.minimum(r - cum, HGRN_EXP2_CLAMP))).astype(MXU_DTYPE) for r in refs_j]
        return sl, qd, kd, jnp.exp2(tot), qt, kt

    ready = prepare(order[0])
    for n, ci in enumerate(order):
        sl, qd, kd, decay, qt, kt = ready
        if n + 1 < len(order):
            ready = prepare(order[n + 1])
        v = v_ref[0, sl, :]
        outs = []
        for h in range(A_HEADS):
            hs = slice(h * A_DK, (h + 1) * A_DK)
            q_big = jnp.concatenate([jnp.where(row_blk == j, qt[:, hs], jnp.zeros_like(qt[:, hs]))
                                     for j in range(nsub)], axis=1)
            k_big = jnp.concatenate([kt[j][:, hs] for j in range(nsub)], axis=1)
            scores = jnp.where(causal, _nt_dot(q_big, k_big), 0.0)
            o_h = _nt_dot(qd[:, hs], s[h].astype(MXU_DTYPE))
            o_h = o_h + jnp.dot(scores.astype(MXU_DTYPE), v[:, hs], preferred_element_type=F32)
            upd = lax.dot_general(v[:, hs], kd[:, hs], (((0,), (0,)), ((), ())), preferred_element_type=F32)
            s[h] = s[h] * decay[:, hs] + upd
            if fuse:
                o_h = o_h + ob_ref[0, sl, hs]
                o_h = o_h * lax.rsqrt(jnp.mean(o_h * o_h, axis=-1, keepdims=True) + EPS)
            outs.append(o_h)
        o = jnp.concatenate(outs, axis=1)
        if fuse:
            og = og_ref[0, sl, :].astype(F32)
            o = o * ng_ref[...] * (og * _sigmoid(og))
        o_ref[0, sl, :] = o.astype(o_ref.dtype)
    for h in range(A_HEADS):
        state[h] = s[h]

    @pl.when(i == pl.num_programs(1) - 1)
    def _():
        for h in range(A_HEADS):
            sT_ref[0, h] = s[h]


def hgrn_scan(z, lbp, s0, rev, o_other=None, norm_g=None):
    bsz, n, _ = z.shape
    fuse = o_other is not None
    t = min(HGRN_STEP_TOKENS, n)
    nt = n // t
    f_off = (OFF_FB if rev else OFF_FF) // A_WIDTH

    def tmap(i):
        return nt - 1 - i if rev else i

    def col(blk):
        return pl.BlockSpec((1, t, A_WIDTH), lambda b, i: (b, tmap(i), blk))

    state_spec = pl.BlockSpec((1, A_HEADS, LANES, LANES), lambda b, i: (b, 0, 0, 0))
    in_specs = [col(OFF_QA // A_WIDTH), col(f_off), col(OFF_IA // A_WIDTH),
                pl.BlockSpec((8, A_WIDTH), lambda b, i: (0, 0)), state_spec]
    args = [z, z, z, lbp, s0]
    if fuse:
        in_specs += [col(0), col(OFF_OG // A_WIDTH), pl.BlockSpec((1, A_WIDTH), lambda b, i: (0, 0))]
        args += [o_other, z, norm_g.reshape(1, A_WIDTH)]
    kern = functools.partial(_hgrn_kernel, rev=rev, n_chunks=t // HGRN_CHUNK, fuse=fuse)
    return pl.pallas_call(
        kern,
        grid=(bsz, nt),
        in_specs=in_specs,
        out_specs=[col(0), state_spec],
        out_shape=[jax.ShapeDtypeStruct((bsz, n, A_WIDTH), MXU_DTYPE if fuse else F32),
                   jax.ShapeDtypeStruct((bsz, A_HEADS, LANES, LANES), F32)],
        scratch_shapes=[pltpu.VMEM((A_HEADS, LANES, LANES), F32)],
        compiler_params=_cparams(("parallel", "arbitrary")),
        name="hgrn_bwd" if rev else "hgrn_fwd",
    )(*args)


def _attn_kernel(q_ref, k_ref, v_ref, o_ref):
    tq = q_ref.shape[2]
    rows = B_GROUP * tq
    lane = lax.broadcasted_iota(jnp.int32, v_ref.shape[1:], 1)
    lane_o = lax.broadcasted_iota(jnp.int32, (tq, LANES), 1)
    s_all = _nt_dot(q_ref[0].reshape(B_HEADS * tq, LANES), k_ref[0])
    for kv in range(B_KV_HEADS):
        s = s_all[kv * rows:(kv + 1) * rows]
        vs = jnp.where((lane // B_HD) == kv, v_ref[0], jnp.ones_like(v_ref[0]))
        p = jnp.exp2(s - jnp.max(s, axis=-1, keepdims=True))
        a = jnp.dot(p.astype(MXU_DTYPE), vs, preferred_element_type=F32)
        o_full = a / pltpu.roll(a, B_HD, 1)
        o_swapped = pltpu.roll(o_full, B_HD, 1)
        for pair in range(B_GROUP // 2):
            g0, g1 = 2 * pair, 2 * pair + 1
            lo_src = o_full if kv == 0 else o_swapped
            hi_src = o_swapped if kv == 0 else o_full
            blk = jnp.where(lane_o < B_HD, lo_src[g0 * tq:(g0 + 1) * tq], hi_src[g1 * tq:(g1 + 1) * tq])
            cb = kv * (B_GROUP // 2) + pair
            o_ref[0, :, cb * LANES:(cb + 1) * LANES] = blk.astype(o_ref.dtype)


def attention(q, k, v):
    bsz, _, n, _ = q.shape
    nkeys = k.shape[1]
    tq = min(128, n)
    return pl.pallas_call(
        _attn_kernel,
        grid=(bsz, n // tq),
        in_specs=[pl.BlockSpec((1, B_HEADS, tq, LANES), lambda b, i: (b, 0, i, 0)),
                  pl.BlockSpec((1, nkeys, LANES), lambda b, i: (b, 0, 0)),
                  pl.BlockSpec((1, nkeys, LANES), lambda b, i: (b, 0, 0))],
        out_specs=pl.BlockSpec((1, tq, B_WIDTH), lambda b, i: (b, i, 0)),
        out_shape=jax.ShapeDtypeStruct((bsz, n, B_WIDTH), MXU_DTYPE),
        compiler_params=_cparams(("parallel", "parallel")),
        name="gqa_attention",
    )(q, k, v)


def _pack_rows(x):
    w = x.shape[1] // 2
    bits = lax.bitcast_convert_type(x.astype(jnp.bfloat16).astype(F32), jnp.uint32)
    packed = (bits[:, :w] & jnp.uint32(0xFFFF0000)) | (bits[:, w:] >> 16)
    return lax.bitcast_convert_type(packed, jnp.int32)


def _unpack_rows(p):
    bits = lax.bitcast_convert_type(p, jnp.uint32)
    hi = lax.bitcast_convert_type(bits & jnp.uint32(0xFFFF0000), F32)
    lo = lax.bitcast_convert_type(bits << 16, F32)
    return jnp.concatenate([hi, lo], axis=1)


def _store_chunks(ref, packed):
    for j in range(ref.shape[0]):
        ref[j] = packed[:, j * LANES:(j + 1) * LANES]


def _load_chunks(ref):
    return jnp.concatenate([ref[j] for j in range(ref.shape[0])], axis=1)


def _merge_kernel(*refs, aliased):
    if aliased:
        refs = refs[:13] + refs[15:]
    (h_ref, a_ref, b_ref, zg_ref, wa_ref, wb_ref, wo_ref, g1_ref, n2_ref, sc_ref, sh_ref, wr_ref, br_ref,
     ho_ref, x2_ref, lg_ref) = refs
    d = h_ref.shape[2]
    tm = h_ref.shape[1]
    sub = min(MERGE_SUB_ROWS, tm)
    for r in range(tm // sub):
        rs = slice(r * sub, (r + 1) * sub)
        ua = jnp.dot(a_ref[0, rs, :], wa_ref[...], preferred_element_type=F32)
        ub = jnp.dot(b_ref[0, rs, :], wb_ref[...], preferred_element_type=F32)
        ga = _sigmoid(zg_ref[0, rs, 0:d].astype(F32))
        gb = _sigmoid(zg_ref[0, rs, d:2 * d].astype(F32))
        mix = (ga * ua + gb * ub).astype(MXU_DTYPE)
        h_new = h_ref[0, rs, :] + g1_ref[0] * jnp.dot(mix, wo_ref[...], preferred_element_type=F32)
        ho_ref[0, rs, :] = h_new
        x2 = _modulated_norm(h_new, n2_ref[...], sc_ref[0], sh_ref[0])
        packed = _pack_rows(x2)
        for j in range(x2_ref.shape[0]):
            x2_ref[j, rs, :] = packed[:, j * LANES:(j + 1) * LANES]
        x_hi = x2.astype(jnp.bfloat16)
        x_lo = (x2 - x_hi.astype(F32)).astype(jnp.bfloat16)
        hi_both = jnp.dot(x_hi, wr_ref[...], preferred_element_type=F32)
        lo_hi = jnp.dot(x_lo, wr_ref[:, 0:LANES], preferred_element_type=F32)
        lg_ref[rs, :] = hi_both[:, 0:LANES] + hi_both[:, LANES:] + lo_hi + br_ref[...]


def merge_proj(h, a, b, z, w_up_a, w_up_b, w_out, g1, norm2_g, sc2, sh2, w_router, b_router, t_total, tok_off,
               bufs=None):
    bsz, n, d = h.shape
    tm = min(512, n)
    nt = n // tm
    off = tok_off // tm
    n_chunks = d // (2 * LANES)
    row = lambda w: pl.BlockSpec((1, tm, w), lambda b_, i: (b_, i, 0))
    full = lambda s: pl.BlockSpec(s, lambda b_, i: (0,) * len(s))
    per_b = pl.BlockSpec((1, 1, d), lambda b_, i: (b_, 0, 0))
    in_specs = [row(d), row(A_WIDTH), row(B_WIDTH),
                pl.BlockSpec((1, tm, 2 * d), lambda b_, i: (b_, i, OFF_GATES)),
                full((A_WIDTH, d)), full((B_WIDTH, d)), full((d, d)),
                per_b, full((1, d)), per_b, per_b, full((d, 2 * LANES)), full((1, LANES))]
    args = [h, a, b, z, w_up_a, w_up_b, w_out, g1, norm2_g.reshape(1, d), sc2, sh2, w_router, b_router]
    aliases = {}
    if bufs is not None:
        in_specs += [pl.BlockSpec(memory_space=pl.ANY)] * 2
        args += list(bufs)
        aliases = {13: 1, 14: 2}
    return pl.pallas_call(
        functools.partial(_merge_kernel, aliased=bufs is not None),
        grid=(bsz, nt),
        in_specs=in_specs,
        out_specs=[row(d),
                   pl.BlockSpec((n_chunks, tm, LANES), lambda b_, i: (0, off + b_ * nt + i, 0)),
                   pl.BlockSpec((tm, LANES), lambda b_, i: (off + b_ * nt + i, 0))],
        out_shape=[jax.ShapeDtypeStruct((bsz, n, d), F32),
                   jax.ShapeDtypeStruct((n_chunks, t_total, LANES), jnp.int32),
                   jax.ShapeDtypeStruct((t_total, LANES), F32)],
        input_output_aliases=aliases,
        compiler_params=_cparams(("parallel", "parallel")),
        name="merge_proj",
    )(*args)


def _first_lane_where(cond, lane):
    return jnp.min(jnp.where(cond, lane, LANES), axis=-1, keepdims=True)


def _router_kernel(lg_ref, slab_ref, slab_t_ref, cnt_ref, base):
    @pl.when(pl.program_id(0) == 0)
    def _():
        base[...] = jnp.zeros_like(base)

    lg = lg_ref[...]
    tr = lg.shape[0]
    lane = lax.broadcasted_iota(jnp.int32, lg.shape, 1)
    neg = jnp.float32(-jnp.inf)
    is_g = lane < N_GROUPS
    gl = jnp.where(is_g, lg, neg)
    gmax = jnp.max(gl, axis=-1, keepdims=True)
    gsel = _first_lane_where(gl == gmax, lane)
    p_group = 1.0 / jnp.sum(jnp.where(is_g, jnp.exp(lg - gmax), 0.0), axis=-1, keepdims=True)
    lo = N_GROUPS + gsel * EXPERTS_PER_GROUP
    in_grp = (lane >= lo) & (lane < lo + EXPERTS_PER_GROUP)
    el = jnp.where(in_grp, lg, neg)
    m1 = jnp.max(el, axis=-1, keepdims=True)
    i1 = _first_lane_where(el == m1, lane)
    el2 = jnp.where(lane == i1, neg, el)
    m2 = jnp.max(el2, axis=-1, keepdims=True)
    i2 = _first_lane_where(el2 == m2, lane)
    zsum = jnp.sum(jnp.where(in_grp, jnp.exp(lg - m1), 0.0), axis=-1, keepdims=True)
    p1 = 1.0 / zsum
    p2 = jnp.exp(m2 - m1) / zsum
    w1 = p1 / (p1 + p2) * p_group
    w2 = p2 / (p1 + p2) * p_group
    hit1 = lane == i1
    hit2 = lane == i2
    onehot = (hit1 | hit2).astype(F32)
    r_i = lax.broadcasted_iota(jnp.int32, (tr, tr), 0)
    c_i = lax.broadcasted_iota(jnp.int32, (tr, tr), 1)
    before = (c_i < r_i).astype(MXU_DTYPE)
    prior = jnp.dot(before, onehot.astype(MXU_DTYPE), preferred_element_type=F32) + base[0:1, :]
    r1 = jnp.sum(jnp.where(hit1, prior, 0.0), axis=-1, keepdims=True)
    r2 = jnp.sum(jnp.where(hit2, prior, 0.0), axis=-1, keepdims=True)
    new_base = base[0:1, :] + jnp.sum(onehot, axis=0, keepdims=True)
    base[...] = jnp.broadcast_to(new_base, base.shape)
    cnt_ref[...] = jnp.broadcast_to(new_base, cnt_ref.shape)
    vals = [(i1 - N_GROUPS).astype(F32), (i2 - N_GROUPS).astype(F32), w1, w2, r1, r2]
    slab = jnp.zeros_like(lg)
    for idx, val in enumerate(vals):
        slab = jnp.where(lane == idx, val, slab)
    slab_ref[...] = slab
    slab_t_ref[...] = slab.T[0:8, :]


def route(logits):
    t = logits.shape[0]
    tr = next(c for c in (512, 256, 128) if t % c == 0)
    return pl.pallas_call(
        _router_kernel,
        grid=(t // tr,),
        in_specs=[pl.BlockSpec((tr, LANES), lambda i: (i, 0))],
        out_specs=[pl.BlockSpec((tr, LANES), lambda i: (i, 0)), pl.BlockSpec((8, tr), lambda i: (0, i)),
                   pl.BlockSpec((8, LANES), lambda i: (0, 0))],
        out_shape=[jax.ShapeDtypeStruct((t, LANES), F32), jax.ShapeDtypeStruct((8, t), F32),
                   jax.ShapeDtypeStruct((8, LANES), F32)],
        scratch_shapes=[pltpu.VMEM((8, LANES), F32)],
        compiler_params=_cparams(("arbitrary",)),
        name="moe_route",
    )(logits)


def _expert_kernel(be_ref, nv_ref, x_ref, wg_ref, wu_ref, wd_ref, o_ref, wg_s, wu_s, wd_s):
    j = pl.program_id(0)
    nv = nv_ref[j]

    @pl.when((j == 0) | (be_ref[j] != be_ref[jnp.maximum(j - 1, 0)]))
    def _():
        wg_s[...] = wg_ref[0].astype(wg_s.dtype)
        wu_s[...] = wu_ref[0].astype(wu_s.dtype)
        wd_s[...] = wd_ref[0].astype(wd_s.dtype)

    @pl.when(nv > 0)
    def _():
        tmx = x_ref.shape[1]
        sub = min(EXPERT_SUB_ROWS, tmx)
        for r in range(tmx // sub):
            rs = slice(r * sub, (r + 1) * sub)
            x = _unpack_rows(jnp.concatenate([x_ref[j, rs, :] for j in range(x_ref.shape[0])], axis=1))
            rows = lax.broadcasted_iota(jnp.int32, x.shape, 0) + r * sub
            x = jnp.where(rows < nv, x, 0.0).astype(MXU_DTYPE)
            hg = jnp.dot(x, wg_s[...], preferred_element_type=F32)
            hu = jnp.dot(x, wu_s[...], preferred_element_type=F32)
            hb = (hg * _sigmoid(hg) * hu).astype(MXU_DTYPE)
            packed = _pack_rows(jnp.dot(hb, wd_s[...], preferred_element_type=F32))
            for j in range(o_ref.shape[0]):
                o_ref[j, rs, :] = packed[:, j * LANES:(j + 1) * LANES]

    @pl.when(nv == 0)
    def _():
        o_ref[...] = jnp.zeros_like(o_ref)


def expert_ffn(x_sorted, block_e, nvalid, w_gate, w_up, w_down, tmx):
    n_chunks, r, _ = x_sorted.shape
    _, d, de = w_gate.shape
    grid_spec = pltpu.PrefetchScalarGridSpec(
        num_scalar_prefetch=2,
        grid=(r // tmx,),
        in_specs=[pl.BlockSpec((n_chunks, tmx, LANES), lambda j, be, nv: (0, j, 0)),
                  pl.BlockSpec((1, d, de), lambda j, be, nv: (be[j], 0, 0)),
                  pl.BlockSpec((1, d, de), lambda j, be, nv: (be[j], 0, 0)),
                  pl.BlockSpec((1, de, d), lambda j, be, nv: (be[j], 0, 0))],
        out_specs=pl.BlockSpec((n_chunks, tmx, LANES), lambda j, be, nv: (0, j, 0)),
        scratch_shapes=[pltpu.VMEM((d, de), MXU_DTYPE), pltpu.VMEM((d, de), MXU_DTYPE),
                        pltpu.VMEM((de, d), MXU_DTYPE)],
    )
    return pl.pallas_call(
        _expert_kernel,
        grid_spec=grid_spec,
        out_shape=jax.ShapeDtypeStruct((n_chunks, r, LANES), jnp.int32),
        compiler_params=_cparams(("arbitrary",)),
        name="expert_ffn",
    )(block_e, nvalid, x_sorted, w_gate, w_up, w_down)


def _combine_kernel(*refs, final):
    if final:
        h_ref, y1_ref, y2_ref, slab_ref, g2_ref, fn_ref, o_ref = refs
    else:
        h_ref, y1_ref, y2_ref, slab_ref, g2_ref, o_ref = refs
    w1 = slab_ref[:, 2:3]
    w2 = slab_ref[:, 3:4]
    y = _unpack_rows(_load_chunks(y1_ref)) * w1 + _unpack_rows(_load_chunks(y2_ref)) * w2
    h_new = h_ref[0] + g2_ref[0] * y
    if final:
        ms = jnp.mean(h_new * h_new, axis=-1, keepdims=True)
        h_new = h_new * lax.rsqrt(ms + EPS) * fn_ref[...]
    o_ref[0] = h_new


def moe_combine(h, y1, y2, slab, g2, tok_off, final_g=None):
    bsz, n, d = h.shape
    n_chunks = y1.shape[0]
    tm = min(512, n)
    nt = n // tm
    off = tok_off // tm
    final = final_g is not None
    yspec = pl.BlockSpec((n_chunks, tm, LANES), lambda b, i: (0, off + b * nt + i, 0))
    in_specs = [pl.BlockSpec((1, tm, d), lambda b, i: (b, i, 0)), yspec, yspec,
                pl.BlockSpec((tm, LANES), lambda b, i: (off + b * nt + i, 0)),
                pl.BlockSpec((1, 1, d), lambda b, i: (b, 0, 0))]
    args = [h, y1, y2, slab, g2]
    if final:
        in_specs.append(pl.BlockSpec((1, d), lambda b, i: (0, 0)))
        args.append(final_g.reshape(1, d))
    return pl.pallas_call(
        functools.partial(_combine_kernel, final=final),
        grid=(bsz, nt),
        in_specs=in_specs,
        out_specs=pl.BlockSpec((1, tm, d), lambda b, i: (b, i, 0)),
        out_shape=jax.ShapeDtypeStruct((bsz, n, d), F32),
        compiler_params=_cparams(("parallel", "parallel")),
        name="moe_combine_final" if final else "moe_combine",
    )(*args)


SC_WINDOW = 128


def _sc_mesh():
    return plsc.VectorSubcoreMesh(core_axis_name="core", subcore_axis_name="subcore")


def sc_gather_rows(table, idx):
    n = idx.shape[0]

    @pl.kernel(out_type=jax.ShapeDtypeStruct((n, LANES), table.dtype), mesh=_sc_mesh())
    def gather_kernel(t_hbm, i_hbm, o_hbm):
        def body(i_vmem, o_vmem):
            pltpu.sync_copy(t_hbm.at[i_vmem.at[0]], o_vmem)

        pltpu.emit_pipeline(
            body,
            grid=(n // SC_WINDOW,),
            in_specs=[pl.BlockSpec((1, SC_WINDOW), lambda i: (0, i))],
            out_specs=[pl.BlockSpec((SC_WINDOW, LANES), lambda i: (i, 0))],
            core_axis_name=("core", "subcore"),
            dimension_semantics=(pltpu.PARALLEL,),
            trace_scopes=False,
        )(i_hbm, o_hbm)

    return gather_kernel(table, idx.reshape(1, n))


def sc_scatter_rows2(rows, idx_a, idx_b, n_out):
    n = rows.shape[0]

    @pl.kernel(out_type=jax.ShapeDtypeStruct((n_out, LANES), rows.dtype), mesh=_sc_mesh())
    def scatter_kernel(r_hbm, ia_hbm, ib_hbm, o_hbm):
        def body(r_vmem, ia_vmem, ib_vmem):
            pltpu.sync_copy(r_vmem, o_hbm.at[ia_vmem.at[0]])
            pltpu.sync_copy(r_vmem, o_hbm.at[ib_vmem.at[0]])

        pltpu.emit_pipeline(
            body,
            grid=(n // SC_WINDOW,),
            in_specs=[pl.BlockSpec((SC_WINDOW, LANES), lambda i: (i, 0)),
                      pl.BlockSpec((1, SC_WINDOW), lambda i: (0, i)),
                      pl.BlockSpec((1, SC_WINDOW), lambda i: (0, i))],
            out_specs=[],
            core_axis_name=("core", "subcore"),
            dimension_semantics=(pltpu.PARALLEL,),
            trace_scopes=False,
        )(r_hbm, ia_hbm, ib_hbm)

    return scatter_kernel(rows, idx_a.reshape(1, n), idx_b.reshape(1, n))


def _plan_kernel(ps_ref, st_ref, ia_ref, ib_ref, *, n_rows):
    e = st_ref[0:2, :].astype(jnp.int32)
    dest = st_ref[4:6, :].astype(jnp.int32)
    for ex in range(N_EXPERTS):
        dest = dest + jnp.where(e == ex, ps_ref[ex], 0)
    for k, ref in enumerate((ia_ref, ib_ref)):
        for j in range(ref.shape[0]):
            ref[j:j + 1, :] = dest[k:k + 1, :] + j * n_rows


def dispatch_indices(slab_t, pad_start, n_chunks, n_rows):
    t = slab_t.shape[1]
    tp = next(c for c in (4096, 2048, 1024, 512, 256, 128) if t % c == 0)
    grid_spec = pltpu.PrefetchScalarGridSpec(
        num_scalar_prefetch=1,
        grid=(t // tp,),
        in_specs=[pl.BlockSpec((8, tp), lambda i, ps: (0, i))],
        out_specs=[pl.BlockSpec((n_chunks, tp), lambda i, ps: (0, i))] * 2,
    )
    return pl.pallas_call(
        functools.partial(_plan_kernel, n_rows=n_rows),
        grid_spec=grid_spec,
        out_shape=[jax.ShapeDtypeStruct((n_chunks, t), jnp.int32)] * 2,
        compiler_params=_cparams(("parallel",)),
        name="moe_plan",
    )(pad_start, slab_t)


def hier_moe(x2, logits, w_gate, w_up, w_down):
    n_chunks, t, _ = x2.shape
    tmx = min(512, t)
    n_blocks = (t * TOP_K + N_EXPERTS * (tmx - 1)) // tmx + 1
    r = n_blocks * tmx
    slab, slab_t, counts = route(logits)
    cnt = counts[0, N_GROUPS:N_GROUPS + N_EXPERTS].astype(jnp.int32)
    padded = (cnt + tmx - 1) // tmx * tmx
    pad_end = jnp.cumsum(padded)
    pad_start = pad_end - padded
    blk_start = jnp.arange(n_blocks, dtype=jnp.int32) * tmx
    block_e = jnp.minimum(jnp.sum(pad_end[None, :] <= blk_start[:, None], axis=1), N_EXPERTS - 1).astype(jnp.int32)
    nvalid = jnp.clip(pad_start[block_e] + cnt[block_e] - blk_start, 0, tmx).astype(jnp.int32)
    idx1, idx2 = dispatch_indices(slab_t, pad_start.astype(jnp.int32), n_chunks, r)
    idx1, idx2 = idx1.reshape(-1), idx2.reshape(-1)
    x_sorted = sc_scatter_rows2(x2.reshape(n_chunks * t, LANES), idx1, idx2, n_chunks * r)
    y_sorted = expert_ffn(x_sorted.reshape(n_chunks, r, LANES), block_e, nvalid, w_gate, w_up, w_down, tmx)
    y_flat = y_sorted.reshape(n_chunks * r, LANES)
    y1 = sc_gather_rows(y_flat, idx1).reshape(n_chunks, t, LANES)
    y2 = sc_gather_rows(y_flat, idx2).reshape(n_chunks, t, LANES)
    return y1, y2, slab


def _lower_bound_rows(lb_logits):
    lb = jnp.cumsum(jax.nn.softmax(lb_logits.astype(F32), axis=1), axis=1)
    lb = lb - lb[:, :1]
    rows = jnp.stack([jnp.log(lb), jnp.log1p(-lb), 1.0 - lb], axis=2)
    rows = jnp.concatenate([rows, jnp.zeros(rows.shape[:2] + (5, A_WIDTH), F32)], axis=2)
    return rows


def _cast_kernel(w_ref, o_ref):
    o_ref[...] = w_ref[...].astype(o_ref.dtype)


def reorder_cast_w_in(w_in):
    depth, d, n = w_in.shape
    gate_cols = 2 * d
    blk = 256
    nb = n // blk
    shift = (n - gate_cols) // blk
    return pl.pallas_call(
        _cast_kernel,
        grid=(depth, nb),
        in_specs=[pl.BlockSpec((1, d, blk), lambda l, j: (l, 0, (j + shift) % nb))],
        out_specs=pl.BlockSpec((1, d, blk), lambda l, j: (l, 0, j)),
        out_shape=jax.ShapeDtypeStruct((depth, d, n), MXU_DTYPE),
        compiler_params=_cparams(("parallel", "parallel")),
        name="reorder_cast_w_in",
    )(w_in)


def kernel(x, c, ctx, c_ctx, w_ada, b_ada, norm1_g, norm2_g, w_in, lb_logits, hgrn_norm_g, q_norm_g, k_norm_g,
           w_up_a, w_up_b, w_out, w_router_group, b_router_group, w_router_expert, b_router_expert, w_gate, w_up,
           w_down, final_norm_g):
    bsz, n, d = x.shape
    n_ctx = ctx.shape[1]
    depth = w_in.shape[0]
    mx = MXU_DTYPE

    w_in_r = reorder_cast_w_in(w_in)
    w_up_a_c, w_up_b_c, w_out_c = w_up_a.astype(mx), w_up_b.astype(mx), w_out.astype(mx)
    n_r = N_GROUPS + N_EXPERTS
    w_router = jnp.concatenate([w_router_group, w_router_expert, jnp.zeros((depth, d, LANES - n_r), F32)], axis=-1)
    w_router_hi = w_router.astype(jnp.bfloat16)
    w_router_lo = (w_router - w_router_hi.astype(F32)).astype(jnp.bfloat16)
    w_router = jnp.concatenate([w_router_hi, w_router_lo], axis=-1)
    b_router = jnp.concatenate([b_router_group, b_router_expert, jnp.zeros((depth, LANES - n_r), F32)], axis=-1)
    lbp = _lower_bound_rows(lb_logits)
    cos_t, sin_t = rope_tables(n)

    n_rows = -(-(bsz + 1) // 8) * 8
    act = jnp.concatenate([c, c_ctx[None, :], jnp.zeros((n_rows - bsz - 1, d), F32)], axis=0)
    mod = ada_mod(act, w_ada, b_ada)

    h, hc = x, ctx
    zero_state = jnp.zeros((bsz, A_HEADS, LANES, LANES), F32)
    for l in range(depth):
        need_ctx = l < depth - 1
        ml = mod[l, :bsz].reshape(bsz, 1, N_MOD, d)
        mc = jnp.broadcast_to(mod[l, bsz].reshape(1, 1, N_MOD, d), (bsz, 1, N_MOD, d))
        sh1, sc1, g1, sh2, sc2, g2 = (ml[:, :, i] for i in range(N_MOD))
        csh1, csc1, cg1, csh2, csc2, cg2 = (mc[:, :, i] for i in range(N_MOD))

        zl, q_l, k_l, v_l = in_proj(h, norm1_g[l], sc1, sh1, w_in_r[l], q_norm_g[l], k_norm_g[l], cos_t, sin_t)
        zc, q_c, k_c, v_c = in_proj(hc, norm1_g[l], csc1, csh1, w_in_r[l], q_norm_g[l], k_norm_g[l])
        ob_c, s_bwd = hgrn_scan(zc, lbp[1, l], zero_state, True)
        if need_ctx:
            a_c, s_fwd = hgrn_scan(zc, lbp[0, l], zero_state, False, ob_c, hgrn_norm_g[l])
        else:
            _, s_fwd = hgrn_scan(zc, lbp[0, l], zero_state, False)
        ob_l, _ = hgrn_scan(zl, lbp[1, l], s_bwd, True)
        a_l, _ = hgrn_scan(zl, lbp[0, l], s_fwd, False, ob_l, hgrn_norm_g[l])

        b_l = attention(q_l, jnp.concatenate([k_c, k_l], axis=1), jnp.concatenate([v_c, v_l], axis=1))

        t_total = bsz * (n + n_ctx) if need_ctx else bsz * n
        bufs = (jnp.zeros((d // (2 * LANES), t_total, LANES), jnp.int32),
                jnp.zeros((t_total, LANES), F32)) if need_ctx else None
        h, x2, lg = merge_proj(h, a_l, b_l, zl, w_up_a_c[l], w_up_b_c[l], w_out_c[l], g1, norm2_g[l], sc2, sh2,
                               w_router[l], b_router[l][None, :], t_total, 0, bufs)
        if need_ctx:
            b_c = attention(q_c, k_c, v_c)
            hc, x2, lg = merge_proj(hc, a_c, b_c, zc, w_up_a_c[l], w_up_b_c[l], w_out_c[l], cg1, norm2_g[l],
                                    csc2, csh2, w_router[l], b_router[l][None, :], t_total, bsz * n, (x2, lg))

        y1, y2, slab = hier_moe(x2, lg, w_gate[l], w_up[l], w_down[l])
        h = moe_combine(h, y1, y2, slab, g2, 0, None if need_ctx else final_norm_g)
        if need_ctx:
            hc = moe_combine(hc, y1, y2, slab, cg2, bsz * n)
    return h
```

```python
import functools

import jax
import jax.numpy as jnp
from jax import lax
from jax.experimental import pallas as pl
from jax.experimental.pallas import tpu as pltpu
from jax.experimental.pallas import tpu_sc as plsc

F32 = jnp.float32
MXU_DTYPE = jnp.bfloat16

EPS = 1e-6
GRID_W = 64
ROPE_THETA = 10000.0
LANES = 128

A_HEADS = 4
A_DK = 128
A_WIDTH = A_HEADS * A_DK
HGRN_CHUNK = 64
HGRN_SUB = 16
HGRN_STEP_TOKENS = 1024
HGRN_EXP2_CLAMP = 86.0
B_HEADS = 8
B_KV_HEADS = 2
B_HD = 64
B_GROUP = B_HEADS // B_KV_HEADS
B_WIDTH = B_HEADS * B_HD
B_KV_WIDTH = B_KV_HEADS * B_HD
AXIS_DIM = B_HD // 2
N_GROUPS = 4
EXPERTS_PER_GROUP = 8
N_EXPERTS = N_GROUPS * EXPERTS_PER_GROUP
TOP_K = 2
N_MOD = 6

OFF_GATES = 0
OFF_QA = 2048
OFF_FF = OFF_QA + A_WIDTH
OFF_FB = OFF_FF + A_WIDTH
OFF_IA = OFF_FB + A_WIDTH
OFF_OG = OFF_IA + A_WIDTH
OFF_QB = OFF_OG + A_WIDTH
OFF_KB = OFF_QB + B_WIDTH
OFF_VB = OFF_KB + B_KV_WIDTH
D_IN = OFF_VB + B_KV_WIDTH

VMEM_LIMIT = 52 * 1024 * 1024
INPROJ_SUB_ROWS = 512
MERGE_ROWS = 1024
MERGE_SUB_ROWS = 256
EXPERT_SUB_ROWS = 512


def _cparams(sem):
    return pltpu.CompilerParams(dimension_semantics=sem, vmem_limit_bytes=VMEM_LIMIT)


LOG2E = 1.4426950408889634


def _sigmoid(x):
    return 0.5 * jnp.tanh(0.5 * x) + 0.5


def _nt_dot(a, b):
    return lax.dot_general(a, b, (((1,), (1,)), ((), ())), preferred_element_type=F32)


def _ada_kernel(a_ref, w_ref, b_ref, o_ref):
    a = a_ref[...]
    a = a * _sigmoid(a)
    o_ref[0] = jnp.dot(a.astype(MXU_DTYPE), w_ref[0].astype(MXU_DTYPE), preferred_element_type=F32) + b_ref[0]


def ada_mod(act, w_ada, b_ada):
    depth, d, n = w_ada.shape
    r = act.shape[0]
    tn = 1024
    return pl.pallas_call(
        _ada_kernel,
        grid=(depth, n // tn),
        in_specs=[
            pl.BlockSpec((r, d), lambda l, j: (0, 0)),
            pl.BlockSpec((1, d, tn), lambda l, j: (l, 0, j)),
            pl.BlockSpec((1, 1, tn), lambda l, j: (l, 0, j)),
        ],
        out_specs=pl.BlockSpec((1, r, tn), lambda l, j: (l, 0, j)),
        out_shape=jax.ShapeDtypeStruct((depth, r, n), F32),
        compiler_params=_cparams(("parallel", "parallel")),
        name="ada_mod",
    )(act, w_ada, b_ada.reshape(depth, 1, n))


def _modulated_norm(x, g, sc, sh):
    ms = jnp.mean(x * x, axis=-1, keepdims=True)
    return (x * lax.rsqrt(ms + EPS) * g) * (1.0 + sc) + sh


def _split3(x):
    hi = x.astype(jnp.bfloat16)
    r1 = x - hi.astype(F32)
    mid = r1.astype(jnp.bfloat16)
    lo = (r1 - mid.astype(F32)).astype(jnp.bfloat16)
    return hi, mid, lo


def _swap16(x):
    lane = lax.broadcasted_iota(jnp.int32, x.shape, 1)
    return jnp.where(lane % 32 < 16, pltpu.roll(x, LANES - 16, 1), pltpu.roll(x, 16, 1))


def _head_norm_rope(x, gsum, g, cos, sin):
    ssum = sum(jnp.dot(t, gsum, preferred_element_type=F32) for t in _split3(x * x))
    y = x * lax.rsqrt(ssum * (1.0 / B_HD) + EPS) * g
    if cos is not None:
        y = y * cos + _swap16(y) * sin
    return y


def _inproj_kernel(*refs, col_chunk, rope):
    if rope:
        h_ref, g_ref, sc_ref, sh_ref, w_ref, qg_ref, kg_ref, cos_ref, sin_ref, z_ref, qo_ref, ko_ref, vo_ref = refs
        cos, sin = cos_ref[...], sin_ref[...]
    else:
        h_ref, g_ref, sc_ref, sh_ref, w_ref, qg_ref, kg_ref, z_ref, qo_ref, ko_ref, vo_ref = refs
        cos = sin = None
    r_i = lax.broadcasted_iota(jnp.int32, (LANES, LANES), 0)
    c_i = lax.broadcasted_iota(jnp.int32, (LANES, LANES), 1)
    gsum = (r_i // B_HD == c_i // B_HD).astype(jnp.bfloat16)
    tm = h_ref.shape[1]
    sub = min(INPROJ_SUB_ROWS, tm)
    lane = lax.broadcasted_iota(jnp.int32, (sub, LANES), 1)
    for r in range(tm // sub):
        rs = slice(r * sub, (r + 1) * sub)
        y = _modulated_norm(h_ref[0, rs, :], g_ref[...], sc_ref[0], sh_ref[0]).astype(MXU_DTYPE)
        gqa = jnp.dot(y, w_ref[0, :, OFF_QB:], preferred_element_type=F32)
        cs_, sn_ = (cos[rs], sin[rs]) if rope else (None, None)
        for cb in range(B_WIDTH // LANES):
            x = gqa[:, cb * LANES:(cb + 1) * LANES]
            q = _head_norm_rope(x, gsum, qg_ref[...], cs_, sn_) * (B_HD ** -0.5 * LOG2E)
            swapped = pltpu.roll(q, B_HD, 1)
            for half in range(2):
                head = 2 * cb + half
                kv = head // B_GROUP
                src = q if half == kv else swapped
                keep = (lane // B_HD) == kv
                qo_ref[0, head, rs, :] = jnp.where(keep, src, 0.0).astype(qo_ref.dtype)
        kn = _head_norm_rope(gqa[:, B_WIDTH:B_WIDTH + B_KV_WIDTH], gsum, kg_ref[...], cs_, sn_)
        ko_ref[0, rs, :] = kn.astype(ko_ref.dtype)
        vo_ref[0, rs, :] = gqa[:, B_WIDTH + B_KV_WIDTH:].astype(vo_ref.dtype)
        for c in range(z_ref.shape[2] // col_chunk):
            cs = slice(c * col_chunk, (c + 1) * col_chunk)
            z_ref[0, rs, cs] = jnp.dot(y, w_ref[0, :, cs], preferred_element_type=F32).astype(z_ref.dtype)


def in_proj(h, norm_g, sc, sh, w_in, layer, q_norm_g, k_norm_g, cos_t=None, sin_t=None):
    bsz, n, d = h.shape
    tm = min(512, n)
    rope = cos_t is not None
    qg = jnp.tile(q_norm_g.astype(F32), LANES // B_HD).reshape(1, LANES)
    kg = jnp.tile(k_norm_g.astype(F32), LANES // B_HD).reshape(1, LANES)
    vec = pl.BlockSpec((1, LANES), lambda b, i: (0, 0))
    in_specs = [
        pl.BlockSpec((1, tm, d), lambda b, i: (b, i, 0)),
        pl.BlockSpec((1, d), lambda b, i: (0, 0)),
        pl.BlockSpec((1, 1, d), lambda b, i: (b, 0, 0)),
        pl.BlockSpec((1, 1, d), lambda b, i: (b, 0, 0)),
        pl.BlockSpec((1, d, D_IN), lambda b, i: (layer, 0, 0)),
        vec, vec,
    ]
    args = [h, norm_g.reshape(1, d), sc, sh, w_in, qg, kg]
    if rope:
        in_specs += [pl.BlockSpec((tm, LANES), lambda b, i: (i, 0))] * 2
        args += [cos_t, sin_t]
    row = pl.BlockSpec((1, tm, LANES), lambda b, i: (b, i, 0))
    return pl.pallas_call(
        functools.partial(_inproj_kernel, col_chunk=768, rope=rope),
        grid=(bsz, n // tm),
        in_specs=in_specs,
        out_specs=[pl.BlockSpec((1, tm, OFF_QB), lambda b, i: (b, i, 0)),
                   pl.BlockSpec((1, B_HEADS, tm, LANES), lambda b, i: (b, 0, i, 0)), row, row],
        out_shape=[jax.ShapeDtypeStruct((bsz, n, OFF_QB), MXU_DTYPE),
                   jax.ShapeDtypeStruct((bsz, B_HEADS, n, LANES), MXU_DTYPE),
                   jax.ShapeDtypeStruct((bsz, n, LANES), MXU_DTYPE),
                   jax.ShapeDtypeStruct((bsz, n, LANES), MXU_DTYPE)],
        compiler_params=_cparams(("parallel", "parallel")),
        name="in_proj_rope" if rope else "in_proj",
    )(*args)


def rope_tables(n):
    t = jnp.arange(n)
    r = (t // GRID_W).astype(F32)
    col = (t % GRID_W).astype(F32)
    inv = ROPE_THETA ** (-jnp.arange(0, AXIS_DIM, 2, dtype=F32) / AXIS_DIM)
    ang_r = r[:, None] * inv
    ang_c = col[:, None] * inv
    cos = jnp.concatenate([jnp.cos(ang_r)] * 2 + [jnp.cos(ang_c)] * 2, axis=1)
    sin = jnp.concatenate([-jnp.sin(ang_r), jnp.sin(ang_r), -jnp.sin(ang_c), jnp.sin(ang_c)], axis=1)
    return jnp.tile(cos, (1, 2)), jnp.tile(sin, (1, 2))


def _hgrn_gates(zf, loglb, log1m, onem):
    l1pe = jnp.log(1.0 + jnp.exp(-jnp.abs(zf)))
    c = log1m + (jnp.minimum(zf, 0.0) - l1pe)
    logf = jnp.maximum(loglb, c) + jnp.log(1.0 + jnp.exp(-jnp.abs(loglb - c)))
    k = onem * jnp.exp(jnp.minimum(-zf, 0.0) - l1pe)
    return k, logf * LOG2E


def _cumsum_rows(cum_mat, x):
    out = jnp.dot(cum_mat, jnp.concatenate(_split3(x), axis=1), preferred_element_type=F32)
    w = x.shape[1]
    return out[:, :w] + out[:, w:2 * w] + out[:, 2 * w:]


def _hgrn_kernel(*refs, rev, n_chunks, fuse):
    if fuse:
        q_ref, f_ref, v_ref, lbp_ref, s0_ref, ob_ref, og_ref, ng_ref, o_ref, sT_ref, state = refs
    else:
        q_ref, f_ref, v_ref, lbp_ref, s0_ref, o_ref, sT_ref, state = refs
    i = pl.program_id(1)

    @pl.when(i == 0)
    def _():
        state[...] = s0_ref[0]

    loglb = lbp_ref[0:1, :]
    log1m = lbp_ref[1:2, :]
    onem = lbp_ref[2:3, :]
    c = HGRN_CHUNK
    nsub = c // HGRN_SUB
    r_i = lax.broadcasted_iota(jnp.int32, (c, c), 0)
    c_i = lax.broadcasted_iota(jnp.int32, (c, c), 1)
    causal = (c_i >= r_i) if rev else (c_i <= r_i)
    cum_mat = causal.astype(jnp.bfloat16)
    row_blk = lax.broadcasted_iota(jnp.int32, (c, A_DK), 0) // HGRN_SUB
    s = [state[h] for h in range(A_HEADS)]
    order = list(range(n_chunks - 1, -1, -1) if rev else range(n_chunks))

    def prepare(ci):
        sl = slice(ci * c, (ci + 1) * c)
        k, logf = _hgrn_gates(f_ref[0, sl, :].astype(F32), loglb, log1m, onem)
        zq = q_ref[0, sl, :].astype(F32)
        q = zq * _sigmoid(zq) * (A_DK ** -0.5)
        cum = _cumsum_rows(cum_mat, logf)
        tot = cum[0:1] if rev else cum[c - 1:c]
        qd = (q * jnp.exp2(cum)).astype(MXU_DTYPE)
        kd = (k * jnp.exp2(tot - cum)).astype(MXU_DTYPE)
        refs_j = []
        for j in range(nsub):
            if rev:
                r = cum[(j + 1) * HGRN_SUB:(j + 1) * HGRN_SUB + 1] if j < nsub - 1 else jnp.zeros_like(tot)
            else:
                r = cum[j * HGRN_SUB - 1:j * HGRN_SUB] if j > 0 else jnp.zeros_like(tot)
            refs_j.append(r)
        ref_rows = jnp.concatenate([jnp.broadcast_to(r, (HGRN_SUB, r.shape[1])) for r in refs_j], axis=0)
        qt = (q * jnp.exp2(cum - ref_rows)).astype(MXU_DTYPE)
        kt = [(k * jnp.exp2(jnp.minimum(r - cum, HGRN_EXP2_CLAMP))).astype(MXU_DTYPE) for r in refs_j]
        return sl, qd, kd, jnp.exp2(tot), qt, kt

    ready = prepare(order[0])
    for n, ci in enumerate(order):
        sl, qd, kd, decay, qt, kt = ready
        if n + 1 < len(order):
            ready = prepare(order[n + 1])
        v = v_ref[0, sl, :]
        outs = []
        for h in range(A_HEADS):
            hs = slice(h * A_DK, (h + 1) * A_DK)
            q_big = jnp.concatenate([jnp.where(row_blk == j, qt[:, hs], jnp.zeros_like(qt[:, hs]))
                                     for j in range(nsub)], axis=1)
            k_big = jnp.concatenate([kt[j][:, hs] for j in range(nsub)], axis=1)
            scores = jnp.where(causal, _nt_dot(q_big, k_big), 0.0)
            o_h = _nt_dot(qd[:, hs], s[h].astype(MXU_DTYPE))
            o_h = o_h + jnp.dot(scores.astype(MXU_DTYPE), v[:, hs], preferred_element_type=F32)
            upd = lax.dot_general(v[:, hs], kd[:, hs], (((0,), (0,)), ((), ())), preferred_element_type=F32)
            s[h] = s[h] * decay[:, hs] + upd
            if fuse:
                o_h = o_h + ob_ref[0, sl, hs]
                o_h = o_h * lax.rsqrt(jnp.mean(o_h * o_h, axis=-1, keepdims=True) + EPS)
            outs.append(o_h)
        o = jnp.concatenate(outs, axis=1)
        if fuse:
            og = og_ref[0, sl, :].astype(F32)
            o = o * ng_ref[...] * (og * _sigmoid(og))
        o_ref[0, sl, :] = o.astype(o_ref.dtype)
    for h in range(A_HEADS):
        state[h] = s[h]

    @pl.when(i == pl.num_programs(1) - 1)
    def _():
        for h in range(A_HEADS):
            sT_ref[0, h] = s[h]


def hgrn_scan(z, lbp, s0, rev, o_other=None, norm_g=None):
    bsz, n, _ = z.shape
    fuse = o_other is not None
    t = min(HGRN_STEP_TOKENS, n)
    nt = n // t
    f_off = (OFF_FB if rev else OFF_FF) // A_WIDTH

    def tmap(i):
        return nt - 1 - i if rev else i

    def col(blk):
        return pl.BlockSpec((1, t, A_WIDTH), lambda b, i: (b, tmap(i), blk))

    state_spec = pl.BlockSpec((1, A_HEADS, LANES, LANES), lambda b, i: (b, 0, 0, 0))
    in_specs = [col(OFF_QA // A_WIDTH), col(f_off), col(OFF_IA // A_WIDTH),
                pl.BlockSpec((8, A_WIDTH), lambda b, i: (0, 0)), state_spec]
    args = [z, z, z, lbp, s0]
    if fuse:
        in_specs += [col(0), col(OFF_OG // A_WIDTH), pl.BlockSpec((1, A_WIDTH), lambda b, i: (0, 0))]
        args += [o_other, z, norm_g.reshape(1, A_WIDTH)]
    kern = functools.partial(_hgrn_kernel, rev=rev, n_chunks=t // HGRN_CHUNK, fuse=fuse)
    return pl.pallas_call(
        kern,
        grid=(bsz, nt),
        in_specs=in_specs,
        out_specs=[col(0), state_spec],
        out_shape=[jax.ShapeDtypeStruct((bsz, n, A_WIDTH), MXU_DTYPE if fuse else F32),
                   jax.ShapeDtypeStruct((bsz, A_HEADS, LANES, LANES), F32)],
        scratch_shapes=[pltpu.VMEM((A_HEADS, LANES, LANES), F32)],
        compiler_params=_cparams(("parallel", "arbitrary")),
        name="hgrn_bwd" if rev else "hgrn_fwd",
    )(*args)


def _attn_kernel(q_ref, k_ref, v_ref, o_ref):
    tq = q_ref.shape[2]
    rows = B_GROUP * tq
    lane = lax.broadcasted_iota(jnp.int32, v_ref.shape[1:], 1)
    lane_o = lax.broadcasted_iota(jnp.int32, (tq, LANES), 1)
    s_all = _nt_dot(q_ref[0].reshape(B_HEADS * tq, LANES), k_ref[0])
    for kv in range(B_KV_HEADS):
        s = s_all[kv * rows:(kv + 1) * rows]
        vs = jnp.where((lane // B_HD) == kv, v_ref[0], jnp.ones_like(v_ref[0]))
        p = jnp.exp2(s - jnp.max(s, axis=-1, keepdims=True))
        a = jnp.dot(p.astype(MXU_DTYPE), vs, preferred_element_type=F32)
        o_full = a / pltpu.roll(a, B_HD, 1)
        o_swapped = pltpu.roll(o_full, B_HD, 1)
        for pair in range(B_GROUP // 2):
            g0, g1 = 2 * pair, 2 * pair + 1
            lo_src = o_full if kv == 0 else o_swapped
            hi_src = o_swapped if kv == 0 else o_full
            blk = jnp.where(lane_o < B_HD, lo_src[g0 * tq:(g0 + 1) * tq], hi_src[g1 * tq:(g1 + 1) * tq])
            cb = kv * (B_GROUP // 2) + pair
            o_ref[0, :, cb * LANES:(cb + 1) * LANES] = blk.astype(o_ref.dtype)


def attention(q, k, v):
    bsz, _, n, _ = q.shape
    nkeys = k.shape[1]
    tq = min(128, n)
    return pl.pallas_call(
        _attn_kernel,
        grid=(bsz, n // tq),
        in_specs=[pl.BlockSpec((1, B_HEADS, tq, LANES), lambda b, i: (b, 0, i, 0)),
                  pl.BlockSpec((1, nkeys, LANES), lambda b, i: (b, 0, 0)),
                  pl.BlockSpec((1, nkeys, LANES), lambda b, i: (b, 0, 0))],
        out_specs=pl.BlockSpec((1, tq, B_WIDTH), lambda b, i: (b, i, 0)),
        out_shape=jax.ShapeDtypeStruct((bsz, n, B_WIDTH), MXU_DTYPE),
        compiler_params=_cparams(("parallel", "parallel")),
        name="gqa_attention",
    )(q, k, v)


def _pack_rows(x):
    w = x.shape[1] // 2
    bits = lax.bitcast_convert_type(x.astype(jnp.bfloat16).astype(F32), jnp.uint32)
    packed = (bits[:, :w] & jnp.uint32(0xFFFF0000)) | (bits[:, w:] >> 16)
    return lax.bitcast_convert_type(packed, jnp.int32)


def _unpack_rows(p):
    bits = lax.bitcast_convert_type(p, jnp.uint32)
    hi = lax.bitcast_convert_type(bits & jnp.uint32(0xFFFF0000), F32)
    lo = lax.bitcast_convert_type(bits << 16, F32)
    return jnp.concatenate([hi, lo], axis=1)


def _store_chunks(ref, packed):
    for j in range(ref.shape[0]):
        ref[j] = packed[:, j * LANES:(j + 1) * LANES]


def _load_chunks(ref):
    return jnp.concatenate([ref[j] for j in range(ref.shape[0])], axis=1)


def _merge_kernel(*refs, aliased):
    if aliased:
        refs = refs[:13] + refs[15:]
    (h_ref, a_ref, b_ref, zg_ref, wa_ref, wb_ref, wo_ref, g1_ref, n2_ref, sc_ref, sh_ref, wr_ref, br_ref,
     ho_ref, x2_ref, lg_ref) = refs
    d = h_ref.shape[2]
    tm = h_ref.shape[1]
    sub = min(MERGE_SUB_ROWS, tm)
    for r in range(tm // sub):
        rs = slice(r * sub, (r + 1) * sub)
        ua = jnp.dot(a_ref[0, rs, :], wa_ref[...], preferred_element_type=F32)
        ub = jnp.dot(b_ref[0, rs, :], wb_ref[...], preferred_element_type=F32)
        ga = _sigmoid(zg_ref[0, rs, 0:d].astype(F32))
        gb = _sigmoid(zg_ref[0, rs, d:2 * d].astype(F32))
        mix = (ga * ua + gb * ub).astype(MXU_DTYPE)
        h_new = h_ref[0, rs, :] + g1_ref[0] * jnp.dot(mix, wo_ref[...], preferred_element_type=F32)
        ho_ref[0, rs, :] = h_new
        x2 = _modulated_norm(h_new, n2_ref[...], sc_ref[0], sh_ref[0])
        packed = _pack_rows(x2)
        for j in range(x2_ref.shape[0]):
            x2_ref[j, rs, :] = packed[:, j * LANES:(j + 1) * LANES]
        x_hi = x2.astype(jnp.bfloat16)
        x_lo = (x2 - x_hi.astype(F32)).astype(jnp.bfloat16)
        hi_both = jnp.dot(x_hi, wr_ref[...], preferred_element_type=F32)
        lo_hi = jnp.dot(x_lo, wr_ref[:, 0:LANES], preferred_element_type=F32)
        lg_ref[rs, :] = hi_both[:, 0:LANES] + hi_both[:, LANES:] + lo_hi + br_ref[...]


def merge_proj(h, a, b, z, w_up_a, w_up_b, w_out, g1, norm2_g, sc2, sh2, w_router, b_router, t_total, tok_off,
               bufs=None):
    bsz, n, d = h.shape
    tm = min(MERGE_ROWS, n)
    nt = n // tm
    off = tok_off // tm
    n_chunks = d // (2 * LANES)
    row = lambda w: pl.BlockSpec((1, tm, w), lambda b_, i: (b_, i, 0))
    full = lambda s: pl.BlockSpec(s, lambda b_, i: (0,) * len(s))
    per_b = pl.BlockSpec((1, 1, d), lambda b_, i: (b_, 0, 0))
    in_specs = [row(d), row(A_WIDTH), row(B_WIDTH),
                pl.BlockSpec((1, tm, 2 * d), lambda b_, i: (b_, i, OFF_GATES)),
                full((A_WIDTH, d)), full((B_WIDTH, d)), full((d, d)),
                per_b, full((1, d)), per_b, per_b, full((d, 2 * LANES)), full((1, LANES))]
    args = [h, a, b, z, w_up_a, w_up_b, w_out, g1, norm2_g.reshape(1, d), sc2, sh2, w_router, b_router]
    aliases = {}
    if bufs is not None:
        in_specs += [pl.BlockSpec(memory_space=pl.ANY)] * 2
        args += list(bufs)
        aliases = {13: 1, 14: 2}
    return pl.pallas_call(
        functools.partial(_merge_kernel, aliased=bufs is not None),
        grid=(bsz, nt),
        in_specs=in_specs,
        out_specs=[row(d),
                   pl.BlockSpec((n_chunks, tm, LANES), lambda b_, i: (0, off + b_ * nt + i, 0)),
                   pl.BlockSpec((tm, LANES), lambda b_, i: (off + b_ * nt + i, 0))],
        out_shape=[jax.ShapeDtypeStruct((bsz, n, d), F32),
                   jax.ShapeDtypeStruct((n_chunks, t_total, LANES), jnp.int32),
                   jax.ShapeDtypeStruct((t_total, LANES), F32)],
        input_output_aliases=aliases,
        compiler_params=_cparams(("parallel", "parallel")),
        name="merge_proj",
    )(*args)


def _first_lane_where(cond, lane):
    return jnp.min(jnp.where(cond, lane, LANES), axis=-1, keepdims=True)


def _router_kernel(lg_ref, slab_ref, slab_t_ref, cnt_ref, base):
    @pl.when(pl.program_id(0) == 0)
    def _():
        base[...] = jnp.zeros_like(base)

    lg = lg_ref[...]
    tr = lg.shape[0]
    lane = lax.broadcasted_iota(jnp.int32, lg.shape, 1)
    neg = jnp.float32(-jnp.inf)
    is_g = lane < N_GROUPS
    gl = jnp.where(is_g, lg, neg)
    gmax = jnp.max(gl, axis=-1, keepdims=True)
    gsel = _first_lane_where(gl == gmax, lane)
    p_group = 1.0 / jnp.sum(jnp.where(is_g, jnp.exp(lg - gmax), 0.0), axis=-1, keepdims=True)
    lo = N_GROUPS + gsel * EXPERTS_PER_GROUP
    in_grp = (lane >= lo) & (lane < lo + EXPERTS_PER_GROUP)
    el = jnp.where(in_grp, lg, neg)
    m1 = jnp.max(el, axis=-1, keepdims=True)
    i1 = _first_lane_where(el == m1, lane)
    el2 = jnp.where(lane == i1, neg, el)
    m2 = jnp.max(el2, axis=-1, keepdims=True)
    i2 = _first_lane_where(el2 == m2, lane)
    zsum = jnp.sum(jnp.where(in_grp, jnp.exp(lg - m1), 0.0), axis=-1, keepdims=True)
    p1 = 1.0 / zsum
    p2 = jnp.exp(m2 - m1) / zsum
    w1 = p1 / (p1 + p2) * p_group
    w2 = p2 / (p1 + p2) * p_group
    hit1 = lane == i1
    hit2 = lane == i2
    onehot = (hit1 | hit2).astype(F32)
    r_i = lax.broadcasted_iota(jnp.int32, (tr, tr), 0)
    c_i = lax.broadcasted_iota(jnp.int32, (tr, tr), 1)
    before = (c_i < r_i).astype(MXU_DTYPE)
    prior = jnp.dot(before, onehot.astype(MXU_DTYPE), preferred_element_type=F32) + base[0:1, :]
    r1 = jnp.sum(jnp.where(hit1, prior, 0.0), axis=-1, keepdims=True)
    r2 = jnp.sum(jnp.where(hit2, prior, 0.0), axis=-1, keepdims=True)
    new_base = base[0:1, :] + jnp.sum(onehot, axis=0, keepdims=True)
    base[...] = jnp.broadcast_to(new_base, base.shape)
    cnt_ref[...] = jnp.broadcast_to(new_base, cnt_ref.shape)
    vals = [(i1 - N_GROUPS).astype(F32), (i2 - N_GROUPS).astype(F32), w1, w2, r1, r2]
    slab = jnp.zeros_like(lg)
    for idx, val in enumerate(vals):
        slab = jnp.where(lane == idx, val, slab)
    slab_ref[...] = slab
    slab_t_ref[...] = slab.T[0:8, :]


def route(logits):
    t = logits.shape[0]
    tr = next(c for c in (1024, 512, 256, 128) if t % c == 0)
    return pl.pallas_call(
        _router_kernel,
        grid=(t // tr,),
        in_specs=[pl.BlockSpec((tr, LANES), lambda i: (i, 0))],
        out_specs=[pl.BlockSpec((tr, LANES), lambda i: (i, 0)), pl.BlockSpec((8, tr), lambda i: (0, i)),
                   pl.BlockSpec((8, LANES), lambda i: (0, 0))],
        out_shape=[jax.ShapeDtypeStruct((t, LANES), F32), jax.ShapeDtypeStruct((8, t), F32),
                   jax.ShapeDtypeStruct((8, LANES), F32)],
        scratch_shapes=[pltpu.VMEM((8, LANES), F32)],
        compiler_params=_cparams(("arbitrary",)),
        name="moe_route",
    )(logits)


def _expert_kernel(be_ref, nv_ref, x_ref, wg_ref, wu_ref, wd_ref, o_ref, wg_s, wu_s, wd_s):
    j = pl.program_id(0)
    nv = nv_ref[j]

    @pl.when((j == 0) | (be_ref[j] != be_ref[jnp.maximum(j - 1, 0)]))
    def _():
        wg_s[...] = wg_ref[0, 0].astype(wg_s.dtype)
        wu_s[...] = wu_ref[0, 0].astype(wu_s.dtype)
        wd_s[...] = wd_ref[0, 0].astype(wd_s.dtype)

    @pl.when(nv > 0)
    def _():
        tmx = x_ref.shape[1]
        sub = min(EXPERT_SUB_ROWS, tmx)
        for r in range(tmx // sub):
            rs = slice(r * sub, (r + 1) * sub)
            x = _unpack_rows(jnp.concatenate([x_ref[j, rs, :] for j in range(x_ref.shape[0])], axis=1))
            rows = lax.broadcasted_iota(jnp.int32, x.shape, 0) + r * sub
            x = jnp.where(rows < nv, x, 0.0).astype(MXU_DTYPE)
            hg = jnp.dot(x, wg_s[...], preferred_element_type=F32)
            hu = jnp.dot(x, wu_s[...], preferred_element_type=F32)
            hb = (hg * _sigmoid(hg) * hu).astype(MXU_DTYPE)
            packed = _pack_rows(jnp.dot(hb, wd_s[...], preferred_element_type=F32))
            for j in range(o_ref.shape[0]):
                o_ref[j, rs, :] = packed[:, j * LANES:(j + 1) * LANES]

    @pl.when(nv == 0)
    def _():
        o_ref[...] = jnp.zeros_like(o_ref)


def expert_ffn(x_sorted, block_e, nvalid, w_gate, w_up, w_down, layer, tmx):
    n_chunks, r, _ = x_sorted.shape
    _, _, d, de = w_gate.shape
    grid_spec = pltpu.PrefetchScalarGridSpec(
        num_scalar_prefetch=2,
        grid=(r // tmx,),
        in_specs=[pl.BlockSpec((n_chunks, tmx, LANES), lambda j, be, nv: (0, j, 0)),
                  pl.BlockSpec((1, 1, d, de), lambda j, be, nv: (layer, be[j], 0, 0)),
                  pl.BlockSpec((1, 1, d, de), lambda j, be, nv: (layer, be[j], 0, 0)),
                  pl.BlockSpec((1, 1, de, d), lambda j, be, nv: (layer, be[j], 0, 0))],
        out_specs=pl.BlockSpec((n_chunks, tmx, LANES), lambda j, be, nv: (0, j, 0)),
        scratch_shapes=[pltpu.VMEM((d, de), MXU_DTYPE), pltpu.VMEM((d, de), MXU_DTYPE),
                        pltpu.VMEM((de, d), MXU_DTYPE)],
    )
    return pl.pallas_call(
        _expert_kernel,
        grid_spec=grid_spec,
        out_shape=jax.ShapeDtypeStruct((n_chunks, r, LANES), jnp.int32),
        compiler_params=_cparams(("arbitrary",)),
        name="expert_ffn",
    )(block_e, nvalid, x_sorted, w_gate, w_up, w_down)


def _combine_kernel(*refs, final):
    if final:
        h_ref, y1_ref, y2_ref, slab_ref, g2_ref, fn_ref, o_ref = refs
    else:
        h_ref, y1_ref, y2_ref, slab_ref, g2_ref, o_ref = refs
    w1 = slab_ref[:, 2:3]
    w2 = slab_ref[:, 3:4]
    y = _unpack_rows(_load_chunks(y1_ref)) * w1 + _unpack_rows(_load_chunks(y2_ref)) * w2
    h_new = h_ref[0] + g2_ref[0] * y
    if final:
        ms = jnp.mean(h_new * h_new, axis=-1, keepdims=True)
        h_new = h_new * lax.rsqrt(ms + EPS) * fn_ref[...]
    o_ref[0] = h_new


def moe_combine(h, y1, y2, slab, g2, tok_off, final_g=None):
    bsz, n, d = h.shape
    n_chunks = y1.shape[0]
    tm = min(512, n)
    nt = n // tm
    off = tok_off // tm
    final = final_g is not None
    yspec = pl.BlockSpec((n_chunks, tm, LANES), lambda b, i: (0, off + b * nt + i, 0))
    in_specs = [pl.BlockSpec((1, tm, d), lambda b, i: (b, i, 0)), yspec, yspec,
                pl.BlockSpec((tm, LANES), lambda b, i: (off + b * nt + i, 0)),
                pl.BlockSpec((1, 1, d), lambda b, i: (b, 0, 0))]
    args = [h, y1, y2, slab, g2]
    if final:
        in_specs.append(pl.BlockSpec((1, d), lambda b, i: (0, 0)))
        args.append(final_g.reshape(1, d))
    return pl.pallas_call(
        functools.partial(_combine_kernel, final=final),
        grid=(bsz, nt),
        in_specs=in_specs,
        out_specs=pl.BlockSpec((1, tm, d), lambda b, i: (b, i, 0)),
        out_shape=jax.ShapeDtypeStruct((bsz, n, d), F32),
        compiler_params=_cparams(("parallel", "parallel")),
        name="moe_combine_final" if final else "moe_combine",
    )(*args)


SC_WINDOW = 128


def _sc_mesh():
    return plsc.VectorSubcoreMesh(core_axis_name="core", subcore_axis_name="subcore")


def sc_gather_rows(table, idx):
    n = idx.shape[0]

    @pl.kernel(out_type=jax.ShapeDtypeStruct((n, LANES), table.dtype), mesh=_sc_mesh())
    def gather_kernel(t_hbm, i_hbm, o_hbm):
        def body(i_vmem, o_vmem):
            pltpu.sync_copy(t_hbm.at[i_vmem.at[0]], o_vmem)

        pltpu.emit_pipeline(
            body,
            grid=(n // SC_WINDOW,),
            in_specs=[pl.BlockSpec((1, SC_WINDOW), lambda i: (0, i))],
            out_specs=[pl.BlockSpec((SC_WINDOW, LANES), lambda i: (i, 0))],
            core_axis_name=("core", "subcore"),
            dimension_semantics=(pltpu.PARALLEL,),
            trace_scopes=False,
        )(i_hbm, o_hbm)

    return gather_kernel(table, idx.reshape(1, n))


def sc_scatter_rows2(rows, idx_a, idx_b, n_out):
    n = rows.shape[0]

    @pl.kernel(out_type=jax.ShapeDtypeStruct((n_out, LANES), rows.dtype), mesh=_sc_mesh())
    def scatter_kernel(r_hbm, ia_hbm, ib_hbm, o_hbm):
        def body(r_vmem, ia_vmem, ib_vmem):
            pltpu.sync_copy(r_vmem, o_hbm.at[ia_vmem.at[0]])
            pltpu.sync_copy(r_vmem, o_hbm.at[ib_vmem.at[0]])

        pltpu.emit_pipeline(
            body,
            grid=(n // SC_WINDOW,),
            in_specs=[pl.BlockSpec((SC_WINDOW, LANES), lambda i: (i, 0)),
                      pl.BlockSpec((1, SC_WINDOW), lambda i: (0, i)),
                      pl.BlockSpec((1, SC_WINDOW), lambda i: (0, i))],
            out_specs=[],
            core_axis_name=("core", "subcore"),
            dimension_semantics=(pltpu.PARALLEL,),
            trace_scopes=False,
        )(r_hbm, ia_hbm, ib_hbm)

    return scatter_kernel(rows, idx_a.reshape(1, n), idx_b.reshape(1, n))


def _plan_kernel(ps_ref, st_ref, ia_ref, ib_ref, *, n_rows):
    e = st_ref[0:2, :].astype(jnp.int32)
    dest = st_ref[4:6, :].astype(jnp.int32)
    for ex in range(N_EXPERTS):
        dest = dest + jnp.where(e == ex, ps_ref[ex], 0)
    for k, ref in enumerate((ia_ref, ib_ref)):
        for j in range(ref.shape[0]):
            ref[j:j + 1, :] = dest[k:k + 1, :] + j * n_rows


def dispatch_indices(slab_t, pad_start, n_chunks, n_rows):
    t = slab_t.shape[1]
    tp = next(c for c in (4096, 2048, 1024, 512, 256, 128) if t % c == 0)
    grid_spec = pltpu.PrefetchScalarGridSpec(
        num_scalar_prefetch=1,
        grid=(t // tp,),
        in_specs=[pl.BlockSpec((8, tp), lambda i, ps: (0, i))],
        out_specs=[pl.BlockSpec((n_chunks, tp), lambda i, ps: (0, i))] * 2,
    )
    return pl.pallas_call(
        functools.partial(_plan_kernel, n_rows=n_rows),
        grid_spec=grid_spec,
        out_shape=[jax.ShapeDtypeStruct((n_chunks, t), jnp.int32)] * 2,
        compiler_params=_cparams(("parallel",)),
        name="moe_plan",
    )(pad_start, slab_t)


def hier_moe(x2, logits, w_gate, w_up, w_down, layer):
    n_chunks, t, _ = x2.shape
    tmx = min(512, t)
    n_blocks = (t * TOP_K + N_EXPERTS * (tmx - 1)) // tmx + 1
    r = n_blocks * tmx
    slab, slab_t, counts = route(logits)
    cnt = counts[0, N_GROUPS:N_GROUPS + N_EXPERTS].astype(jnp.int32)
    padded = (cnt + tmx - 1) // tmx * tmx
    pad_end = jnp.cumsum(padded)
    pad_start = pad_end - padded
    blk_start = jnp.arange(n_blocks, dtype=jnp.int32) * tmx
    block_e = jnp.minimum(jnp.sum(pad_end[None, :] <= blk_start[:, None], axis=1), N_EXPERTS - 1).astype(jnp.int32)
    nvalid = jnp.clip(pad_start[block_e] + cnt[block_e] - blk_start, 0, tmx).astype(jnp.int32)
    idx1, idx2 = dispatch_indices(slab_t, pad_start.astype(jnp.int32), n_chunks, r)
    idx1, idx2 = idx1.reshape(-1), idx2.reshape(-1)
    x_sorted = sc_scatter_rows2(x2.reshape(n_chunks * t, LANES), idx1, idx2, n_chunks * r)
    y_sorted = expert_ffn(x_sorted.reshape(n_chunks, r, LANES), block_e, nvalid, w_gate, w_up, w_down, layer, tmx)
    y_flat = y_sorted.reshape(n_chunks * r, LANES)
    y1 = sc_gather_rows(y_flat, idx1).reshape(n_chunks, t, LANES)
    y2 = sc_gather_rows(y_flat, idx2).reshape(n_chunks, t, LANES)
    return y1, y2, slab


def _lower_bound_rows(lb_logits):
    lb = jnp.cumsum(jax.nn.softmax(lb_logits.astype(F32), axis=1), axis=1)
    lb = lb - lb[:, :1]
    rows = jnp.stack([jnp.log(lb), jnp.log1p(-lb), 1.0 - lb], axis=2)
    rows = jnp.concatenate([rows, jnp.zeros(rows.shape[:2] + (5, A_WIDTH), F32)], axis=2)
    return rows


def _cast_kernel(w_ref, o_ref):
    o_ref[...] = w_ref[...].astype(o_ref.dtype)


def reorder_cast_w_in(w_in):
    depth, d, n = w_in.shape
    gate_cols = 2 * d
    blk = 256
    nb = n // blk
    shift = (n - gate_cols) // blk
    return pl.pallas_call(
        _cast_kernel,
        grid=(depth, nb),
        in_specs=[pl.BlockSpec((1, d, blk), lambda l, j: (l, 0, (j + shift) % nb))],
        out_specs=pl.BlockSpec((1, d, blk), lambda l, j: (l, 0, j)),
        out_shape=jax.ShapeDtypeStruct((depth, d, n), MXU_DTYPE),
        compiler_params=_cparams(("parallel", "parallel")),
        name="reorder_cast_w_in",
    )(w_in)


def kernel(x, c, ctx, c_ctx, w_ada, b_ada, norm1_g, norm2_g, w_in, lb_logits, hgrn_norm_g, q_norm_g, k_norm_g,
           w_up_a, w_up_b, w_out, w_router_group, b_router_group, w_router_expert, b_router_expert, w_gate, w_up,
           w_down, final_norm_g):
    bsz, n, d = x.shape
    n_ctx = ctx.shape[1]
    depth = w_in.shape[0]
    mx = MXU_DTYPE

    w_in_r = reorder_cast_w_in(w_in)
    w_up_a_c, w_up_b_c, w_out_c = w_up_a.astype(mx), w_up_b.astype(mx), w_out.astype(mx)
    n_r = N_GROUPS + N_EXPERTS
    w_router = jnp.concatenate([w_router_group, w_router_expert, jnp.zeros((depth, d, LANES - n_r), F32)], axis=-1)
    w_router_hi = w_router.astype(jnp.bfloat16)
    w_router_lo = (w_router - w_router_hi.astype(F32)).astype(jnp.bfloat16)
    w_router = jnp.concatenate([w_router_hi, w_router_lo], axis=-1)
    b_router = jnp.concatenate([b_router_group, b_router_expert, jnp.zeros((depth, LANES - n_r), F32)], axis=-1)
    lbp = _lower_bound_rows(lb_logits)
    cos_t, sin_t = rope_tables(n)

    n_rows = -(-(bsz + 1) // 8) * 8
    act = jnp.concatenate([c, c_ctx[None, :], jnp.zeros((n_rows - bsz - 1, d), F32)], axis=0)
    mod = ada_mod(act, w_ada, b_ada)

    h, hc = x, ctx
    zero_state = jnp.zeros((bsz, A_HEADS, LANES, LANES), F32)
    for l in range(depth):
        need_ctx = l < depth - 1
        ml = mod[l, :bsz].reshape(bsz, 1, N_MOD, d)
        mc = jnp.broadcast_to(mod[l, bsz].reshape(1, 1, N_MOD, d), (bsz, 1, N_MOD, d))
        sh1, sc1, g1, sh2, sc2, g2 = (ml[:, :, i] for i in range(N_MOD))
        csh1, csc1, cg1, csh2, csc2, cg2 = (mc[:, :, i] for i in range(N_MOD))

        zl, q_l, k_l, v_l = in_proj(h, norm1_g[l], sc1, sh1, w_in_r, l, q_norm_g[l], k_norm_g[l], cos_t, sin_t)
        zc, q_c, k_c, v_c = in_proj(hc, norm1_g[l], csc1, csh1, w_in_r, l, q_norm_g[l], k_norm_g[l])
        ob_c, s_bwd = hgrn_scan(zc, lbp[1, l], zero_state, True)
        if need_ctx:
            a_c, s_fwd = hgrn_scan(zc, lbp[0, l], zero_state, False, ob_c, hgrn_norm_g[l])
        else:
            _, s_fwd = hgrn_scan(zc, lbp[0, l], zero_state, False)
        ob_l, _ = hgrn_scan(zl, lbp[1, l], s_bwd, True)
        a_l, _ = hgrn_scan(zl, lbp[0, l], s_fwd, False, ob_l, hgrn_norm_g[l])

        b_l = attention(q_l, jnp.concatenate([k_c, k_l], axis=1), jnp.concatenate([v_c, v_l], axis=1))

        t_total = bsz * (n + n_ctx) if need_ctx else bsz * n
        bufs = (jnp.zeros((d // (2 * LANES), t_total, LANES), jnp.int32),
                jnp.zeros((t_total, LANES), F32)) if need_ctx else None
        h, x2, lg = merge_proj(h, a_l, b_l, zl, w_up_a_c[l], w_up_b_c[l], w_out_c[l], g1, norm2_g[l], sc2, sh2,
                               w_router[l], b_router[l][None, :], t_total, 0, bufs)
        if need_ctx:
            b_c = attention(q_c, k_c, v_c)
            hc, x2, lg = merge_proj(hc, a_c, b_c, zc, w_up_a_c[l], w_up_b_c[l], w_out_c[l], cg1, norm2_g[l],
                                    csc2, csh2, w_router[l], b_router[l][None, :], t_total, bsz * n, (x2, lg))

        y1, y2, slab = hier_moe(x2, lg, w_gate, w_up, w_down, l)
        h = moe_combine(h, y1, y2, slab, g2, 0, None if need_ctx else final_norm_g)
        if need_ctx:
            hc = moe_combine(hc, y1, y2, slab, cg2, bsz * n)
    return h
```

```python
import functools

import jax
import jax.numpy as jnp
from jax import lax
from jax.experimental import pallas as pl
from jax.experimental.pallas import tpu as pltpu
from jax.experimental.pallas import tpu_sc as plsc

F32 = jnp.float32
MXU_DTYPE = jnp.bfloat16

EPS = 1e-6
GRID_W = 64
ROPE_THETA = 10000.0
LANES = 128

A_HEADS = 4
A_DK = 128
A_WIDTH = A_HEADS * A_DK
HGRN_CHUNK = 64
HGRN_SUB = 16
HGRN_STEP_TOKENS = 1024
HGRN_EXP2_CLAMP = 86.0
B_HEADS = 8
B_KV_HEADS = 2
B_HD = 64
B_GROUP = B_HEADS // B_KV_HEADS
B_WIDTH = B_HEADS * B_HD
B_KV_WIDTH = B_KV_HEADS * B_HD
AXIS_DIM = B_HD // 2
N_GROUPS = 4
EXPERTS_PER_GROUP = 8
N_EXPERTS = N_GROUPS * EXPERTS_PER_GROUP
TOP_K = 2
N_MOD = 6

OFF_GATES = 0
OFF_QA = 2048
OFF_FF = OFF_QA + A_WIDTH
OFF_FB = OFF_FF + A_WIDTH
OFF_IA = OFF_FB + A_WIDTH
OFF_OG = OFF_IA + A_WIDTH
OFF_QB = OFF_OG + A_WIDTH
OFF_KB = OFF_QB + B_WIDTH
OFF_VB = OFF_KB + B_KV_WIDTH
D_IN = OFF_VB + B_KV_WIDTH

VMEM_LIMIT = 52 * 1024 * 1024
INPROJ_SUB_ROWS = 512
MERGE_ROWS = 1024
MERGE_SUB_ROWS = 256
EXPERT_SUB_ROWS = 512


def _cparams(sem):
    return pltpu.CompilerParams(dimension_semantics=sem, vmem_limit_bytes=VMEM_LIMIT)


LOG2E = 1.4426950408889634


def _sigmoid(x):
    return 0.5 * jnp.tanh(0.5 * x) + 0.5


def _nt_dot(a, b):
    return lax.dot_general(a, b, (((1,), (1,)), ((), ())), preferred_element_type=F32)


def _ada_kernel(a_ref, w_ref, b_ref, o_ref):
    a = a_ref[...]
    a = a * _sigmoid(a)
    o_ref[0] = jnp.dot(a.astype(MXU_DTYPE), w_ref[0].astype(MXU_DTYPE), preferred_element_type=F32) + b_ref[0]


def ada_mod(act, w_ada, b_ada):
    depth, d, n = w_ada.shape
    r = act.shape[0]
    tn = 1024
    return pl.pallas_call(
        _ada_kernel,
        grid=(depth, n // tn),
        in_specs=[
            pl.BlockSpec((r, d), lambda l, j: (0, 0)),
            pl.BlockSpec((1, d, tn), lambda l, j: (l, 0, j)),
            pl.BlockSpec((1, 1, tn), lambda l, j: (l, 0, j)),
        ],
        out_specs=pl.BlockSpec((1, r, tn), lambda l, j: (l, 0, j)),
        out_shape=jax.ShapeDtypeStruct((depth, r, n), F32),
        compiler_params=_cparams(("parallel", "parallel")),
        name="ada_mod",
    )(act, w_ada, b_ada.reshape(depth, 1, n))


def _modulated_norm(x, g, sc, sh):
    ms = jnp.mean(x * x, axis=-1, keepdims=True)
    return (x * lax.rsqrt(ms + EPS) * g) * (1.0 + sc) + sh


def _split3(x):
    hi = x.astype(jnp.bfloat16)
    r1 = x - hi.astype(F32)
    mid = r1.astype(jnp.bfloat16)
    lo = (r1 - mid.astype(F32)).astype(jnp.bfloat16)
    return hi, mid, lo


def _swap16(x):
    lane = lax.broadcasted_iota(jnp.int32, x.shape, 1)
    return jnp.where(lane % 32 < 16, pltpu.roll(x, LANES - 16, 1), pltpu.roll(x, 16, 1))


def _head_norm_rope(x, gsum, g, cos, sin):
    ssum = sum(jnp.dot(t, gsum, preferred_element_type=F32) for t in _split3(x * x))
    y = x * lax.rsqrt(ssum * (1.0 / B_HD) + EPS) * g
    if cos is not None:
        y = y * cos + _swap16(y) * sin
    return y


def _inproj_kernel(*refs, col_chunk, rope):
    if rope:
        (h_ref, g_ref, sc_ref, sh_ref, w_ref, qg_ref, kg_ref, cos_ref, sin_ref,
         z_ref, qo_ref, ko_ref, vo_ref, st_ref) = refs
        cos, sin = cos_ref[...], sin_ref[...]
    else:
        h_ref, g_ref, sc_ref, sh_ref, w_ref, qg_ref, kg_ref, z_ref, qo_ref, ko_ref, vo_ref, st_ref = refs
        cos = sin = None
    r_i = lax.broadcasted_iota(jnp.int32, (LANES, LANES), 0)
    c_i = lax.broadcasted_iota(jnp.int32, (LANES, LANES), 1)
    gsum = (r_i // B_HD == c_i // B_HD).astype(jnp.bfloat16)
    tm = h_ref.shape[1]
    sub = min(INPROJ_SUB_ROWS, tm)
    lane = lax.broadcasted_iota(jnp.int32, (sub, LANES), 1)
    stats = [None, None]
    for r in range(tm // sub):
        rs = slice(r * sub, (r + 1) * sub)
        y = _modulated_norm(h_ref[0, rs, :], g_ref[...], sc_ref[0], sh_ref[0]).astype(MXU_DTYPE)
        gqa = jnp.dot(y, w_ref[0, :, OFF_QB:], preferred_element_type=F32)
        cs_, sn_ = (cos[rs], sin[rs]) if rope else (None, None)
        for cb in range(B_WIDTH // LANES):
            x = gqa[:, cb * LANES:(cb + 1) * LANES]
            q = _head_norm_rope(x, gsum, qg_ref[...], cs_, sn_) * (B_HD ** -0.5 * LOG2E)
            swapped = pltpu.roll(q, B_HD, 1)
            for half in range(2):
                head = 2 * cb + half
                kv = head // B_GROUP
                src = q if half == kv else swapped
                keep = (lane // B_HD) == kv
                qo_ref[0, head, rs, :] = jnp.where(keep, src, 0.0).astype(qo_ref.dtype)
        kn = _head_norm_rope(gqa[:, B_WIDTH:B_WIDTH + B_KV_WIDTH], gsum, kg_ref[...], cs_, sn_)
        ko_ref[0, rs, :] = kn.astype(ko_ref.dtype)
        vo_ref[0, rs, :] = gqa[:, B_WIDTH + B_KV_WIDTH:].astype(vo_ref.dtype)
        for c in range(z_ref.shape[2] // col_chunk):
            cs = slice(c * col_chunk, (c + 1) * col_chunk)
            zc = jnp.dot(y, w_ref[0, :, cs], preferred_element_type=F32).astype(z_ref.dtype)
            z_ref[0, rs, cs] = zc
            for d, off in enumerate((OFF_FF, OFF_FB)):
                lo, hi = max(off, cs.start), min(off + A_WIDTH, cs.stop)
                if lo < hi:
                    zneg = jnp.minimum(zc[:, lo - cs.start:hi - cs.start].astype(F32), 0.0)
                    blocks = zneg.reshape(sub // HGRN_SUB, HGRN_SUB, hi - lo).sum(axis=1)
                    sums = jnp.min(blocks, axis=0, keepdims=True)
                    for lb in range(sums.shape[1] // LANES):
                        piece = sums[:, lb * LANES:(lb + 1) * LANES]
                        stats[d] = piece if stats[d] is None else jnp.minimum(stats[d], piece)
    st_ref[0, 0] = jnp.concatenate([stats[0], stats[1], jnp.zeros((6, LANES), F32)], axis=0)


def in_proj(h, norm_g, sc, sh, w_in, layer, q_norm_g, k_norm_g, cos_t=None, sin_t=None):
    bsz, n, d = h.shape
    tm = min(512, n)
    rope = cos_t is not None
    qg = jnp.tile(q_norm_g.astype(F32), LANES // B_HD).reshape(1, LANES)
    kg = jnp.tile(k_norm_g.astype(F32), LANES // B_HD).reshape(1, LANES)
    vec = pl.BlockSpec((1, LANES), lambda b, i: (0, 0))
    in_specs = [
        pl.BlockSpec((1, tm, d), lambda b, i: (b, i, 0)),
        pl.BlockSpec((1, d), lambda b, i: (0, 0)),
        pl.BlockSpec((1, 1, d), lambda b, i: (b, 0, 0)),
        pl.BlockSpec((1, 1, d), lambda b, i: (b, 0, 0)),
        pl.BlockSpec((1, d, D_IN), lambda b, i: (layer, 0, 0)),
        vec, vec,
    ]
    args = [h, norm_g.reshape(1, d), sc, sh, w_in, qg, kg]
    if rope:
        in_specs += [pl.BlockSpec((tm, LANES), lambda b, i: (i, 0))] * 2
        args += [cos_t, sin_t]
    row = pl.BlockSpec((1, tm, LANES), lambda b, i: (b, i, 0))
    return pl.pallas_call(
        functools.partial(_inproj_kernel, col_chunk=768, rope=rope),
        grid=(bsz, n // tm),
        in_specs=in_specs,
        out_specs=[pl.BlockSpec((1, tm, OFF_QB), lambda b, i: (b, i, 0)),
                   pl.BlockSpec((1, B_HEADS, tm, LANES), lambda b, i: (b, 0, i, 0)), row, row,
                   pl.BlockSpec((1, 1, 8, LANES), lambda b, i: (b, i, 0, 0))],
        out_shape=[jax.ShapeDtypeStruct((bsz, n, OFF_QB), MXU_DTYPE),
                   jax.ShapeDtypeStruct((bsz, B_HEADS, n, LANES), MXU_DTYPE),
                   jax.ShapeDtypeStruct((bsz, n, LANES), MXU_DTYPE),
                   jax.ShapeDtypeStruct((bsz, n, LANES), MXU_DTYPE),
                   jax.ShapeDtypeStruct((bsz, n // tm, 8, LANES), F32)],
        compiler_params=_cparams(("parallel", "parallel")),
        name="in_proj_rope" if rope else "in_proj",
    )(*args)


def rope_tables(n):
    t = jnp.arange(n)
    r = (t // GRID_W).astype(F32)
    col = (t % GRID_W).astype(F32)
    inv = ROPE_THETA ** (-jnp.arange(0, AXIS_DIM, 2, dtype=F32) / AXIS_DIM)
    ang_r = r[:, None] * inv
    ang_c = col[:, None] * inv
    cos = jnp.concatenate([jnp.cos(ang_r)] * 2 + [jnp.cos(ang_c)] * 2, axis=1)
    sin = jnp.concatenate([-jnp.sin(ang_r), jnp.sin(ang_r), -jnp.sin(ang_c), jnp.sin(ang_c)], axis=1)
    return jnp.tile(cos, (1, 2)), jnp.tile(sin, (1, 2))


def _hgrn_gates(zf, loglb, log1m, onem):
    l1pe = jnp.log(1.0 + jnp.exp(-jnp.abs(zf)))
    c = log1m + (jnp.minimum(zf, 0.0) - l1pe)
    logf = jnp.maximum(loglb, c) + jnp.log(1.0 + jnp.exp(-jnp.abs(loglb - c)))
    k = onem * jnp.exp(jnp.minimum(-zf, 0.0) - l1pe)
    return k, logf * LOG2E


def _cumsum_rows(cum_mat, x):
    out = jnp.dot(cum_mat, jnp.concatenate(_split3(x), axis=1), preferred_element_type=F32)
    w = x.shape[1]
    return out[:, :w] + out[:, w:2 * w] + out[:, 2 * w:]


def _hgrn_kernel(*refs, rev, n_chunks, fuse):
    if fuse:
        safe_ref, q_ref, f_ref, v_ref, lbp_ref, s0_ref, ob_ref, og_ref, ng_ref, o_ref, sT_ref, state = refs
    else:
        safe_ref, q_ref, f_ref, v_ref, lbp_ref, s0_ref, o_ref, sT_ref, state = refs
        ob_ref = og_ref = ng_ref = None
    i = pl.program_id(1)
    nt = pl.num_programs(1)

    @pl.when(i == 0)
    def _():
        state[...] = s0_ref[0]

    loglb = lbp_ref[0:1, :]
    log1m = lbp_ref[1:2, :]
    onem = lbp_ref[2:3, :]
    c = HGRN_CHUNK
    nsub = c // HGRN_SUB
    r_i = lax.broadcasted_iota(jnp.int32, (c, c), 0)
    c_i = lax.broadcasted_iota(jnp.int32, (c, c), 1)
    causal = (c_i >= r_i) if rev else (c_i <= r_i)
    cum_mat = causal.astype(jnp.bfloat16)
    row_blk = lax.broadcasted_iota(jnp.int32, (c, A_DK), 0) // HGRN_SUB

    def prepare(sl, exact):
        k, logf = _hgrn_gates(f_ref[0, sl, :].astype(F32), loglb, log1m, onem)
        zq = q_ref[0, sl, :].astype(F32)
        q = zq * _sigmoid(zq) * (A_DK ** -0.5)
        cum = _cumsum_rows(cum_mat, logf)
        tot = cum[0:1] if rev else cum[c - 1:c]
        qd = (q * jnp.exp2(cum)).astype(MXU_DTYPE)
        kd = (k * jnp.exp2(tot - cum)).astype(MXU_DTYPE)
        refs_j = []
        for j in range(nsub):
            if rev:
                r = cum[(j + 1) * HGRN_SUB:(j + 1) * HGRN_SUB + 1] if j < nsub - 1 else jnp.zeros_like(tot)
            else:
                r = cum[j * HGRN_SUB - 1:j * HGRN_SUB] if j > 0 else jnp.zeros_like(tot)
            refs_j.append(r)
        ref_rows = jnp.concatenate([jnp.broadcast_to(r, (HGRN_SUB, r.shape[1])) for r in refs_j], axis=0)
        qt = (q * jnp.exp2(cum - ref_rows)).astype(MXU_DTYPE)
        if exact:
            blk512 = lax.broadcasted_iota(jnp.int32, cum.shape, 0) // HGRN_SUB
            kt = [jnp.where((blk512 > j) if rev else (blk512 < j),
                            k * jnp.exp2(jnp.minimum(r - cum, 0.0)), 0.0).astype(MXU_DTYPE)
                  for j, r in enumerate(refs_j)]
            return qd, kd, jnp.exp2(tot), qt, kt, (q, k, cum)
        kt = [(k * jnp.exp2(jnp.minimum(r - cum, HGRN_EXP2_CLAMP))).astype(MXU_DTYPE) for r in refs_j]
        return qd, kd, jnp.exp2(tot), qt, kt, None

    def exact_diagonal(q, k, cum, hs):
        lane16 = lax.broadcasted_iota(jnp.int32, (HGRN_SUB, HGRN_SUB), 1)
        strips = []
        for j in range(nsub):
            rows = slice(j * HGRN_SUB, (j + 1) * HGRN_SUB)
            qb, kb, cb = q[rows, hs], k[rows, hs], cum[rows, hs]
            blk = jnp.zeros((HGRN_SUB, HGRN_SUB), F32)
            for t in range(HGRN_SUB):
                w = qb * kb[t:t + 1] * jnp.exp2(jnp.minimum(cb - cb[t:t + 1], 0.0))
                blk = jnp.where(lane16 == t, jnp.sum(w, axis=-1, keepdims=True), blk)
            pieces = [blk if jj == j else jnp.zeros((HGRN_SUB, HGRN_SUB), F32) for jj in range(nsub)]
            strips.append(jnp.concatenate(pieces, axis=1))
        return jnp.concatenate(strips, axis=0)

    def finish(sl, prepped, s):
        qd, kd, decay, qt, kt, raw = prepped
        v = v_ref[0, sl, :]
        outs = []
        for h in range(A_HEADS):
            hs = slice(h * A_DK, (h + 1) * A_DK)
            q_big = jnp.concatenate([jnp.where(row_blk == j, qt[:, hs], jnp.zeros_like(qt[:, hs]))
                                     for j in range(nsub)], axis=1)
            k_big = jnp.concatenate([kt[j][:, hs] for j in range(nsub)], axis=1)
            scores = _nt_dot(q_big, k_big)
            if raw is not None:
                scores = scores + exact_diagonal(*raw, hs)
            scores = jnp.where(causal, scores, 0.0)
            o_h = _nt_dot(qd[:, hs], s[h].astype(MXU_DTYPE))
            o_h = o_h + jnp.dot(scores.astype(MXU_DTYPE), v[:, hs], preferred_element_type=F32)
            upd = lax.dot_general(v[:, hs], kd[:, hs], (((0,), (0,)), ((), ())), preferred_element_type=F32)
            s[h] = s[h] * decay[:, hs] + upd
            if fuse:
                o_h = o_h + ob_ref[0, sl, hs]
                o_h = o_h * lax.rsqrt(jnp.mean(o_h * o_h, axis=-1, keepdims=True) + EPS)
            outs.append(o_h)
        o = jnp.concatenate(outs, axis=1)
        if fuse:
            og = og_ref[0, sl, :].astype(F32)
            o = o * ng_ref[...] * (og * _sigmoid(og))
        o_ref[0, sl, :] = o.astype(o_ref.dtype)

    safe = safe_ref[pl.program_id(0), (nt - 1 - i) if rev else i] != 0

    @pl.when(safe)
    def _():
        s = [state[h] for h in range(A_HEADS)]
        order = list(range(n_chunks - 1, -1, -1) if rev else range(n_chunks))
        slices = [slice(ci * c, (ci + 1) * c) for ci in order]
        ready = prepare(slices[0], False)
        for n, sl in enumerate(slices):
            prepped = ready
            if n + 1 < len(slices):
                ready = prepare(slices[n + 1], False)
            finish(sl, prepped, s)
        for h in range(A_HEADS):
            state[h] = s[h]

    @pl.when(jnp.logical_not(safe))
    def _():
        def chunk(n, carry):
            ci = (n_chunks - 1 - n) if rev else n
            sl = pl.ds(pl.multiple_of(ci * c, c), c)
            s = [state[h] for h in range(A_HEADS)]
            finish(sl, prepare(sl, True), s)
            for h in range(A_HEADS):
                state[h] = s[h]
            return carry

        lax.fori_loop(0, n_chunks, chunk, 0)

    @pl.when(i == pl.num_programs(1) - 1)
    def _():
        sT_ref[0] = state[...]


def hgrn_safe_flags(stats, direction, n, step_tokens):
    worst = stats[:, :, direction, :].min(axis=-1)
    bsz, nb = worst.shape
    per_step = nb * step_tokens // n
    worst = worst.reshape(bsz, nb // per_step, per_step).min(axis=-1)
    return (worst * LOG2E - HGRN_SUB >= -HGRN_EXP2_CLAMP).astype(jnp.int32)


def hgrn_scan(z, stats, lbp, s0, rev, o_other=None, norm_g=None):
    bsz, n, _ = z.shape
    fuse = o_other is not None
    t = min(HGRN_STEP_TOKENS, n)
    nt = n // t
    f_off = (OFF_FB if rev else OFF_FF) // A_WIDTH
    safe = hgrn_safe_flags(stats, 1 if rev else 0, n, t)

    def tmap(i):
        return nt - 1 - i if rev else i

    def col(blk):
        return pl.BlockSpec((1, t, A_WIDTH), lambda b, i, sf: (b, tmap(i), blk))

    state_spec = pl.BlockSpec((1, A_HEADS, LANES, LANES), lambda b, i, sf: (b, 0, 0, 0))
    in_specs = [col(OFF_QA // A_WIDTH), col(f_off), col(OFF_IA // A_WIDTH),
                pl.BlockSpec((8, A_WIDTH), lambda b, i, sf: (0, 0)), state_spec]
    args = [z, z, z, lbp, s0]
    if fuse:
        in_specs += [col(0), col(OFF_OG // A_WIDTH), pl.BlockSpec((1, A_WIDTH), lambda b, i, sf: (0, 0))]
        args += [o_other, z, norm_g.reshape(1, A_WIDTH)]
    kern = functools.partial(_hgrn_kernel, rev=rev, n_chunks=t // HGRN_CHUNK, fuse=fuse)
    grid_spec = pltpu.PrefetchScalarGridSpec(
        num_scalar_prefetch=1,
        grid=(bsz, nt),
        in_specs=in_specs,
        out_specs=[col(0), state_spec],
        scratch_shapes=[pltpu.VMEM((A_HEADS, LANES, LANES), F32)],
    )
    return pl.pallas_call(
        kern,
        grid_spec=grid_spec,
        out_shape=[jax.ShapeDtypeStruct((bsz, n, A_WIDTH), MXU_DTYPE if fuse else F32),
                   jax.ShapeDtypeStruct((bsz, A_HEADS, LANES, LANES), F32)],
        compiler_params=_cparams(("parallel", "arbitrary")),
        name="hgrn_bwd" if rev else "hgrn_fwd",
    )(safe, *args)


def _attn_kernel(q_ref, k_ref, v_ref, o_ref):
    tq = q_ref.shape[2]
    rows = B_GROUP * tq
    lane = lax.broadcasted_iota(jnp.int32, v_ref.shape[1:], 1)
    lane_o = lax.broadcasted_iota(jnp.int32, (tq, LANES), 1)
    s_all = _nt_dot(q_ref[0].reshape(B_HEADS * tq, LANES), k_ref[0])
    for kv in range(B_KV_HEADS):
        s = s_all[kv * rows:(kv + 1) * rows]
        vs = jnp.where((lane // B_HD) == kv, v_ref[0], jnp.ones_like(v_ref[0]))
        p = jnp.exp2(s - jnp.max(s, axis=-1, keepdims=True))
        a = jnp.dot(p.astype(MXU_DTYPE), vs, preferred_element_type=F32)
        o_full = a / pltpu.roll(a, B_HD, 1)
        o_swapped = pltpu.roll(o_full, B_HD, 1)
        for pair in range(B_GROUP // 2):
            g0, g1 = 2 * pair, 2 * pair + 1
            lo_src = o_full if kv == 0 else o_swapped
            hi_src = o_swapped if kv == 0 else o_full
            blk = jnp.where(lane_o < B_HD, lo_src[g0 * tq:(g0 + 1) * tq], hi_src[g1 * tq:(g1 + 1) * tq])
            cb = kv * (B_GROUP // 2) + pair
            o_ref[0, :, cb * LANES:(cb + 1) * LANES] = blk.astype(o_ref.dtype)


def attention(q, k, v):
    bsz, _, n, _ = q.shape
    nkeys = k.shape[1]
    tq = min(128, n)
    return pl.pallas_call(
        _attn_kernel,
        grid=(bsz, n // tq),
        in_specs=[pl.BlockSpec((1, B_HEADS, tq, LANES), lambda b, i: (b, 0, i, 0)),
                  pl.BlockSpec((1, nkeys, LANES), lambda b, i: (b, 0, 0)),
                  pl.BlockSpec((1, nkeys, LANES), lambda b, i: (b, 0, 0))],
        out_specs=pl.BlockSpec((1, tq, B_WIDTH), lambda b, i: (b, i, 0)),
        out_shape=jax.ShapeDtypeStruct((bsz, n, B_WIDTH), MXU_DTYPE),
        compiler_params=_cparams(("parallel", "parallel")),
        name="gqa_attention",
    )(q, k, v)


def _pack_rows(x):
    w = x.shape[1] // 2
    bits = lax.bitcast_convert_type(x.astype(jnp.bfloat16).astype(F32), jnp.uint32)
    packed = (bits[:, :w] & jnp.uint32(0xFFFF0000)) | (bits[:, w:] >> 16)
    return lax.bitcast_convert_type(packed, jnp.int32)


def _unpack_rows(p):
    bits = lax.bitcast_convert_type(p, jnp.uint32)
    hi = lax.bitcast_convert_type(bits & jnp.uint32(0xFFFF0000), F32)
    lo = lax.bitcast_convert_type(bits << 16, F32)
    return jnp.concatenate([hi, lo], axis=1)


def _store_chunks(ref, packed):
    for j in range(ref.shape[0]):
        ref[j] = packed[:, j * LANES:(j + 1) * LANES]


def _load_chunks(ref):
    return jnp.concatenate([ref[j] for j in range(ref.shape[0])], axis=1)


def _merge_kernel(*refs, aliased):
    if aliased:
        refs = refs[:13] + refs[15:]
    (h_ref, a_ref, b_ref, zg_ref, wa_ref, wb_ref, wo_ref, g1_ref, n2_ref, sc_ref, sh_ref, wr_ref, br_ref,
     ho_ref, x2_ref, lg_ref) = refs
    d = h_ref.shape[2]
    tm = h_ref.shape[1]
    sub = min(MERGE_SUB_ROWS, tm)
    for r in range(tm // sub):
        rs = slice(r * sub, (r + 1) * sub)
        ua = jnp.dot(a_ref[0, rs, :], wa_ref[...], preferred_element_type=F32)
        ub = jnp.dot(b_ref[0, rs, :], wb_ref[...], preferred_element_type=F32)
        ga = _sigmoid(zg_ref[0, rs, 0:d].astype(F32))
        gb = _sigmoid(zg_ref[0, rs, d:2 * d].astype(F32))
        mix = (ga * ua + gb * ub).astype(MXU_DTYPE)
        h_new = h_ref[0, rs, :] + g1_ref[0] * jnp.dot(mix, wo_ref[...], preferred_element_type=F32)
        ho_ref[0, rs, :] = h_new
        x2 = _modulated_norm(h_new, n2_ref[...], sc_ref[0], sh_ref[0])
        packed = _pack_rows(x2)
        for j in range(x2_ref.shape[0]):
            x2_ref[j, rs, :] = packed[:, j * LANES:(j + 1) * LANES]
        x_hi = x2.astype(jnp.bfloat16)
        x_lo = (x2 - x_hi.astype(F32)).astype(jnp.bfloat16)
        hi_both = jnp.dot(x_hi, wr_ref[...], preferred_element_type=F32)
        lo_hi = jnp.dot(x_lo, wr_ref[:, 0:LANES], preferred_element_type=F32)
        lg_ref[rs, :] = hi_both[:, 0:LANES] + hi_both[:, LANES:] + lo_hi + br_ref[...]


def merge_proj(h, a, b, z, w_up_a, w_up_b, w_out, g1, norm2_g, sc2, sh2, w_router, b_router, t_total, tok_off,
               bufs=None):
    bsz, n, d = h.shape
    tm = min(MERGE_ROWS, n)
    nt = n // tm
    off = tok_off // tm
    n_chunks = d // (2 * LANES)
    row = lambda w: pl.BlockSpec((1, tm, w), lambda b_, i: (b_, i, 0))
    full = lambda s: pl.BlockSpec(s, lambda b_, i: (0,) * len(s))
    per_b = pl.BlockSpec((1, 1, d), lambda b_, i: (b_, 0, 0))
    in_specs = [row(d), row(A_WIDTH), row(B_WIDTH),
                pl.BlockSpec((1, tm, 2 * d), lambda b_, i: (b_, i, OFF_GATES)),
                full((A_WIDTH, d)), full((B_WIDTH, d)), full((d, d)),
                per_b, full((1, d)), per_b, per_b, full((d, 2 * LANES)), full((1, LANES))]
    args = [h, a, b, z, w_up_a, w_up_b, w_out, g1, norm2_g.reshape(1, d), sc2, sh2, w_router, b_router]
    aliases = {}
    if bufs is not None:
        in_specs += [pl.BlockSpec(memory_space=pl.ANY)] * 2
        args += list(bufs)
        aliases = {13: 1, 14: 2}
    return pl.pallas_call(
        functools.partial(_merge_kernel, aliased=bufs is not None),
        grid=(bsz, nt),
        in_specs=in_specs,
        out_specs=[row(d),
                   pl.BlockSpec((n_chunks, tm, LANES), lambda b_, i: (0, off + b_ * nt + i, 0)),
                   pl.BlockSpec((tm, LANES), lambda b_, i: (off + b_ * nt + i, 0))],
        out_shape=[jax.ShapeDtypeStruct((bsz, n, d), F32),
                   jax.ShapeDtypeStruct((n_chunks, t_total, LANES), jnp.int32),
                   jax.ShapeDtypeStruct((t_total, LANES), F32)],
        input_output_aliases=aliases,
        compiler_params=_cparams(("parallel", "parallel")),
        name="merge_proj",
    )(*args)


def _first_lane_where(cond, lane):
    return jnp.min(jnp.where(cond, lane, LANES), axis=-1, keepdims=True)


def _router_kernel(lg_ref, slab_ref, slab_t_ref, cnt_ref, base):
    @pl.when(pl.program_id(0) == 0)
    def _():
        base[...] = jnp.zeros_like(base)

    lg = lg_ref[...]
    tr = lg.shape[0]
    lane = lax.broadcasted_iota(jnp.int32, lg.shape, 1)
    neg = jnp.float32(-jnp.inf)
    is_g = lane < N_GROUPS
    gl = jnp.where(is_g, lg, neg)
    gmax = jnp.max(gl, axis=-1, keepdims=True)
    gsel = _first_lane_where(gl == gmax, lane)
    p_group = 1.0 / jnp.sum(jnp.where(is_g, jnp.exp(lg - gmax), 0.0), axis=-1, keepdims=True)
    lo = N_GROUPS + gsel * EXPERTS_PER_GROUP
    in_grp = (lane >= lo) & (lane < lo + EXPERTS_PER_GROUP)
    el = jnp.where(in_grp, lg, neg)
    m1 = jnp.max(el, axis=-1, keepdims=True)
    i1 = _first_lane_where(el == m1, lane)
    el2 = jnp.where(lane == i1, neg, el)
    m2 = jnp.max(el2, axis=-1, keepdims=True)
    i2 = _first_lane_where(el2 == m2, lane)
    zsum = jnp.sum(jnp.where(in_grp, jnp.exp(lg - m1), 0.0), axis=-1, keepdims=True)
    p1 = 1.0 / zsum
    p2 = jnp.exp(m2 - m1) / zsum
    w1 = p1 / (p1 + p2) * p_group
    w2 = p2 / (p1 + p2) * p_group
    hit1 = lane == i1
    hit2 = lane == i2
    onehot = (hit1 | hit2).astype(F32)
    r_i = lax.broadcasted_iota(jnp.int32, (tr, tr), 0)
    c_i = lax.broadcasted_iota(jnp.int32, (tr, tr), 1)
    before = (c_i < r_i).astype(MXU_DTYPE)
    prior = jnp.dot(before, onehot.astype(MXU_DTYPE), preferred_element_type=F32) + base[0:1, :]
    r1 = jnp.sum(jnp.where(hit1, prior, 0.0), axis=-1, keepdims=True)
    r2 = jnp.sum(jnp.where(hit2, prior, 0.0), axis=-1, keepdims=True)
    new_base = base[0:1, :] + jnp.sum(onehot, axis=0, keepdims=True)
    base[...] = jnp.broadcast_to(new_base, base.shape)
    cnt_ref[...] = jnp.broadcast_to(new_base, cnt_ref.shape)
    vals = [(i1 - N_GROUPS).astype(F32), (i2 - N_GROUPS).astype(F32), w1, w2, r1, r2]
    slab = jnp.zeros_like(lg)
    for idx, val in enumerate(vals):
        slab = jnp.where(lane == idx, val, slab)
    slab_ref[...] = slab
    slab_t_ref[...] = slab.T[0:8, :]


def route(logits):
    t = logits.shape[0]
    tr = next(c for c in (1024, 512, 256, 128) if t % c == 0)
    return pl.pallas_call(
        _router_kernel,
        grid=(t // tr,),
        in_specs=[pl.BlockSpec((tr, LANES), lambda i: (i, 0))],
        out_specs=[pl.BlockSpec((tr, LANES), lambda i: (i, 0)), pl.BlockSpec((8, tr), lambda i: (0, i)),
                   pl.BlockSpec((8, LANES), lambda i: (0, 0))],
        out_shape=[jax.ShapeDtypeStruct((t, LANES), F32), jax.ShapeDtypeStruct((8, t), F32),
                   jax.ShapeDtypeStruct((8, LANES), F32)],
        scratch_shapes=[pltpu.VMEM((8, LANES), F32)],
        compiler_params=_cparams(("arbitrary",)),
        name="moe_route",
    )(logits)


def _expert_kernel(be_ref, nv_ref, x_ref, wg_ref, wu_ref, wd_ref, o_ref, wg_s, wu_s, wd_s):
    j = pl.program_id(0)
    nv = nv_ref[j]

    @pl.when((j == 0) | (be_ref[j] != be_ref[jnp.maximum(j - 1, 0)]))
    def _():
        wg_s[...] = wg_ref[0, 0].astype(wg_s.dtype)
        wu_s[...] = wu_ref[0, 0].astype(wu_s.dtype)
        wd_s[...] = wd_ref[0, 0].astype(wd_s.dtype)

    @pl.when(nv > 0)
    def _():
        tmx = x_ref.shape[1]
        sub = min(EXPERT_SUB_ROWS, tmx)
        for r in range(tmx // sub):
            rs = slice(r * sub, (r + 1) * sub)
            x = _unpack_rows(jnp.concatenate([x_ref[j, rs, :] for j in range(x_ref.shape[0])], axis=1))
            rows = lax.broadcasted_iota(jnp.int32, x.shape, 0) + r * sub
            x = jnp.where(rows < nv, x, 0.0).astype(MXU_DTYPE)
            hg = jnp.dot(x, wg_s[...], preferred_element_type=F32)
            hu = jnp.dot(x, wu_s[...], preferred_element_type=F32)
            hb = (hg * _sigmoid(hg) * hu).astype(MXU_DTYPE)
            packed = _pack_rows(jnp.dot(hb, wd_s[...], preferred_element_type=F32))
            for j in range(o_ref.shape[0]):
                o_ref[j, rs, :] = packed[:, j * LANES:(j + 1) * LANES]

    @pl.when(nv == 0)
    def _():
        o_ref[...] = jnp.zeros_like(o_ref)


def expert_ffn(x_sorted, block_e, nvalid, w_gate, w_up, w_down, layer, tmx):
    n_chunks, r, _ = x_sorted.shape
    _, _, d, de = w_gate.shape
    grid_spec = pltpu.PrefetchScalarGridSpec(
        num_scalar_prefetch=2,
        grid=(r // tmx,),
        in_specs=[pl.BlockSpec((n_chunks, tmx, LANES), lambda j, be, nv: (0, j, 0)),
                  pl.BlockSpec((1, 1, d, de), lambda j, be, nv: (layer, be[j], 0, 0)),
                  pl.BlockSpec((1, 1, d, de), lambda j, be, nv: (layer, be[j], 0, 0)),
                  pl.BlockSpec((1, 1, de, d), lambda j, be, nv: (layer, be[j], 0, 0))],
        out_specs=pl.BlockSpec((n_chunks, tmx, LANES), lambda j, be, nv: (0, j, 0)),
        scratch_shapes=[pltpu.VMEM((d, de), MXU_DTYPE), pltpu.VMEM((d, de), MXU_DTYPE),
                        pltpu.VMEM((de, d), MXU_DTYPE)],
    )
    return pl.pallas_call(
        _expert_kernel,
        grid_spec=grid_spec,
        out_shape=jax.ShapeDtypeStruct((n_chunks, r, LANES), jnp.int32),
        compiler_params=_cparams(("arbitrary",)),
        name="expert_ffn",
    )(block_e, nvalid, x_sorted, w_gate, w_up, w_down)


def _combine_kernel(*refs, final):
    if final:
        h_ref, y1_ref, y2_ref, slab_ref, g2_ref, fn_ref, o_ref = refs
    else:
        h_ref, y1_ref, y2_ref, slab_ref, g2_ref, o_ref = refs
    w1 = slab_ref[:, 2:3]
    w2 = slab_ref[:, 3:4]
    y = _unpack_rows(_load_chunks(y1_ref)) * w1 + _unpack_rows(_load_chunks(y2_ref)) * w2
    h_new = h_ref[0] + g2_ref[0] * y
    if final:
        ms = jnp.mean(h_new * h_new, axis=-1, keepdims=True)
        h_new = h_new * lax.rsqrt(ms + EPS) * fn_ref[...]
    o_ref[0] = h_new


def moe_combine(h, y1, y2, slab, g2, tok_off, final_g=None):
    bsz, n, d = h.shape
    n_chunks = y1.shape[0]
    tm = min(512, n)
    nt = n // tm
    off = tok_off // tm
    final = final_g is not None
    yspec = pl.BlockSpec((n_chunks, tm, LANES), lambda b, i: (0, off + b * nt + i, 0))
    in_specs = [pl.BlockSpec((1, tm, d), lambda b, i: (b, i, 0)), yspec, yspec,
                pl.BlockSpec((tm, LANES), lambda b, i: (off + b * nt + i, 0)),
                pl.BlockSpec((1, 1, d), lambda b, i: (b, 0, 0))]
    args = [h, y1, y2, slab, g2]
    if final:
        in_specs.append(pl.BlockSpec((1, d), lambda b, i: (0, 0)))
        args.append(final_g.reshape(1, d))
    return pl.pallas_call(
        functools.partial(_combine_kernel, final=final),
        grid=(bsz, nt),
        in_specs=in_specs,
        out_specs=pl.BlockSpec((1, tm, d), lambda b, i: (b, i, 0)),
        out_shape=jax.ShapeDtypeStruct((bsz, n, d), F32),
        compiler_params=_cparams(("parallel", "parallel")),
        name="moe_combine_final" if final else "moe_combine",
    )(*args)


SC_WINDOW = 128


def _sc_mesh():
    return plsc.VectorSubcoreMesh(core_axis_name="core", subcore_axis_name="subcore")


def sc_gather_rows(table, idx):
    n = idx.shape[0]

    @pl.kernel(out_type=jax.ShapeDtypeStruct((n, LANES), table.dtype), mesh=_sc_mesh())
    def gather_kernel(t_hbm, i_hbm, o_hbm):
        def body(i_vmem, o_vmem):
            pltpu.sync_copy(t_hbm.at[i_vmem.at[0]], o_vmem)

        pltpu.emit_pipeline(
            body,
            grid=(n // SC_WINDOW,),
            in_specs=[pl.BlockSpec((1, SC_WINDOW), lambda i: (0, i))],
            out_specs=[pl.BlockSpec((SC_WINDOW, LANES), lambda i: (i, 0))],
            core_axis_name=("core", "subcore"),
            dimension_semantics=(pltpu.PARALLEL,),
            trace_scopes=False,
        )(i_hbm, o_hbm)

    return gather_kernel(table, idx.reshape(1, n))


def sc_scatter_rows2(rows, idx_a, idx_b, n_out):
    n = rows.shape[0]

    @pl.kernel(out_type=jax.ShapeDtypeStruct((n_out, LANES), rows.dtype), mesh=_sc_mesh())
    def scatter_kernel(r_hbm, ia_hbm, ib_hbm, o_hbm):
        def body(r_vmem, ia_vmem, ib_vmem):
            pltpu.sync_copy(r_vmem, o_hbm.at[ia_vmem.at[0]])
            pltpu.sync_copy(r_vmem, o_hbm.at[ib_vmem.at[0]])

        pltpu.emit_pipeline(
            body,
            grid=(n // SC_WINDOW,),
            in_specs=[pl.BlockSpec((SC_WINDOW, LANES), lambda i: (i, 0)),
                      pl.BlockSpec((1, SC_WINDOW), lambda i: (0, i)),
                      pl.BlockSpec((1, SC_WINDOW), lambda i: (0, i))],
            out_specs=[],
            core_axis_name=("core", "subcore"),
            dimension_semantics=(pltpu.PARALLEL,),
            trace_scopes=False,
        )(r_hbm, ia_hbm, ib_hbm)

    return scatter_kernel(rows, idx_a.reshape(1, n), idx_b.reshape(1, n))


def _plan_kernel(ps_ref, st_ref, ia_ref, ib_ref, *, n_rows):
    e = st_ref[0:2, :].astype(jnp.int32)
    dest = st_ref[4:6, :].astype(jnp.int32)
    for ex in range(N_EXPERTS):
        dest = dest + jnp.where(e == ex, ps_ref[ex], 0)
    for k, ref in enumerate((ia_ref, ib_ref)):
        for j in range(ref.shape[0]):
            ref[j:j + 1, :] = dest[k:k + 1, :] + j * n_rows


def dispatch_indices(slab_t, pad_start, n_chunks, n_rows):
    t = slab_t.shape[1]
    tp = next(c for c in (4096, 2048, 1024, 512, 256, 128) if t % c == 0)
    grid_spec = pltpu.PrefetchScalarGridSpec(
        num_scalar_prefetch=1,
        grid=(t // tp,),
        in_specs=[pl.BlockSpec((8, tp), lambda i, ps: (0, i))],
        out_specs=[pl.BlockSpec((n_chunks, tp), lambda i, ps: (0, i))] * 2,
    )
    return pl.pallas_call(
        functools.partial(_plan_kernel, n_rows=n_rows),
        grid_spec=grid_spec,
        out_shape=[jax.ShapeDtypeStruct((n_chunks, t), jnp.int32)] * 2,
        compiler_params=_cparams(("parallel",)),
        name="moe_plan",
    )(pad_start, slab_t)


def hier_moe(x2, logits, w_gate, w_up, w_down, layer):
    n_chunks, t, _ = x2.shape
    tmx = min(512, t)
    n_blocks = (t * TOP_K + N_EXPERTS * (tmx - 1)) // tmx + 1
    r = n_blocks * tmx
    slab, slab_t, counts = route(logits)
    cnt = counts[0, N_GROUPS:N_GROUPS + N_EXPERTS].astype(jnp.int32)
    padded = (cnt + tmx - 1) // tmx * tmx
    pad_end = jnp.cumsum(padded)
    pad_start = pad_end - padded
    blk_start = jnp.arange(n_blocks, dtype=jnp.int32) * tmx
    block_e = jnp.minimum(jnp.sum(pad_end[None, :] <= blk_start[:, None], axis=1), N_EXPERTS - 1).astype(jnp.int32)
    nvalid = jnp.clip(pad_start[block_e] + cnt[block_e] - blk_start, 0, tmx).astype(jnp.int32)
    idx1, idx2 = dispatch_indices(slab_t, pad_start.astype(jnp.int32), n_chunks, r)
    idx1, idx2 = idx1.reshape(-1), idx2.reshape(-1)
    x_sorted = sc_scatter_rows2(x2.reshape(n_chunks * t, LANES), idx1, idx2, n_chunks * r)
    y_sorted = expert_ffn(x_sorted.reshape(n_chunks, r, LANES), block_e, nvalid, w_gate, w_up, w_down, layer, tmx)
    y_flat = y_sorted.reshape(n_chunks * r, LANES)
    y1 = sc_gather_rows(y_flat, idx1).reshape(n_chunks, t, LANES)
    y2 = sc_gather_rows(y_flat, idx2).reshape(n_chunks, t, LANES)
    return y1, y2, slab


def _lower_bound_rows(lb_logits):
    lb = jnp.cumsum(jax.nn.softmax(lb_logits.astype(F32), axis=1), axis=1)
    lb = lb - lb[:, :1]
    rows = jnp.stack([jnp.log(lb), jnp.log1p(-lb), 1.0 - lb], axis=2)
    rows = jnp.concatenate([rows, jnp.zeros(rows.shape[:2] + (5, A_WIDTH), F32)], axis=2)
    return rows


def _cast_kernel(w_ref, o_ref):
    o_ref[...] = w_ref[...].astype(o_ref.dtype)


def reorder_cast_w_in(w_in):
    depth, d, n = w_in.shape
    gate_cols = 2 * d
    blk = 256
    nb = n // blk
    shift = (n - gate_cols) // blk
    return pl.pallas_call(
        _cast_kernel,
        grid=(depth, nb),
        in_specs=[pl.BlockSpec((1, d, blk), lambda l, j: (l, 0, (j + shift) % nb))],
        out_specs=pl.BlockSpec((1, d, blk), lambda l, j: (l, 0, j)),
        out_shape=jax.ShapeDtypeStruct((depth, d, n), MXU_DTYPE),
        compiler_params=_cparams(("parallel", "parallel")),
        name="reorder_cast_w_in",
    )(w_in)


def kernel(x, c, ctx, c_ctx, w_ada, b_ada, norm1_g, norm2_g, w_in, lb_logits, hgrn_norm_g, q_norm_g, k_norm_g,
           w_up_a, w_up_b, w_out, w_router_group, b_router_group, w_router_expert, b_router_expert, w_gate, w_up,
           w_down, final_norm_g):
    bsz, n, d = x.shape
    n_ctx = ctx.shape[1]
    depth = w_in.shape[0]
    mx = MXU_DTYPE

    w_in_r = reorder_cast_w_in(w_in)
    w_up_a_c, w_up_b_c, w_out_c = w_up_a.astype(mx), w_up_b.astype(mx), w_out.astype(mx)
    n_r = N_GROUPS + N_EXPERTS
    w_router = jnp.concatenate([w_router_group, w_router_expert, jnp.zeros((depth, d, LANES - n_r), F32)], axis=-1)
    w_router_hi = w_router.astype(jnp.bfloat16)
    w_router_lo = (w_router - w_router_hi.astype(F32)).astype(jnp.bfloat16)
    w_router = jnp.concatenate([w_router_hi, w_router_lo], axis=-1)
    b_router = jnp.concatenate([b_router_group, b_router_expert, jnp.zeros((depth, LANES - n_r), F32)], axis=-1)
    lbp = _lower_bound_rows(lb_logits)
    cos_t, sin_t = rope_tables(n)

    n_rows = -(-(bsz + 1) // 8) * 8
    act = jnp.concatenate([c, c_ctx[None, :], jnp.zeros((n_rows - bsz - 1, d), F32)], axis=0)
    mod = ada_mod(act, w_ada, b_ada)

    h, hc = x, ctx
    zero_state = jnp.zeros((bsz, A_HEADS, LANES, LANES), F32)
    for l in range(depth):
        need_ctx = l < depth - 1
        ml = mod[l, :bsz].reshape(bsz, 1, N_MOD, d)
        mc = jnp.broadcast_to(mod[l, bsz].reshape(1, 1, N_MOD, d), (bsz, 1, N_MOD, d))
        sh1, sc1, g1, sh2, sc2, g2 = (ml[:, :, i] for i in range(N_MOD))
        csh1, csc1, cg1, csh2, csc2, cg2 = (mc[:, :, i] for i in range(N_MOD))

        zl, q_l, k_l, v_l, st_l = in_proj(h, norm1_g[l], sc1, sh1, w_in_r, l, q_norm_g[l], k_norm_g[l], cos_t, sin_t)
        zc, q_c, k_c, v_c, st_c = in_proj(hc, norm1_g[l], csc1, csh1, w_in_r, l, q_norm_g[l], k_norm_g[l])
        ob_c, s_bwd = hgrn_scan(zc, st_c, lbp[1, l], zero_state, True)
        if need_ctx:
            a_c, s_fwd = hgrn_scan(zc, st_c, lbp[0, l], zero_state, False, ob_c, hgrn_norm_g[l])
        else:
            _, s_fwd = hgrn_scan(zc, st_c, lbp[0, l], zero_state, False)
        ob_l, _ = hgrn_scan(zl, st_l, lbp[1, l], s_bwd, True)
        a_l, _ = hgrn_scan(zl, st_l, lbp[0, l], s_fwd, False, ob_l, hgrn_norm_g[l])

        b_l = attention(q_l, jnp.concatenate([k_c, k_l], axis=1), jnp.concatenate([v_c, v_l], axis=1))

        t_total = bsz * (n + n_ctx) if need_ctx else bsz * n
        bufs = (jnp.zeros((d // (2 * LANES), t_total, LANES), jnp.int32),
                jnp.zeros((t_total, LANES), F32)) if need_ctx else None
        h, x2, lg = merge_proj(h, a_l, b_l, zl, w_up_a_c[l], w_up_b_c[l], w_out_c[l], g1, norm2_g[l], sc2, sh2,
                               w_router[l], b_router[l][None, :], t_total, 0, bufs)
        if need_ctx:
            b_c = attention(q_c, k_c, v_c)
            hc, x2, lg = merge_proj(hc, a_c, b_c, zc, w_up_a_c[l], w_up_b_c[l], w_out_c[l], cg1, norm2_g[l],
                                    csc2, csh2, w_router[l], b_router[l][None, :], t_total, bsz * n, (x2, lg))

        y1, y2, slab = hier_moe(x2, lg, w_gate, w_up, w_down, l)
        h = moe_combine(h, y1, y2, slab, g2, 0, None if need_ctx else final_norm_g)
        if need_ctx:
            hc = moe_combine(hc, y1, y2, slab, cg2, bsz * n)
    return h
```

```python
import functools

import jax
import jax.numpy as jnp
from jax import lax
from jax.experimental import pallas as pl
from jax.experimental.pallas import tpu as pltpu
from jax.experimental.pallas import tpu_sc as plsc

F32 = jnp.float32
MXU_DTYPE = jnp.bfloat16

EPS = 1e-6
GRID_W = 64
ROPE_THETA = 10000.0
LANES = 128

A_HEADS = 4
A_DK = 128
A_WIDTH = A_HEADS * A_DK
HGRN_CHUNK = 64
HGRN_SUB = 16
HGRN_STEP_TOKENS = 1024
HGRN_EXP2_CLAMP = 86.0
B_HEADS = 8
B_KV_HEADS = 2
B_HD = 64
B_GROUP = B_HEADS // B_KV_HEADS
B_WIDTH = B_HEADS * B_HD
B_KV_WIDTH = B_KV_HEADS * B_HD
AXIS_DIM = B_HD // 2
N_GROUPS = 4
EXPERTS_PER_GROUP = 8
N_EXPERTS = N_GROUPS * EXPERTS_PER_GROUP
TOP_K = 2
N_MOD = 6

OFF_GATES = 0
OFF_QA = 2048
OFF_FF = OFF_QA + A_WIDTH
OFF_FB = OFF_FF + A_WIDTH
OFF_IA = OFF_FB + A_WIDTH
OFF_OG = OFF_IA + A_WIDTH
OFF_QB = OFF_OG + A_WIDTH
OFF_KB = OFF_QB + B_WIDTH
OFF_VB = OFF_KB + B_KV_WIDTH
D_IN = OFF_VB + B_KV_WIDTH

VMEM_LIMIT = 52 * 1024 * 1024
INPROJ_SUB_ROWS = 256
MERGE_ROWS = 1024
MERGE_SUB_ROWS = 256
EXPERT_SUB_ROWS = 512


def _cparams(sem):
    return pltpu.CompilerParams(dimension_semantics=sem, vmem_limit_bytes=VMEM_LIMIT)


LOG2E = 1.4426950408889634


def _sigmoid(x):
    return 0.5 * jnp.tanh(0.5 * x) + 0.5


def _nt_dot(a, b):
    return lax.dot_general(a, b, (((1,), (1,)), ((), ())), preferred_element_type=F32)


def _ada_kernel(a_ref, w_ref, b_ref, o_ref):
    a = a_ref[...]
    a = a * _sigmoid(a)
    o_ref[0] = jnp.dot(a.astype(MXU_DTYPE), w_ref[0].astype(MXU_DTYPE), preferred_element_type=F32) + b_ref[0]


def ada_mod(act, w_ada, b_ada):
    depth, d, n = w_ada.shape
    r = act.shape[0]
    tn = 1024
    return pl.pallas_call(
        _ada_kernel,
        grid=(depth, n // tn),
        in_specs=[
            pl.BlockSpec((r, d), lambda l, j: (0, 0)),
            pl.BlockSpec((1, d, tn), lambda l, j: (l, 0, j)),
            pl.BlockSpec((1, 1, tn), lambda l, j: (l, 0, j)),
        ],
        out_specs=pl.BlockSpec((1, r, tn), lambda l, j: (l, 0, j)),
        out_shape=jax.ShapeDtypeStruct((depth, r, n), F32),
        compiler_params=_cparams(("parallel", "parallel")),
        name="ada_mod",
    )(act, w_ada, b_ada.reshape(depth, 1, n))


def _modulated_norm(x, g, sc, sh):
    ms = jnp.mean(x * x, axis=-1, keepdims=True)
    return (x * lax.rsqrt(ms + EPS) * g) * (1.0 + sc) + sh


def _split3(x):
    hi = x.astype(jnp.bfloat16)
    r1 = x - hi.astype(F32)
    mid = r1.astype(jnp.bfloat16)
    lo = (r1 - mid.astype(F32)).astype(jnp.bfloat16)
    return hi, mid, lo


def _swap16(x):
    lane = lax.broadcasted_iota(jnp.int32, x.shape, 1)
    return jnp.where(lane % 32 < 16, pltpu.roll(x, LANES - 16, 1), pltpu.roll(x, 16, 1))


def _head_norm_rope(x, gsum, g, cos, sin):
    ssum = sum(jnp.dot(t, gsum, preferred_element_type=F32) for t in _split3(x * x))
    y = x * lax.rsqrt(ssum * (1.0 / B_HD) + EPS) * g
    if cos is not None:
        y = y * cos + _swap16(y) * sin
    return y


def _inproj_kernel(*refs, col_chunk, rope, combine):
    refs = list(refs)
    h_ref, g_ref, sc_ref, sh_ref, w_ref, qg_ref, kg_ref = refs[:7]
    del refs[:7]
    cos = sin = None
    if rope:
        cos, sin = refs[0][...], refs[1][...]
        del refs[:2]
    if combine:
        y1_ref, y2_ref, slab_ref, g2_ref = refs[:4]
        del refs[:4]
        ho_ref = refs.pop()
    z_ref, qo_ref, ko_ref, vo_ref, st_ref = refs
    r_i = lax.broadcasted_iota(jnp.int32, (LANES, LANES), 0)
    c_i = lax.broadcasted_iota(jnp.int32, (LANES, LANES), 1)
    gsum = (r_i // B_HD == c_i // B_HD).astype(jnp.bfloat16)
    tm = h_ref.shape[1]
    sub = min(INPROJ_SUB_ROWS, tm)
    lane = lax.broadcasted_iota(jnp.int32, (sub, LANES), 1)
    stats = [None, None]
    for r in range(tm // sub):
        rs = slice(r * sub, (r + 1) * sub)
        h = h_ref[0, rs, :]
        if combine:
            y12 = (_unpack_rows(jnp.concatenate([y1_ref[j, rs, :] for j in range(y1_ref.shape[0])], axis=1))
                   * slab_ref[rs, 2:3]
                   + _unpack_rows(jnp.concatenate([y2_ref[j, rs, :] for j in range(y2_ref.shape[0])], axis=1))
                   * slab_ref[rs, 3:4])
            h = h + g2_ref[0] * y12
            ho_ref[0, rs, :] = h
        y = _modulated_norm(h, g_ref[...], sc_ref[0], sh_ref[0]).astype(MXU_DTYPE)
        gqa = jnp.dot(y, w_ref[0, :, OFF_QB:], preferred_element_type=F32)
        cs_, sn_ = (cos[rs], sin[rs]) if rope else (None, None)
        for cb in range(B_WIDTH // LANES):
            x = gqa[:, cb * LANES:(cb + 1) * LANES]
            q = _head_norm_rope(x, gsum, qg_ref[...], cs_, sn_) * (B_HD ** -0.5 * LOG2E)
            swapped = pltpu.roll(q, B_HD, 1)
            for half in range(2):
                head = 2 * cb + half
                kv = head // B_GROUP
                src = q if half == kv else swapped
                keep = (lane // B_HD) == kv
                qo_ref[0, head, rs, :] = jnp.where(keep, src, 0.0).astype(qo_ref.dtype)
        kn = _head_norm_rope(gqa[:, B_WIDTH:B_WIDTH + B_KV_WIDTH], gsum, kg_ref[...], cs_, sn_)
        ko_ref[0, rs, :] = kn.astype(ko_ref.dtype)
        vo_ref[0, rs, :] = gqa[:, B_WIDTH + B_KV_WIDTH:].astype(vo_ref.dtype)
        for c in range(z_ref.shape[2] // col_chunk):
            cs = slice(c * col_chunk, (c + 1) * col_chunk)
            zc = jnp.dot(y, w_ref[0, :, cs], preferred_element_type=F32).astype(z_ref.dtype)
            z_ref[0, rs, cs] = zc
            for d, off in enumerate((OFF_FF, OFF_FB)):
                lo, hi = max(off, cs.start), min(off + A_WIDTH, cs.stop)
                if lo < hi:
                    zneg = jnp.minimum(zc[:, lo - cs.start:hi - cs.start].astype(F32), 0.0)
                    blocks = zneg.reshape(sub // HGRN_SUB, HGRN_SUB, hi - lo).sum(axis=1)
                    sums = jnp.min(blocks, axis=0, keepdims=True)
                    for lb in range(sums.shape[1] // LANES):
                        piece = sums[:, lb * LANES:(lb + 1) * LANES]
                        stats[d] = piece if stats[d] is None else jnp.minimum(stats[d], piece)
    st_ref[0, 0] = jnp.concatenate([stats[0], stats[1], jnp.zeros((6, LANES), F32)], axis=0)


def in_proj(h, norm_g, sc, sh, w_in, layer, q_norm_g, k_norm_g, cos_t=None, sin_t=None, moe=None):
    bsz, n, d = h.shape
    tm = min(512, n)
    nt = n // tm
    rope = cos_t is not None
    qg = jnp.tile(q_norm_g.astype(F32), LANES // B_HD).reshape(1, LANES)
    kg = jnp.tile(k_norm_g.astype(F32), LANES // B_HD).reshape(1, LANES)
    vec = pl.BlockSpec((1, LANES), lambda b, i: (0, 0))
    hrow = pl.BlockSpec((1, tm, d), lambda b, i: (b, i, 0))
    per_b = pl.BlockSpec((1, 1, d), lambda b, i: (b, 0, 0))
    in_specs = [hrow, pl.BlockSpec((1, d), lambda b, i: (0, 0)), per_b, per_b,
                pl.BlockSpec((1, d, D_IN), lambda b, i: (layer, 0, 0)), vec, vec]
    args = [h, norm_g.reshape(1, d), sc, sh, w_in, qg, kg]
    if rope:
        in_specs += [pl.BlockSpec((tm, LANES), lambda b, i: (i, 0))] * 2
        args += [cos_t, sin_t]
    row = pl.BlockSpec((1, tm, LANES), lambda b, i: (b, i, 0))
    out_specs = [pl.BlockSpec((1, tm, OFF_QB), lambda b, i: (b, i, 0)),
                 pl.BlockSpec((1, B_HEADS, tm, LANES), lambda b, i: (b, 0, i, 0)), row, row,
                 pl.BlockSpec((1, 1, 8, LANES), lambda b, i: (b, i, 0, 0))]
    out_shape = [jax.ShapeDtypeStruct((bsz, n, OFF_QB), MXU_DTYPE),
                 jax.ShapeDtypeStruct((bsz, B_HEADS, n, LANES), MXU_DTYPE),
                 jax.ShapeDtypeStruct((bsz, n, LANES), MXU_DTYPE),
                 jax.ShapeDtypeStruct((bsz, n, LANES), MXU_DTYPE),
                 jax.ShapeDtypeStruct((bsz, nt, 8, LANES), F32)]
    if moe is not None:
        y1, y2, slab, g2, tok_off = moe
        off = tok_off // tm
        yspec = pl.BlockSpec((y1.shape[0], tm, LANES), lambda b, i: (0, off + b * nt + i, 0))
        in_specs += [yspec, yspec, pl.BlockSpec((tm, LANES), lambda b, i: (off + b * nt + i, 0)), per_b]
        args += [y1, y2, slab, g2]
        out_specs.append(hrow)
        out_shape.append(jax.ShapeDtypeStruct((bsz, n, d), F32))
    return pl.pallas_call(
        functools.partial(_inproj_kernel, col_chunk=768, rope=rope, combine=moe is not None),
        grid=(bsz, nt),
        in_specs=in_specs,
        out_specs=out_specs,
        out_shape=out_shape,
        compiler_params=_cparams(("parallel", "parallel")),
        name="in_proj_rope" if rope else "in_proj",
    )(*args)


def rope_tables(n):
    t = jnp.arange(n)
    r = (t // GRID_W).astype(F32)
    col = (t % GRID_W).astype(F32)
    inv = ROPE_THETA ** (-jnp.arange(0, AXIS_DIM, 2, dtype=F32) / AXIS_DIM)
    ang_r = r[:, None] * inv
    ang_c = col[:, None] * inv
    cos = jnp.concatenate([jnp.cos(ang_r)] * 2 + [jnp.cos(ang_c)] * 2, axis=1)
    sin = jnp.concatenate([-jnp.sin(ang_r), jnp.sin(ang_r), -jnp.sin(ang_c), jnp.sin(ang_c)], axis=1)
    return jnp.tile(cos, (1, 2)), jnp.tile(sin, (1, 2))


def _hgrn_gates(zf, loglb, log1m, onem):
    l1pe = jnp.log(1.0 + jnp.exp(-jnp.abs(zf)))
    c = log1m + (jnp.minimum(zf, 0.0) - l1pe)
    logf = jnp.maximum(loglb, c) + jnp.log(1.0 + jnp.exp(-jnp.abs(loglb - c)))
    k = onem * jnp.exp(jnp.minimum(-zf, 0.0) - l1pe)
    return k, logf * LOG2E


def _cumsum_rows(cum_mat, x):
    out = jnp.dot(cum_mat, jnp.concatenate(_split3(x), axis=1), preferred_element_type=F32)
    w = x.shape[1]
    return out[:, :w] + out[:, w:2 * w] + out[:, 2 * w:]


def _hgrn_kernel(*refs, rev, n_chunks, fuse):
    if fuse:
        safe_ref, q_ref, f_ref, v_ref, lbp_ref, s0_ref, ob_ref, og_ref, ng_ref, o_ref, sT_ref, state = refs
    else:
        safe_ref, q_ref, f_ref, v_ref, lbp_ref, s0_ref, o_ref, sT_ref, state = refs
        ob_ref = og_ref = ng_ref = None
    i = pl.program_id(1)
    nt = pl.num_programs(1)

    @pl.when(i == 0)
    def _():
        state[...] = s0_ref[0]

    loglb = lbp_ref[0:1, :]
    log1m = lbp_ref[1:2, :]
    onem = lbp_ref[2:3, :]
    c = HGRN_CHUNK
    nsub = c // HGRN_SUB
    r_i = lax.broadcasted_iota(jnp.int32, (c, c), 0)
    c_i = lax.broadcasted_iota(jnp.int32, (c, c), 1)
    causal = (c_i >= r_i) if rev else (c_i <= r_i)
    cum_mat = causal.astype(jnp.bfloat16)
    row_blk = lax.broadcasted_iota(jnp.int32, (c, A_DK), 0) // HGRN_SUB

    def prepare(sl, exact):
        k, logf = _hgrn_gates(f_ref[0, sl, :].astype(F32), loglb, log1m, onem)
        zq = q_ref[0, sl, :].astype(F32)
        q = zq * _sigmoid(zq) * (A_DK ** -0.5)
        cum = _cumsum_rows(cum_mat, logf)
        tot = cum[0:1] if rev else cum[c - 1:c]
        qd = (q * jnp.exp2(cum)).astype(MXU_DTYPE)
        kd = (k * jnp.exp2(tot - cum)).astype(MXU_DTYPE)
        refs_j = []
        for j in range(nsub):
            if rev:
                r = cum[(j + 1) * HGRN_SUB:(j + 1) * HGRN_SUB + 1] if j < nsub - 1 else jnp.zeros_like(tot)
            else:
                r = cum[j * HGRN_SUB - 1:j * HGRN_SUB] if j > 0 else jnp.zeros_like(tot)
            refs_j.append(r)
        ref_rows = jnp.concatenate([jnp.broadcast_to(r, (HGRN_SUB, r.shape[1])) for r in refs_j], axis=0)
        qt = (q * jnp.exp2(cum - ref_rows)).astype(MXU_DTYPE)
        if exact:
            blk512 = lax.broadcasted_iota(jnp.int32, cum.shape, 0) // HGRN_SUB
            kt = [jnp.where((blk512 > j) if rev else (blk512 < j),
                            k * jnp.exp2(jnp.minimum(r - cum, 0.0)), 0.0).astype(MXU_DTYPE)
                  for j, r in enumerate(refs_j)]
            return qd, kd, jnp.exp2(tot), qt, kt, (q, k, cum)
        kt = [(k * jnp.exp2(jnp.minimum(r - cum, HGRN_EXP2_CLAMP))).astype(MXU_DTYPE) for r in refs_j]
        return qd, kd, jnp.exp2(tot), qt, kt, None

    def exact_diagonal(q, k, cum, hs):
        lane16 = lax.broadcasted_iota(jnp.int32, (HGRN_SUB, HGRN_SUB), 1)
        strips = []
        for j in range(nsub):
            rows = slice(j * HGRN_SUB, (j + 1) * HGRN_SUB)
            qb, kb, cb = q[rows, hs], k[rows, hs], cum[rows, hs]
            blk = jnp.zeros((HGRN_SUB, HGRN_SUB), F32)
            for t in range(HGRN_SUB):
                w = qb * kb[t:t + 1] * jnp.exp2(jnp.minimum(cb - cb[t:t + 1], 0.0))
                blk = jnp.where(lane16 == t, jnp.sum(w, axis=-1, keepdims=True), blk)
            pieces = [blk if jj == j else jnp.zeros((HGRN_SUB, HGRN_SUB), F32) for jj in range(nsub)]
            strips.append(jnp.concatenate(pieces, axis=1))
        return jnp.concatenate(strips, axis=0)

    def finish(sl, prepped, s):
        qd, kd, decay, qt, kt, raw = prepped
        v = v_ref[0, sl, :]
        outs = []
        for h in range(A_HEADS):
            hs = slice(h * A_DK, (h + 1) * A_DK)
            q_big = jnp.concatenate([jnp.where(row_blk == j, qt[:, hs], jnp.zeros_like(qt[:, hs]))
                                     for j in range(nsub)], axis=1)
            k_big = jnp.concatenate([kt[j][:, hs] for j in range(nsub)], axis=1)
            scores = _nt_dot(q_big, k_big)
            if raw is not None:
                scores = scores + exact_diagonal(*raw, hs)
            scores = jnp.where(causal, scores, 0.0)
            o_h = _nt_dot(qd[:, hs], s[h].astype(MXU_DTYPE))
            o_h = o_h + jnp.dot(scores.astype(MXU_DTYPE), v[:, hs], preferred_element_type=F32)
            upd = lax.dot_general(v[:, hs], kd[:, hs], (((0,), (0,)), ((), ())), preferred_element_type=F32)
            s[h] = s[h] * decay[:, hs] + upd
            if fuse:
                o_h = o_h + ob_ref[0, sl, hs]
                o_h = o_h * lax.rsqrt(jnp.mean(o_h * o_h, axis=-1, keepdims=True) + EPS)
            outs.append(o_h)
        o = jnp.concatenate(outs, axis=1)
        if fuse:
            og = og_ref[0, sl, :].astype(F32)
            o = o * ng_ref[...] * (og * _sigmoid(og))
        o_ref[0, sl, :] = o.astype(o_ref.dtype)

    safe = safe_ref[pl.program_id(0), (nt - 1 - i) if rev else i] != 0

    @pl.when(safe)
    def _():
        s = [state[h] for h in range(A_HEADS)]
        order = list(range(n_chunks - 1, -1, -1) if rev else range(n_chunks))
        slices = [slice(ci * c, (ci + 1) * c) for ci in order]
        ready = prepare(slices[0], False)
        for n, sl in enumerate(slices):
            prepped = ready
            if n + 1 < len(slices):
                ready = prepare(slices[n + 1], False)
            finish(sl, prepped, s)
        for h in range(A_HEADS):
            state[h] = s[h]

    @pl.when(jnp.logical_not(safe))
    def _():
        def chunk(n, carry):
            ci = (n_chunks - 1 - n) if rev else n
            sl = pl.ds(pl.multiple_of(ci * c, c), c)
            s = [state[h] for h in range(A_HEADS)]
            finish(sl, prepare(sl, True), s)
            for h in range(A_HEADS):
                state[h] = s[h]
            return carry

        lax.fori_loop(0, n_chunks, chunk, 0)

    @pl.when(i == pl.num_programs(1) - 1)
    def _():
        sT_ref[0] = state[...]


def hgrn_safe_flags(stats, direction, n, step_tokens):
    worst = stats[:, :, direction, :].min(axis=-1)
    bsz, nb = worst.shape
    per_step = nb * step_tokens // n
    worst = worst.reshape(bsz, nb // per_step, per_step).min(axis=-1)
    return (worst * LOG2E - HGRN_SUB >= -HGRN_EXP2_CLAMP).astype(jnp.int32)


def hgrn_scan(z, stats, lbp, s0, rev, o_other=None, norm_g=None):
    bsz, n, _ = z.shape
    fuse = o_other is not None
    t = min(HGRN_STEP_TOKENS, n)
    nt = n // t
    f_off = (OFF_FB if rev else OFF_FF) // A_WIDTH
    safe = hgrn_safe_flags(stats, 1 if rev else 0, n, t)

    def tmap(i):
        return nt - 1 - i if rev else i

    def col(blk):
        return pl.BlockSpec((1, t, A_WIDTH), lambda b, i, sf: (b, tmap(i), blk))

    state_spec = pl.BlockSpec((1, A_HEADS, LANES, LANES), lambda b, i, sf: (b, 0, 0, 0))
    in_specs = [col(OFF_QA // A_WIDTH), col(f_off), col(OFF_IA // A_WIDTH),
                pl.BlockSpec((8, A_WIDTH), lambda b, i, sf: (0, 0)), state_spec]
    args = [z, z, z, lbp, s0]
    if fuse:
        in_specs += [col(0), col(OFF_OG // A_WIDTH), pl.BlockSpec((1, A_WIDTH), lambda b, i, sf: (0, 0))]
        args += [o_other, z, norm_g.reshape(1, A_WIDTH)]
    kern = functools.partial(_hgrn_kernel, rev=rev, n_chunks=t // HGRN_CHUNK, fuse=fuse)
    grid_spec = pltpu.PrefetchScalarGridSpec(
        num_scalar_prefetch=1,
        grid=(bsz, nt),
        in_specs=in_specs,
        out_specs=[col(0), state_spec],
        scratch_shapes=[pltpu.VMEM((A_HEADS, LANES, LANES), F32)],
    )
    return pl.pallas_call(
        kern,
        grid_spec=grid_spec,
        out_shape=[jax.ShapeDtypeStruct((bsz, n, A_WIDTH), MXU_DTYPE if fuse else F32),
                   jax.ShapeDtypeStruct((bsz, A_HEADS, LANES, LANES), F32)],
        compiler_params=_cparams(("parallel", "arbitrary")),
        name="hgrn_bwd" if rev else "hgrn_fwd",
    )(safe, *args)


def _attn_kernel(q_ref, k_ref, v_ref, o_ref):
    tq = q_ref.shape[2]
    rows = B_GROUP * tq
    lane = lax.broadcasted_iota(jnp.int32, v_ref.shape[1:], 1)
    lane_o = lax.broadcasted_iota(jnp.int32, (tq, LANES), 1)
    s_all = _nt_dot(q_ref[0].reshape(B_HEADS * tq, LANES), k_ref[0])
    for kv in range(B_KV_HEADS):
        s = s_all[kv * rows:(kv + 1) * rows]
        vs = jnp.where((lane // B_HD) == kv, v_ref[0], jnp.ones_like(v_ref[0]))
        p = jnp.exp2(s - jnp.max(s, axis=-1, keepdims=True))
        a = jnp.dot(p.astype(MXU_DTYPE), vs, preferred_element_type=F32)
        o_full = a / pltpu.roll(a, B_HD, 1)
        o_swapped = pltpu.roll(o_full, B_HD, 1)
        for pair in range(B_GROUP // 2):
            g0, g1 = 2 * pair, 2 * pair + 1
            lo_src = o_full if kv == 0 else o_swapped
            hi_src = o_swapped if kv == 0 else o_full
            blk = jnp.where(lane_o < B_HD, lo_src[g0 * tq:(g0 + 1) * tq], hi_src[g1 * tq:(g1 + 1) * tq])
            cb = kv * (B_GROUP // 2) + pair
            o_ref[0, :, cb * LANES:(cb + 1) * LANES] = blk.astype(o_ref.dtype)


def attention(q, k, v):
    bsz, _, n, _ = q.shape
    nkeys = k.shape[1]
    tq = min(128, n)
    return pl.pallas_call(
        _attn_kernel,
        grid=(bsz, n // tq),
        in_specs=[pl.BlockSpec((1, B_HEADS, tq, LANES), lambda b, i: (b, 0, i, 0)),
                  pl.BlockSpec((1, nkeys, LANES), lambda b, i: (b, 0, 0)),
                  pl.BlockSpec((1, nkeys, LANES), lambda b, i: (b, 0, 0))],
        out_specs=pl.BlockSpec((1, tq, B_WIDTH), lambda b, i: (b, i, 0)),
        out_shape=jax.ShapeDtypeStruct((bsz, n, B_WIDTH), MXU_DTYPE),
        compiler_params=_cparams(("parallel", "parallel")),
        name="gqa_attention",
    )(q, k, v)


def _pack_rows(x):
    w = x.shape[1] // 2
    bits = lax.bitcast_convert_type(x.astype(jnp.bfloat16).astype(F32), jnp.uint32)
    packed = (bits[:, :w] & jnp.uint32(0xFFFF0000)) | (bits[:, w:] >> 16)
    return lax.bitcast_convert_type(packed, jnp.int32)


def _unpack_rows(p):
    bits = lax.bitcast_convert_type(p, jnp.uint32)
    hi = lax.bitcast_convert_type(bits & jnp.uint32(0xFFFF0000), F32)
    lo = lax.bitcast_convert_type(bits << 16, F32)
    return jnp.concatenate([hi, lo], axis=1)


def _store_chunks(ref, packed):
    for j in range(ref.shape[0]):
        ref[j] = packed[:, j * LANES:(j + 1) * LANES]


def _load_chunks(ref):
    return jnp.concatenate([ref[j] for j in range(ref.shape[0])], axis=1)


def _merge_kernel(*refs, aliased):
    if aliased:
        refs = refs[:13] + refs[15:]
    (h_ref, a_ref, b_ref, zg_ref, wa_ref, wb_ref, wo_ref, g1_ref, n2_ref, sc_ref, sh_ref, wr_ref, br_ref,
     ho_ref, x2_ref, lg_ref) = refs
    d = h_ref.shape[2]
    tm = h_ref.shape[1]
    sub = min(MERGE_SUB_ROWS, tm)
    for r in range(tm // sub):
        rs = slice(r * sub, (r + 1) * sub)
        ua = jnp.dot(a_ref[0, rs, :], wa_ref[...], preferred_element_type=F32)
        ub = jnp.dot(b_ref[0, rs, :], wb_ref[...], preferred_element_type=F32)
        ga = _sigmoid(zg_ref[0, rs, 0:d].astype(F32))
        gb = _sigmoid(zg_ref[0, rs, d:2 * d].astype(F32))
        mix = (ga * ua + gb * ub).astype(MXU_DTYPE)
        h_new = h_ref[0, rs, :] + g1_ref[0] * jnp.dot(mix, wo_ref[...], preferred_element_type=F32)
        ho_ref[0, rs, :] = h_new
        x2 = _modulated_norm(h_new, n2_ref[...], sc_ref[0], sh_ref[0])
        packed = _pack_rows(x2)
        for j in range(x2_ref.shape[0]):
            x2_ref[j, rs, :] = packed[:, j * LANES:(j + 1) * LANES]
        x_hi = x2.astype(jnp.bfloat16)
        x_lo = (x2 - x_hi.astype(F32)).astype(jnp.bfloat16)
        hi_both = jnp.dot(x_hi, wr_ref[...], preferred_element_type=F32)
        lo_hi = jnp.dot(x_lo, wr_ref[:, 0:LANES], preferred_element_type=F32)
        lg_ref[rs, :] = hi_both[:, 0:LANES] + hi_both[:, LANES:] + lo_hi + br_ref[...]


def merge_proj(h, a, b, z, w_up_a, w_up_b, w_out, g1, norm2_g, sc2, sh2, w_router, b_router, t_total, tok_off,
               bufs=None):
    bsz, n, d = h.shape
    tm = min(MERGE_ROWS, n)
    nt = n // tm
    off = tok_off // tm
    n_chunks = d // (2 * LANES)
    row = lambda w: pl.BlockSpec((1, tm, w), lambda b_, i: (b_, i, 0))
    full = lambda s: pl.BlockSpec(s, lambda b_, i: (0,) * len(s))
    per_b = pl.BlockSpec((1, 1, d), lambda b_, i: (b_, 0, 0))
    in_specs = [row(d), row(A_WIDTH), row(B_WIDTH),
                pl.BlockSpec((1, tm, 2 * d), lambda b_, i: (b_, i, OFF_GATES)),
                full((A_WIDTH, d)), full((B_WIDTH, d)), full((d, d)),
                per_b, full((1, d)), per_b, per_b, full((d, 2 * LANES)), full((1, LANES))]
    args = [h, a, b, z, w_up_a, w_up_b, w_out, g1, norm2_g.reshape(1, d), sc2, sh2, w_router, b_router]
    aliases = {}
    if bufs is not None:
        in_specs += [pl.BlockSpec(memory_space=pl.ANY)] * 2
        args += list(bufs)
        aliases = {13: 1, 14: 2}
    return pl.pallas_call(
        functools.partial(_merge_kernel, aliased=bufs is not None),
        grid=(bsz, nt),
        in_specs=in_specs,
        out_specs=[row(d),
                   pl.BlockSpec((n_chunks, tm, LANES), lambda b_, i: (0, off + b_ * nt + i, 0)),
                   pl.BlockSpec((tm, LANES), lambda b_, i: (off + b_ * nt + i, 0))],
        out_shape=[jax.ShapeDtypeStruct((bsz, n, d), F32),
                   jax.ShapeDtypeStruct((n_chunks, t_total, LANES), jnp.int32),
                   jax.ShapeDtypeStruct((t_total, LANES), F32)],
        input_output_aliases=aliases,
        compiler_params=_cparams(("parallel", "parallel")),
        name="merge_proj",
    )(*args)


def _first_lane_where(cond, lane):
    return jnp.min(jnp.where(cond, lane, LANES), axis=-1, keepdims=True)


def _router_kernel(lg_ref, slab_ref, slab_t_ref, cnt_ref, base):
    @pl.when(pl.program_id(0) == 0)
    def _():
        base[...] = jnp.zeros_like(base)

    lg = lg_ref[...]
    tr = lg.shape[0]
    lane = lax.broadcasted_iota(jnp.int32, lg.shape, 1)
    neg = jnp.float32(-jnp.inf)
    is_g = lane < N_GROUPS
    gl = jnp.where(is_g, lg, neg)
    gmax = jnp.max(gl, axis=-1, keepdims=True)
    gsel = _first_lane_where(gl == gmax, lane)
    p_group = 1.0 / jnp.sum(jnp.where(is_g, jnp.exp(lg - gmax), 0.0), axis=-1, keepdims=True)
    lo = N_GROUPS + gsel * EXPERTS_PER_GROUP
    in_grp = (lane >= lo) & (lane < lo + EXPERTS_PER_GROUP)
    el = jnp.where(in_grp, lg, neg)
    m1 = jnp.max(el, axis=-1, keepdims=True)
    i1 = _first_lane_where(el == m1, lane)
    el2 = jnp.where(lane == i1, neg, el)
    m2 = jnp.max(el2, axis=-1, keepdims=True)
    i2 = _first_lane_where(el2 == m2, lane)
    zsum = jnp.sum(jnp.where(in_grp, jnp.exp(lg - m1), 0.0), axis=-1, keepdims=True)
    p1 = 1.0 / zsum
    p2 = jnp.exp(m2 - m1) / zsum
    w1 = p1 / (p1 + p2) * p_group
    w2 = p2 / (p1 + p2) * p_group
    hit1 = lane == i1
    hit2 = lane == i2
    onehot = (hit1 | hit2).astype(F32)
    r_i = lax.broadcasted_iota(jnp.int32, (tr, tr), 0)
    c_i = lax.broadcasted_iota(jnp.int32, (tr, tr), 1)
    before = (c_i < r_i).astype(MXU_DTYPE)
    prior = jnp.dot(before, onehot.astype(MXU_DTYPE), preferred_element_type=F32) + base[0:1, :]
    r1 = jnp.sum(jnp.where(hit1, prior, 0.0), axis=-1, keepdims=True)
    r2 = jnp.sum(jnp.where(hit2, prior, 0.0), axis=-1, keepdims=True)
    new_base = base[0:1, :] + jnp.sum(onehot, axis=0, keepdims=True)
    base[...] = jnp.broadcast_to(new_base, base.shape)
    cnt_ref[...] = jnp.broadcast_to(new_base, cnt_ref.shape)
    vals = [(i1 - N_GROUPS).astype(F32), (i2 - N_GROUPS).astype(F32), w1, w2, r1, r2]
    slab = jnp.zeros_like(lg)
    for idx, val in enumerate(vals):
        slab = jnp.where(lane == idx, val, slab)
    slab_ref[...] = slab
    slab_t_ref[...] = slab.T[0:8, :]


def route(logits):
    t = logits.shape[0]
    tr = next(c for c in (1024, 512, 256, 128) if t % c == 0)
    return pl.pallas_call(
        _router_kernel,
        grid=(t // tr,),
        in_specs=[pl.BlockSpec((tr, LANES), lambda i: (i, 0))],
        out_specs=[pl.BlockSpec((tr, LANES), lambda i: (i, 0)), pl.BlockSpec((8, tr), lambda i: (0, i)),
                   pl.BlockSpec((8, LANES), lambda i: (0, 0))],
        out_shape=[jax.ShapeDtypeStruct((t, LANES), F32), jax.ShapeDtypeStruct((8, t), F32),
                   jax.ShapeDtypeStruct((8, LANES), F32)],
        scratch_shapes=[pltpu.VMEM((8, LANES), F32)],
        compiler_params=_cparams(("arbitrary",)),
        name="moe_route",
    )(logits)


def _expert_kernel(be_ref, nv_ref, x_ref, wg_ref, wu_ref, wd_ref, o_ref, wg_s, wu_s, wd_s):
    j = pl.program_id(0)
    nv = nv_ref[j]

    @pl.when((j == 0) | (be_ref[j] != be_ref[jnp.maximum(j - 1, 0)]))
    def _():
        wg_s[...] = wg_ref[0, 0].astype(wg_s.dtype)
        wu_s[...] = wu_ref[0, 0].astype(wu_s.dtype)
        wd_s[...] = wd_ref[0, 0].astype(wd_s.dtype)

    @pl.when(nv > 0)
    def _():
        tmx = x_ref.shape[1]
        sub = min(EXPERT_SUB_ROWS, tmx)
        for r in range(tmx // sub):
            rs = slice(r * sub, (r + 1) * sub)
            x = _unpack_rows(jnp.concatenate([x_ref[j, rs, :] for j in range(x_ref.shape[0])], axis=1))
            rows = lax.broadcasted_iota(jnp.int32, x.shape, 0) + r * sub
            x = jnp.where(rows < nv, x, 0.0).astype(MXU_DTYPE)
            hg = jnp.dot(x, wg_s[...], preferred_element_type=F32)
            hu = jnp.dot(x, wu_s[...], preferred_element_type=F32)
            hb = (hg * _sigmoid(hg) * hu).astype(MXU_DTYPE)
            packed = _pack_rows(jnp.dot(hb, wd_s[...], preferred_element_type=F32))
            for j in range(o_ref.shape[0]):
                o_ref[j, rs, :] = packed[:, j * LANES:(j + 1) * LANES]

    @pl.when(nv == 0)
    def _():
        o_ref[...] = jnp.zeros_like(o_ref)


def expert_ffn(x_sorted, block_e, nvalid, w_gate, w_up, w_down, layer, tmx):
    n_chunks, r, _ = x_sorted.shape
    _, _, d, de = w_gate.shape
    grid_spec = pltpu.PrefetchScalarGridSpec(
        num_scalar_prefetch=2,
        grid=(r // tmx,),
        in_specs=[pl.BlockSpec((n_chunks, tmx, LANES), lambda j, be, nv: (0, j, 0)),
                  pl.BlockSpec((1, 1, d, de), lambda j, be, nv: (layer, be[j], 0, 0)),
                  pl.BlockSpec((1, 1, d, de), lambda j, be, nv: (layer, be[j], 0, 0)),
                  pl.BlockSpec((1, 1, de, d), lambda j, be, nv: (layer, be[j], 0, 0))],
        out_specs=pl.BlockSpec((n_chunks, tmx, LANES), lambda j, be, nv: (0, j, 0)),
        scratch_shapes=[pltpu.VMEM((d, de), MXU_DTYPE), pltpu.VMEM((d, de), MXU_DTYPE),
                        pltpu.VMEM((de, d), MXU_DTYPE)],
    )
    return pl.pallas_call(
        _expert_kernel,
        grid_spec=grid_spec,
        out_shape=jax.ShapeDtypeStruct((n_chunks, r, LANES), jnp.int32),
        compiler_params=_cparams(("arbitrary",)),
        name="expert_ffn",
    )(block_e, nvalid, x_sorted, w_gate, w_up, w_down)


def _combine_kernel(*refs, final):
    if final:
        h_ref, y1_ref, y2_ref, slab_ref, g2_ref, fn_ref, o_ref = refs
    else:
        h_ref, y1_ref, y2_ref, slab_ref, g2_ref, o_ref = refs
    w1 = slab_ref[:, 2:3]
    w2 = slab_ref[:, 3:4]
    y = _unpack_rows(_load_chunks(y1_ref)) * w1 + _unpack_rows(_load_chunks(y2_ref)) * w2
    h_new = h_ref[0] + g2_ref[0] * y
    if final:
        ms = jnp.mean(h_new * h_new, axis=-1, keepdims=True)
        h_new = h_new * lax.rsqrt(ms + EPS) * fn_ref[...]
    o_ref[0] = h_new


def moe_combine(h, y1, y2, slab, g2, tok_off, final_g=None):
    bsz, n, d = h.shape
    n_chunks = y1.shape[0]
    tm = min(512, n)
    nt = n // tm
    off = tok_off // tm
    final = final_g is not None
    yspec = pl.BlockSpec((n_chunks, tm, LANES), lambda b, i: (0, off + b * nt + i, 0))
    in_specs = [pl.BlockSpec((1, tm, d), lambda b, i: (b, i, 0)), yspec, yspec,
                pl.BlockSpec((tm, LANES), lambda b, i: (off + b * nt + i, 0)),
                pl.BlockSpec((1, 1, d), lambda b, i: (b, 0, 0))]
    args = [h, y1, y2, slab, g2]
    if final:
        in_specs.append(pl.BlockSpec((1, d), lambda b, i: (0, 0)))
        args.append(final_g.reshape(1, d))
    return pl.pallas_call(
        functools.partial(_combine_kernel, final=final),
        grid=(bsz, nt),
        in_specs=in_specs,
        out_specs=pl.BlockSpec((1, tm, d), lambda b, i: (b, i, 0)),
        out_shape=jax.ShapeDtypeStruct((bsz, n, d), F32),
        compiler_params=_cparams(("parallel", "parallel")),
        name="moe_combine_final" if final else "moe_combine",
    )(*args)


SC_WINDOW = 128


def _sc_mesh():
    return plsc.VectorSubcoreMesh(core_axis_name="core", subcore_axis_name="subcore")


def sc_gather_rows(table, idx):
    n = idx.shape[0]

    @pl.kernel(out_type=jax.ShapeDtypeStruct((n, LANES), table.dtype), mesh=_sc_mesh())
    def gather_kernel(t_hbm, i_hbm, o_hbm):
        def body(i_vmem, o_vmem):
            pltpu.sync_copy(t_hbm.at[i_vmem.at[0]], o_vmem)

        pltpu.emit_pipeline(
            body,
            grid=(n // SC_WINDOW,),
            in_specs=[pl.BlockSpec((1, SC_WINDOW), lambda i: (0, i))],
            out_specs=[pl.BlockSpec((SC_WINDOW, LANES), lambda i: (i, 0))],
            core_axis_name=("core", "subcore"),
            dimension_semantics=(pltpu.PARALLEL,),
            trace_scopes=False,
        )(i_hbm, o_hbm)

    return gather_kernel(table, idx.reshape(1, n))


def sc_scatter_rows2(rows, idx_a, idx_b, n_out):
    n = rows.shape[0]

    @pl.kernel(out_type=jax.ShapeDtypeStruct((n_out, LANES), rows.dtype), mesh=_sc_mesh())
    def scatter_kernel(r_hbm, ia_hbm, ib_hbm, o_hbm):
        def body(r_vmem, ia_vmem, ib_vmem):
            pltpu.sync_copy(r_vmem, o_hbm.at[ia_vmem.at[0]])
            pltpu.sync_copy(r_vmem, o_hbm.at[ib_vmem.at[0]])

        pltpu.emit_pipeline(
            body,
            grid=(n // SC_WINDOW,),
            in_specs=[pl.BlockSpec((SC_WINDOW, LANES), lambda i: (i, 0)),
                      pl.BlockSpec((1, SC_WINDOW), lambda i: (0, i)),
                      pl.BlockSpec((1, SC_WINDOW), lambda i: (0, i))],
            out_specs=[],
            core_axis_name=("core", "subcore"),
            dimension_semantics=(pltpu.PARALLEL,),
            trace_scopes=False,
        )(r_hbm, ia_hbm, ib_hbm)

    return scatter_kernel(rows, idx_a.reshape(1, n), idx_b.reshape(1, n))


def _plan_kernel(ps_ref, st_ref, ia_ref, ib_ref, *, n_rows):
    e = st_ref[0:2, :].astype(jnp.int32)
    dest = st_ref[4:6, :].astype(jnp.int32)
    for ex in range(N_EXPERTS):
        dest = dest + jnp.where(e == ex, ps_ref[ex], 0)
    for k, ref in enumerate((ia_ref, ib_ref)):
        for j in range(ref.shape[0]):
            ref[j:j + 1, :] = dest[k:k + 1, :] + j * n_rows


def dispatch_indices(slab_t, pad_start, n_chunks, n_rows):
    t = slab_t.shape[1]
    tp = next(c for c in (4096, 2048, 1024, 512, 256, 128) if t % c == 0)
    grid_spec = pltpu.PrefetchScalarGridSpec(
        num_scalar_prefetch=1,
        grid=(t // tp,),
        in_specs=[pl.BlockSpec((8, tp), lambda i, ps: (0, i))],
        out_specs=[pl.BlockSpec((n_chunks, tp), lambda i, ps: (0, i))] * 2,
    )
    return pl.pallas_call(
        functools.partial(_plan_kernel, n_rows=n_rows),
        grid_spec=grid_spec,
        out_shape=[jax.ShapeDtypeStruct((n_chunks, t), jnp.int32)] * 2,
        compiler_params=_cparams(("parallel",)),
        name="moe_plan",
    )(pad_start, slab_t)


def hier_moe(x2, logits, w_gate, w_up, w_down, layer):
    n_chunks, t, _ = x2.shape
    tmx = min(512, t)
    n_blocks = (t * TOP_K + N_EXPERTS * (tmx - 1)) // tmx + 1
    r = n_blocks * tmx
    slab, slab_t, counts = route(logits)
    cnt = counts[0, N_GROUPS:N_GROUPS + N_EXPERTS].astype(jnp.int32)
    padded = (cnt + tmx - 1) // tmx * tmx
    pad_end = jnp.cumsum(padded)
    pad_start = pad_end - padded
    blk_start = jnp.arange(n_blocks, dtype=jnp.int32) * tmx
    block_e = jnp.minimum(jnp.sum(pad_end[None, :] <= blk_start[:, None], axis=1), N_EXPERTS - 1).astype(jnp.int32)
    nvalid = jnp.clip(pad_start[block_e] + cnt[block_e] - blk_start, 0, tmx).astype(jnp.int32)
    idx1, idx2 = dispatch_indices(slab_t, pad_start.astype(jnp.int32), n_chunks, r)
    idx1, idx2 = idx1.reshape(-1), idx2.reshape(-1)
    x_sorted = sc_scatter_rows2(x2.reshape(n_chunks * t, LANES), idx1, idx2, n_chunks * r)
    y_sorted = expert_ffn(x_sorted.reshape(n_chunks, r, LANES), block_e, nvalid, w_gate, w_up, w_down, layer, tmx)
    y_flat = y_sorted.reshape(n_chunks * r, LANES)
    y1 = sc_gather_rows(y_flat, idx1).reshape(n_chunks, t, LANES)
    y2 = sc_gather_rows(y_flat, idx2).reshape(n_chunks, t, LANES)
    return y1, y2, slab


def _lower_bound_rows(lb_logits):
    lb = jnp.cumsum(jax.nn.softmax(lb_logits.astype(F32), axis=1), axis=1)
    lb = lb - lb[:, :1]
    rows = jnp.stack([jnp.log(lb), jnp.log1p(-lb), 1.0 - lb], axis=2)
    rows = jnp.concatenate([rows, jnp.zeros(rows.shape[:2] + (5, A_WIDTH), F32)], axis=2)
    return rows


def _cast_kernel(w_ref, o_ref):
    o_ref[...] = w_ref[...].astype(o_ref.dtype)


def reorder_cast_w_in(w_in):
    depth, d, n = w_in.shape
    gate_cols = 2 * d
    blk = 256
    nb = n // blk
    shift = (n - gate_cols) // blk
    return pl.pallas_call(
        _cast_kernel,
        grid=(depth, nb),
        in_specs=[pl.BlockSpec((1, d, blk), lambda l, j: (l, 0, (j + shift) % nb))],
        out_specs=pl.BlockSpec((1, d, blk), lambda l, j: (l, 0, j)),
        out_shape=jax.ShapeDtypeStruct((depth, d, n), MXU_DTYPE),
        compiler_params=_cparams(("parallel", "parallel")),
        name="reorder_cast_w_in",
    )(w_in)


def kernel(x, c, ctx, c_ctx, w_ada, b_ada, norm1_g, norm2_g, w_in, lb_logits, hgrn_norm_g, q_norm_g, k_norm_g,
           w_up_a, w_up_b, w_out, w_router_group, b_router_group, w_router_expert, b_router_expert, w_gate, w_up,
           w_down, final_norm_g):
    bsz, n, d = x.shape
    n_ctx = ctx.shape[1]
    depth = w_in.shape[0]
    mx = MXU_DTYPE

    w_in_r = reorder_cast_w_in(w_in)
    w_up_a_c, w_up_b_c, w_out_c = w_up_a.astype(mx), w_up_b.astype(mx), w_out.astype(mx)
    n_r = N_GROUPS + N_EXPERTS
    w_router = jnp.concatenate([w_router_group, w_router_expert, jnp.zeros((depth, d, LANES - n_r), F32)], axis=-1)
    w_router_hi = w_router.astype(jnp.bfloat16)
    w_router_lo = (w_router - w_router_hi.astype(F32)).astype(jnp.bfloat16)
    w_router = jnp.concatenate([w_router_hi, w_router_lo], axis=-1)
    b_router = jnp.concatenate([b_router_group, b_router_expert, jnp.zeros((depth, LANES - n_r), F32)], axis=-1)
    lbp = _lower_bound_rows(lb_logits)
    cos_t, sin_t = rope_tables(n)

    n_rows = -(-(bsz + 1) // 8) * 8
    act = jnp.concatenate([c, c_ctx[None, :], jnp.zeros((n_rows - bsz - 1, d), F32)], axis=0)
    mod = ada_mod(act, w_ada, b_ada)

    h, hc = x, ctx
    pending = None
    zero_state = jnp.zeros((bsz, A_HEADS, LANES, LANES), F32)
    for l in range(depth):
        need_ctx = l < depth - 1
        ml = mod[l, :bsz].reshape(bsz, 1, N_MOD, d)
        mc = jnp.broadcast_to(mod[l, bsz].reshape(1, 1, N_MOD, d), (bsz, 1, N_MOD, d))
        sh1, sc1, g1, sh2, sc2, g2 = (ml[:, :, i] for i in range(N_MOD))
        csh1, csc1, cg1, csh2, csc2, cg2 = (mc[:, :, i] for i in range(N_MOD))

        moe_l = moe_c = None
        if pending is not None:
            y1, y2, slab, g2_prev, cg2_prev = pending
            moe_l = (y1, y2, slab, g2_prev, 0)
            moe_c = (y1, y2, slab, cg2_prev, bsz * n)
        zl, q_l, k_l, v_l, st_l, *h_new = in_proj(h, norm1_g[l], sc1, sh1, w_in_r, l, q_norm_g[l], k_norm_g[l],
                                                  cos_t, sin_t, moe_l)
        zc, q_c, k_c, v_c, st_c, *hc_new = in_proj(hc, norm1_g[l], csc1, csh1, w_in_r, l, q_norm_g[l], k_norm_g[l],
                                                   None, None, moe_c)
        if pending is not None:
            h, hc = h_new[0], hc_new[0]
        ob_c, s_bwd = hgrn_scan(zc, st_c, lbp[1, l], zero_state, True)
        if need_ctx:
            a_c, s_fwd = hgrn_scan(zc, st_c, lbp[0, l], zero_state, False, ob_c, hgrn_norm_g[l])
        else:
            _, s_fwd = hgrn_scan(zc, st_c, lbp[0, l], zero_state, False)
        ob_l, _ = hgrn_scan(zl, st_l, lbp[1, l], s_bwd, True)
        a_l, _ = hgrn_scan(zl, st_l, lbp[0, l], s_fwd, False, ob_l, hgrn_norm_g[l])

        b_l = attention(q_l, jnp.concatenate([k_c, k_l], axis=1), jnp.concatenate([v_c, v_l], axis=1))

        t_total = bsz * (n + n_ctx) if need_ctx else bsz * n
        bufs = (jnp.zeros((d // (2 * LANES), t_total, LANES), jnp.int32),
                jnp.zeros((t_total, LANES), F32)) if need_ctx else None
        h, x2, lg = merge_proj(h, a_l, b_l, zl, w_up_a_c[l], w_up_b_c[l], w_out_c[l], g1, norm2_g[l], sc2, sh2,
                               w_router[l], b_router[l][None, :], t_total, 0, bufs)
        if need_ctx:
            b_c = attention(q_c, k_c, v_c)
            hc, x2, lg = merge_proj(hc, a_c, b_c, zc, w_up_a_c[l], w_up_b_c[l], w_out_c[l], cg1, norm2_g[l],
                                    csc2, csh2, w_router[l], b_router[l][None, :], t_total, bsz * n, (x2, lg))

        y1, y2, slab = hier_moe(x2, lg, w_gate, w_up, w_down, l)
        pending = (y1, y2, slab, g2, cg2)
    y1, y2, slab, g2, _ = pending
    h = moe_combine(h, y1, y2, slab, g2, 0, final_norm_g)
    return h
```

```python
import functools

import jax
import jax.numpy as jnp
from jax import lax
from jax.experimental import pallas as pl
from jax.experimental.pallas import tpu as pltpu
from jax.experimental.pallas import tpu_sc as plsc

F32 = jnp.float32
MXU_DTYPE = jnp.bfloat16

EPS = 1e-6
GRID_W = 64
ROPE_THETA = 10000.0
LANES = 128

A_HEADS = 4
A_DK = 128
A_WIDTH = A_HEADS * A_DK
HGRN_CHUNK = 64
HGRN_SUB = 16
HGRN_STEP_TOKENS = 1024
HGRN_EXP2_CLAMP = 86.0
B_HEADS = 8
B_KV_HEADS = 2
B_HD = 64
B_GROUP = B_HEADS // B_KV_HEADS
B_WIDTH = B_HEADS * B_HD
B_KV_WIDTH = B_KV_HEADS * B_HD
AXIS_DIM = B_HD // 2
N_GROUPS = 4
EXPERTS_PER_GROUP = 8
N_EXPERTS = N_GROUPS * EXPERTS_PER_GROUP
TOP_K = 2
N_MOD = 6

OFF_GATES = 0
OFF_QA = 2048
OFF_FF = OFF_QA + A_WIDTH
OFF_FB = OFF_FF + A_WIDTH
OFF_IA = OFF_FB + A_WIDTH
OFF_OG = OFF_IA + A_WIDTH
OFF_QB = OFF_OG + A_WIDTH
OFF_KB = OFF_QB + B_WIDTH
OFF_VB = OFF_KB + B_KV_WIDTH
D_IN = OFF_VB + B_KV_WIDTH

VMEM_LIMIT = 52 * 1024 * 1024
INPROJ_ROWS = 512
INPROJ_SUB_ROWS = 256
ATTN_Q_ROWS = 128
MERGE_ROWS = 1024
MERGE_SUB_ROWS = 256
ROUTER_ROWS = (1024, 512, 256, 128)
PLAN_TOKENS = (4096, 2048, 1024, 512, 256, 128)
EXPERT_ROWS = 1024
COMBINE_ROWS = 512


def _cparams(sem):
    return pltpu.CompilerParams(dimension_semantics=sem, vmem_limit_bytes=VMEM_LIMIT)


LOG2E = 1.4426950408889634


def _sigmoid(x):
    return 0.5 * jnp.tanh(0.5 * x) + 0.5


def _nt_dot(a, b):
    return lax.dot_general(a, b, (((1,), (1,)), ((), ())), preferred_element_type=F32)


def _ada_kernel(a_ref, w_ref, b_ref, o_ref):
    a = a_ref[...]
    a = a * _sigmoid(a)
    o_ref[0] = jnp.dot(a.astype(MXU_DTYPE), w_ref[0].astype(MXU_DTYPE), preferred_element_type=F32) + b_ref[0]


def ada_mod(act, w_ada, b_ada):
    depth, d, n = w_ada.shape
    r = act.shape[0]
    tn = 1024
    return pl.pallas_call(
        _ada_kernel,
        grid=(depth, n // tn),
        in_specs=[
            pl.BlockSpec((r, d), lambda l, j: (0, 0)),
            pl.BlockSpec((1, d, tn), lambda l, j: (l, 0, j)),
            pl.BlockSpec((1, 1, tn), lambda l, j: (l, 0, j)),
        ],
        out_specs=pl.BlockSpec((1, r, tn), lambda l, j: (l, 0, j)),
        out_shape=jax.ShapeDtypeStruct((depth, r, n), F32),
        compiler_params=_cparams(("parallel", "parallel")),
        name="ada_mod",
    )(act, w_ada, b_ada.reshape(depth, 1, n))


def _modulated_norm(x, g, sc, sh):
    ms = jnp.mean(x * x, axis=-1, keepdims=True)
    return (x * lax.rsqrt(ms + EPS) * g) * (1.0 + sc) + sh


def _split3(x):
    hi = x.astype(jnp.bfloat16)
    r1 = x - hi.astype(F32)
    mid = r1.astype(jnp.bfloat16)
    lo = (r1 - mid.astype(F32)).astype(jnp.bfloat16)
    return hi, mid, lo


def _swap16(x):
    lane = lax.broadcasted_iota(jnp.int32, x.shape, 1)
    return jnp.where(lane % 32 < 16, pltpu.roll(x, LANES - 16, 1), pltpu.roll(x, 16, 1))


def _head_norm_rope(x, gsum, g, cos, sin):
    ssum = sum(jnp.dot(t, gsum, preferred_element_type=F32) for t in _split3(x * x))
    y = x * lax.rsqrt(ssum * (1.0 / B_HD) + EPS) * g
    if cos is not None:
        y = y * cos + _swap16(y) * sin
    return y


def _inproj_kernel(*refs, col_chunk, rope, combine):
    refs = list(refs)
    h_ref, g_ref, sc_ref, sh_ref, w_ref, qg_ref, kg_ref = refs[:7]
    del refs[:7]
    cos = sin = None
    if rope:
        cos, sin = refs[0][...], refs[1][...]
        del refs[:2]
    if combine:
        y1_ref, y2_ref, slab_ref, g2_ref = refs[:4]
        del refs[:4]
        ho_ref = refs.pop()
    z_ref, qo_ref, ko_ref, vo_ref, st_ref = refs
    r_i = lax.broadcasted_iota(jnp.int32, (LANES, LANES), 0)
    c_i = lax.broadcasted_iota(jnp.int32, (LANES, LANES), 1)
    gsum = (r_i // B_HD == c_i // B_HD).astype(jnp.bfloat16)
    tm = h_ref.shape[1]
    sub = min(INPROJ_SUB_ROWS, tm)
    lane = lax.broadcasted_iota(jnp.int32, (sub, LANES), 1)
    stats = [None, None]
    for r in range(tm // sub):
        rs = slice(r * sub, (r + 1) * sub)
        h = h_ref[0, rs, :]
        if combine:
            y12 = (_unpack_rows(jnp.concatenate([y1_ref[j, rs, :] for j in range(y1_ref.shape[0])], axis=1))
                   * slab_ref[rs, 2:3]
                   + _unpack_rows(jnp.concatenate([y2_ref[j, rs, :] for j in range(y2_ref.shape[0])], axis=1))
                   * slab_ref[rs, 3:4])
            h = h + g2_ref[0] * y12
            ho_ref[0, rs, :] = h
        y = _modulated_norm(h, g_ref[...], sc_ref[0], sh_ref[0]).astype(MXU_DTYPE)
        gqa = jnp.dot(y, w_ref[0, :, OFF_QB:], preferred_element_type=F32)
        cs_, sn_ = (cos[rs], sin[rs]) if rope else (None, None)
        for cb in range(B_WIDTH // LANES):
            x = gqa[:, cb * LANES:(cb + 1) * LANES]
            q = _head_norm_rope(x, gsum, qg_ref[...], cs_, sn_) * (B_HD ** -0.5 * LOG2E)
            swapped = pltpu.roll(q, B_HD, 1)
            for half in range(2):
                head = 2 * cb + half
                kv = head // B_GROUP
                src = q if half == kv else swapped
                keep = (lane // B_HD) == kv
                qo_ref[0, head, rs, :] = jnp.where(keep, src, 0.0).astype(qo_ref.dtype)
        kn = _head_norm_rope(gqa[:, B_WIDTH:B_WIDTH + B_KV_WIDTH], gsum, kg_ref[...], cs_, sn_)
        ko_ref[0, rs, :] = kn.astype(ko_ref.dtype)
        vo_ref[0, rs, :] = gqa[:, B_WIDTH + B_KV_WIDTH:].astype(vo_ref.dtype)
        for c in range(z_ref.shape[2] // col_chunk):
            cs = slice(c * col_chunk, (c + 1) * col_chunk)
            zc = jnp.dot(y, w_ref[0, :, cs], preferred_element_type=F32).astype(z_ref.dtype)
            z_ref[0, rs, cs] = zc
            for d, off in enumerate((OFF_FF, OFF_FB)):
                lo, hi = max(off, cs.start), min(off + A_WIDTH, cs.stop)
                if lo < hi:
                    zneg = jnp.minimum(zc[:, lo - cs.start:hi - cs.start].astype(F32), 0.0)
                    blocks = zneg.reshape(sub // HGRN_SUB, HGRN_SUB, hi - lo).sum(axis=1)
                    sums = jnp.min(blocks, axis=0, keepdims=True)
                    for lb in range(sums.shape[1] // LANES):
                        piece = sums[:, lb * LANES:(lb + 1) * LANES]
                        stats[d] = piece if stats[d] is None else jnp.minimum(stats[d], piece)
    st_ref[0, 0] = jnp.concatenate([stats[0], stats[1], jnp.zeros((6, LANES), F32)], axis=0)


def in_proj(h, norm_g, sc, sh, w_in, layer, q_norm_g, k_norm_g, cos_t=None, sin_t=None, moe=None):
    bsz, n, d = h.shape
    tm = min(INPROJ_ROWS, n)
    nt = n // tm
    rope = cos_t is not None
    qg = jnp.tile(q_norm_g.astype(F32), LANES // B_HD).reshape(1, LANES)
    kg = jnp.tile(k_norm_g.astype(F32), LANES // B_HD).reshape(1, LANES)
    vec = pl.BlockSpec((1, LANES), lambda b, i: (0, 0))
    hrow = pl.BlockSpec((1, tm, d), lambda b, i: (b, i, 0))
    per_b = pl.BlockSpec((1, 1, d), lambda b, i: (b, 0, 0))
    in_specs = [hrow, pl.BlockSpec((1, d), lambda b, i: (0, 0)), per_b, per_b,
                pl.BlockSpec((1, d, D_IN), lambda b, i: (layer, 0, 0)), vec, vec]
    args = [h, norm_g.reshape(1, d), sc, sh, w_in, qg, kg]
    if rope:
        in_specs += [pl.BlockSpec((tm, LANES), lambda b, i: (i, 0))] * 2
        args += [cos_t, sin_t]
    row = pl.BlockSpec((1, tm, LANES), lambda b, i: (b, i, 0))
    out_specs = [pl.BlockSpec((1, tm, OFF_QB), lambda b, i: (b, i, 0)),
                 pl.BlockSpec((1, B_HEADS, tm, LANES), lambda b, i: (b, 0, i, 0)), row, row,
                 pl.BlockSpec((1, 1, 8, LANES), lambda b, i: (b, i, 0, 0))]
    out_shape = [jax.ShapeDtypeStruct((bsz, n, OFF_QB), MXU_DTYPE),
                 jax.ShapeDtypeStruct((bsz, B_HEADS, n, LANES), MXU_DTYPE),
                 jax.ShapeDtypeStruct((bsz, n, LANES), MXU_DTYPE),
                 jax.ShapeDtypeStruct((bsz, n, LANES), MXU_DTYPE),
                 jax.ShapeDtypeStruct((bsz, nt, 8, LANES), F32)]
    if moe is not None:
        y1, y2, slab, g2, tok_off = moe
        off = tok_off // tm
        yspec = pl.BlockSpec((y1.shape[0], tm, LANES), lambda b, i: (0, off + b * nt + i, 0))
        in_specs += [yspec, yspec, pl.BlockSpec((tm, LANES), lambda b, i: (off + b * nt + i, 0)), per_b]
        args += [y1, y2, slab, g2]
        out_specs.append(hrow)
        out_shape.append(jax.ShapeDtypeStruct((bsz, n, d), F32))
    return pl.pallas_call(
        functools.partial(_inproj_kernel, col_chunk=768, rope=rope, combine=moe is not None),
        grid=(bsz, nt),
        in_specs=in_specs,
        out_specs=out_specs,
        out_shape=out_shape,
        compiler_params=_cparams(("parallel", "parallel")),
        name="in_proj_rope" if rope else "in_proj",
    )(*args)


def rope_tables(n):
    t = jnp.arange(n)
    r = (t // GRID_W).astype(F32)
    col = (t % GRID_W).astype(F32)
    inv = ROPE_THETA ** (-jnp.arange(0, AXIS_DIM, 2, dtype=F32) / AXIS_DIM)
    ang_r = r[:, None] * inv
    ang_c = col[:, None] * inv
    cos = jnp.concatenate([jnp.cos(ang_r)] * 2 + [jnp.cos(ang_c)] * 2, axis=1)
    sin = jnp.concatenate([-jnp.sin(ang_r), jnp.sin(ang_r), -jnp.sin(ang_c), jnp.sin(ang_c)], axis=1)
    return jnp.tile(cos, (1, 2)), jnp.tile(sin, (1, 2))


def _hgrn_gates(zf, loglb, log1m, onem):
    l1pe = jnp.log(1.0 + jnp.exp(-jnp.abs(zf)))
    c = log1m + (jnp.minimum(zf, 0.0) - l1pe)
    logf = jnp.maximum(loglb, c) + jnp.log(1.0 + jnp.exp(-jnp.abs(loglb - c)))
    k = onem * jnp.exp(jnp.minimum(-zf, 0.0) - l1pe)
    return k, logf * LOG2E


def _cumsum_rows(cum_mat, x):
    out = jnp.dot(cum_mat, jnp.concatenate(_split3(x), axis=1), preferred_element_type=F32)
    w = x.shape[1]
    return out[:, :w] + out[:, w:2 * w] + out[:, 2 * w:]


def _hgrn_kernel(*refs, rev, n_chunks, fuse):
    if fuse:
        safe_ref, q_ref, f_ref, v_ref, lbp_ref, s0_ref, ob_ref, og_ref, ng_ref, o_ref, sT_ref, state = refs
    else:
        safe_ref, q_ref, f_ref, v_ref, lbp_ref, s0_ref, o_ref, sT_ref, state = refs
        ob_ref = og_ref = ng_ref = None
    i = pl.program_id(1)
    nt = pl.num_programs(1)

    @pl.when(i == 0)
    def _():
        state[...] = s0_ref[0]

    loglb = lbp_ref[0:1, :]
    log1m = lbp_ref[1:2, :]
    onem = lbp_ref[2:3, :]
    c = HGRN_CHUNK
    nsub = c // HGRN_SUB
    r_i = lax.broadcasted_iota(jnp.int32, (c, c), 0)
    c_i = lax.broadcasted_iota(jnp.int32, (c, c), 1)
    causal = (c_i >= r_i) if rev else (c_i <= r_i)
    cum_mat = causal.astype(jnp.bfloat16)
    row_blk = lax.broadcasted_iota(jnp.int32, (c, A_DK), 0) // HGRN_SUB

    def prepare(sl, exact):
        k, logf = _hgrn_gates(f_ref[0, sl, :].astype(F32), loglb, log1m, onem)
        zq = q_ref[0, sl, :].astype(F32)
        q = zq * _sigmoid(zq) * (A_DK ** -0.5)
        cum = _cumsum_rows(cum_mat, logf)
        tot = cum[0:1] if rev else cum[c - 1:c]
        qd = (q * jnp.exp2(cum)).astype(MXU_DTYPE)
        kd = (k * jnp.exp2(tot - cum)).astype(MXU_DTYPE)
        refs_j = []
        for j in range(nsub):
            if rev:
                r = cum[(j + 1) * HGRN_SUB:(j + 1) * HGRN_SUB + 1] if j < nsub - 1 else jnp.zeros_like(tot)
            else:
                r = cum[j * HGRN_SUB - 1:j * HGRN_SUB] if j > 0 else jnp.zeros_like(tot)
            refs_j.append(r)
        ref_rows = jnp.concatenate([jnp.broadcast_to(r, (HGRN_SUB, r.shape[1])) for r in refs_j], axis=0)
        qt = (q * jnp.exp2(cum - ref_rows)).astype(MXU_DTYPE)
        if exact:
            blk512 = lax.broadcasted_iota(jnp.int32, cum.shape, 0) // HGRN_SUB
            kt = [jnp.where((blk512 > j) if rev else (blk512 < j),
                            k * jnp.exp2(jnp.minimum(r - cum, 0.0)), 0.0).astype(MXU_DTYPE)
                  for j, r in enumerate(refs_j)]
            return qd, kd, jnp.exp2(tot), qt, kt, (q, k, cum)
        kt = [(k * jnp.exp2(jnp.minimum(r - cum, HGRN_EXP2_CLAMP))).astype(MXU_DTYPE) for r in refs_j]
        return qd, kd, jnp.exp2(tot), qt, kt, None

    def exact_diagonal(q, k, cum, hs):
        lane16 = lax.broadcasted_iota(jnp.int32, (HGRN_SUB, HGRN_SUB), 1)
        strips = []
        for j in range(nsub):
            rows = slice(j * HGRN_SUB, (j + 1) * HGRN_SUB)
            qb, kb, cb = q[rows, hs], k[rows, hs], cum[rows, hs]
            blk = jnp.zeros((HGRN_SUB, HGRN_SUB), F32)
            for t in range(HGRN_SUB):
                w = qb * kb[t:t + 1] * jnp.exp2(jnp.minimum(cb - cb[t:t + 1], 0.0))
                blk = jnp.where(lane16 == t, jnp.sum(w, axis=-1, keepdims=True), blk)
            pieces = [blk if jj == j else jnp.zeros((HGRN_SUB, HGRN_SUB), F32) for jj in range(nsub)]
            strips.append(jnp.concatenate(pieces, axis=1))
        return jnp.concatenate(strips, axis=0)

    def finish(sl, prepped, s):
        qd, kd, decay, qt, kt, raw = prepped
        v = v_ref[0, sl, :]
        outs = []
        for h in range(A_HEADS):
            hs = slice(h * A_DK, (h + 1) * A_DK)
            q_big = jnp.concatenate([jnp.where(row_blk == j, qt[:, hs], jnp.zeros_like(qt[:, hs]))
                                     for j in range(nsub)], axis=1)
            k_big = jnp.concatenate([kt[j][:, hs] for j in range(nsub)], axis=1)
            scores = _nt_dot(q_big, k_big)
            if raw is not None:
                scores = scores + exact_diagonal(*raw, hs)
            scores = jnp.where(causal, scores, 0.0)
            o_h = _nt_dot(qd[:, hs], s[h].astype(MXU_DTYPE))
            o_h = o_h + jnp.dot(scores.astype(MXU_DTYPE), v[:, hs], preferred_element_type=F32)
            upd = lax.dot_general(v[:, hs], kd[:, hs], (((0,), (0,)), ((), ())), preferred_element_type=F32)
            s[h] = s[h] * decay[:, hs] + upd
            if fuse:
                o_h = o_h + ob_ref[0, sl, hs]
                o_h = o_h * lax.rsqrt(jnp.mean(o_h * o_h, axis=-1, keepdims=True) + EPS)
            outs.append(o_h)
        o = jnp.concatenate(outs, axis=1)
        if fuse:
            og = og_ref[0, sl, :].astype(F32)
            o = o * ng_ref[...] * (og * _sigmoid(og))
        o_ref[0, sl, :] = o.astype(o_ref.dtype)

    safe = safe_ref[pl.program_id(0), (nt - 1 - i) if rev else i] != 0

    @pl.when(safe)
    def _():
        s = [state[h] for h in range(A_HEADS)]
        order = list(range(n_chunks - 1, -1, -1) if rev else range(n_chunks))
        slices = [slice(ci * c, (ci + 1) * c) for ci in order]
        ready = prepare(slices[0], False)
        for n, sl in enumerate(slices):
            prepped = ready
            if n + 1 < len(slices):
                ready = prepare(slices[n + 1], False)
            finish(sl, prepped, s)
        for h in range(A_HEADS):
            state[h] = s[h]

    @pl.when(jnp.logical_not(safe))
    def _():
        def chunk(n, carry):
            ci = (n_chunks - 1 - n) if rev else n
            sl = pl.ds(pl.multiple_of(ci * c, c), c)
            s = [state[h] for h in range(A_HEADS)]
            finish(sl, prepare(sl, True), s)
            for h in range(A_HEADS):
                state[h] = s[h]
            return carry

        lax.fori_loop(0, n_chunks, chunk, 0)

    @pl.when(i == pl.num_programs(1) - 1)
    def _():
        sT_ref[0] = state[...]


def hgrn_safe_flags(stats, direction, n, step_tokens):
    worst = stats[:, :, direction, :].min(axis=-1)
    bsz, nb = worst.shape
    per_step = nb * step_tokens // n
    worst = worst.reshape(bsz, nb // per_step, per_step).min(axis=-1)
    return (worst * LOG2E - HGRN_SUB >= -HGRN_EXP2_CLAMP).astype(jnp.int32)


def hgrn_scan(z, stats, lbp, s0, rev, o_other=None, norm_g=None):
    bsz, n, _ = z.shape
    fuse = o_other is not None
    t = min(HGRN_STEP_TOKENS, n)
    nt = n // t
    f_off = (OFF_FB if rev else OFF_FF) // A_WIDTH
    safe = hgrn_safe_flags(stats, 1 if rev else 0, n, t)

    def tmap(i):
        return nt - 1 - i if rev else i

    def col(blk):
        return pl.BlockSpec((1, t, A_WIDTH), lambda b, i, sf: (b, tmap(i), blk))

    state_spec = pl.BlockSpec((1, A_HEADS, LANES, LANES), lambda b, i, sf: (b, 0, 0, 0))
    in_specs = [col(OFF_QA // A_WIDTH), col(f_off), col(OFF_IA // A_WIDTH),
                pl.BlockSpec((8, A_WIDTH), lambda b, i, sf: (0, 0)), state_spec]
    args = [z, z, z, lbp, s0]
    if fuse:
        in_specs += [col(0), col(OFF_OG // A_WIDTH), pl.BlockSpec((1, A_WIDTH), lambda b, i, sf: (0, 0))]
        args += [o_other, z, norm_g.reshape(1, A_WIDTH)]
    kern = functools.partial(_hgrn_kernel, rev=rev, n_chunks=t // HGRN_CHUNK, fuse=fuse)
    grid_spec = pltpu.PrefetchScalarGridSpec(
        num_scalar_prefetch=1,
        grid=(bsz, nt),
        in_specs=in_specs,
        out_specs=[col(0), state_spec],
        scratch_shapes=[pltpu.VMEM((A_HEADS, LANES, LANES), F32)],
    )
    return pl.pallas_call(
        kern,
        grid_spec=grid_spec,
        out_shape=[jax.ShapeDtypeStruct((bsz, n, A_WIDTH), MXU_DTYPE if fuse else F32),
                   jax.ShapeDtypeStruct((bsz, A_HEADS, LANES, LANES), F32)],
        compiler_params=_cparams(("parallel", "arbitrary")),
        name="hgrn_bwd" if rev else "hgrn_fwd",
    )(safe, *args)


def _attn_kernel(q_ref, k_ref, v_ref, o_ref):
    tq = q_ref.shape[2]
    rows = B_GROUP * tq
    lane = lax.broadcasted_iota(jnp.int32, v_ref.shape[1:], 1)
    lane_o = lax.broadcasted_iota(jnp.int32, (tq, LANES), 1)
    s_all = _nt_dot(q_ref[0].reshape(B_HEADS * tq, LANES), k_ref[0])
    for kv in range(B_KV_HEADS):
        s = s_all[kv * rows:(kv + 1) * rows]
        vs = jnp.where((lane // B_HD) == kv, v_ref[0], jnp.ones_like(v_ref[0]))
        p = jnp.exp2(s - jnp.max(s, axis=-1, keepdims=True))
        a = jnp.dot(p.astype(MXU_DTYPE), vs, preferred_element_type=F32)
        o_full = a / pltpu.roll(a, B_HD, 1)
        o_swapped = pltpu.roll(o_full, B_HD, 1)
        for pair in range(B_GROUP // 2):
            g0, g1 = 2 * pair, 2 * pair + 1
            lo_src = o_full if kv == 0 else o_swapped
            hi_src = o_swapped if kv == 0 else o_full
            blk = jnp.where(lane_o < B_HD, lo_src[g0 * tq:(g0 + 1) * tq], hi_src[g1 * tq:(g1 + 1) * tq])
            cb = kv * (B_GROUP // 2) + pair
            o_ref[0, :, cb * LANES:(cb + 1) * LANES] = blk.astype(o_ref.dtype)


def attention(q, k, v):
    bsz, _, n, _ = q.shape
    nkeys = k.shape[1]
    tq = min(ATTN_Q_ROWS, n)
    return pl.pallas_call(
        _attn_kernel,
        grid=(bsz, n // tq),
        in_specs=[pl.BlockSpec((1, B_HEADS, tq, LANES), lambda b, i: (b, 0, i, 0)),
                  pl.BlockSpec((1, nkeys, LANES), lambda b, i: (b, 0, 0)),
                  pl.BlockSpec((1, nkeys, LANES), lambda b, i: (b, 0, 0))],
        out_specs=pl.BlockSpec((1, tq, B_WIDTH), lambda b, i: (b, i, 0)),
        out_shape=jax.ShapeDtypeStruct((bsz, n, B_WIDTH), MXU_DTYPE),
        compiler_params=_cparams(("parallel", "parallel")),
        name="gqa_attention",
    )(q, k, v)


def _pack_rows(x):
    w = x.shape[1] // 2
    bits = lax.bitcast_convert_type(x.astype(jnp.bfloat16).astype(F32), jnp.uint32)
    packed = (bits[:, :w] & jnp.uint32(0xFFFF0000)) | (bits[:, w:] >> 16)
    return lax.bitcast_convert_type(packed, jnp.int32)


def _unpack_rows(p):
    bits = lax.bitcast_convert_type(p, jnp.uint32)
    hi = lax.bitcast_convert_type(bits & jnp.uint32(0xFFFF0000), F32)
    lo = lax.bitcast_convert_type(bits << 16, F32)
    return jnp.concatenate([hi, lo], axis=1)


def _store_chunks(ref, packed):
    for j in range(ref.shape[0]):
        ref[j] = packed[:, j * LANES:(j + 1) * LANES]


def _load_chunks(ref):
    return jnp.concatenate([ref[j] for j in range(ref.shape[0])], axis=1)


def _merge_kernel(*refs, aliased):
    if aliased:
        refs = refs[:13] + refs[15:]
    (h_ref, a_ref, b_ref, zg_ref, wa_ref, wb_ref, wo_ref, g1_ref, n2_ref, sc_ref, sh_ref, wr_ref, br_ref,
     ho_ref, x2_ref, lg_ref) = refs
    d = h_ref.shape[2]
    tm = h_ref.shape[1]
    sub = min(MERGE_SUB_ROWS, tm)
    for r in range(tm // sub):
        rs = slice(r * sub, (r + 1) * sub)
        ua = jnp.dot(a_ref[0, rs, :], wa_ref[...], preferred_element_type=F32)
        ub = jnp.dot(b_ref[0, rs, :], wb_ref[...], preferred_element_type=F32)
        ga = _sigmoid(zg_ref[0, rs, 0:d].astype(F32))
        gb = _sigmoid(zg_ref[0, rs, d:2 * d].astype(F32))
        mix = (ga * ua + gb * ub).astype(MXU_DTYPE)
        h_new = h_ref[0, rs, :] + g1_ref[0] * jnp.dot(mix, wo_ref[...], preferred_element_type=F32)
        ho_ref[0, rs, :] = h_new
        x2 = _modulated_norm(h_new, n2_ref[...], sc_ref[0], sh_ref[0])
        packed = _pack_rows(x2)
        for j in range(x2_ref.shape[0]):
            x2_ref[j, rs, :] = packed[:, j * LANES:(j + 1) * LANES]
        x_hi = x2.astype(jnp.bfloat16)
        x_lo = (x2 - x_hi.astype(F32)).astype(jnp.bfloat16)
        hi_both = jnp.dot(x_hi, wr_ref[...], preferred_element_type=F32)
        lo_hi = jnp.dot(x_lo, wr_ref[:, 0:LANES], preferred_element_type=F32)
        lg_ref[rs, :] = hi_both[:, 0:LANES] + hi_both[:, LANES:] + lo_hi + br_ref[...]


def merge_proj(h, a, b, z, w_up_a, w_up_b, w_out, g1, norm2_g, sc2, sh2, w_router, b_router, t_total, tok_off,
               bufs=None):
    bsz, n, d = h.shape
    tm = min(MERGE_ROWS, n)
    nt = n // tm
    off = tok_off // tm
    n_chunks = d // (2 * LANES)
    row = lambda w: pl.BlockSpec((1, tm, w), lambda b_, i: (b_, i, 0))
    full = lambda s: pl.BlockSpec(s, lambda b_, i: (0,) * len(s))
    per_b = pl.BlockSpec((1, 1, d), lambda b_, i: (b_, 0, 0))
    in_specs = [row(d), row(A_WIDTH), row(B_WIDTH),
                pl.BlockSpec((1, tm, 2 * d), lambda b_, i: (b_, i, OFF_GATES)),
                full((A_WIDTH, d)), full((B_WIDTH, d)), full((d, d)),
                per_b, full((1, d)), per_b, per_b, full((d, 2 * LANES)), full((1, LANES))]
    args = [h, a, b, z, w_up_a, w_up_b, w_out, g1, norm2_g.reshape(1, d), sc2, sh2, w_router, b_router]
    aliases = {}
    if bufs is not None:
        in_specs += [pl.BlockSpec(memory_space=pl.ANY)] * 2
        args += list(bufs)
        aliases = {13: 1, 14: 2}
    return pl.pallas_call(
        functools.partial(_merge_kernel, aliased=bufs is not None),
        grid=(bsz, nt),
        in_specs=in_specs,
        out_specs=[row(d),
                   pl.BlockSpec((n_chunks, tm, LANES), lambda b_, i: (0, off + b_ * nt + i, 0)),
                   pl.BlockSpec((tm, LANES), lambda b_, i: (off + b_ * nt + i, 0))],
        out_shape=[jax.ShapeDtypeStruct((bsz, n, d), F32),
                   jax.ShapeDtypeStruct((n_chunks, t_total, LANES), jnp.int32),
                   jax.ShapeDtypeStruct((t_total, LANES), F32)],
        input_output_aliases=aliases,
        compiler_params=_cparams(("parallel", "parallel")),
        name="merge_proj",
    )(*args)


def _first_lane_where(cond, lane):
    return jnp.min(jnp.where(cond, lane, LANES), axis=-1, keepdims=True)


def _router_kernel(lg_ref, slab_ref, slab_t_ref, cnt_ref, base):
    @pl.when(pl.program_id(0) == 0)
    def _():
        base[...] = jnp.zeros_like(base)

    lg = lg_ref[...]
    tr = lg.shape[0]
    lane = lax.broadcasted_iota(jnp.int32, lg.shape, 1)
    neg = jnp.float32(-jnp.inf)
    is_g = lane < N_GROUPS
    gl = jnp.where(is_g, lg, neg)
    gmax = jnp.max(gl, axis=-1, keepdims=True)
    gsel = _first_lane_where(gl == gmax, lane)
    p_group = 1.0 / jnp.sum(jnp.where(is_g, jnp.exp(lg - gmax), 0.0), axis=-1, keepdims=True)
    lo = N_GROUPS + gsel * EXPERTS_PER_GROUP
    in_grp = (lane >= lo) & (lane < lo + EXPERTS_PER_GROUP)
    el = jnp.where(in_grp, lg, neg)
    m1 = jnp.max(el, axis=-1, keepdims=True)
    i1 = _first_lane_where(el == m1, lane)
    el2 = jnp.where(lane == i1, neg, el)
    m2 = jnp.max(el2, axis=-1, keepdims=True)
    i2 = _first_lane_where(el2 == m2, lane)
    zsum = jnp.sum(jnp.where(in_grp, jnp.exp(lg - m1), 0.0), axis=-1, keepdims=True)
    p1 = 1.0 / zsum
    p2 = jnp.exp(m2 - m1) / zsum
    w1 = p1 / (p1 + p2) * p_group
    w2 = p2 / (p1 + p2) * p_group
    hit1 = lane == i1
    hit2 = lane == i2
    onehot = (hit1 | hit2).astype(F32)
    r_i = lax.broadcasted_iota(jnp.int32, (tr, tr), 0)
    c_i = lax.broadcasted_iota(jnp.int32, (tr, tr), 1)
    before = (c_i < r_i).astype(MXU_DTYPE)
    prior = jnp.dot(before, onehot.astype(MXU_DTYPE), preferred_element_type=F32) + base[0:1, :]
    r1 = jnp.sum(jnp.where(hit1, prior, 0.0), axis=-1, keepdims=True)
    r2 = jnp.sum(jnp.where(hit2, prior, 0.0), axis=-1, keepdims=True)
    new_base = base[0:1, :] + jnp.sum(onehot, axis=0, keepdims=True)
    base[...] = jnp.broadcast_to(new_base, base.shape)
    cnt_ref[...] = jnp.broadcast_to(new_base, cnt_ref.shape)
    vals = [(i1 - N_GROUPS).astype(F32), (i2 - N_GROUPS).astype(F32), w1, w2, r1, r2]
    slab = jnp.zeros_like(lg)
    for idx, val in enumerate(vals):
        slab = jnp.where(lane == idx, val, slab)
    slab_ref[...] = slab
    slab_t_ref[...] = slab.T[0:8, :]


def route(logits):
    t = logits.shape[0]
    tr = next(c for c in ROUTER_ROWS if t % c == 0)
    return pl.pallas_call(
        _router_kernel,
        grid=(t // tr,),
        in_specs=[pl.BlockSpec((tr, LANES), lambda i: (i, 0))],
        out_specs=[pl.BlockSpec((tr, LANES), lambda i: (i, 0)), pl.BlockSpec((8, tr), lambda i: (0, i)),
                   pl.BlockSpec((8, LANES), lambda i: (0, 0))],
        out_shape=[jax.ShapeDtypeStruct((t, LANES), F32), jax.ShapeDtypeStruct((8, t), F32),
                   jax.ShapeDtypeStruct((8, LANES), F32)],
        scratch_shapes=[pltpu.VMEM((8, LANES), F32)],
        compiler_params=_cparams(("arbitrary",)),
        name="moe_route",
    )(logits)


def _expert_kernel(be_ref, nv_ref, x_ref, wg_ref, wu_ref, wd_ref, o_ref, wg_s, wu_s, wd_s):
    j = pl.program_id(0)
    nv = nv_ref[j]

    @pl.when((j == 0) | (be_ref[j] != be_ref[jnp.maximum(j - 1, 0)]))
    def _():
        wg_s[...] = wg_ref[0, 0].astype(wg_s.dtype)
        wu_s[...] = wu_ref[0, 0].astype(wu_s.dtype)
        wd_s[...] = wd_ref[0, 0].astype(wd_s.dtype)

    @pl.when(nv > 0)
    def _():
        x = _unpack_rows(_load_chunks(x_ref))
        rows = lax.broadcasted_iota(jnp.int32, x.shape, 0)
        x = jnp.where(rows < nv, x, 0.0).astype(MXU_DTYPE)
        hg = jnp.dot(x, wg_s[...], preferred_element_type=F32)
        hu = jnp.dot(x, wu_s[...], preferred_element_type=F32)
        hb = (hg * _sigmoid(hg) * hu).astype(MXU_DTYPE)
        _store_chunks(o_ref, _pack_rows(jnp.dot(hb, wd_s[...], preferred_element_type=F32)))

    @pl.when(nv == 0)
    def _():
        o_ref[...] = jnp.zeros_like(o_ref)


def expert_ffn(x_sorted, block_e, nvalid, w_gate, w_up, w_down, layer, tmx):
    n_chunks, r, _ = x_sorted.shape
    _, _, d, de = w_gate.shape
    grid_spec = pltpu.PrefetchScalarGridSpec(
        num_scalar_prefetch=2,
        grid=(r // tmx,),
        in_specs=[pl.BlockSpec((n_chunks, tmx, LANES), lambda j, be, nv: (0, j, 0)),
                  pl.BlockSpec((1, 1, d, de), lambda j, be, nv: (layer, be[j], 0, 0)),
                  pl.BlockSpec((1, 1, d, de), lambda j, be, nv: (layer, be[j], 0, 0)),
                  pl.BlockSpec((1, 1, de, d), lambda j, be, nv: (layer, be[j], 0, 0))],
        out_specs=pl.BlockSpec((n_chunks, tmx, LANES), lambda j, be, nv: (0, j, 0)),
        scratch_shapes=[pltpu.VMEM((d, de), MXU_DTYPE), pltpu.VMEM((d, de), MXU_DTYPE),
                        pltpu.VMEM((de, d), MXU_DTYPE)],
    )
    return pl.pallas_call(
        _expert_kernel,
        grid_spec=grid_spec,
        out_shape=jax.ShapeDtypeStruct((n_chunks, r, LANES), jnp.int32),
        compiler_params=_cparams(("arbitrary",)),
        name="expert_ffn",
    )(block_e, nvalid, x_sorted, w_gate, w_up, w_down)


def _combine_final_kernel(h_ref, y1_ref, y2_ref, slab_ref, g2_ref, fn_ref, o_ref):
    y = (_unpack_rows(_load_chunks(y1_ref)) * slab_ref[:, 2:3]
         + _unpack_rows(_load_chunks(y2_ref)) * slab_ref[:, 3:4])
    h_new = h_ref[0] + g2_ref[0] * y
    ms = jnp.mean(h_new * h_new, axis=-1, keepdims=True)
    o_ref[0] = h_new * lax.rsqrt(ms + EPS) * fn_ref[...]


def moe_combine_final(h, y1, y2, slab, g2, final_g):
    bsz, n, d = h.shape
    tm = min(COMBINE_ROWS, n)
    nt = n // tm
    yspec = pl.BlockSpec((y1.shape[0], tm, LANES), lambda b, i: (0, b * nt + i, 0))
    return pl.pallas_call(
        _combine_final_kernel,
        grid=(bsz, nt),
        in_specs=[pl.BlockSpec((1, tm, d), lambda b, i: (b, i, 0)), yspec, yspec,
                  pl.BlockSpec((tm, LANES), lambda b, i: (b * nt + i, 0)),
                  pl.BlockSpec((1, 1, d), lambda b, i: (b, 0, 0)),
                  pl.BlockSpec((1, d), lambda b, i: (0, 0))],
        out_specs=pl.BlockSpec((1, tm, d), lambda b, i: (b, i, 0)),
        out_shape=jax.ShapeDtypeStruct((bsz, n, d), F32),
        compiler_params=_cparams(("parallel", "parallel")),
        name="moe_combine_final",
    )(h, y1, y2, slab, g2, final_g.reshape(1, d))


SC_WINDOW = 128


def _sc_mesh():
    return plsc.VectorSubcoreMesh(core_axis_name="core", subcore_axis_name="subcore")


def sc_gather_rows(table, idx):
    n = idx.shape[0]

    @pl.kernel(out_type=jax.ShapeDtypeStruct((n, LANES), table.dtype), mesh=_sc_mesh())
    def gather_kernel(t_hbm, i_hbm, o_hbm):
        def body(i_vmem, o_vmem):
            pltpu.sync_copy(t_hbm.at[i_vmem.at[0]], o_vmem)

        pltpu.emit_pipeline(
            body,
            grid=(n // SC_WINDOW,),
            in_specs=[pl.BlockSpec((1, SC_WINDOW), lambda i: (0, i))],
            out_specs=[pl.BlockSpec((SC_WINDOW, LANES), lambda i: (i, 0))],
            core_axis_name=("core", "subcore"),
            dimension_semantics=(pltpu.PARALLEL,),
            trace_scopes=False,
        )(i_hbm, o_hbm)

    return gather_kernel(table, idx.reshape(1, n))


def sc_scatter_rows2(rows, idx_a, idx_b, n_out):
    n = rows.shape[0]

    @pl.kernel(out_type=jax.ShapeDtypeStruct((n_out, LANES), rows.dtype), mesh=_sc_mesh())
    def scatter_kernel(r_hbm, ia_hbm, ib_hbm, o_hbm):
        def body(r_vmem, ia_vmem, ib_vmem):
            pltpu.sync_copy(r_vmem, o_hbm.at[ia_vmem.at[0]])
            pltpu.sync_copy(r_vmem, o_hbm.at[ib_vmem.at[0]])

        pltpu.emit_pipeline(
            body,
            grid=(n // SC_WINDOW,),
            in_specs=[pl.BlockSpec((SC_WINDOW, LANES), lambda i: (i, 0)),
                      pl.BlockSpec((1, SC_WINDOW), lambda i: (0, i)),
                      pl.BlockSpec((1, SC_WINDOW), lambda i: (0, i))],
            out_specs=[],
            core_axis_name=("core", "subcore"),
            dimension_semantics=(pltpu.PARALLEL,),
            trace_scopes=False,
        )(r_hbm, ia_hbm, ib_hbm)

    return scatter_kernel(rows, idx_a.reshape(1, n), idx_b.reshape(1, n))


def _plan_kernel(ps_ref, st_ref, ia_ref, ib_ref, *, n_rows):
    e = st_ref[0:2, :].astype(jnp.int32)
    dest = st_ref[4:6, :].astype(jnp.int32)
    for ex in range(N_EXPERTS):
        dest = dest + jnp.where(e == ex, ps_ref[ex], 0)
    for k, ref in enumerate((ia_ref, ib_ref)):
        for j in range(ref.shape[0]):
            ref[j:j + 1, :] = dest[k:k + 1, :] + j * n_rows


def dispatch_indices(slab_t, pad_start, n_chunks, n_rows):
    t = slab_t.shape[1]
    tp = next(c for c in PLAN_TOKENS if t % c == 0)
    grid_spec = pltpu.PrefetchScalarGridSpec(
        num_scalar_prefetch=1,
        grid=(t // tp,),
        in_specs=[pl.BlockSpec((8, tp), lambda i, ps: (0, i))],
        out_specs=[pl.BlockSpec((n_chunks, tp), lambda i, ps: (0, i))] * 2,
    )
    return pl.pallas_call(
        functools.partial(_plan_kernel, n_rows=n_rows),
        grid_spec=grid_spec,
        out_shape=[jax.ShapeDtypeStruct((n_chunks, t), jnp.int32)] * 2,
        compiler_params=_cparams(("parallel",)),
        name="moe_plan",
    )(pad_start, slab_t)


def hier_moe(x2, logits, w_gate, w_up, w_down, layer):
    n_chunks, t, _ = x2.shape
    tmx = min(EXPERT_ROWS, t)
    n_blocks = (t * TOP_K + N_EXPERTS * (tmx - 1)) // tmx + 1
    r = n_blocks * tmx
    slab, slab_t, counts = route(logits)
    cnt = counts[0, N_GROUPS:N_GROUPS + N_EXPERTS].astype(jnp.int32)
    padded = (cnt + tmx - 1) // tmx * tmx
    pad_end = jnp.cumsum(padded)
    pad_start = pad_end - padded
    blk_start = jnp.arange(n_blocks, dtype=jnp.int32) * tmx
    block_e = jnp.minimum(jnp.sum(pad_end[None, :] <= blk_start[:, None], axis=1), N_EXPERTS - 1).astype(jnp.int32)
    nvalid = jnp.clip(pad_start[block_e] + cnt[block_e] - blk_start, 0, tmx).astype(jnp.int32)
    idx1, idx2 = dispatch_indices(slab_t, pad_start.astype(jnp.int32), n_chunks, r)
    idx1, idx2 = idx1.reshape(-1), idx2.reshape(-1)
    x_sorted = sc_scatter_rows2(x2.reshape(n_chunks * t, LANES), idx1, idx2, n_chunks * r)
    y_sorted = expert_ffn(x_sorted.reshape(n_chunks, r, LANES), block_e, nvalid, w_gate, w_up, w_down, layer, tmx)
    y_flat = y_sorted.reshape(n_chunks * r, LANES)
    y1 = sc_gather_rows(y_flat, idx1).reshape(n_chunks, t, LANES)
    y2 = sc_gather_rows(y_flat, idx2).reshape(n_chunks, t, LANES)
    return y1, y2, slab


def _lower_bound_rows(lb_logits):
    lb = jnp.cumsum(jax.nn.softmax(lb_logits.astype(F32), axis=1), axis=1)
    lb = lb - lb[:, :1]
    rows = jnp.stack([jnp.log(lb), jnp.log1p(-lb), 1.0 - lb], axis=2)
    rows = jnp.concatenate([rows, jnp.zeros(rows.shape[:2] + (5, A_WIDTH), F32)], axis=2)
    return rows


def _cast_kernel(w_ref, o_ref):
    o_ref[...] = w_ref[...].astype(o_ref.dtype)


def reorder_cast_w_in(w_in):
    depth, d, n = w_in.shape
    gate_cols = 2 * d
    blk = 256
    nb = n // blk
    shift = (n - gate_cols) // blk
    return pl.pallas_call(
        _cast_kernel,
        grid=(depth, nb),
        in_specs=[pl.BlockSpec((1, d, blk), lambda l, j: (l, 0, (j + shift) % nb))],
        out_specs=pl.BlockSpec((1, d, blk), lambda l, j: (l, 0, j)),
        out_shape=jax.ShapeDtypeStruct((depth, d, n), MXU_DTYPE),
        compiler_params=_cparams(("parallel", "parallel")),
        name="reorder_cast_w_in",
    )(w_in)


def kernel(x, c, ctx, c_ctx, w_ada, b_ada, norm1_g, norm2_g, w_in, lb_logits, hgrn_norm_g, q_norm_g, k_norm_g,
           w_up_a, w_up_b, w_out, w_router_group, b_router_group, w_router_expert, b_router_expert, w_gate, w_up,
           w_down, final_norm_g):
    bsz, n, d = x.shape
    n_ctx = ctx.shape[1]
    depth = w_in.shape[0]
    mx = MXU_DTYPE

    w_in_r = reorder_cast_w_in(w_in)
    w_up_a_c, w_up_b_c, w_out_c = w_up_a.astype(mx), w_up_b.astype(mx), w_out.astype(mx)
    n_r = N_GROUPS + N_EXPERTS
    w_router = jnp.concatenate([w_router_group, w_router_expert, jnp.zeros((depth, d, LANES - n_r), F32)], axis=-1)
    w_router_hi = w_router.astype(jnp.bfloat16)
    w_router_lo = (w_router - w_router_hi.astype(F32)).astype(jnp.bfloat16)
    w_router = jnp.concatenate([w_router_hi, w_router_lo], axis=-1)
    b_router = jnp.concatenate([b_router_group, b_router_expert, jnp.zeros((depth, LANES - n_r), F32)], axis=-1)
    lbp = _lower_bound_rows(lb_logits)
    cos_t, sin_t = rope_tables(n)

    n_rows = -(-(bsz + 1) // 8) * 8
    act = jnp.concatenate([c, c_ctx[None, :], jnp.zeros((n_rows - bsz - 1, d), F32)], axis=0)
    mod = ada_mod(act, w_ada, b_ada)

    h, hc = x, ctx
    pending = None
    zero_state = jnp.zeros((bsz, A_HEADS, LANES, LANES), F32)
    for l in range(depth):
        need_ctx = l < depth - 1
        ml = mod[l, :bsz].reshape(bsz, 1, N_MOD, d)
        mc = jnp.broadcast_to(mod[l, bsz].reshape(1, 1, N_MOD, d), (bsz, 1, N_MOD, d))
        sh1, sc1, g1, sh2, sc2, g2 = (ml[:, :, i] for i in range(N_MOD))
        csh1, csc1, cg1, csh2, csc2, cg2 = (mc[:, :, i] for i in range(N_MOD))

        moe_l = moe_c = None
        if pending is not None:
            y1, y2, slab, g2_prev, cg2_prev = pending
            moe_l = (y1, y2, slab, g2_prev, 0)
            moe_c = (y1, y2, slab, cg2_prev, bsz * n)
        zl, q_l, k_l, v_l, st_l, *h_new = in_proj(h, norm1_g[l], sc1, sh1, w_in_r, l, q_norm_g[l], k_norm_g[l],
                                                  cos_t, sin_t, moe_l)
        zc, q_c, k_c, v_c, st_c, *hc_new = in_proj(hc, norm1_g[l], csc1, csh1, w_in_r, l, q_norm_g[l], k_norm_g[l],
                                                   None, None, moe_c)
        if pending is not None:
            h, hc = h_new[0], hc_new[0]
        ob_c, s_bwd = hgrn_scan(zc, st_c, lbp[1, l], zero_state, True)
        if need_ctx:
            a_c, s_fwd = hgrn_scan(zc, st_c, lbp[0, l], zero_state, False, ob_c, hgrn_norm_g[l])
        else:
            _, s_fwd = hgrn_scan(zc, st_c, lbp[0, l], zero_state, False)
        ob_l, _ = hgrn_scan(zl, st_l, lbp[1, l], s_bwd, True)
        a_l, _ = hgrn_scan(zl, st_l, lbp[0, l], s_fwd, False, ob_l, hgrn_norm_g[l])

        b_l = attention(q_l, jnp.concatenate([k_c, k_l], axis=1), jnp.concatenate([v_c, v_l], axis=1))

        t_total = bsz * (n + n_ctx) if need_ctx else bsz * n
        bufs = (jnp.zeros((d // (2 * LANES), t_total, LANES), jnp.int32),
                jnp.zeros((t_total, LANES), F32)) if need_ctx else None
        h, x2, lg = merge_proj(h, a_l, b_l, zl, w_up_a_c[l], w_up_b_c[l], w_out_c[l], g1, norm2_g[l], sc2, sh2,
                               w_router[l], b_router[l][None, :], t_total, 0, bufs)
        if need_ctx:
            b_c = attention(q_c, k_c, v_c)
            hc, x2, lg = merge_proj(hc, a_c, b_c, zc, w_up_a_c[l], w_up_b_c[l], w_out_c[l], cg1, norm2_g[l],
                                    csc2, csh2, w_router[l], b_router[l][None, :], t_total, bsz * n, (x2, lg))

        y1, y2, slab = hier_moe(x2, lg, w_gate, w_up, w_down, l)
        pending = (y1, y2, slab, g2, cg2)
    y1, y2, slab, g2, _ = pending
    return moe_combine_final(h, y1, y2, slab, g2, final_norm_g)
```

```python
import functools

import jax
import jax.numpy as jnp
from jax import lax
from jax.experimental import pallas as pl
from jax.experimental.pallas import tpu as pltpu
from jax.experimental.pallas import tpu_sc as plsc

F32 = jnp.float32
MXU_DTYPE = jnp.bfloat16

EPS = 1e-6
GRID_W = 64
ROPE_THETA = 10000.0
LANES = 128

A_HEADS = 4
A_DK = 128
A_WIDTH = A_HEADS * A_DK
HGRN_CHUNK = 64
HGRN_SUB = 16
HGRN_STEP_TOKENS = 1024
HGRN_EXP2_CLAMP = 86.0
B_HEADS = 8
B_KV_HEADS = 2
B_HD = 64
B_GROUP = B_HEADS // B_KV_HEADS
B_WIDTH = B_HEADS * B_HD
B_KV_WIDTH = B_KV_HEADS * B_HD
AXIS_DIM = B_HD // 2
N_GROUPS = 4
EXPERTS_PER_GROUP = 8
N_EXPERTS = N_GROUPS * EXPERTS_PER_GROUP
TOP_K = 2
N_MOD = 6

OFF_GATES = 0
OFF_QA = 2048
OFF_FF = OFF_QA + A_WIDTH
OFF_FB = OFF_FF + A_WIDTH
OFF_IA = OFF_FB + A_WIDTH
OFF_OG = OFF_IA + A_WIDTH
OFF_QB = OFF_OG + A_WIDTH
OFF_KB = OFF_QB + B_WIDTH
OFF_VB = OFF_KB + B_KV_WIDTH
D_IN = OFF_VB + B_KV_WIDTH

VMEM_LIMIT = 52 * 1024 * 1024
ADA_COLS = 1024
REORDER_COLS = 256
INPROJ_ROWS = 512
INPROJ_SUB_ROWS = 256
INPROJ_COL_CHUNK = 768
ATTN_Q_ROWS = 128
MERGE_ROWS = 1024
MERGE_SUB_ROWS = 1024
ROUTER_ROWS = (1024, 512, 256, 128)
PLAN_TOKENS = (4096, 2048, 1024, 512, 256, 128)
EXPERT_ROWS = 1024
COMBINE_ROWS = 512


def _cparams(sem):
    return pltpu.CompilerParams(dimension_semantics=sem, vmem_limit_bytes=VMEM_LIMIT)


LOG2E = 1.4426950408889634


def _sigmoid(x):
    return 0.5 * jnp.tanh(0.5 * x) + 0.5


def _nt_dot(a, b):
    return lax.dot_general(a, b, (((1,), (1,)), ((), ())), preferred_element_type=F32)


def _ada_kernel(a_ref, w_ref, b_ref, o_ref):
    a = a_ref[...]
    a = a * _sigmoid(a)
    o_ref[0] = jnp.dot(a.astype(MXU_DTYPE), w_ref[0].astype(MXU_DTYPE), preferred_element_type=F32) + b_ref[0]


def ada_mod(act, w_ada, b_ada):
    depth, d, n = w_ada.shape
    r = act.shape[0]
    tn = ADA_COLS
    return pl.pallas_call(
        _ada_kernel,
        grid=(depth, n // tn),
        in_specs=[
            pl.BlockSpec((r, d), lambda l, j: (0, 0)),
            pl.BlockSpec((1, d, tn), lambda l, j: (l, 0, j)),
            pl.BlockSpec((1, 1, tn), lambda l, j: (l, 0, j)),
        ],
        out_specs=pl.BlockSpec((1, r, tn), lambda l, j: (l, 0, j)),
        out_shape=jax.ShapeDtypeStruct((depth, r, n), F32),
        compiler_params=_cparams(("parallel", "parallel")),
        name="ada_mod",
    )(act, w_ada, b_ada.reshape(depth, 1, n))


def _modulated_norm(x, g, sc, sh):
    ms = jnp.mean(x * x, axis=-1, keepdims=True)
    return (x * lax.rsqrt(ms + EPS) * g) * (1.0 + sc) + sh


def _split3(x):
    hi = x.astype(jnp.bfloat16)
    r1 = x - hi.astype(F32)
    mid = r1.astype(jnp.bfloat16)
    lo = (r1 - mid.astype(F32)).astype(jnp.bfloat16)
    return hi, mid, lo


def _swap16(x):
    lane = lax.broadcasted_iota(jnp.int32, x.shape, 1)
    return jnp.where(lane % 32 < 16, pltpu.roll(x, LANES - 16, 1), pltpu.roll(x, 16, 1))


def _head_norm_rope(x, gsum, g, cos, sin):
    ssum = sum(jnp.dot(t, gsum, preferred_element_type=F32) for t in _split3(x * x))
    y = x * lax.rsqrt(ssum * (1.0 / B_HD) + EPS) * g
    if cos is not None:
        y = y * cos + _swap16(y) * sin
    return y


def _inproj_kernel(*refs, col_chunk, rope, combine):
    refs = list(refs)
    h_ref, g_ref, sc_ref, sh_ref, w_ref, qg_ref, kg_ref = refs[:7]
    del refs[:7]
    cos = sin = None
    if rope:
        cos, sin = refs[0][...], refs[1][...]
        del refs[:2]
    if combine:
        y1_ref, y2_ref, slab_ref, g2_ref = refs[:4]
        del refs[:4]
        ho_ref = refs.pop()
    z_ref, qo_ref, ko_ref, vo_ref, st_ref = refs
    r_i = lax.broadcasted_iota(jnp.int32, (LANES, LANES), 0)
    c_i = lax.broadcasted_iota(jnp.int32, (LANES, LANES), 1)
    gsum = (r_i // B_HD == c_i // B_HD).astype(jnp.bfloat16)
    tm = h_ref.shape[1]
    sub = min(INPROJ_SUB_ROWS, tm)
    lane = lax.broadcasted_iota(jnp.int32, (sub, LANES), 1)
    stats = [None, None]
    for r in range(tm // sub):
        rs = slice(r * sub, (r + 1) * sub)
        h = h_ref[0, rs, :]
        if combine:
            y12 = (_unpack_rows(jnp.concatenate([y1_ref[j, rs, :] for j in range(y1_ref.shape[0])], axis=1))
                   * slab_ref[rs, 2:3]
                   + _unpack_rows(jnp.concatenate([y2_ref[j, rs, :] for j in range(y2_ref.shape[0])], axis=1))
                   * slab_ref[rs, 3:4])
            h = h + g2_ref[0] * y12
            ho_ref[0, rs, :] = h
        y = _modulated_norm(h, g_ref[...], sc_ref[0], sh_ref[0]).astype(MXU_DTYPE)
        gqa = jnp.dot(y, w_ref[0, :, OFF_QB:], preferred_element_type=F32)
        cs_, sn_ = (cos[rs], sin[rs]) if rope else (None, None)
        for cb in range(B_WIDTH // LANES):
            x = gqa[:, cb * LANES:(cb + 1) * LANES]
            q = _head_norm_rope(x, gsum, qg_ref[...], cs_, sn_) * (B_HD ** -0.5 * LOG2E)
            swapped = pltpu.roll(q, B_HD, 1)
            for half in range(2):
                head = 2 * cb + half
                kv = head // B_GROUP
                src = q if half == kv else swapped
                keep = (lane // B_HD) == kv
                qo_ref[0, head, rs, :] = jnp.where(keep, src, 0.0).astype(qo_ref.dtype)
        kn = _head_norm_rope(gqa[:, B_WIDTH:B_WIDTH + B_KV_WIDTH], gsum, kg_ref[...], cs_, sn_)
        ko_ref[0, rs, :] = kn.astype(ko_ref.dtype)
        vo_ref[0, rs, :] = gqa[:, B_WIDTH + B_KV_WIDTH:].astype(vo_ref.dtype)
        for c in range(z_ref.shape[2] // col_chunk):
            cs = slice(c * col_chunk, (c + 1) * col_chunk)
            zc = jnp.dot(y, w_ref[0, :, cs], preferred_element_type=F32).astype(z_ref.dtype)
            z_ref[0, rs, cs] = zc
            for d, off in enumerate((OFF_FF, OFF_FB)):
                lo, hi = max(off, cs.start), min(off + A_WIDTH, cs.stop)
                if lo < hi:
                    zneg = jnp.minimum(zc[:, lo - cs.start:hi - cs.start].astype(F32), 0.0)
                    blocks = zneg.reshape(sub // HGRN_SUB, HGRN_SUB, hi - lo).sum(axis=1)
                    sums = jnp.min(blocks, axis=0, keepdims=True)
                    for lb in range(sums.shape[1] // LANES):
                        piece = sums[:, lb * LANES:(lb + 1) * LANES]
                        stats[d] = piece if stats[d] is None else jnp.minimum(stats[d], piece)
    st_ref[0, 0] = jnp.concatenate([stats[0], stats[1], jnp.zeros((6, LANES), F32)], axis=0)


def in_proj(h, norm_g, sc, sh, w_in, layer, q_norm_g, k_norm_g, cos_t=None, sin_t=None, moe=None):
    bsz, n, d = h.shape
    tm = min(INPROJ_ROWS, n)
    nt = n // tm
    rope = cos_t is not None
    qg = jnp.tile(q_norm_g.astype(F32), LANES // B_HD).reshape(1, LANES)
    kg = jnp.tile(k_norm_g.astype(F32), LANES // B_HD).reshape(1, LANES)
    vec = pl.BlockSpec((1, LANES), lambda b, i: (0, 0))
    hrow = pl.BlockSpec((1, tm, d), lambda b, i: (b, i, 0))
    per_b = pl.BlockSpec((1, 1, d), lambda b, i: (b, 0, 0))
    in_specs = [hrow, pl.BlockSpec((1, d), lambda b, i: (0, 0)), per_b, per_b,
                pl.BlockSpec((1, d, D_IN), lambda b, i: (layer, 0, 0)), vec, vec]
    args = [h, norm_g.reshape(1, d), sc, sh, w_in, qg, kg]
    if rope:
        in_specs += [pl.BlockSpec((tm, LANES), lambda b, i: (i, 0))] * 2
        args += [cos_t, sin_t]
    row = pl.BlockSpec((1, tm, LANES), lambda b, i: (b, i, 0))
    out_specs = [pl.BlockSpec((1, tm, OFF_QB), lambda b, i: (b, i, 0)),
                 pl.BlockSpec((1, B_HEADS, tm, LANES), lambda b, i: (b, 0, i, 0)), row, row,
                 pl.BlockSpec((1, 1, 8, LANES), lambda b, i: (b, i, 0, 0))]
    out_shape = [jax.ShapeDtypeStruct((bsz, n, OFF_QB), MXU_DTYPE),
                 jax.ShapeDtypeStruct((bsz, B_HEADS, n, LANES), MXU_DTYPE),
                 jax.ShapeDtypeStruct((bsz, n, LANES), MXU_DTYPE),
                 jax.ShapeDtypeStruct((bsz, n, LANES), MXU_DTYPE),
                 jax.ShapeDtypeStruct((bsz, nt, 8, LANES), F32)]
    if moe is not None:
        y1, y2, slab, g2, tok_off = moe
        off = tok_off // tm
        yspec = pl.BlockSpec((y1.shape[0], tm, LANES), lambda b, i: (0, off + b * nt + i, 0))
        in_specs += [yspec, yspec, pl.BlockSpec((tm, LANES), lambda b, i: (off + b * nt + i, 0)), per_b]
        args += [y1, y2, slab, g2]
        out_specs.append(hrow)
        out_shape.append(jax.ShapeDtypeStruct((bsz, n, d), F32))
    return pl.pallas_call(
        functools.partial(_inproj_kernel, col_chunk=INPROJ_COL_CHUNK, rope=rope, combine=moe is not None),
        grid=(bsz, nt),
        in_specs=in_specs,
        out_specs=out_specs,
        out_shape=out_shape,
        compiler_params=_cparams(("parallel", "parallel")),
        name="in_proj_rope" if rope else "in_proj",
    )(*args)


def rope_tables(n):
    t = jnp.arange(n)
    r = (t // GRID_W).astype(F32)
    col = (t % GRID_W).astype(F32)
    inv = ROPE_THETA ** (-jnp.arange(0, AXIS_DIM, 2, dtype=F32) / AXIS_DIM)
    ang_r = r[:, None] * inv
    ang_c = col[:, None] * inv
    cos = jnp.concatenate([jnp.cos(ang_r)] * 2 + [jnp.cos(ang_c)] * 2, axis=1)
    sin = jnp.concatenate([-jnp.sin(ang_r), jnp.sin(ang_r), -jnp.sin(ang_c), jnp.sin(ang_c)], axis=1)
    return jnp.tile(cos, (1, 2)), jnp.tile(sin, (1, 2))


def _hgrn_gates(zf, loglb, log1m, onem):
    l1pe = jnp.log(1.0 + jnp.exp(-jnp.abs(zf)))
    c = log1m + (jnp.minimum(zf, 0.0) - l1pe)
    logf = jnp.maximum(loglb, c) + jnp.log(1.0 + jnp.exp(-jnp.abs(loglb - c)))
    k = onem * jnp.exp(jnp.minimum(-zf, 0.0) - l1pe)
    return k, logf * LOG2E


def _cumsum_rows(cum_mat, x):
    out = jnp.dot(cum_mat, jnp.concatenate(_split3(x), axis=1), preferred_element_type=F32)
    w = x.shape[1]
    return out[:, :w] + out[:, w:2 * w] + out[:, 2 * w:]


def _hgrn_kernel(*refs, rev, n_chunks, fuse):
    if fuse:
        safe_ref, q_ref, f_ref, v_ref, lbp_ref, s0_ref, ob_ref, og_ref, ng_ref, o_ref, sT_ref, state = refs
    else:
        safe_ref, q_ref, f_ref, v_ref, lbp_ref, s0_ref, o_ref, sT_ref, state = refs
        ob_ref = og_ref = ng_ref = None
    i = pl.program_id(1)
    nt = pl.num_programs(1)

    @pl.when(i == 0)
    def _():
        state[...] = s0_ref[0]

    loglb = lbp_ref[0:1, :]
    log1m = lbp_ref[1:2, :]
    onem = lbp_ref[2:3, :]
    c = HGRN_CHUNK
    nsub = c // HGRN_SUB
    r_i = lax.broadcasted_iota(jnp.int32, (c, c), 0)
    c_i = lax.broadcasted_iota(jnp.int32, (c, c), 1)
    causal = (c_i >= r_i) if rev else (c_i <= r_i)
    cum_mat = causal.astype(jnp.bfloat16)
    row_blk = lax.broadcasted_iota(jnp.int32, (c, A_DK), 0) // HGRN_SUB

    def prepare(sl, exact):
        k, logf = _hgrn_gates(f_ref[0, sl, :].astype(F32), loglb, log1m, onem)
        zq = q_ref[0, sl, :].astype(F32)
        q = zq * _sigmoid(zq) * (A_DK ** -0.5)
        cum = _cumsum_rows(cum_mat, logf)
        tot = cum[0:1] if rev else cum[c - 1:c]
        qd = (q * jnp.exp2(cum)).astype(MXU_DTYPE)
        kd = (k * jnp.exp2(tot - cum)).astype(MXU_DTYPE)
        refs_j = []
        for j in range(nsub):
            if rev:
                r = cum[(j + 1) * HGRN_SUB:(j + 1) * HGRN_SUB + 1] if j < nsub - 1 else jnp.zeros_like(tot)
            else:
                r = cum[j * HGRN_SUB - 1:j * HGRN_SUB] if j > 0 else jnp.zeros_like(tot)
            refs_j.append(r)
        ref_rows = jnp.concatenate([jnp.broadcast_to(r, (HGRN_SUB, r.shape[1])) for r in refs_j], axis=0)
        qt = (q * jnp.exp2(cum - ref_rows)).astype(MXU_DTYPE)
        if exact:
            blk512 = lax.broadcasted_iota(jnp.int32, cum.shape, 0) // HGRN_SUB
            kt = [jnp.where((blk512 > j) if rev else (blk512 < j),
                            k * jnp.exp2(jnp.minimum(r - cum, 0.0)), 0.0).astype(MXU_DTYPE)
                  for j, r in enumerate(refs_j)]
            return qd, kd, jnp.exp2(tot), qt, kt, (q, k, cum)
        kt = [(k * jnp.exp2(jnp.minimum(r - cum, HGRN_EXP2_CLAMP))).astype(MXU_DTYPE) for r in refs_j]
        return qd, kd, jnp.exp2(tot), qt, kt, None

    def exact_diagonal(q, k, cum, hs):
        lane16 = lax.broadcasted_iota(jnp.int32, (HGRN_SUB, HGRN_SUB), 1)
        strips = []
        for j in range(nsub):
            rows = slice(j * HGRN_SUB, (j + 1) * HGRN_SUB)
            qb, kb, cb = q[rows, hs], k[rows, hs], cum[rows, hs]
            blk = jnp.zeros((HGRN_SUB, HGRN_SUB), F32)
            for t in range(HGRN_SUB):
                w = qb * kb[t:t + 1] * jnp.exp2(jnp.minimum(cb - cb[t:t + 1], 0.0))
                blk = jnp.where(lane16 == t, jnp.sum(w, axis=-1, keepdims=True), blk)
            pieces = [blk if jj == j else jnp.zeros((HGRN_SUB, HGRN_SUB), F32) for jj in range(nsub)]
            strips.append(jnp.concatenate(pieces, axis=1))
        return jnp.concatenate(strips, axis=0)

    def finish(sl, prepped, s):
        qd, kd, decay, qt, kt, raw = prepped
        v = v_ref[0, sl, :]
        outs = []
        for h in range(A_HEADS):
            hs = slice(h * A_DK, (h + 1) * A_DK)
            q_big = jnp.concatenate([jnp.where(row_blk == j, qt[:, hs], jnp.zeros_like(qt[:, hs]))
                                     for j in range(nsub)], axis=1)
            k_big = jnp.concatenate([kt[j][:, hs] for j in range(nsub)], axis=1)
            scores = _nt_dot(q_big, k_big)
            if raw is not None:
                scores = scores + exact_diagonal(*raw, hs)
            scores = jnp.where(causal, scores, 0.0)
            o_h = _nt_dot(qd[:, hs], s[h].astype(MXU_DTYPE))
            o_h = o_h + jnp.dot(scores.astype(MXU_DTYPE), v[:, hs], preferred_element_type=F32)
            upd = lax.dot_general(v[:, hs], kd[:, hs], (((0,), (0,)), ((), ())), preferred_element_type=F32)
            s[h] = s[h] * decay[:, hs] + upd
            if fuse:
                o_h = o_h + ob_ref[0, sl, hs]
                o_h = o_h * lax.rsqrt(jnp.mean(o_h * o_h, axis=-1, keepdims=True) + EPS)
            outs.append(o_h)
        o = jnp.concatenate(outs, axis=1)
        if fuse:
            og = og_ref[0, sl, :].astype(F32)
            o = o * ng_ref[...] * (og * _sigmoid(og))
        o_ref[0, sl, :] = o.astype(o_ref.dtype)

    safe = safe_ref[pl.program_id(0), (nt - 1 - i) if rev else i] != 0

    @pl.when(safe)
    def _():
        s = [state[h] for h in range(A_HEADS)]
        order = list(range(n_chunks - 1, -1, -1) if rev else range(n_chunks))
        slices = [slice(ci * c, (ci + 1) * c) for ci in order]
        ready = prepare(slices[0], False)
        for n, sl in enumerate(slices):
            prepped = ready
            if n + 1 < len(slices):
                ready = prepare(slices[n + 1], False)
            finish(sl, prepped, s)
        for h in range(A_HEADS):
            state[h] = s[h]

    @pl.when(jnp.logical_not(safe))
    def _():
        def chunk(n, carry):
            ci = (n_chunks - 1 - n) if rev else n
            sl = pl.ds(pl.multiple_of(ci * c, c), c)
            s = [state[h] for h in range(A_HEADS)]
            finish(sl, prepare(sl, True), s)
            for h in range(A_HEADS):
                state[h] = s[h]
            return carry

        lax.fori_loop(0, n_chunks, chunk, 0)

    @pl.when(i == pl.num_programs(1) - 1)
    def _():
        sT_ref[0] = state[...]


def hgrn_safe_flags(stats, direction, n, step_tokens):
    worst = stats[:, :, direction, :].min(axis=-1)
    bsz, nb = worst.shape
    per_step = nb * step_tokens // n
    worst = worst.reshape(bsz, nb // per_step, per_step).min(axis=-1)
    return (worst * LOG2E - HGRN_SUB >= -HGRN_EXP2_CLAMP).astype(jnp.int32)


def hgrn_scan(z, stats, lbp, s0, rev, o_other=None, norm_g=None):
    bsz, n, _ = z.shape
    fuse = o_other is not None
    t = min(HGRN_STEP_TOKENS, n)
    nt = n // t
    f_off = (OFF_FB if rev else OFF_FF) // A_WIDTH
    safe = hgrn_safe_flags(stats, 1 if rev else 0, n, t)

    def tmap(i):
        return nt - 1 - i if rev else i

    def col(blk):
        return pl.BlockSpec((1, t, A_WIDTH), lambda b, i, sf: (b, tmap(i), blk))

    state_spec = pl.BlockSpec((1, A_HEADS, LANES, LANES), lambda b, i, sf: (b, 0, 0, 0))
    in_specs = [col(OFF_QA // A_WIDTH), col(f_off), col(OFF_IA // A_WIDTH),
                pl.BlockSpec((8, A_WIDTH), lambda b, i, sf: (0, 0)), state_spec]
    args = [z, z, z, lbp, s0]
    if fuse:
        in_specs += [col(0), col(OFF_OG // A_WIDTH), pl.BlockSpec((1, A_WIDTH), lambda b, i, sf: (0, 0))]
        args += [o_other, z, norm_g.reshape(1, A_WIDTH)]
    kern = functools.partial(_hgrn_kernel, rev=rev, n_chunks=t // HGRN_CHUNK, fuse=fuse)
    grid_spec = pltpu.PrefetchScalarGridSpec(
        num_scalar_prefetch=1,
        grid=(bsz, nt),
        in_specs=in_specs,
        out_specs=[col(0), state_spec],
        scratch_shapes=[pltpu.VMEM((A_HEADS, LANES, LANES), F32)],
    )
    return pl.pallas_call(
        kern,
        grid_spec=grid_spec,
        out_shape=[jax.ShapeDtypeStruct((bsz, n, A_WIDTH), MXU_DTYPE if fuse else F32),
                   jax.ShapeDtypeStruct((bsz, A_HEADS, LANES, LANES), F32)],
        compiler_params=_cparams(("parallel", "arbitrary")),
        name="hgrn_bwd" if rev else "hgrn_fwd",
    )(safe, *args)


def _attn_kernel(q_ref, k_ref, v_ref, o_ref):
    tq = q_ref.shape[2]
    rows = B_GROUP * tq
    lane = lax.broadcasted_iota(jnp.int32, v_ref.shape[1:], 1)
    lane_o = lax.broadcasted_iota(jnp.int32, (tq, LANES), 1)
    s_all = _nt_dot(q_ref[0].reshape(B_HEADS * tq, LANES), k_ref[0])
    for kv in range(B_KV_HEADS):
        s = s_all[kv * rows:(kv + 1) * rows]
        vs = jnp.where((lane // B_HD) == kv, v_ref[0], jnp.ones_like(v_ref[0]))
        p = jnp.exp2(s - jnp.max(s, axis=-1, keepdims=True))
        a = jnp.dot(p.astype(MXU_DTYPE), vs, preferred_element_type=F32)
        o_full = a / pltpu.roll(a, B_HD, 1)
        o_swapped = pltpu.roll(o_full, B_HD, 1)
        for pair in range(B_GROUP // 2):
            g0, g1 = 2 * pair, 2 * pair + 1
            lo_src = o_full if kv == 0 else o_swapped
            hi_src = o_swapped if kv == 0 else o_full
            blk = jnp.where(lane_o < B_HD, lo_src[g0 * tq:(g0 + 1) * tq], hi_src[g1 * tq:(g1 + 1) * tq])
            cb = kv * (B_GROUP // 2) + pair
            o_ref[0, :, cb * LANES:(cb + 1) * LANES] = blk.astype(o_ref.dtype)


def attention(q, k, v):
    bsz, _, n, _ = q.shape
    nkeys = k.shape[1]
    tq = min(ATTN_Q_ROWS, n)
    return pl.pallas_call(
        _attn_kernel,
        grid=(bsz, n // tq),
        in_specs=[pl.BlockSpec((1, B_HEADS, tq, LANES), lambda b, i: (b, 0, i, 0)),
                  pl.BlockSpec((1, nkeys, LANES), lambda b, i: (b, 0, 0)),
                  pl.BlockSpec((1, nkeys, LANES), lambda b, i: (b, 0, 0))],
        out_specs=pl.BlockSpec((1, tq, B_WIDTH), lambda b, i: (b, i, 0)),
        out_shape=jax.ShapeDtypeStruct((bsz, n, B_WIDTH), MXU_DTYPE),
        compiler_params=_cparams(("parallel", "parallel")),
        name="gqa_attention",
    )(q, k, v)


def _pack_rows(x):
    w = x.shape[1] // 2
    bits = lax.bitcast_convert_type(x.astype(jnp.bfloat16).astype(F32), jnp.uint32)
    packed = (bits[:, :w] & jnp.uint32(0xFFFF0000)) | (bits[:, w:] >> 16)
    return lax.bitcast_convert_type(packed, jnp.int32)


def _unpack_rows(p):
    bits = lax.bitcast_convert_type(p, jnp.uint32)
    hi = lax.bitcast_convert_type(bits & jnp.uint32(0xFFFF0000), F32)
    lo = lax.bitcast_convert_type(bits << 16, F32)
    return jnp.concatenate([hi, lo], axis=1)


def _store_chunks(ref, packed):
    for j in range(ref.shape[0]):
        ref[j] = packed[:, j * LANES:(j + 1) * LANES]


def _load_chunks(ref):
    return jnp.concatenate([ref[j] for j in range(ref.shape[0])], axis=1)


def _merge_kernel(*refs, aliased):
    if aliased:
        refs = refs[:13] + refs[15:]
    (h_ref, a_ref, b_ref, zg_ref, wa_ref, wb_ref, wo_ref, g1_ref, n2_ref, sc_ref, sh_ref, wr_ref, br_ref,
     ho_ref, x2_ref, lg_ref) = refs
    d = h_ref.shape[2]
    tm = h_ref.shape[1]
    sub = min(MERGE_SUB_ROWS, tm)
    for r in range(tm // sub):
        rs = slice(r * sub, (r + 1) * sub)
        ua = jnp.dot(a_ref[0, rs, :], wa_ref[...], preferred_element_type=F32)
        ub = jnp.dot(b_ref[0, rs, :], wb_ref[...], preferred_element_type=F32)
        ga = _sigmoid(zg_ref[0, rs, 0:d].astype(F32))
        gb = _sigmoid(zg_ref[0, rs, d:2 * d].astype(F32))
        mix = (ga * ua + gb * ub).astype(MXU_DTYPE)
        h_new = h_ref[0, rs, :] + g1_ref[0] * jnp.dot(mix, wo_ref[...], preferred_element_type=F32)
        ho_ref[0, rs, :] = h_new
        x2 = _modulated_norm(h_new, n2_ref[...], sc_ref[0], sh_ref[0])
        packed = _pack_rows(x2)
        for j in range(x2_ref.shape[0]):
            x2_ref[j, rs, :] = packed[:, j * LANES:(j + 1) * LANES]
        x_hi = x2.astype(jnp.bfloat16)
        x_lo = (x2 - x_hi.astype(F32)).astype(jnp.bfloat16)
        hi_both = jnp.dot(x_hi, wr_ref[...], preferred_element_type=F32)
        lo_hi = jnp.dot(x_lo, wr_ref[:, 0:LANES], preferred_element_type=F32)
        lg_ref[rs, :] = hi_both[:, 0:LANES] + hi_both[:, LANES:] + lo_hi + br_ref[...]


def merge_proj(h, a, b, z, w_up_a, w_up_b, w_out, g1, norm2_g, sc2, sh2, w_router, b_router, t_total, tok_off,
               bufs=None):
    bsz, n, d = h.shape
    tm = min(MERGE_ROWS, n)
    nt = n // tm
    off = tok_off // tm
    n_chunks = d // (2 * LANES)
    row = lambda w: pl.BlockSpec((1, tm, w), lambda b_, i: (b_, i, 0))
    full = lambda s: pl.BlockSpec(s, lambda b_, i: (0,) * len(s))
    per_b = pl.BlockSpec((1, 1, d), lambda b_, i: (b_, 0, 0))
    in_specs = [row(d), row(A_WIDTH), row(B_WIDTH),
                pl.BlockSpec((1, tm, 2 * d), lambda b_, i: (b_, i, OFF_GATES)),
                full((A_WIDTH, d)), full((B_WIDTH, d)), full((d, d)),
                per_b, full((1, d)), per_b, per_b, full((d, 2 * LANES)), full((1, LANES))]
    args = [h, a, b, z, w_up_a, w_up_b, w_out, g1, norm2_g.reshape(1, d), sc2, sh2, w_router, b_router]
    aliases = {}
    if bufs is not None:
        in_specs += [pl.BlockSpec(memory_space=pl.ANY)] * 2
        args += list(bufs)
        aliases = {13: 1, 14: 2}
    return pl.pallas_call(
        functools.partial(_merge_kernel, aliased=bufs is not None),
        grid=(bsz, nt),
        in_specs=in_specs,
        out_specs=[row(d),
                   pl.BlockSpec((n_chunks, tm, LANES), lambda b_, i: (0, off + b_ * nt + i, 0)),
                   pl.BlockSpec((tm, LANES), lambda b_, i: (off + b_ * nt + i, 0))],
        out_shape=[jax.ShapeDtypeStruct((bsz, n, d), F32),
                   jax.ShapeDtypeStruct((n_chunks, t_total, LANES), jnp.int32),
                   jax.ShapeDtypeStruct((t_total, LANES), F32)],
        input_output_aliases=aliases,
        compiler_params=_cparams(("parallel", "parallel")),
        name="merge_proj",
    )(*args)


def _first_lane_where(cond, lane):
    return jnp.min(jnp.where(cond, lane, LANES), axis=-1, keepdims=True)


def _router_kernel(lg_ref, slab_ref, slab_t_ref, cnt_ref, base):
    @pl.when(pl.program_id(0) == 0)
    def _():
        base[...] = jnp.zeros_like(base)

    lg = lg_ref[...]
    tr = lg.shape[0]
    lane = lax.broadcasted_iota(jnp.int32, lg.shape, 1)
    neg = jnp.float32(-jnp.inf)
    is_g = lane < N_GROUPS
    gl = jnp.where(is_g, lg, neg)
    gmax = jnp.max(gl, axis=-1, keepdims=True)
    gsel = _first_lane_where(gl == gmax, lane)
    p_group = 1.0 / jnp.sum(jnp.where(is_g, jnp.exp(lg - gmax), 0.0), axis=-1, keepdims=True)
    lo = N_GROUPS + gsel * EXPERTS_PER_GROUP
    in_grp = (lane >= lo) & (lane < lo + EXPERTS_PER_GROUP)
    el = jnp.where(in_grp, lg, neg)
    m1 = jnp.max(el, axis=-1, keepdims=True)
    i1 = _first_lane_where(el == m1, lane)
    el2 = jnp.where(lane == i1, neg, el)
    m2 = jnp.max(el2, axis=-1, keepdims=True)
    i2 = _first_lane_where(el2 == m2, lane)
    zsum = jnp.sum(jnp.where(in_grp, jnp.exp(lg - m1), 0.0), axis=-1, keepdims=True)
    p1 = 1.0 / zsum
    p2 = jnp.exp(m2 - m1) / zsum
    w1 = p1 / (p1 + p2) * p_group
    w2 = p2 / (p1 + p2) * p_group
    hit1 = lane == i1
    hit2 = lane == i2
    onehot = (hit1 | hit2).astype(F32)
    r_i = lax.broadcasted_iota(jnp.int32, (tr, tr), 0)
    c_i = lax.broadcasted_iota(jnp.int32, (tr, tr), 1)
    before = (c_i < r_i).astype(MXU_DTYPE)
    prior = jnp.dot(before, onehot.astype(MXU_DTYPE), preferred_element_type=F32) + base[0:1, :]
    r1 = jnp.sum(jnp.where(hit1, prior, 0.0), axis=-1, keepdims=True)
    r2 = jnp.sum(jnp.where(hit2, prior, 0.0), axis=-1, keepdims=True)
    new_base = base[0:1, :] + jnp.sum(onehot, axis=0, keepdims=True)
    base[...] = jnp.broadcast_to(new_base, base.shape)
    cnt_ref[...] = jnp.broadcast_to(new_base, cnt_ref.shape)
    vals = [(i1 - N_GROUPS).astype(F32), (i2 - N_GROUPS).astype(F32), w1, w2, r1, r2]
    slab = jnp.zeros_like(lg)
    for idx, val in enumerate(vals):
        slab = jnp.where(lane == idx, val, slab)
    slab_ref[...] = slab
    slab_t_ref[...] = slab.T[0:8, :]


def route(logits):
    t = logits.shape[0]
    tr = next(c for c in ROUTER_ROWS if t % c == 0)
    return pl.pallas_call(
        _router_kernel,
        grid=(t // tr,),
        in_specs=[pl.BlockSpec((tr, LANES), lambda i: (i, 0))],
        out_specs=[pl.BlockSpec((tr, LANES), lambda i: (i, 0)), pl.BlockSpec((8, tr), lambda i: (0, i)),
                   pl.BlockSpec((8, LANES), lambda i: (0, 0))],
        out_shape=[jax.ShapeDtypeStruct((t, LANES), F32), jax.ShapeDtypeStruct((8, t), F32),
                   jax.ShapeDtypeStruct((8, LANES), F32)],
        scratch_shapes=[pltpu.VMEM((8, LANES), F32)],
        compiler_params=_cparams(("arbitrary",)),
        name="moe_route",
    )(logits)


def _expert_kernel(be_ref, nv_ref, x_ref, wg_ref, wu_ref, wd_ref, o_ref, wg_s, wu_s, wd_s):
    j = pl.program_id(0)
    nv = nv_ref[j]

    @pl.when((j == 0) | (be_ref[j] != be_ref[jnp.maximum(j - 1, 0)]))
    def _():
        wg_s[...] = wg_ref[0, 0].astype(wg_s.dtype)
        wu_s[...] = wu_ref[0, 0].astype(wu_s.dtype)
        wd_s[...] = wd_ref[0, 0].astype(wd_s.dtype)

    @pl.when(nv > 0)
    def _():
        x = _unpack_rows(_load_chunks(x_ref))
        rows = lax.broadcasted_iota(jnp.int32, x.shape, 0)
        x = jnp.where(rows < nv, x, 0.0).astype(MXU_DTYPE)
        hg = jnp.dot(x, wg_s[...], preferred_element_type=F32)
        hu = jnp.dot(x, wu_s[...], preferred_element_type=F32)
        hb = (hg * _sigmoid(hg) * hu).astype(MXU_DTYPE)
        _store_chunks(o_ref, _pack_rows(jnp.dot(hb, wd_s[...], preferred_element_type=F32)))

    @pl.when(nv == 0)
    def _():
        o_ref[...] = jnp.zeros_like(o_ref)


def expert_ffn(x_sorted, block_e, nvalid, w_gate, w_up, w_down, layer, tmx):
    n_chunks, r, _ = x_sorted.shape
    _, _, d, de = w_gate.shape
    grid_spec = pltpu.PrefetchScalarGridSpec(
        num_scalar_prefetch=2,
        grid=(r // tmx,),
        in_specs=[pl.BlockSpec((n_chunks, tmx, LANES), lambda j, be, nv: (0, j, 0)),
                  pl.BlockSpec((1, 1, d, de), lambda j, be, nv: (layer, be[j], 0, 0)),
                  pl.BlockSpec((1, 1, d, de), lambda j, be, nv: (layer, be[j], 0, 0)),
                  pl.BlockSpec((1, 1, de, d), lambda j, be, nv: (layer, be[j], 0, 0))],
        out_specs=pl.BlockSpec((n_chunks, tmx, LANES), lambda j, be, nv: (0, j, 0)),
        scratch_shapes=[pltpu.VMEM((d, de), MXU_DTYPE), pltpu.VMEM((d, de), MXU_DTYPE),
                        pltpu.VMEM((de, d), MXU_DTYPE)],
    )
    return pl.pallas_call(
        _expert_kernel,
        grid_spec=grid_spec,
        out_shape=jax.ShapeDtypeStruct((n_chunks, r, LANES), jnp.int32),
        compiler_params=_cparams(("arbitrary",)),
        name="expert_ffn",
    )(block_e, nvalid, x_sorted, w_gate, w_up, w_down)


def _combine_final_kernel(h_ref, y1_ref, y2_ref, slab_ref, g2_ref, fn_ref, o_ref):
    y = (_unpack_rows(_load_chunks(y1_ref)) * slab_ref[:, 2:3]
         + _unpack_rows(_load_chunks(y2_ref)) * slab_ref[:, 3:4])
    h_new = h_ref[0] + g2_ref[0] * y
    ms = jnp.mean(h_new * h_new, axis=-1, keepdims=True)
    o_ref[0] = h_new * lax.rsqrt(ms + EPS) * fn_ref[...]


def moe_combine_final(h, y1, y2, slab, g2, final_g):
    bsz, n, d = h.shape
    tm = min(COMBINE_ROWS, n)
    nt = n // tm
    yspec = pl.BlockSpec((y1.shape[0], tm, LANES), lambda b, i: (0, b * nt + i, 0))
    return pl.pallas_call(
        _combine_final_kernel,
        grid=(bsz, nt),
        in_specs=[pl.BlockSpec((1, tm, d), lambda b, i: (b, i, 0)), yspec, yspec,
                  pl.BlockSpec((tm, LANES), lambda b, i: (b * nt + i, 0)),
                  pl.BlockSpec((1, 1, d), lambda b, i: (b, 0, 0)),
                  pl.BlockSpec((1, d), lambda b, i: (0, 0))],
        out_specs=pl.BlockSpec((1, tm, d), lambda b, i: (b, i, 0)),
        out_shape=jax.ShapeDtypeStruct((bsz, n, d), F32),
        compiler_params=_cparams(("parallel", "parallel")),
        name="moe_combine_final",
    )(h, y1, y2, slab, g2, final_g.reshape(1, d))


SC_WINDOW = 128


def _sc_mesh():
    return plsc.VectorSubcoreMesh(core_axis_name="core", subcore_axis_name="subcore")


def sc_gather_rows(table, idx):
    n = idx.shape[0]

    @pl.kernel(out_type=jax.ShapeDtypeStruct((n, LANES), table.dtype), mesh=_sc_mesh())
    def gather_kernel(t_hbm, i_hbm, o_hbm):
        def body(i_vmem, o_vmem):
            pltpu.sync_copy(t_hbm.at[i_vmem.at[0]], o_vmem)

        pltpu.emit_pipeline(
            body,
            grid=(n // SC_WINDOW,),
            in_specs=[pl.BlockSpec((1, SC_WINDOW), lambda i: (0, i))],
            out_specs=[pl.BlockSpec((SC_WINDOW, LANES), lambda i: (i, 0))],
            core_axis_name=("core", "subcore"),
            dimension_semantics=(pltpu.PARALLEL,),
            trace_scopes=False,
        )(i_hbm, o_hbm)

    return gather_kernel(table, idx.reshape(1, n))


def sc_scatter_rows2(rows, idx_a, idx_b, n_out):
    n = rows.shape[0]

    @pl.kernel(out_type=jax.ShapeDtypeStruct((n_out, LANES), rows.dtype), mesh=_sc_mesh())
    def scatter_kernel(r_hbm, ia_hbm, ib_hbm, o_hbm):
        def body(r_vmem, ia_vmem, ib_vmem):
            pltpu.sync_copy(r_vmem, o_hbm.at[ia_vmem.at[0]])
            pltpu.sync_copy(r_vmem, o_hbm.at[ib_vmem.at[0]])

        pltpu.emit_pipeline(
            body,
            grid=(n // SC_WINDOW,),
            in_specs=[pl.BlockSpec((SC_WINDOW, LANES), lambda i: (i, 0)),
                      pl.BlockSpec((1, SC_WINDOW), lambda i: (0, i)),
                      pl.BlockSpec((1, SC_WINDOW), lambda i: (0, i))],
            out_specs=[],
            core_axis_name=("core", "subcore"),
            dimension_semantics=(pltpu.PARALLEL,),
            trace_scopes=False,
        )(r_hbm, ia_hbm, ib_hbm)

    return scatter_kernel(rows, idx_a.reshape(1, n), idx_b.reshape(1, n))


def _plan_kernel(ps_ref, st_ref, ia_ref, ib_ref, *, n_rows):
    e = st_ref[0:2, :].astype(jnp.int32)
    dest = st_ref[4:6, :].astype(jnp.int32)
    for ex in range(N_EXPERTS):
        dest = dest + jnp.where(e == ex, ps_ref[ex], 0)
    for k, ref in enumerate((ia_ref, ib_ref)):
        for j in range(ref.shape[0]):
            ref[j:j + 1, :] = dest[k:k + 1, :] + j * n_rows


def dispatch_indices(slab_t, pad_start, n_chunks, n_rows):
    t = slab_t.shape[1]
    tp = next(c for c in PLAN_TOKENS if t % c == 0)
    grid_spec = pltpu.PrefetchScalarGridSpec(
        num_scalar_prefetch=1,
        grid=(t // tp,),
        in_specs=[pl.BlockSpec((8, tp), lambda i, ps: (0, i))],
        out_specs=[pl.BlockSpec((n_chunks, tp), lambda i, ps: (0, i))] * 2,
    )
    return pl.pallas_call(
        functools.partial(_plan_kernel, n_rows=n_rows),
        grid_spec=grid_spec,
        out_shape=[jax.ShapeDtypeStruct((n_chunks, t), jnp.int32)] * 2,
        compiler_params=_cparams(("parallel",)),
        name="moe_plan",
    )(pad_start, slab_t)


def hier_moe(x2, logits, w_gate, w_up, w_down, layer):
    n_chunks, t, _ = x2.shape
    tmx = min(EXPERT_ROWS, t)
    n_blocks = (t * TOP_K + N_EXPERTS * (tmx - 1)) // tmx + 1
    r = n_blocks * tmx
    slab, slab_t, counts = route(logits)
    cnt = counts[0, N_GROUPS:N_GROUPS + N_EXPERTS].astype(jnp.int32)
    padded = (cnt + tmx - 1) // tmx * tmx
    pad_end = jnp.cumsum(padded)
    pad_start = pad_end - padded
    blk_start = jnp.arange(n_blocks, dtype=jnp.int32) * tmx
    block_e = jnp.minimum(jnp.sum(pad_end[None, :] <= blk_start[:, None], axis=1), N_EXPERTS - 1).astype(jnp.int32)
    nvalid = jnp.clip(pad_start[block_e] + cnt[block_e] - blk_start, 0, tmx).astype(jnp.int32)
    idx1, idx2 = dispatch_indices(slab_t, pad_start.astype(jnp.int32), n_chunks, r)
    idx1, idx2 = idx1.reshape(-1), idx2.reshape(-1)
    x_sorted = sc_scatter_rows2(x2.reshape(n_chunks * t, LANES), idx1, idx2, n_chunks * r)
    y_sorted = expert_ffn(x_sorted.reshape(n_chunks, r, LANES), block_e, nvalid, w_gate, w_up, w_down, layer, tmx)
    y_flat = y_sorted.reshape(n_chunks * r, LANES)
    y1 = sc_gather_rows(y_flat, idx1).reshape(n_chunks, t, LANES)
    y2 = sc_gather_rows(y_flat, idx2).reshape(n_chunks, t, LANES)
    return y1, y2, slab


def _lower_bound_rows(lb_logits):
    lb = jnp.cumsum(jax.nn.softmax(lb_logits.astype(F32), axis=1), axis=1)
    lb = lb - lb[:, :1]
    rows = jnp.stack([jnp.log(lb), jnp.log1p(-lb), 1.0 - lb], axis=2)
    rows = jnp.concatenate([rows, jnp.zeros(rows.shape[:2] + (5, A_WIDTH), F32)], axis=2)
    return rows


def _cast_kernel(w_ref, o_ref):
    o_ref[...] = w_ref[...].astype(o_ref.dtype)


def reorder_cast_w_in(w_in):
    depth, d, n = w_in.shape
    gate_cols = 2 * d
    blk = REORDER_COLS
    nb = n // blk
    shift = (n - gate_cols) // blk
    return pl.pallas_call(
        _cast_kernel,
        grid=(depth, nb),
        in_specs=[pl.BlockSpec((1, d, blk), lambda l, j: (l, 0, (j + shift) % nb))],
        out_specs=pl.BlockSpec((1, d, blk), lambda l, j: (l, 0, j)),
        out_shape=jax.ShapeDtypeStruct((depth, d, n), MXU_DTYPE),
        compiler_params=_cparams(("parallel", "parallel")),
        name="reorder_cast_w_in",
    )(w_in)


def kernel(x, c, ctx, c_ctx, w_ada, b_ada, norm1_g, norm2_g, w_in, lb_logits, hgrn_norm_g, q_norm_g, k_norm_g,
           w_up_a, w_up_b, w_out, w_router_group, b_router_group, w_router_expert, b_router_expert, w_gate, w_up,
           w_down, final_norm_g):
    bsz, n, d = x.shape
    n_ctx = ctx.shape[1]
    depth = w_in.shape[0]
    mx = MXU_DTYPE

    w_in_r = reorder_cast_w_in(w_in)
    w_up_a_c, w_up_b_c, w_out_c = w_up_a.astype(mx), w_up_b.astype(mx), w_out.astype(mx)
    n_r = N_GROUPS + N_EXPERTS
    w_router = jnp.concatenate([w_router_group, w_router_expert, jnp.zeros((depth, d, LANES - n_r), F32)], axis=-1)
    w_router_hi = w_router.astype(jnp.bfloat16)
    w_router_lo = (w_router - w_router_hi.astype(F32)).astype(jnp.bfloat16)
    w_router = jnp.concatenate([w_router_hi, w_router_lo], axis=-1)
    b_router = jnp.concatenate([b_router_group, b_router_expert, jnp.zeros((depth, LANES - n_r), F32)], axis=-1)
    lbp = _lower_bound_rows(lb_logits)
    cos_t, sin_t = rope_tables(n)

    n_rows = -(-(bsz + 1) // 8) * 8
    act = jnp.concatenate([c, c_ctx[None, :], jnp.zeros((n_rows - bsz - 1, d), F32)], axis=0)
    mod = ada_mod(act, w_ada, b_ada)

    h, hc = x, ctx
    pending = None
    zero_state = jnp.zeros((bsz, A_HEADS, LANES, LANES), F32)
    for l in range(depth):
        need_ctx = l < depth - 1
        ml = mod[l, :bsz].reshape(bsz, 1, N_MOD, d)
        mc = jnp.broadcast_to(mod[l, bsz].reshape(1, 1, N_MOD, d), (bsz, 1, N_MOD, d))
        sh1, sc1, g1, sh2, sc2, g2 = (ml[:, :, i] for i in range(N_MOD))
        csh1, csc1, cg1, csh2, csc2, cg2 = (mc[:, :, i] for i in range(N_MOD))

        moe_l = moe_c = None
        if pending is not None:
            y1, y2, slab, g2_prev, cg2_prev = pending
            moe_l = (y1, y2, slab, g2_prev, 0)
            moe_c = (y1, y2, slab, cg2_prev, bsz * n)
        zl, q_l, k_l, v_l, st_l, *h_new = in_proj(h, norm1_g[l], sc1, sh1, w_in_r, l, q_norm_g[l], k_norm_g[l],
                                                  cos_t, sin_t, moe_l)
        zc, q_c, k_c, v_c, st_c, *hc_new = in_proj(hc, norm1_g[l], csc1, csh1, w_in_r, l, q_norm_g[l], k_norm_g[l],
                                                   None, None, moe_c)
        if pending is not None:
            h, hc = h_new[0], hc_new[0]
        ob_c, s_bwd = hgrn_scan(zc, st_c, lbp[1, l], zero_state, True)
        if need_ctx:
            a_c, s_fwd = hgrn_scan(zc, st_c, lbp[0, l], zero_state, False, ob_c, hgrn_norm_g[l])
        else:
            _, s_fwd = hgrn_scan(zc, st_c, lbp[0, l], zero_state, False)
        ob_l, _ = hgrn_scan(zl, st_l, lbp[1, l], s_bwd, True)
        a_l, _ = hgrn_scan(zl, st_l, lbp[0, l], s_fwd, False, ob_l, hgrn_norm_g[l])

        b_l = attention(q_l, jnp.concatenate([k_c, k_l], axis=1), jnp.concatenate([v_c, v_l], axis=1))

        t_total = bsz * (n + n_ctx) if need_ctx else bsz * n
        bufs = (jnp.zeros((d // (2 * LANES), t_total, LANES), jnp.int32),
                jnp.zeros((t_total, LANES), F32)) if need_ctx else None
        h, x2, lg = merge_proj(h, a_l, b_l, zl, w_up_a_c[l], w_up_b_c[l], w_out_c[l], g1, norm2_g[l], sc2, sh2,
                               w_router[l], b_router[l][None, :], t_total, 0, bufs)
        if need_ctx:
            b_c = attention(q_c, k_c, v_c)
            hc, x2, lg = merge_proj(hc, a_c, b_c, zc, w_up_a_c[l], w_up_b_c[l], w_out_c[l], cg1, norm2_g[l],
                                    csc2, csh2, w_router[l], b_router[l][None, :], t_total, bsz * n, (x2, lg))

        y1, y2, slab = hier_moe(x2, lg, w_gate, w_up, w_down, l)
        pending = (y1, y2, slab, g2, cg2)
    y1, y2, slab, g2, _ = pending
    return moe_combine_final(h, y1, y2, slab, g2, final_norm_g)
```

```python
import functools

import jax
import jax.numpy as jnp
from jax import lax
from jax.experimental import pallas as pl
from jax.experimental.pallas import tpu as pltpu
from jax.experimental.pallas import tpu_sc as plsc

F32 = jnp.float32
MXU_DTYPE = jnp.bfloat16

EPS = 1e-6
GRID_W = 64
ROPE_THETA = 10000.0
LANES = 128

A_HEADS = 4
A_DK = 128
A_WIDTH = A_HEADS * A_DK
HGRN_CHUNK = 64
HGRN_SUB = 16
HGRN_STEP_TOKENS = 1024
HGRN_EXP2_CLAMP = 86.0
B_HEADS = 8
B_KV_HEADS = 2
B_HD = 64
B_GROUP = B_HEADS // B_KV_HEADS
B_WIDTH = B_HEADS * B_HD
B_KV_WIDTH = B_KV_HEADS * B_HD
AXIS_DIM = B_HD // 2
N_GROUPS = 4
EXPERTS_PER_GROUP = 8
N_EXPERTS = N_GROUPS * EXPERTS_PER_GROUP
TOP_K = 2
N_MOD = 6

OFF_GATES = 0
OFF_QA = 2048
OFF_FF = OFF_QA + A_WIDTH
OFF_FB = OFF_FF + A_WIDTH
OFF_IA = OFF_FB + A_WIDTH
OFF_OG = OFF_IA + A_WIDTH
OFF_QB = OFF_OG + A_WIDTH
OFF_KB = OFF_QB + B_WIDTH
OFF_VB = OFF_KB + B_KV_WIDTH
D_IN = OFF_VB + B_KV_WIDTH

VMEM_LIMIT = 52 * 1024 * 1024
ADA_COLS = 1024
REORDER_COLS = 256
INPROJ_ROWS = 512
INPROJ_SUB_ROWS = 256
INPROJ_COL_CHUNK = 768
ATTN_Q_ROWS = 128
MERGE_ROWS = 1024
MERGE_SUB_ROWS = 1024
ROUTER_ROWS = (1024, 512, 256, 128)
PLAN_TOKENS = (4096, 2048, 1024, 512, 256, 128)
EXPERT_ROWS = 1024
COMBINE_ROWS = 512


def _cparams(sem):
    return pltpu.CompilerParams(dimension_semantics=sem, vmem_limit_bytes=VMEM_LIMIT)


LOG2E = 1.4426950408889634


def _sigmoid(x):
    return 0.5 * jnp.tanh(0.5 * x) + 0.5


def _nt_dot(a, b):
    return lax.dot_general(a, b, (((1,), (1,)), ((), ())), preferred_element_type=F32)


def _ada_kernel(a_ref, w_ref, b_ref, o_ref):
    a = a_ref[...]
    a = a * _sigmoid(a)
    o_ref[0] = jnp.dot(a.astype(MXU_DTYPE), w_ref[0].astype(MXU_DTYPE), preferred_element_type=F32) + b_ref[0]


def ada_mod(act, w_ada, b_ada):
    depth, d, n = w_ada.shape
    r = act.shape[0]
    tn = ADA_COLS
    return pl.pallas_call(
        _ada_kernel,
        grid=(depth, n // tn),
        in_specs=[
            pl.BlockSpec((r, d), lambda l, j: (0, 0)),
            pl.BlockSpec((1, d, tn), lambda l, j: (l, 0, j)),
            pl.BlockSpec((1, 1, tn), lambda l, j: (l, 0, j)),
        ],
        out_specs=pl.BlockSpec((1, r, tn), lambda l, j: (l, 0, j)),
        out_shape=jax.ShapeDtypeStruct((depth, r, n), F32),
        compiler_params=_cparams(("parallel", "parallel")),
        name="ada_mod",
    )(act, w_ada, b_ada.reshape(depth, 1, n))


def _modulated_norm(x, g, sc, sh):
    ms = jnp.mean(x * x, axis=-1, keepdims=True)
    return (x * lax.rsqrt(ms + EPS) * g) * (1.0 + sc) + sh


def _split3(x):
    hi = x.astype(jnp.bfloat16)
    r1 = x - hi.astype(F32)
    mid = r1.astype(jnp.bfloat16)
    lo = (r1 - mid.astype(F32)).astype(jnp.bfloat16)
    return hi, mid, lo


def _swap16(x):
    lane = lax.broadcasted_iota(jnp.int32, x.shape, 1)
    return jnp.where(lane % 32 < 16, pltpu.roll(x, LANES - 16, 1), pltpu.roll(x, 16, 1))


def _head_norm_rope(x, gsum, g, cos, sin):
    ssum = sum(jnp.dot(t, gsum, preferred_element_type=F32) for t in _split3(x * x))
    y = x * lax.rsqrt(ssum * (1.0 / B_HD) + EPS) * g
    if cos is not None:
        y = y * cos + _swap16(y) * sin
    return y


def _inproj_kernel(*refs, col_chunk, rope, combine):
    refs = list(refs)
    h_ref, g_ref, sc_ref, sh_ref, w_ref, qg_ref, kg_ref = refs[:7]
    del refs[:7]
    cos = sin = None
    if rope:
        cos, sin = refs[0][...], refs[1][...]
        del refs[:2]
    if combine:
        y1_ref, y2_ref, slab_ref, g2_ref = refs[:4]
        del refs[:4]
        ho_ref = refs.pop()
    z_ref, qo_ref, ko_ref, vo_ref, st_ref = refs
    r_i = lax.broadcasted_iota(jnp.int32, (LANES, LANES), 0)
    c_i = lax.broadcasted_iota(jnp.int32, (LANES, LANES), 1)
    gsum = (r_i // B_HD == c_i // B_HD).astype(jnp.bfloat16)
    tm = h_ref.shape[1]
    sub = min(INPROJ_SUB_ROWS, tm)
    lane = lax.broadcasted_iota(jnp.int32, (sub, LANES), 1)
    stats = [None, None]
    for r in range(tm // sub):
        rs = slice(r * sub, (r + 1) * sub)
        h = h_ref[0, rs, :]
        if combine:
            y12 = (_unpack_rows(jnp.concatenate([y1_ref[j, rs, :] for j in range(y1_ref.shape[0])], axis=1))
                   * slab_ref[rs, 2:3]
                   + _unpack_rows(jnp.concatenate([y2_ref[j, rs, :] for j in range(y2_ref.shape[0])], axis=1))
                   * slab_ref[rs, 3:4])
            h = h + g2_ref[0] * y12
            ho_ref[0, rs, :] = h
        y = _modulated_norm(h, g_ref[...], sc_ref[0], sh_ref[0]).astype(MXU_DTYPE)
        gqa = jnp.dot(y, w_ref[0, :, OFF_QB:], preferred_element_type=F32)
        cs_, sn_ = (cos[rs], sin[rs]) if rope else (None, None)
        for cb in range(B_WIDTH // LANES):
            x = gqa[:, cb * LANES:(cb + 1) * LANES]
            q = _head_norm_rope(x, gsum, qg_ref[...], cs_, sn_) * (B_HD ** -0.5 * LOG2E)
            swapped = pltpu.roll(q, B_HD, 1)
            for half in range(2):
                head = 2 * cb + half
                kv = head // B_GROUP
                src = q if half == kv else swapped
                keep = (lane // B_HD) == kv
                qo_ref[0, head, rs, :] = jnp.where(keep, src, 0.0).astype(qo_ref.dtype)
        kn = _head_norm_rope(gqa[:, B_WIDTH:B_WIDTH + B_KV_WIDTH], gsum, kg_ref[...], cs_, sn_)
        ko_ref[0, rs, :] = kn.astype(ko_ref.dtype)
        vo_ref[0, rs, :] = gqa[:, B_WIDTH + B_KV_WIDTH:].astype(vo_ref.dtype)
        for c in range(z_ref.shape[2] // col_chunk):
            cs = slice(c * col_chunk, (c + 1) * col_chunk)
            zc = jnp.dot(y, w_ref[0, :, cs], preferred_element_type=F32).astype(z_ref.dtype)
            z_ref[0, rs, cs] = zc
            for d, off in enumerate((OFF_FF, OFF_FB)):
                lo, hi = max(off, cs.start), min(off + A_WIDTH, cs.stop)
                if lo < hi:
                    zneg = jnp.minimum(zc[:, lo - cs.start:hi - cs.start].astype(F32), 0.0)
                    blocks = zneg.reshape(sub // HGRN_SUB, HGRN_SUB, hi - lo).sum(axis=1)
                    sums = jnp.min(blocks, axis=0, keepdims=True)
                    for lb in range(sums.shape[1] // LANES):
                        piece = sums[:, lb * LANES:(lb + 1) * LANES]
                        stats[d] = piece if stats[d] is None else jnp.minimum(stats[d], piece)
    st_ref[0, 0] = jnp.concatenate([stats[0], stats[1], jnp.zeros((6, LANES), F32)], axis=0)


def in_proj(h, norm_g, sc, sh, w_in, layer, q_norm_g, k_norm_g, cos_t=None, sin_t=None, moe=None):
    bsz, n, d = h.shape
    tm = min(INPROJ_ROWS, n)
    nt = n // tm
    rope = cos_t is not None
    qg = jnp.tile(q_norm_g.astype(F32), LANES // B_HD).reshape(1, LANES)
    kg = jnp.tile(k_norm_g.astype(F32), LANES // B_HD).reshape(1, LANES)
    vec = pl.BlockSpec((1, LANES), lambda b, i: (0, 0))
    hrow = pl.BlockSpec((1, tm, d), lambda b, i: (b, i, 0))
    per_b = pl.BlockSpec((1, 1, d), lambda b, i: (b, 0, 0))
    in_specs = [hrow, pl.BlockSpec((1, d), lambda b, i: (0, 0)), per_b, per_b,
                pl.BlockSpec((1, d, D_IN), lambda b, i: (layer, 0, 0)), vec, vec]
    args = [h, norm_g.reshape(1, d), sc, sh, w_in, qg, kg]
    if rope:
        in_specs += [pl.BlockSpec((tm, LANES), lambda b, i: (i, 0))] * 2
        args += [cos_t, sin_t]
    row = pl.BlockSpec((1, tm, LANES), lambda b, i: (b, i, 0))
    out_specs = [pl.BlockSpec((1, tm, OFF_QB), lambda b, i: (b, i, 0)),
                 pl.BlockSpec((1, B_HEADS, tm, LANES), lambda b, i: (b, 0, i, 0)), row, row,
                 pl.BlockSpec((1, 1, 8, LANES), lambda b, i: (b, i, 0, 0))]
    out_shape = [jax.ShapeDtypeStruct((bsz, n, OFF_QB), MXU_DTYPE),
                 jax.ShapeDtypeStruct((bsz, B_HEADS, n, LANES), MXU_DTYPE),
                 jax.ShapeDtypeStruct((bsz, n, LANES), MXU_DTYPE),
                 jax.ShapeDtypeStruct((bsz, n, LANES), MXU_DTYPE),
                 jax.ShapeDtypeStruct((bsz, nt, 8, LANES), F32)]
    if moe is not None:
        y1, y2, slab, g2, tok_off = moe
        off = tok_off // tm
        yspec = pl.BlockSpec((y1.shape[0], tm, LANES), lambda b, i: (0, off + b * nt + i, 0))
        in_specs += [yspec, yspec, pl.BlockSpec((tm, LANES), lambda b, i: (off + b * nt + i, 0)), per_b]
        args += [y1, y2, slab, g2]
        out_specs.append(hrow)
        out_shape.append(jax.ShapeDtypeStruct((bsz, n, d), F32))
    return pl.pallas_call(
        functools.partial(_inproj_kernel, col_chunk=INPROJ_COL_CHUNK, rope=rope, combine=moe is not None),
        grid=(bsz, nt),
        in_specs=in_specs,
        out_specs=out_specs,
        out_shape=out_shape,
        compiler_params=_cparams(("parallel", "parallel")),
        name="in_proj_rope" if rope else "in_proj",
    )(*args)


def rope_tables(n):
    t = jnp.arange(n)
    r = (t // GRID_W).astype(F32)
    col = (t % GRID_W).astype(F32)
    inv = ROPE_THETA ** (-jnp.arange(0, AXIS_DIM, 2, dtype=F32) / AXIS_DIM)
    ang_r = r[:, None] * inv
    ang_c = col[:, None] * inv
    cos = jnp.concatenate([jnp.cos(ang_r)] * 2 + [jnp.cos(ang_c)] * 2, axis=1)
    sin = jnp.concatenate([-jnp.sin(ang_r), jnp.sin(ang_r), -jnp.sin(ang_c), jnp.sin(ang_c)], axis=1)
    return jnp.tile(cos, (1, 2)), jnp.tile(sin, (1, 2))


def _hgrn_gates(zf, loglb, log1m, onem):
    l1pe = jnp.log(1.0 + jnp.exp(-jnp.abs(zf)))
    c = log1m + (jnp.minimum(zf, 0.0) - l1pe)
    logf = jnp.maximum(loglb, c) + jnp.log(1.0 + jnp.exp(-jnp.abs(loglb - c)))
    k = onem * jnp.exp(jnp.minimum(-zf, 0.0) - l1pe)
    return k, logf * LOG2E


def _cumsum_rows(cum_mat, x):
    out = jnp.dot(cum_mat, jnp.concatenate(_split3(x), axis=1), preferred_element_type=F32)
    w = x.shape[1]
    return out[:, :w] + out[:, w:2 * w] + out[:, 2 * w:]


def _hgrn_kernel(*refs, rev, n_chunks, fuse):
    if fuse:
        safe_ref, q_ref, f_ref, v_ref, lbp_ref, s0_ref, ob_ref, og_ref, ng_ref, o_ref, sT_ref, state = refs
    else:
        safe_ref, q_ref, f_ref, v_ref, lbp_ref, s0_ref, o_ref, sT_ref, state = refs
        ob_ref = og_ref = ng_ref = None
    i = pl.program_id(1)
    nt = pl.num_programs(1)

    @pl.when(i == 0)
    def _():
        state[...] = s0_ref[0]

    loglb = lbp_ref[0:1, :]
    log1m = lbp_ref[1:2, :]
    onem = lbp_ref[2:3, :]
    c = HGRN_CHUNK
    nsub = c // HGRN_SUB
    r_i = lax.broadcasted_iota(jnp.int32, (c, c), 0)
    c_i = lax.broadcasted_iota(jnp.int32, (c, c), 1)
    causal = (c_i >= r_i) if rev else (c_i <= r_i)
    cum_mat = causal.astype(jnp.bfloat16)

    def prepare(sl, exact):
        k, logf = _hgrn_gates(f_ref[0, sl, :].astype(F32), loglb, log1m, onem)
        zq = q_ref[0, sl, :].astype(F32)
        q = zq * _sigmoid(zq) * (A_DK ** -0.5)
        cum = _cumsum_rows(cum_mat, logf)
        tot = cum[0:1] if rev else cum[c - 1:c]
        qd = (q * jnp.exp2(cum)).astype(MXU_DTYPE)
        kd = (k * jnp.exp2(tot - cum)).astype(MXU_DTYPE)
        refs_j = []
        for j in range(nsub):
            if rev:
                r = cum[(j + 1) * HGRN_SUB:(j + 1) * HGRN_SUB + 1] if j < nsub - 1 else jnp.zeros_like(tot)
            else:
                r = cum[j * HGRN_SUB - 1:j * HGRN_SUB] if j > 0 else jnp.zeros_like(tot)
            refs_j.append(r)
        ref_rows = jnp.concatenate([jnp.broadcast_to(r, (HGRN_SUB, r.shape[1])) for r in refs_j], axis=0)
        qt = (q * jnp.exp2(cum - ref_rows)).astype(MXU_DTYPE)
        if exact:
            blk512 = lax.broadcasted_iota(jnp.int32, cum.shape, 0) // HGRN_SUB
            kt = [jnp.where((blk512 > j) if rev else (blk512 < j),
                            k * jnp.exp2(jnp.minimum(r - cum, 0.0)), 0.0).astype(MXU_DTYPE)
                  for j, r in enumerate(refs_j)]
            return qd, kd, jnp.exp2(tot), qt, kt, (q, k, cum)
        kt = [(k * jnp.exp2(jnp.minimum(r - cum, HGRN_EXP2_CLAMP))).astype(MXU_DTYPE) for r in refs_j]
        return qd, kd, jnp.exp2(tot), qt, kt, None

    def exact_diagonal(q, k, cum, hs):
        lane16 = lax.broadcasted_iota(jnp.int32, (HGRN_SUB, HGRN_SUB), 1)
        strips = []
        for j in range(nsub):
            rows = slice(j * HGRN_SUB, (j + 1) * HGRN_SUB)
            qb, kb, cb = q[rows, hs], k[rows, hs], cum[rows, hs]
            blk = jnp.zeros((HGRN_SUB, HGRN_SUB), F32)
            for t in range(HGRN_SUB):
                w = qb * kb[t:t + 1] * jnp.exp2(jnp.minimum(cb - cb[t:t + 1], 0.0))
                blk = jnp.where(lane16 == t, jnp.sum(w, axis=-1, keepdims=True), blk)
            pieces = [blk if jj == j else jnp.zeros((HGRN_SUB, HGRN_SUB), F32) for jj in range(nsub)]
            strips.append(jnp.concatenate(pieces, axis=1))
        return jnp.concatenate(strips, axis=0)

    def finish(sl, prepped, s):
        qd, kd, decay, qt, kt, raw = prepped
        v = v_ref[0, sl, :]
        head = [slice(h * A_DK, (h + 1) * A_DK) for h in range(A_HEADS)]
        score_list = []
        for hs in head:
            scores = jnp.concatenate([_nt_dot(qt[j * HGRN_SUB:(j + 1) * HGRN_SUB, hs], kt[j][:, hs])
                                      for j in range(nsub)], axis=0)
            if raw is not None:
                scores = scores + exact_diagonal(*raw, hs)
            score_list.append(jnp.where(causal, scores, 0.0).astype(MXU_DTYPE))
        s_new = []
        for h0 in range(0, A_HEADS, 2):
            pair = slice(h0 * A_DK, (h0 + 2) * A_DK)
            upd = lax.dot_general(v[:, pair], kd[:, pair], (((0,), (0,)), ((), ())), preferred_element_type=F32)
            for h in (h0, h0 + 1):
                off = (h - h0) * A_DK
                s_new.append(s[h] * decay[:, head[h]] + upd[off:off + A_DK, off:off + A_DK])
        outs = []
        zero_s = jnp.zeros((A_DK, A_DK), MXU_DTYPE)
        zero_v = jnp.zeros((c, A_DK), MXU_DTYPE)
        for h0 in range(0, A_HEADS, 2):
            h1 = h0 + 1
            pair = slice(h0 * A_DK, (h1 + 1) * A_DK)
            s_pair = jnp.concatenate(
                [jnp.concatenate([s[h0].astype(MXU_DTYPE), zero_s], axis=1),
                 jnp.concatenate([zero_s, s[h1].astype(MXU_DTYPE)], axis=1)], axis=0)
            v_pair = jnp.concatenate([jnp.concatenate([v[:, head[h0]], zero_v], axis=1),
                                      jnp.concatenate([zero_v, v[:, head[h1]]], axis=1)], axis=0)
            a_pair = jnp.concatenate([score_list[h0], score_list[h1]], axis=1)
            o_pair = _nt_dot(qd[:, pair], s_pair) + jnp.dot(a_pair, v_pair, preferred_element_type=F32)
            for h in (h0, h1):
                off = (h - h0) * A_DK
                o_h = o_pair[:, off:off + A_DK]
                if fuse:
                    o_h = o_h + ob_ref[0, sl, head[h]]
                    o_h = o_h * lax.rsqrt(jnp.mean(o_h * o_h, axis=-1, keepdims=True) + EPS)
                outs.append(o_h)
        s[:] = s_new
        o = jnp.concatenate(outs, axis=1)
        if fuse:
            og = og_ref[0, sl, :].astype(F32)
            o = o * ng_ref[...] * (og * _sigmoid(og))
        o_ref[0, sl, :] = o.astype(o_ref.dtype)

    safe = safe_ref[pl.program_id(0), (nt - 1 - i) if rev else i] != 0

    @pl.when(safe)
    def _():
        s = [state[h] for h in range(A_HEADS)]
        order = list(range(n_chunks - 1, -1, -1) if rev else range(n_chunks))
        slices = [slice(ci * c, (ci + 1) * c) for ci in order]
        ready = prepare(slices[0], False)
        for n, sl in enumerate(slices):
            prepped = ready
            if n + 1 < len(slices):
                ready = prepare(slices[n + 1], False)
            finish(sl, prepped, s)
        for h in range(A_HEADS):
            state[h] = s[h]

    @pl.when(jnp.logical_not(safe))
    def _():
        def chunk(n, carry):
            ci = (n_chunks - 1 - n) if rev else n
            sl = pl.ds(pl.multiple_of(ci * c, c), c)
            s = [state[h] for h in range(A_HEADS)]
            finish(sl, prepare(sl, True), s)
            for h in range(A_HEADS):
                state[h] = s[h]
            return carry

        lax.fori_loop(0, n_chunks, chunk, 0)

    @pl.when(i == pl.num_programs(1) - 1)
    def _():
        sT_ref[0] = state[...]


def hgrn_safe_flags(stats, direction, n, step_tokens):
    worst = stats[:, :, direction, :].min(axis=-1)
    bsz, nb = worst.shape
    per_step = nb * step_tokens // n
    worst = worst.reshape(bsz, nb // per_step, per_step).min(axis=-1)
    return (worst * LOG2E - HGRN_SUB >= -HGRN_EXP2_CLAMP).astype(jnp.int32)


def hgrn_scan(z, stats, lbp, s0, rev, o_other=None, norm_g=None):
    bsz, n, _ = z.shape
    fuse = o_other is not None
    t = min(HGRN_STEP_TOKENS, n)
    nt = n // t
    f_off = (OFF_FB if rev else OFF_FF) // A_WIDTH
    safe = hgrn_safe_flags(stats, 1 if rev else 0, n, t)

    def tmap(i):
        return nt - 1 - i if rev else i

    def col(blk):
        return pl.BlockSpec((1, t, A_WIDTH), lambda b, i, sf: (b, tmap(i), blk))

    state_spec = pl.BlockSpec((1, A_HEADS, LANES, LANES), lambda b, i, sf: (b, 0, 0, 0))
    in_specs = [col(OFF_QA // A_WIDTH), col(f_off), col(OFF_IA // A_WIDTH),
                pl.BlockSpec((8, A_WIDTH), lambda b, i, sf: (0, 0)), state_spec]
    args = [z, z, z, lbp, s0]
    if fuse:
        in_specs += [col(0), col(OFF_OG // A_WIDTH), pl.BlockSpec((1, A_WIDTH), lambda b, i, sf: (0, 0))]
        args += [o_other, z, norm_g.reshape(1, A_WIDTH)]
    kern = functools.partial(_hgrn_kernel, rev=rev, n_chunks=t // HGRN_CHUNK, fuse=fuse)
    grid_spec = pltpu.PrefetchScalarGridSpec(
        num_scalar_prefetch=1,
        grid=(bsz, nt),
        in_specs=in_specs,
        out_specs=[col(0), state_spec],
        scratch_shapes=[pltpu.VMEM((A_HEADS, LANES, LANES), F32)],
    )
    return pl.pallas_call(
        kern,
        grid_spec=grid_spec,
        out_shape=[jax.ShapeDtypeStruct((bsz, n, A_WIDTH), MXU_DTYPE if fuse else F32),
                   jax.ShapeDtypeStruct((bsz, A_HEADS, LANES, LANES), F32)],
        compiler_params=_cparams(("parallel", "arbitrary")),
        name="hgrn_bwd" if rev else "hgrn_fwd",
    )(safe, *args)


def _attn_kernel(q_ref, k_ref, v_ref, o_ref):
    tq = q_ref.shape[2]
    rows = B_GROUP * tq
    lane = lax.broadcasted_iota(jnp.int32, v_ref.shape[1:], 1)
    lane_o = lax.broadcasted_iota(jnp.int32, (tq, LANES), 1)
    s_all = _nt_dot(q_ref[0].reshape(B_HEADS * tq, LANES), k_ref[0])
    for kv in range(B_KV_HEADS):
        s = s_all[kv * rows:(kv + 1) * rows]
        vs = jnp.where((lane // B_HD) == kv, v_ref[0], jnp.ones_like(v_ref[0]))
        p = jnp.exp2(s - jnp.max(s, axis=-1, keepdims=True))
        a = jnp.dot(p.astype(MXU_DTYPE), vs, preferred_element_type=F32)
        o_full = a / pltpu.roll(a, B_HD, 1)
        o_swapped = pltpu.roll(o_full, B_HD, 1)
        for pair in range(B_GROUP // 2):
            g0, g1 = 2 * pair, 2 * pair + 1
            lo_src = o_full if kv == 0 else o_swapped
            hi_src = o_swapped if kv == 0 else o_full
            blk = jnp.where(lane_o < B_HD, lo_src[g0 * tq:(g0 + 1) * tq], hi_src[g1 * tq:(g1 + 1) * tq])
            cb = kv * (B_GROUP // 2) + pair
            o_ref[0, :, cb * LANES:(cb + 1) * LANES] = blk.astype(o_ref.dtype)


def attention(q, k, v):
    bsz, _, n, _ = q.shape
    nkeys = k.shape[1]
    tq = min(ATTN_Q_ROWS, n)
    return pl.pallas_call(
        _attn_kernel,
        grid=(bsz, n // tq),
        in_specs=[pl.BlockSpec((1, B_HEADS, tq, LANES), lambda b, i: (b, 0, i, 0)),
                  pl.BlockSpec((1, nkeys, LANES), lambda b, i: (b, 0, 0)),
                  pl.BlockSpec((1, nkeys, LANES), lambda b, i: (b, 0, 0))],
        out_specs=pl.BlockSpec((1, tq, B_WIDTH), lambda b, i: (b, i, 0)),
        out_shape=jax.ShapeDtypeStruct((bsz, n, B_WIDTH), MXU_DTYPE),
        compiler_params=_cparams(("parallel", "parallel")),
        name="gqa_attention",
    )(q, k, v)


def _pack_rows(x):
    w = x.shape[1] // 2
    bits = lax.bitcast_convert_type(x.astype(jnp.bfloat16).astype(F32), jnp.uint32)
    packed = (bits[:, :w] & jnp.uint32(0xFFFF0000)) | (bits[:, w:] >> 16)
    return lax.bitcast_convert_type(packed, jnp.int32)


def _unpack_rows(p):
    bits = lax.bitcast_convert_type(p, jnp.uint32)
    hi = lax.bitcast_convert_type(bits & jnp.uint32(0xFFFF0000), F32)
    lo = lax.bitcast_convert_type(bits << 16, F32)
    return jnp.concatenate([hi, lo], axis=1)


def _store_chunks(ref, packed):
    for j in range(ref.shape[0]):
        ref[j] = packed[:, j * LANES:(j + 1) * LANES]


def _load_chunks(ref):
    return jnp.concatenate([ref[j] for j in range(ref.shape[0])], axis=1)


def _merge_kernel(*refs, aliased):
    if aliased:
        refs = refs[:13] + refs[15:]
    (h_ref, a_ref, b_ref, zg_ref, wa_ref, wb_ref, wo_ref, g1_ref, n2_ref, sc_ref, sh_ref, wr_ref, br_ref,
     ho_ref, x2_ref, lg_ref) = refs
    d = h_ref.shape[2]
    tm = h_ref.shape[1]
    sub = min(MERGE_SUB_ROWS, tm)
    for r in range(tm // sub):
        rs = slice(r * sub, (r + 1) * sub)
        ua = jnp.dot(a_ref[0, rs, :], wa_ref[...], preferred_element_type=F32)
        ub = jnp.dot(b_ref[0, rs, :], wb_ref[...], preferred_element_type=F32)
        ga = _sigmoid(zg_ref[0, rs, 0:d].astype(F32))
        gb = _sigmoid(zg_ref[0, rs, d:2 * d].astype(F32))
        mix = (ga * ua + gb * ub).astype(MXU_DTYPE)
        h_new = h_ref[0, rs, :] + g1_ref[0] * jnp.dot(mix, wo_ref[...], preferred_element_type=F32)
        ho_ref[0, rs, :] = h_new
        x2 = _modulated_norm(h_new, n2_ref[...], sc_ref[0], sh_ref[0])
        packed = _pack_rows(x2)
        for j in range(x2_ref.shape[0]):
            x2_ref[j, rs, :] = packed[:, j * LANES:(j + 1) * LANES]
        x_hi = x2.astype(jnp.bfloat16)
        x_lo = (x2 - x_hi.astype(F32)).astype(jnp.bfloat16)
        hi_both = jnp.dot(x_hi, wr_ref[...], preferred_element_type=F32)
        lo_hi = jnp.dot(x_lo, wr_ref[:, 0:LANES], preferred_element_type=F32)
        lg_ref[rs, :] = hi_both[:, 0:LANES] + hi_both[:, LANES:] + lo_hi + br_ref[...]


def merge_proj(h, a, b, z, w_up_a, w_up_b, w_out, g1, norm2_g, sc2, sh2, w_router, b_router, t_total, tok_off,
               bufs=None):
    bsz, n, d = h.shape
    tm = min(MERGE_ROWS, n)
    nt = n // tm
    off = tok_off // tm
    n_chunks = d // (2 * LANES)
    row = lambda w: pl.BlockSpec((1, tm, w), lambda b_, i: (b_, i, 0))
    full = lambda s: pl.BlockSpec(s, lambda b_, i: (0,) * len(s))
    per_b = pl.BlockSpec((1, 1, d), lambda b_, i: (b_, 0, 0))
    in_specs = [row(d), row(A_WIDTH), row(B_WIDTH),
                pl.BlockSpec((1, tm, 2 * d), lambda b_, i: (b_, i, OFF_GATES)),
                full((A_WIDTH, d)), full((B_WIDTH, d)), full((d, d)),
                per_b, full((1, d)), per_b, per_b, full((d, 2 * LANES)), full((1, LANES))]
    args = [h, a, b, z, w_up_a, w_up_b, w_out, g1, norm2_g.reshape(1, d), sc2, sh2, w_router, b_router]
    aliases = {}
    if bufs is not None:
        in_specs += [pl.BlockSpec(memory_space=pl.ANY)] * 2
        args += list(bufs)
        aliases = {13: 1, 14: 2}
    return pl.pallas_call(
        functools.partial(_merge_kernel, aliased=bufs is not None),
        grid=(bsz, nt),
        in_specs=in_specs,
        out_specs=[row(d),
                   pl.BlockSpec((n_chunks, tm, LANES), lambda b_, i: (0, off + b_ * nt + i, 0)),
                   pl.BlockSpec((tm, LANES), lambda b_, i: (off + b_ * nt + i, 0))],
        out_shape=[jax.ShapeDtypeStruct((bsz, n, d), F32),
                   jax.ShapeDtypeStruct((n_chunks, t_total, LANES), jnp.int32),
                   jax.ShapeDtypeStruct((t_total, LANES), F32)],
        input_output_aliases=aliases,
        compiler_params=_cparams(("parallel", "parallel")),
        name="merge_proj",
    )(*args)


def _first_lane_where(cond, lane):
    return jnp.min(jnp.where(cond, lane, LANES), axis=-1, keepdims=True)


def _router_kernel(lg_ref, slab_ref, slab_t_ref, cnt_ref, base):
    @pl.when(pl.program_id(0) == 0)
    def _():
        base[...] = jnp.zeros_like(base)

    lg = lg_ref[...]
    tr = lg.shape[0]
    lane = lax.broadcasted_iota(jnp.int32, lg.shape, 1)
    neg = jnp.float32(-jnp.inf)
    is_g = lane < N_GROUPS
    gl = jnp.where(is_g, lg, neg)
    gmax = jnp.max(gl, axis=-1, keepdims=True)
    gsel = _first_lane_where(gl == gmax, lane)
    p_group = 1.0 / jnp.sum(jnp.where(is_g, jnp.exp(lg - gmax), 0.0), axis=-1, keepdims=True)
    lo = N_GROUPS + gsel * EXPERTS_PER_GROUP
    in_grp = (lane >= lo) & (lane < lo + EXPERTS_PER_GROUP)
    el = jnp.where(in_grp, lg, neg)
    m1 = jnp.max(el, axis=-1, keepdims=True)
    i1 = _first_lane_where(el == m1, lane)
    el2 = jnp.where(lane == i1, neg, el)
    m2 = jnp.max(el2, axis=-1, keepdims=True)
    i2 = _first_lane_where(el2 == m2, lane)
    zsum = jnp.sum(jnp.where(in_grp, jnp.exp(lg - m1), 0.0), axis=-1, keepdims=True)
    p1 = 1.0 / zsum
    p2 = jnp.exp(m2 - m1) / zsum
    w1 = p1 / (p1 + p2) * p_group
    w2 = p2 / (p1 + p2) * p_group
    hit1 = lane == i1
    hit2 = lane == i2
    onehot = (hit1 | hit2).astype(F32)
    r_i = lax.broadcasted_iota(jnp.int32, (tr, tr), 0)
    c_i = lax.broadcasted_iota(jnp.int32, (tr, tr), 1)
    before = (c_i < r_i).astype(MXU_DTYPE)
    prior = jnp.dot(before, onehot.astype(MXU_DTYPE), preferred_element_type=F32) + base[0:1, :]
    r1 = jnp.sum(jnp.where(hit1, prior, 0.0), axis=-1, keepdims=True)
    r2 = jnp.sum(jnp.where(hit2, prior, 0.0), axis=-1, keepdims=True)
    new_base = base[0:1, :] + jnp.sum(onehot, axis=0, keepdims=True)
    base[...] = jnp.broadcast_to(new_base, base.shape)
    cnt_ref[...] = jnp.broadcast_to(new_base, cnt_ref.shape)
    vals = [(i1 - N_GROUPS).astype(F32), (i2 - N_GROUPS).astype(F32), w1, w2, r1, r2]
    slab = jnp.zeros_like(lg)
    for idx, val in enumerate(vals):
        slab = jnp.where(lane == idx, val, slab)
    slab_ref[...] = slab
    slab_t_ref[...] = slab.T[0:8, :]


def route(logits):
    t = logits.shape[0]
    tr = next(c for c in ROUTER_ROWS if t % c == 0)
    return pl.pallas_call(
        _router_kernel,
        grid=(t // tr,),
        in_specs=[pl.BlockSpec((tr, LANES), lambda i: (i, 0))],
        out_specs=[pl.BlockSpec((tr, LANES), lambda i: (i, 0)), pl.BlockSpec((8, tr), lambda i: (0, i)),
                   pl.BlockSpec((8, LANES), lambda i: (0, 0))],
        out_shape=[jax.ShapeDtypeStruct((t, LANES), F32), jax.ShapeDtypeStruct((8, t), F32),
                   jax.ShapeDtypeStruct((8, LANES), F32)],
        scratch_shapes=[pltpu.VMEM((8, LANES), F32)],
        compiler_params=_cparams(("arbitrary",)),
        name="moe_route",
    )(logits)


def _expert_kernel(be_ref, nv_ref, x_ref, wg_ref, wu_ref, wd_ref, o_ref, wg_s, wu_s, wd_s):
    j = pl.program_id(0)
    nv = nv_ref[j]

    @pl.when((j == 0) | (be_ref[j] != be_ref[jnp.maximum(j - 1, 0)]))
    def _():
        wg_s[...] = wg_ref[0, 0].astype(wg_s.dtype)
        wu_s[...] = wu_ref[0, 0].astype(wu_s.dtype)
        wd_s[...] = wd_ref[0, 0].astype(wd_s.dtype)

    @pl.when(nv > 0)
    def _():
        x = _unpack_rows(_load_chunks(x_ref))
        rows = lax.broadcasted_iota(jnp.int32, x.shape, 0)
        x = jnp.where(rows < nv, x, 0.0).astype(MXU_DTYPE)
        hg = jnp.dot(x, wg_s[...], preferred_element_type=F32)
        hu = jnp.dot(x, wu_s[...], preferred_element_type=F32)
        hb = (hg * _sigmoid(hg) * hu).astype(MXU_DTYPE)
        _store_chunks(o_ref, _pack_rows(jnp.dot(hb, wd_s[...], preferred_element_type=F32)))

    @pl.when(nv == 0)
    def _():
        o_ref[...] = jnp.zeros_like(o_ref)


def expert_ffn(x_sorted, block_e, nvalid, w_gate, w_up, w_down, layer, tmx):
    n_chunks, r, _ = x_sorted.shape
    _, _, d, de = w_gate.shape
    grid_spec = pltpu.PrefetchScalarGridSpec(
        num_scalar_prefetch=2,
        grid=(r // tmx,),
        in_specs=[pl.BlockSpec((n_chunks, tmx, LANES), lambda j, be, nv: (0, j, 0)),
                  pl.BlockSpec((1, 1, d, de), lambda j, be, nv: (layer, be[j], 0, 0)),
                  pl.BlockSpec((1, 1, d, de), lambda j, be, nv: (layer, be[j], 0, 0)),
                  pl.BlockSpec((1, 1, de, d), lambda j, be, nv: (layer, be[j], 0, 0))],
        out_specs=pl.BlockSpec((n_chunks, tmx, LANES), lambda j, be, nv: (0, j, 0)),
        scratch_shapes=[pltpu.VMEM((d, de), MXU_DTYPE), pltpu.VMEM((d, de), MXU_DTYPE),
                        pltpu.VMEM((de, d), MXU_DTYPE)],
    )
    return pl.pallas_call(
        _expert_kernel,
        grid_spec=grid_spec,
        out_shape=jax.ShapeDtypeStruct((n_chunks, r, LANES), jnp.int32),
        compiler_params=_cparams(("arbitrary",)),
        name="expert_ffn",
    )(block_e, nvalid, x_sorted, w_gate, w_up, w_down)


def _combine_final_kernel(h_ref, y1_ref, y2_ref, slab_ref, g2_ref, fn_ref, o_ref):
    y = (_unpack_rows(_load_chunks(y1_ref)) * slab_ref[:, 2:3]
         + _unpack_rows(_load_chunks(y2_ref)) * slab_ref[:, 3:4])
    h_new = h_ref[0] + g2_ref[0] * y
    ms = jnp.mean(h_new * h_new, axis=-1, keepdims=True)
    o_ref[0] = h_new * lax.rsqrt(ms + EPS) * fn_ref[...]


def moe_combine_final(h, y1, y2, slab, g2, final_g):
    bsz, n, d = h.shape
    tm = min(COMBINE_ROWS, n)
    nt = n // tm
    yspec = pl.BlockSpec((y1.shape[0], tm, LANES), lambda b, i: (0, b * nt + i, 0))
    return pl.pallas_call(
        _combine_final_kernel,
        grid=(bsz, nt),
        in_specs=[pl.BlockSpec((1, tm, d), lambda b, i: (b, i, 0)), yspec, yspec,
                  pl.BlockSpec((tm, LANES), lambda b, i: (b * nt + i, 0)),
                  pl.BlockSpec((1, 1, d), lambda b, i: (b, 0, 0)),
                  pl.BlockSpec((1, d), lambda b, i: (0, 0))],
        out_specs=pl.BlockSpec((1, tm, d), lambda b, i: (b, i, 0)),
        out_shape=jax.ShapeDtypeStruct((bsz, n, d), F32),
        compiler_params=_cparams(("parallel", "parallel")),
        name="moe_combine_final",
    )(h, y1, y2, slab, g2, final_g.reshape(1, d))


SC_WINDOW = 128


def _sc_mesh():
    return plsc.VectorSubcoreMesh(core_axis_name="core", subcore_axis_name="subcore")


def sc_gather_rows(table, idx):
    n = idx.shape[0]

    @pl.kernel(out_type=jax.ShapeDtypeStruct((n, LANES), table.dtype), mesh=_sc_mesh())
    def gather_kernel(t_hbm, i_hbm, o_hbm):
        def body(i_vmem, o_vmem):
            pltpu.sync_copy(t_hbm.at[i_vmem.at[0]], o_vmem)

        pltpu.emit_pipeline(
            body,
            grid=(n // SC_WINDOW,),
            in_specs=[pl.BlockSpec((1, SC_WINDOW), lambda i: (0, i))],
            out_specs=[pl.BlockSpec((SC_WINDOW, LANES), lambda i: (i, 0))],
            core_axis_name=("core", "subcore"),
            dimension_semantics=(pltpu.PARALLEL,),
            trace_scopes=False,
        )(i_hbm, o_hbm)

    return gather_kernel(table, idx.reshape(1, n))


def sc_scatter_rows2(rows, idx_a, idx_b, n_out):
    n = rows.shape[0]

    @pl.kernel(out_type=jax.ShapeDtypeStruct((n_out, LANES), rows.dtype), mesh=_sc_mesh())
    def scatter_kernel(r_hbm, ia_hbm, ib_hbm, o_hbm):
        def body(r_vmem, ia_vmem, ib_vmem):
            pltpu.sync_copy(r_vmem, o_hbm.at[ia_vmem.at[0]])
            pltpu.sync_copy(r_vmem, o_hbm.at[ib_vmem.at[0]])

        pltpu.emit_pipeline(
            body,
            grid=(n // SC_WINDOW,),
            in_specs=[pl.BlockSpec((SC_WINDOW, LANES), lambda i: (i, 0)),
                      pl.BlockSpec((1, SC_WINDOW), lambda i: (0, i)),
                      pl.BlockSpec((1, SC_WINDOW), lambda i: (0, i))],
            out_specs=[],
            core_axis_name=("core", "subcore"),
            dimension_semantics=(pltpu.PARALLEL,),
            trace_scopes=False,
        )(r_hbm, ia_hbm, ib_hbm)

    return scatter_kernel(rows, idx_a.reshape(1, n), idx_b.reshape(1, n))


def _plan_kernel(ps_ref, st_ref, ia_ref, ib_ref, *, n_rows):
    e = st_ref[0:2, :].astype(jnp.int32)
    dest = st_ref[4:6, :].astype(jnp.int32)
    for ex in range(N_EXPERTS):
        dest = dest + jnp.where(e == ex, ps_ref[ex], 0)
    for k, ref in enumerate((ia_ref, ib_ref)):
        for j in range(ref.shape[0]):
            ref[j:j + 1, :] = dest[k:k + 1, :] + j * n_rows


def dispatch_indices(slab_t, pad_start, n_chunks, n_rows):
    t = slab_t.shape[1]
    tp = next(c for c in PLAN_TOKENS if t % c == 0)
    grid_spec = pltpu.PrefetchScalarGridSpec(
        num_scalar_prefetch=1,
        grid=(t // tp,),
        in_specs=[pl.BlockSpec((8, tp), lambda i, ps: (0, i))],
        out_specs=[pl.BlockSpec((n_chunks, tp), lambda i, ps: (0, i))] * 2,
    )
    return pl.pallas_call(
        functools.partial(_plan_kernel, n_rows=n_rows),
        grid_spec=grid_spec,
        out_shape=[jax.ShapeDtypeStruct((n_chunks, t), jnp.int32)] * 2,
        compiler_params=_cparams(("parallel",)),
        name="moe_plan",
    )(pad_start, slab_t)


def hier_moe(x2, logits, w_gate, w_up, w_down, layer):
    n_chunks, t, _ = x2.shape
    tmx = min(EXPERT_ROWS, t)
    n_blocks = (t * TOP_K + N_EXPERTS * (tmx - 1)) // tmx + 1
    r = n_blocks * tmx
    slab, slab_t, counts = route(logits)
    cnt = counts[0, N_GROUPS:N_GROUPS + N_EXPERTS].astype(jnp.int32)
    padded = (cnt + tmx - 1) // tmx * tmx
    pad_end = jnp.cumsum(padded)
    pad_start = pad_end - padded
    blk_start = jnp.arange(n_blocks, dtype=jnp.int32) * tmx
    block_e = jnp.minimum(jnp.sum(pad_end[None, :] <= blk_start[:, None], axis=1), N_EXPERTS - 1).astype(jnp.int32)
    nvalid = jnp.clip(pad_start[block_e] + cnt[block_e] - blk_start, 0, tmx).astype(jnp.int32)
    idx1, idx2 = dispatch_indices(slab_t, pad_start.astype(jnp.int32), n_chunks, r)
    idx1, idx2 = idx1.reshape(-1), idx2.reshape(-1)
    x_sorted = sc_scatter_rows2(x2.reshape(n_chunks * t, LANES), idx1, idx2, n_chunks * r)
    y_sorted = expert_ffn(x_sorted.reshape(n_chunks, r, LANES), block_e, nvalid, w_gate, w_up, w_down, layer, tmx)
    y_flat = y_sorted.reshape(n_chunks * r, LANES)
    y1 = sc_gather_rows(y_flat, idx1).reshape(n_chunks, t, LANES)
    y2 = sc_gather_rows(y_flat, idx2).reshape(n_chunks, t, LANES)
    return y1, y2, slab


def _lower_bound_rows(lb_logits):
    lb = jnp.cumsum(jax.nn.softmax(lb_logits.astype(F32), axis=1), axis=1)
    lb = lb - lb[:, :1]
    rows = jnp.stack([jnp.log(lb), jnp.log1p(-lb), 1.0 - lb], axis=2)
    rows = jnp.concatenate([rows, jnp.zeros(rows.shape[:2] + (5, A_WIDTH), F32)], axis=2)
    return rows


def _cast_kernel(w_ref, o_ref):
    o_ref[...] = w_ref[...].astype(o_ref.dtype)


def reorder_cast_w_in(w_in):
    depth, d, n = w_in.shape
    gate_cols = 2 * d
    blk = REORDER_COLS
    nb = n // blk
    shift = (n - gate_cols) // blk
    return pl.pallas_call(
        _cast_kernel,
        grid=(depth, nb),
        in_specs=[pl.BlockSpec((1, d, blk), lambda l, j: (l, 0, (j + shift) % nb))],
        out_specs=pl.BlockSpec((1, d, blk), lambda l, j: (l, 0, j)),
        out_shape=jax.ShapeDtypeStruct((depth, d, n), MXU_DTYPE),
        compiler_params=_cparams(("parallel", "parallel")),
        name="reorder_cast_w_in",
    )(w_in)


def kernel(x, c, ctx, c_ctx, w_ada, b_ada, norm1_g, norm2_g, w_in, lb_logits, hgrn_norm_g, q_norm_g, k_norm_g,
           w_up_a, w_up_b, w_out, w_router_group, b_router_group, w_router_expert, b_router_expert, w_gate, w_up,
           w_down, final_norm_g):
    bsz, n, d = x.shape
    n_ctx = ctx.shape[1]
    depth = w_in.shape[0]
    mx = MXU_DTYPE

    w_in_r = reorder_cast_w_in(w_in)
    w_up_a_c, w_up_b_c, w_out_c = w_up_a.astype(mx), w_up_b.astype(mx), w_out.astype(mx)
    n_r = N_GROUPS + N_EXPERTS
    w_router = jnp.concatenate([w_router_group, w_router_expert, jnp.zeros((depth, d, LANES - n_r), F32)], axis=-1)
    w_router_hi = w_router.astype(jnp.bfloat16)
    w_router_lo = (w_router - w_router_hi.astype(F32)).astype(jnp.bfloat16)
    w_router = jnp.concatenate([w_router_hi, w_router_lo], axis=-1)
    b_router = jnp.concatenate([b_router_group, b_router_expert, jnp.zeros((depth, LANES - n_r), F32)], axis=-1)
    lbp = _lower_bound_rows(lb_logits)
    cos_t, sin_t = rope_tables(n)

    n_rows = -(-(bsz + 1) // 8) * 8
    act = jnp.concatenate([c, c_ctx[None, :], jnp.zeros((n_rows - bsz - 1, d), F32)], axis=0)
    mod = ada_mod(act, w_ada, b_ada)

    h, hc = x, ctx
    pending = None
    zero_state = jnp.zeros((bsz, A_HEADS, LANES, LANES), F32)
    for l in range(depth):
        need_ctx = l < depth - 1
        ml = mod[l, :bsz].reshape(bsz, 1, N_MOD, d)
        mc = jnp.broadcast_to(mod[l, bsz].reshape(1, 1, N_MOD, d), (bsz, 1, N_MOD, d))
        sh1, sc1, g1, sh2, sc2, g2 = (ml[:, :, i] for i in range(N_MOD))
        csh1, csc1, cg1, csh2, csc2, cg2 = (mc[:, :, i] for i in range(N_MOD))

        moe_l = moe_c = None
        if pending is not None:
            y1, y2, slab, g2_prev, cg2_prev = pending
            moe_l = (y1, y2, slab, g2_prev, 0)
            moe_c = (y1, y2, slab, cg2_prev, bsz * n)
        zl, q_l, k_l, v_l, st_l, *h_new = in_proj(h, norm1_g[l], sc1, sh1, w_in_r, l, q_norm_g[l], k_norm_g[l],
                                                  cos_t, sin_t, moe_l)
        zc, q_c, k_c, v_c, st_c, *hc_new = in_proj(hc, norm1_g[l], csc1, csh1, w_in_r, l, q_norm_g[l], k_norm_g[l],
                                                   None, None, moe_c)
        if pending is not None:
            h, hc = h_new[0], hc_new[0]
        ob_c, s_bwd = hgrn_scan(zc, st_c, lbp[1, l], zero_state, True)
        if need_ctx:
            a_c, s_fwd = hgrn_scan(zc, st_c, lbp[0, l], zero_state, False, ob_c, hgrn_norm_g[l])
        else:
            _, s_fwd = hgrn_scan(zc, st_c, lbp[0, l], zero_state, False)
        ob_l, _ = hgrn_scan(zl, st_l, lbp[1, l], s_bwd, True)
        a_l, _ = hgrn_scan(zl, st_l, lbp[0, l], s_fwd, False, ob_l, hgrn_norm_g[l])

        b_l = attention(q_l, jnp.concatenate([k_c, k_l], axis=1), jnp.concatenate([v_c, v_l], axis=1))

        t_total = bsz * (n + n_ctx) if need_ctx else bsz * n
        bufs = (jnp.zeros((d // (2 * LANES), t_total, LANES), jnp.int32),
                jnp.zeros((t_total, LANES), F32)) if need_ctx else None
        h, x2, lg = merge_proj(h, a_l, b_l, zl, w_up_a_c[l], w_up_b_c[l], w_out_c[l], g1, norm2_g[l], sc2, sh2,
                               w_router[l], b_router[l][None, :], t_total, 0, bufs)
        if need_ctx:
            b_c = attention(q_c, k_c, v_c)
            hc, x2, lg = merge_proj(hc, a_c, b_c, zc, w_up_a_c[l], w_up_b_c[l], w_out_c[l], cg1, norm2_g[l],
                                    csc2, csh2, w_router[l], b_router[l][None, :], t_total, bsz * n, (x2, lg))

        y1, y2, slab = hier_moe(x2, lg, w_gate, w_up, w_down, l)
        pending = (y1, y2, slab, g2, cg2)
    y1, y2, slab, g2, _ = pending
    return moe_combine_final(h, y1, y2, slab, g2, final_norm_g)
```

```python
import functools

import jax
import jax.numpy as jnp
from jax import lax
from jax.experimental import pallas as pl
from jax.experimental.pallas import tpu as pltpu
from jax.experimental.pallas import tpu_sc as plsc

F32 = jnp.float32
MXU_DTYPE = jnp.bfloat16

EPS = 1e-6
GRID_W = 64
ROPE_THETA = 10000.0
LANES = 128

A_HEADS = 4
A_DK = 128
A_WIDTH = A_HEADS * A_DK
HGRN_CHUNK = 64
HGRN_SUB = 16
HGRN_STEP_TOKENS = 1024
HGRN_EXP2_CLAMP = 86.0
B_HEADS = 8
B_KV_HEADS = 2
B_HD = 64
B_GROUP = B_HEADS // B_KV_HEADS
B_WIDTH = B_HEADS * B_HD
B_KV_WIDTH = B_KV_HEADS * B_HD
AXIS_DIM = B_HD // 2
N_GROUPS = 4
EXPERTS_PER_GROUP = 8
N_EXPERTS = N_GROUPS * EXPERTS_PER_GROUP
TOP_K = 2
N_MOD = 6

OFF_GATES = 0
OFF_QA = 2048
OFF_FF = OFF_QA + A_WIDTH
OFF_FB = OFF_FF + A_WIDTH
OFF_IA = OFF_FB + A_WIDTH
OFF_OG = OFF_IA + A_WIDTH
OFF_QB = OFF_OG + A_WIDTH
OFF_KB = OFF_QB + B_WIDTH
OFF_VB = OFF_KB + B_KV_WIDTH
D_IN = OFF_VB + B_KV_WIDTH

VMEM_LIMIT = 52 * 1024 * 1024
ADA_COLS = 1024
REORDER_COLS = 256
INPROJ_ROWS = 512
INPROJ_SUB_ROWS = 256
INPROJ_COL_CHUNK = 768
ATTN_Q_ROWS = 128
MERGE_ROWS = 1024
MERGE_SUB_ROWS = 1024
ROUTER_ROWS = (1024, 512, 256, 128)
PLAN_TOKENS = (4096, 2048, 1024, 512, 256, 128)
EXPERT_ROWS = 1024
COMBINE_ROWS = 512


def _cparams(sem):
    return pltpu.CompilerParams(dimension_semantics=sem, vmem_limit_bytes=VMEM_LIMIT)


LOG2E = 1.4426950408889634


def _sigmoid(x):
    return 0.5 * jnp.tanh(0.5 * x) + 0.5


def _nt_dot(a, b):
    return lax.dot_general(a, b, (((1,), (1,)), ((), ())), preferred_element_type=F32)


def _ada_kernel(a_ref, w_ref, b_ref, o_ref):
    a = a_ref[...]
    a = a * _sigmoid(a)
    o_ref[0] = jnp.dot(a.astype(MXU_DTYPE), w_ref[0].astype(MXU_DTYPE), preferred_element_type=F32) + b_ref[0]


def ada_mod(act, w_ada, b_ada):
    depth, d, n = w_ada.shape
    r = act.shape[0]
    tn = ADA_COLS
    return pl.pallas_call(
        _ada_kernel,
        grid=(depth, n // tn),
        in_specs=[
            pl.BlockSpec((r, d), lambda l, j: (0, 0)),
            pl.BlockSpec((1, d, tn), lambda l, j: (l, 0, j)),
            pl.BlockSpec((1, 1, tn), lambda l, j: (l, 0, j)),
        ],
        out_specs=pl.BlockSpec((1, r, tn), lambda l, j: (l, 0, j)),
        out_shape=jax.ShapeDtypeStruct((depth, r, n), F32),
        compiler_params=_cparams(("parallel", "parallel")),
        name="ada_mod",
    )(act, w_ada, b_ada.reshape(depth, 1, n))


def _modulated_norm(x, g, sc, sh):
    ms = jnp.mean(x * x, axis=-1, keepdims=True)
    return (x * lax.rsqrt(ms + EPS) * g) * (1.0 + sc) + sh


def _split3(x):
    hi = x.astype(jnp.bfloat16)
    r1 = x - hi.astype(F32)
    mid = r1.astype(jnp.bfloat16)
    lo = (r1 - mid.astype(F32)).astype(jnp.bfloat16)
    return hi, mid, lo


def _swap16(x):
    lane = lax.broadcasted_iota(jnp.int32, x.shape, 1)
    return jnp.where(lane % 32 < 16, pltpu.roll(x, LANES - 16, 1), pltpu.roll(x, 16, 1))


def _head_norm_rope(x, gsum, g, cos, sin):
    ssum = sum(jnp.dot(t, gsum, preferred_element_type=F32) for t in _split3(x * x))
    y = x * lax.rsqrt(ssum * (1.0 / B_HD) + EPS) * g
    if cos is not None:
        y = y * cos + _swap16(y) * sin
    return y


def _inproj_kernel(*refs, col_chunk, rope, combine):
    refs = list(refs)
    h_ref, g_ref, sc_ref, sh_ref, w_ref, qg_ref, kg_ref = refs[:7]
    del refs[:7]
    cos = sin = None
    if rope:
        cos, sin = refs[0][...], refs[1][...]
        del refs[:2]
    if combine:
        y1_ref, y2_ref, slab_ref, g2_ref = refs[:4]
        del refs[:4]
        ho_ref = refs.pop()
    z_ref, qo_ref, ko_ref, vo_ref, st_ref = refs
    r_i = lax.broadcasted_iota(jnp.int32, (LANES, LANES), 0)
    c_i = lax.broadcasted_iota(jnp.int32, (LANES, LANES), 1)
    gsum = (r_i // B_HD == c_i // B_HD).astype(jnp.bfloat16)
    tm = h_ref.shape[1]
    sub = min(INPROJ_SUB_ROWS, tm)
    lane = lax.broadcasted_iota(jnp.int32, (sub, LANES), 1)
    stats = [None, None]
    for r in range(tm // sub):
        rs = slice(r * sub, (r + 1) * sub)
        h = h_ref[0, rs, :]
        if combine:
            y12 = (_unpack_rows(jnp.concatenate([y1_ref[j, rs, :] for j in range(y1_ref.shape[0])], axis=1))
                   * slab_ref[rs, 2:3]
                   + _unpack_rows(jnp.concatenate([y2_ref[j, rs, :] for j in range(y2_ref.shape[0])], axis=1))
                   * slab_ref[rs, 3:4])
            h = h + g2_ref[0] * y12
            ho_ref[0, rs, :] = h
        y = _modulated_norm(h, g_ref[...], sc_ref[0], sh_ref[0]).astype(MXU_DTYPE)
        gqa = jnp.dot(y, w_ref[0, :, OFF_QB:], preferred_element_type=F32)
        cs_, sn_ = (cos[rs], sin[rs]) if rope else (None, None)
        for cb in range(B_WIDTH // LANES):
            x = gqa[:, cb * LANES:(cb + 1) * LANES]
            q = _head_norm_rope(x, gsum, qg_ref[...], cs_, sn_) * (B_HD ** -0.5 * LOG2E)
            swapped = pltpu.roll(q, B_HD, 1)
            for half in range(2):
                head = 2 * cb + half
                kv = head // B_GROUP
                src = q if half == kv else swapped
                keep = (lane // B_HD) == kv
                qo_ref[0, head, rs, :] = jnp.where(keep, src, 0.0).astype(qo_ref.dtype)
        kn = _head_norm_rope(gqa[:, B_WIDTH:B_WIDTH + B_KV_WIDTH], gsum, kg_ref[...], cs_, sn_)
        ko_ref[0, rs, :] = kn.astype(ko_ref.dtype)
        vo_ref[0, rs, :] = gqa[:, B_WIDTH + B_KV_WIDTH:].astype(vo_ref.dtype)
        for c in range(z_ref.shape[2] // col_chunk):
            cs = slice(c * col_chunk, (c + 1) * col_chunk)
            zc = jnp.dot(y, w_ref[0, :, cs], preferred_element_type=F32).astype(z_ref.dtype)
            z_ref[0, rs, cs] = zc
            for d, off in enumerate((OFF_FF, OFF_FB)):
                lo, hi = max(off, cs.start), min(off + A_WIDTH, cs.stop)
                if lo < hi:
                    zneg = jnp.minimum(zc[:, lo - cs.start:hi - cs.start].astype(F32), 0.0)
                    blocks = zneg.reshape(sub // HGRN_SUB, HGRN_SUB, hi - lo).sum(axis=1)
                    sums = jnp.min(blocks, axis=0, keepdims=True)
                    for lb in range(sums.shape[1] // LANES):
                        piece = sums[:, lb * LANES:(lb + 1) * LANES]
                        stats[d] = piece if stats[d] is None else jnp.minimum(stats[d], piece)
    st_ref[0, 0] = jnp.concatenate([stats[0], stats[1], jnp.zeros((6, LANES), F32)], axis=0)


def in_proj(h, norm_g, sc, sh, w_in, layer, q_norm_g, k_norm_g, cos_t=None, sin_t=None, moe=None):
    bsz, n, d = h.shape
    tm = min(INPROJ_ROWS, n)
    nt = n // tm
    rope = cos_t is not None
    qg = jnp.tile(q_norm_g.astype(F32), LANES // B_HD).reshape(1, LANES)
    kg = jnp.tile(k_norm_g.astype(F32), LANES // B_HD).reshape(1, LANES)
    vec = pl.BlockSpec((1, LANES), lambda b, i: (0, 0))
    hrow = pl.BlockSpec((1, tm, d), lambda b, i: (b, i, 0))
    per_b = pl.BlockSpec((1, 1, d), lambda b, i: (b, 0, 0))
    in_specs = [hrow, pl.BlockSpec((1, d), lambda b, i: (0, 0)), per_b, per_b,
                pl.BlockSpec((1, d, D_IN), lambda b, i: (layer, 0, 0)), vec, vec]
    args = [h, norm_g.reshape(1, d), sc, sh, w_in, qg, kg]
    if rope:
        in_specs += [pl.BlockSpec((tm, LANES), lambda b, i: (i, 0))] * 2
        args += [cos_t, sin_t]
    row = pl.BlockSpec((1, tm, LANES), lambda b, i: (b, i, 0))
    out_specs = [pl.BlockSpec((1, tm, OFF_QB), lambda b, i: (b, i, 0)),
                 pl.BlockSpec((1, B_HEADS, tm, LANES), lambda b, i: (b, 0, i, 0)), row, row,
                 pl.BlockSpec((1, 1, 8, LANES), lambda b, i: (b, i, 0, 0))]
    out_shape = [jax.ShapeDtypeStruct((bsz, n, OFF_QB), MXU_DTYPE),
                 jax.ShapeDtypeStruct((bsz, B_HEADS, n, LANES), MXU_DTYPE),
                 jax.ShapeDtypeStruct((bsz, n, LANES), MXU_DTYPE),
                 jax.ShapeDtypeStruct((bsz, n, LANES), MXU_DTYPE),
                 jax.ShapeDtypeStruct((bsz, nt, 8, LANES), F32)]
    if moe is not None:
        y1, y2, slab, g2, tok_off = moe
        off = tok_off // tm
        yspec = pl.BlockSpec((y1.shape[0], tm, LANES), lambda b, i: (0, off + b * nt + i, 0))
        in_specs += [yspec, yspec, pl.BlockSpec((tm, LANES), lambda b, i: (off + b * nt + i, 0)), per_b]
        args += [y1, y2, slab, g2]
        out_specs.append(hrow)
        out_shape.append(jax.ShapeDtypeStruct((bsz, n, d), F32))
    return pl.pallas_call(
        functools.partial(_inproj_kernel, col_chunk=INPROJ_COL_CHUNK, rope=rope, combine=moe is not None),
        grid=(bsz, nt),
        in_specs=in_specs,
        out_specs=out_specs,
        out_shape=out_shape,
        compiler_params=_cparams(("parallel", "parallel")),
        name="in_proj_rope" if rope else "in_proj",
    )(*args)


def rope_tables(n):
    t = jnp.arange(n)
    r = (t // GRID_W).astype(F32)
    col = (t % GRID_W).astype(F32)
    inv = ROPE_THETA ** (-jnp.arange(0, AXIS_DIM, 2, dtype=F32) / AXIS_DIM)
    ang_r = r[:, None] * inv
    ang_c = col[:, None] * inv
    cos = jnp.concatenate([jnp.cos(ang_r)] * 2 + [jnp.cos(ang_c)] * 2, axis=1)
    sin = jnp.concatenate([-jnp.sin(ang_r), jnp.sin(ang_r), -jnp.sin(ang_c), jnp.sin(ang_c)], axis=1)
    return jnp.tile(cos, (1, 2)), jnp.tile(sin, (1, 2))


def _hgrn_gates(zf, loglb, log1m, onem):
    l1pe = jnp.log(1.0 + jnp.exp(-jnp.abs(zf)))
    c = log1m + (jnp.minimum(zf, 0.0) - l1pe)
    logf = jnp.maximum(loglb, c) + jnp.log(1.0 + jnp.exp(-jnp.abs(loglb - c)))
    k = onem * jnp.exp(jnp.minimum(-zf, 0.0) - l1pe)
    return k, logf * LOG2E


def _cumsum_rows(cum_mat, x):
    out = jnp.dot(cum_mat, jnp.concatenate(_split3(x), axis=1), preferred_element_type=F32)
    w = x.shape[1]
    return out[:, :w] + out[:, w:2 * w] + out[:, 2 * w:]


def _hgrn_kernel(*refs, rev, n_chunks, fuse):
    if fuse:
        safe_ref, q_ref, f_ref, v_ref, lbp_ref, s0_ref, ob_ref, og_ref, ng_ref, o_ref, sT_ref, state = refs
    else:
        safe_ref, q_ref, f_ref, v_ref, lbp_ref, s0_ref, o_ref, sT_ref, state = refs
        ob_ref = og_ref = ng_ref = None
    i = pl.program_id(1)
    nt = pl.num_programs(1)

    @pl.when(i == 0)
    def _():
        state[...] = s0_ref[0]

    loglb = lbp_ref[0:1, :]
    log1m = lbp_ref[1:2, :]
    onem = lbp_ref[2:3, :]
    c = HGRN_CHUNK
    nsub = c // HGRN_SUB
    r_i = lax.broadcasted_iota(jnp.int32, (c, c), 0)
    c_i = lax.broadcasted_iota(jnp.int32, (c, c), 1)
    causal = (c_i >= r_i) if rev else (c_i <= r_i)
    cum_mat = causal.astype(jnp.bfloat16)

    def prepare(sl, exact):
        k, logf = _hgrn_gates(f_ref[0, sl, :].astype(F32), loglb, log1m, onem)
        zq = q_ref[0, sl, :].astype(F32)
        q = zq * _sigmoid(zq) * (A_DK ** -0.5)
        cum = _cumsum_rows(cum_mat, logf)
        tot = cum[0:1] if rev else cum[c - 1:c]
        qd = (q * jnp.exp2(cum)).astype(MXU_DTYPE)
        kd = (k * jnp.exp2(tot - cum)).astype(MXU_DTYPE)
        refs_j = []
        for j in range(nsub):
            if rev:
                r = cum[(j + 1) * HGRN_SUB:(j + 1) * HGRN_SUB + 1] if j < nsub - 1 else jnp.zeros_like(tot)
            else:
                r = cum[j * HGRN_SUB - 1:j * HGRN_SUB] if j > 0 else jnp.zeros_like(tot)
            refs_j.append(r)
        ref_rows = jnp.concatenate([jnp.broadcast_to(r, (HGRN_SUB, r.shape[1])) for r in refs_j], axis=0)
        qt = (q * jnp.exp2(cum - ref_rows)).astype(MXU_DTYPE)
        if exact:
            blk512 = lax.broadcasted_iota(jnp.int32, cum.shape, 0) // HGRN_SUB
            kt = [jnp.where((blk512 > j) if rev else (blk512 < j),
                            k * jnp.exp2(jnp.minimum(r - cum, 0.0)), 0.0).astype(MXU_DTYPE)
                  for j, r in enumerate(refs_j)]
            return qd, kd, jnp.exp2(tot), qt, kt, (q, k, cum)
        kt = [(k * jnp.exp2(jnp.minimum(r - cum, HGRN_EXP2_CLAMP))).astype(MXU_DTYPE) for r in refs_j]
        return qd, kd, jnp.exp2(tot), qt, kt, None

    def exact_diagonal(q, k, cum, hs):
        lane16 = lax.broadcasted_iota(jnp.int32, (HGRN_SUB, HGRN_SUB), 1)
        strips = []
        for j in range(nsub):
            rows = slice(j * HGRN_SUB, (j + 1) * HGRN_SUB)
            qb, kb, cb = q[rows, hs], k[rows, hs], cum[rows, hs]
            blk = jnp.zeros((HGRN_SUB, HGRN_SUB), F32)
            for t in range(HGRN_SUB):
                w = qb * kb[t:t + 1] * jnp.exp2(jnp.minimum(cb - cb[t:t + 1], 0.0))
                blk = jnp.where(lane16 == t, jnp.sum(w, axis=-1, keepdims=True), blk)
            pieces = [blk if jj == j else jnp.zeros((HGRN_SUB, HGRN_SUB), F32) for jj in range(nsub)]
            strips.append(jnp.concatenate(pieces, axis=1))
        return jnp.concatenate(strips, axis=0)

    def finish(sl, prepped, s):
        qd, kd, decay, qt, kt, raw = prepped
        v = v_ref[0, sl, :]
        head = [slice(h * A_DK, (h + 1) * A_DK) for h in range(A_HEADS)]
        score_list = [None] * A_HEADS
        zero_q = jnp.zeros((HGRN_SUB, A_DK), MXU_DTYPE)
        for h0 in range(0, A_HEADS, 2):
            h1 = h0 + 1
            pair = slice(h0 * A_DK, (h1 + 1) * A_DK)
            strips = []
            for j in range(nsub):
                rows = slice(j * HGRN_SUB, (j + 1) * HGRN_SUB)
                lhs = jnp.concatenate([jnp.concatenate([qt[rows, head[h0]], zero_q], axis=1),
                                       jnp.concatenate([zero_q, qt[rows, head[h1]]], axis=1)], axis=0)
                strips.append(_nt_dot(lhs, kt[j][:, pair]))
            for h in (h0, h1):
                off = (h - h0) * HGRN_SUB
                scores = jnp.concatenate([st[off:off + HGRN_SUB] for st in strips], axis=0)
                if raw is not None:
                    scores = scores + exact_diagonal(*raw, head[h])
                score_list[h] = jnp.where(causal, scores, 0.0).astype(MXU_DTYPE)
        s_new = []
        for h0 in range(0, A_HEADS, 2):
            pair = slice(h0 * A_DK, (h0 + 2) * A_DK)
            upd = lax.dot_general(v[:, pair], kd[:, pair], (((0,), (0,)), ((), ())), preferred_element_type=F32)
            for h in (h0, h0 + 1):
                off = (h - h0) * A_DK
                s_new.append(s[h] * decay[:, head[h]] + upd[off:off + A_DK, off:off + A_DK])
        outs = []
        zero_s = jnp.zeros((A_DK, A_DK), MXU_DTYPE)
        zero_v = jnp.zeros((c, A_DK), MXU_DTYPE)
        for h0 in range(0, A_HEADS, 2):
            h1 = h0 + 1
            pair = slice(h0 * A_DK, (h1 + 1) * A_DK)
            s_pair = jnp.concatenate(
                [jnp.concatenate([s[h0].astype(MXU_DTYPE), zero_s], axis=1),
                 jnp.concatenate([zero_s, s[h1].astype(MXU_DTYPE)], axis=1)], axis=0)
            v_pair = jnp.concatenate([jnp.concatenate([v[:, head[h0]], zero_v], axis=1),
                                      jnp.concatenate([zero_v, v[:, head[h1]]], axis=1)], axis=0)
            a_pair = jnp.concatenate([score_list[h0], score_list[h1]], axis=1)
            o_pair = _nt_dot(qd[:, pair], s_pair) + jnp.dot(a_pair, v_pair, preferred_element_type=F32)
            for h in (h0, h1):
                off = (h - h0) * A_DK
                o_h = o_pair[:, off:off + A_DK]
                if fuse:
                    o_h = o_h + ob_ref[0, sl, head[h]]
                    o_h = o_h * lax.rsqrt(jnp.mean(o_h * o_h, axis=-1, keepdims=True) + EPS)
                outs.append(o_h)
        s[:] = s_new
        o = jnp.concatenate(outs, axis=1)
        if fuse:
            og = og_ref[0, sl, :].astype(F32)
            o = o * ng_ref[...] * (og * _sigmoid(og))
        o_ref[0, sl, :] = o.astype(o_ref.dtype)

    safe = safe_ref[pl.program_id(0), (nt - 1 - i) if rev else i] != 0

    @pl.when(safe)
    def _():
        s = [state[h] for h in range(A_HEADS)]
        order = list(range(n_chunks - 1, -1, -1) if rev else range(n_chunks))
        slices = [slice(ci * c, (ci + 1) * c) for ci in order]
        ready = prepare(slices[0], False)
        for n, sl in enumerate(slices):
            prepped = ready
            if n + 1 < len(slices):
                ready = prepare(slices[n + 1], False)
            finish(sl, prepped, s)
        for h in range(A_HEADS):
            state[h] = s[h]

    @pl.when(jnp.logical_not(safe))
    def _():
        def chunk(n, carry):
            ci = (n_chunks - 1 - n) if rev else n
            sl = pl.ds(pl.multiple_of(ci * c, c), c)
            s = [state[h] for h in range(A_HEADS)]
            finish(sl, prepare(sl, True), s)
            for h in range(A_HEADS):
                state[h] = s[h]
            return carry

        lax.fori_loop(0, n_chunks, chunk, 0)

    @pl.when(i == pl.num_programs(1) - 1)
    def _():
        sT_ref[0] = state[...]


def hgrn_safe_flags(stats, direction, n, step_tokens):
    worst = stats[:, :, direction, :].min(axis=-1)
    bsz, nb = worst.shape
    per_step = nb * step_tokens // n
    worst = worst.reshape(bsz, nb // per_step, per_step).min(axis=-1)
    return (worst * LOG2E - HGRN_SUB >= -HGRN_EXP2_CLAMP).astype(jnp.int32)


def hgrn_scan(z, stats, lbp, s0, rev, o_other=None, norm_g=None):
    bsz, n, _ = z.shape
    fuse = o_other is not None
    t = min(HGRN_STEP_TOKENS, n)
    nt = n // t
    f_off = (OFF_FB if rev else OFF_FF) // A_WIDTH
    safe = hgrn_safe_flags(stats, 1 if rev else 0, n, t)

    def tmap(i):
        return nt - 1 - i if rev else i

    def col(blk):
        return pl.BlockSpec((1, t, A_WIDTH), lambda b, i, sf: (b, tmap(i), blk))

    state_spec = pl.BlockSpec((1, A_HEADS, LANES, LANES), lambda b, i, sf: (b, 0, 0, 0))
    in_specs = [col(OFF_QA // A_WIDTH), col(f_off), col(OFF_IA // A_WIDTH),
                pl.BlockSpec((8, A_WIDTH), lambda b, i, sf: (0, 0)), state_spec]
    args = [z, z, z, lbp, s0]
    if fuse:
        in_specs += [col(0), col(OFF_OG // A_WIDTH), pl.BlockSpec((1, A_WIDTH), lambda b, i, sf: (0, 0))]
        args += [o_other, z, norm_g.reshape(1, A_WIDTH)]
    kern = functools.partial(_hgrn_kernel, rev=rev, n_chunks=t // HGRN_CHUNK, fuse=fuse)
    grid_spec = pltpu.PrefetchScalarGridSpec(
        num_scalar_prefetch=1,
        grid=(bsz, nt),
        in_specs=in_specs,
        out_specs=[col(0), state_spec],
        scratch_shapes=[pltpu.VMEM((A_HEADS, LANES, LANES), F32)],
    )
    return pl.pallas_call(
        kern,
        grid_spec=grid_spec,
        out_shape=[jax.ShapeDtypeStruct((bsz, n, A_WIDTH), MXU_DTYPE if fuse else F32),
                   jax.ShapeDtypeStruct((bsz, A_HEADS, LANES, LANES), F32)],
        compiler_params=_cparams(("parallel", "arbitrary")),
        name="hgrn_bwd" if rev else "hgrn_fwd",
    )(safe, *args)


def _attn_kernel(q_ref, k_ref, v_ref, o_ref):
    tq = q_ref.shape[2]
    rows = B_GROUP * tq
    lane = lax.broadcasted_iota(jnp.int32, v_ref.shape[1:], 1)
    lane_o = lax.broadcasted_iota(jnp.int32, (tq, LANES), 1)
    s_all = _nt_dot(q_ref[0].reshape(B_HEADS * tq, LANES), k_ref[0])
    for kv in range(B_KV_HEADS):
        s = s_all[kv * rows:(kv + 1) * rows]
        vs = jnp.where((lane // B_HD) == kv, v_ref[0], jnp.ones_like(v_ref[0]))
        p = jnp.exp2(s - jnp.max(s, axis=-1, keepdims=True))
        a = jnp.dot(p.astype(MXU_DTYPE), vs, preferred_element_type=F32)
        o_full = a / pltpu.roll(a, B_HD, 1)
        o_swapped = pltpu.roll(o_full, B_HD, 1)
        for pair in range(B_GROUP // 2):
            g0, g1 = 2 * pair, 2 * pair + 1
            lo_src = o_full if kv == 0 else o_swapped
            hi_src = o_swapped if kv == 0 else o_full
            blk = jnp.where(lane_o < B_HD, lo_src[g0 * tq:(g0 + 1) * tq], hi_src[g1 * tq:(g1 + 1) * tq])
            cb = kv * (B_GROUP // 2) + pair
            o_ref[0, :, cb * LANES:(cb + 1) * LANES] = blk.astype(o_ref.dtype)


def attention(q, k, v):
    bsz, _, n, _ = q.shape
    nkeys = k.shape[1]
    tq = min(ATTN_Q_ROWS, n)
    return pl.pallas_call(
        _attn_kernel,
        grid=(bsz, n // tq),
        in_specs=[pl.BlockSpec((1, B_HEADS, tq, LANES), lambda b, i: (b, 0, i, 0)),
                  pl.BlockSpec((1, nkeys, LANES), lambda b, i: (b, 0, 0)),
                  pl.BlockSpec((1, nkeys, LANES), lambda b, i: (b, 0, 0))],
        out_specs=pl.BlockSpec((1, tq, B_WIDTH), lambda b, i: (b, i, 0)),
        out_shape=jax.ShapeDtypeStruct((bsz, n, B_WIDTH), MXU_DTYPE),
        compiler_params=_cparams(("parallel", "parallel")),
        name="gqa_attention",
    )(q, k, v)


def _pack_rows(x):
    w = x.shape[1] // 2
    bits = lax.bitcast_convert_type(x.astype(jnp.bfloat16).astype(F32), jnp.uint32)
    packed = (bits[:, :w] & jnp.uint32(0xFFFF0000)) | (bits[:, w:] >> 16)
    return lax.bitcast_convert_type(packed, jnp.int32)


def _unpack_rows(p):
    bits = lax.bitcast_convert_type(p, jnp.uint32)
    hi = lax.bitcast_convert_type(bits & jnp.uint32(0xFFFF0000), F32)
    lo = lax.bitcast_convert_type(bits << 16, F32)
    return jnp.concatenate([hi, lo], axis=1)


def _store_chunks(ref, packed):
    for j in range(ref.shape[0]):
        ref[j] = packed[:, j * LANES:(j + 1) * LANES]


def _load_chunks(ref):
    return jnp.concatenate([ref[j] for j in range(ref.shape[0])], axis=1)


def _merge_kernel(*refs, aliased):
    if aliased:
        refs = refs[:13] + refs[15:]
    (h_ref, a_ref, b_ref, zg_ref, wa_ref, wb_ref, wo_ref, g1_ref, n2_ref, sc_ref, sh_ref, wr_ref, br_ref,
     ho_ref, x2_ref, lg_ref) = refs
    d = h_ref.shape[2]
    tm = h_ref.shape[1]
    sub = min(MERGE_SUB_ROWS, tm)
    for r in range(tm // sub):
        rs = slice(r * sub, (r + 1) * sub)
        ua = jnp.dot(a_ref[0, rs, :], wa_ref[...], preferred_element_type=F32)
        ub = jnp.dot(b_ref[0, rs, :], wb_ref[...], preferred_element_type=F32)
        ga = _sigmoid(zg_ref[0, rs, 0:d].astype(F32))
        gb = _sigmoid(zg_ref[0, rs, d:2 * d].astype(F32))
        mix = (ga * ua + gb * ub).astype(MXU_DTYPE)
        h_new = h_ref[0, rs, :] + g1_ref[0] * jnp.dot(mix, wo_ref[...], preferred_element_type=F32)
        ho_ref[0, rs, :] = h_new
        x2 = _modulated_norm(h_new, n2_ref[...], sc_ref[0], sh_ref[0])
        packed = _pack_rows(x2)
        for j in range(x2_ref.shape[0]):
            x2_ref[j, rs, :] = packed[:, j * LANES:(j + 1) * LANES]
        x_hi = x2.astype(jnp.bfloat16)
        x_lo = (x2 - x_hi.astype(F32)).astype(jnp.bfloat16)
        hi_both = jnp.dot(x_hi, wr_ref[...], preferred_element_type=F32)
        lo_hi = jnp.dot(x_lo, wr_ref[:, 0:LANES], preferred_element_type=F32)
        lg_ref[rs, :] = hi_both[:, 0:LANES] + hi_both[:, LANES:] + lo_hi + br_ref[...]


def merge_proj(h, a, b, z, w_up_a, w_up_b, w_out, g1, norm2_g, sc2, sh2, w_router, b_router, t_total, tok_off,
               bufs=None):
    bsz, n, d = h.shape
    tm = min(MERGE_ROWS, n)
    nt = n // tm
    off = tok_off // tm
    n_chunks = d // (2 * LANES)
    row = lambda w: pl.BlockSpec((1, tm, w), lambda b_, i: (b_, i, 0))
    full = lambda s: pl.BlockSpec(s, lambda b_, i: (0,) * len(s))
    per_b = pl.BlockSpec((1, 1, d), lambda b_, i: (b_, 0, 0))
    in_specs = [row(d), row(A_WIDTH), row(B_WIDTH),
                pl.BlockSpec((1, tm, 2 * d), lambda b_, i: (b_, i, OFF_GATES)),
                full((A_WIDTH, d)), full((B_WIDTH, d)), full((d, d)),
                per_b, full((1, d)), per_b, per_b, full((d, 2 * LANES)), full((1, LANES))]
    args = [h, a, b, z, w_up_a, w_up_b, w_out, g1, norm2_g.reshape(1, d), sc2, sh2, w_router, b_router]
    aliases = {}
    if bufs is not None:
        in_specs += [pl.BlockSpec(memory_space=pl.ANY)] * 2
        args += list(bufs)
        aliases = {13: 1, 14: 2}
    return pl.pallas_call(
        functools.partial(_merge_kernel, aliased=bufs is not None),
        grid=(bsz, nt),
        in_specs=in_specs,
        out_specs=[row(d),
                   pl.BlockSpec((n_chunks, tm, LANES), lambda b_, i: (0, off + b_ * nt + i, 0)),
                   pl.BlockSpec((tm, LANES), lambda b_, i: (off + b_ * nt + i, 0))],
        out_shape=[jax.ShapeDtypeStruct((bsz, n, d), F32),
                   jax.ShapeDtypeStruct((n_chunks, t_total, LANES), jnp.int32),
                   jax.ShapeDtypeStruct((t_total, LANES), F32)],
        input_output_aliases=aliases,
        compiler_params=_cparams(("parallel", "parallel")),
        name="merge_proj",
    )(*args)


def _first_lane_where(cond, lane):
    return jnp.min(jnp.where(cond, lane, LANES), axis=-1, keepdims=True)


def _router_kernel(lg_ref, slab_ref, slab_t_ref, cnt_ref, base):
    @pl.when(pl.program_id(0) == 0)
    def _():
        base[...] = jnp.zeros_like(base)

    lg = lg_ref[...]
    tr = lg.shape[0]
    lane = lax.broadcasted_iota(jnp.int32, lg.shape, 1)
    neg = jnp.float32(-jnp.inf)
    is_g = lane < N_GROUPS
    gl = jnp.where(is_g, lg, neg)
    gmax = jnp.max(gl, axis=-1, keepdims=True)
    gsel = _first_lane_where(gl == gmax, lane)
    p_group = 1.0 / jnp.sum(jnp.where(is_g, jnp.exp(lg - gmax), 0.0), axis=-1, keepdims=True)
    lo = N_GROUPS + gsel * EXPERTS_PER_GROUP
    in_grp = (lane >= lo) & (lane < lo + EXPERTS_PER_GROUP)
    el = jnp.where(in_grp, lg, neg)
    m1 = jnp.max(el, axis=-1, keepdims=True)
    i1 = _first_lane_where(el == m1, lane)
    el2 = jnp.where(lane == i1, neg, el)
    m2 = jnp.max(el2, axis=-1, keepdims=True)
    i2 = _first_lane_where(el2 == m2, lane)
    zsum = jnp.sum(jnp.where(in_grp, jnp.exp(lg - m1), 0.0), axis=-1, keepdims=True)
    p1 = 1.0 / zsum
    p2 = jnp.exp(m2 - m1) / zsum
    w1 = p1 / (p1 + p2) * p_group
    w2 = p2 / (p1 + p2) * p_group
    hit1 = lane == i1
    hit2 = lane == i2
    onehot = (hit1 | hit2).astype(F32)
    r_i = lax.broadcasted_iota(jnp.int32, (tr, tr), 0)
    c_i = lax.broadcasted_iota(jnp.int32, (tr, tr), 1)
    before = (c_i < r_i).astype(MXU_DTYPE)
    prior = jnp.dot(before, onehot.astype(MXU_DTYPE), preferred_element_type=F32) + base[0:1, :]
    r1 = jnp.sum(jnp.where(hit1, prior, 0.0), axis=-1, keepdims=True)
    r2 = jnp.sum(jnp.where(hit2, prior, 0.0), axis=-1, keepdims=True)
    new_base = base[0:1, :] + jnp.sum(onehot, axis=0, keepdims=True)
    base[...] = jnp.broadcast_to(new_base, base.shape)
    cnt_ref[...] = jnp.broadcast_to(new_base, cnt_ref.shape)
    vals = [(i1 - N_GROUPS).astype(F32), (i2 - N_GROUPS).astype(F32), w1, w2, r1, r2]
    slab = jnp.zeros_like(lg)
    for idx, val in enumerate(vals):
        slab = jnp.where(lane == idx, val, slab)
    slab_ref[...] = slab
    slab_t_ref[...] = slab.T[0:8, :]


def route(logits):
    t = logits.shape[0]
    tr = next(c for c in ROUTER_ROWS if t % c == 0)
    return pl.pallas_call(
        _router_kernel,
        grid=(t // tr,),
        in_specs=[pl.BlockSpec((tr, LANES), lambda i: (i, 0))],
        out_specs=[pl.BlockSpec((tr, LANES), lambda i: (i, 0)), pl.BlockSpec((8, tr), lambda i: (0, i)),
                   pl.BlockSpec((8, LANES), lambda i: (0, 0))],
        out_shape=[jax.ShapeDtypeStruct((t, LANES), F32), jax.ShapeDtypeStruct((8, t), F32),
                   jax.ShapeDtypeStruct((8, LANES), F32)],
        scratch_shapes=[pltpu.VMEM((8, LANES), F32)],
        compiler_params=_cparams(("arbitrary",)),
        name="moe_route",
    )(logits)


def _expert_kernel(be_ref, nv_ref, x_ref, wg_ref, wu_ref, wd_ref, o_ref, wg_s, wu_s, wd_s):
    j = pl.program_id(0)
    nv = nv_ref[j]

    @pl.when((j == 0) | (be_ref[j] != be_ref[jnp.maximum(j - 1, 0)]))
    def _():
        wg_s[...] = wg_ref[0, 0].astype(wg_s.dtype)
        wu_s[...] = wu_ref[0, 0].astype(wu_s.dtype)
        wd_s[...] = wd_ref[0, 0].astype(wd_s.dtype)

    @pl.when(nv > 0)
    def _():
        x = _unpack_rows(_load_chunks(x_ref))
        rows = lax.broadcasted_iota(jnp.int32, x.shape, 0)
        x = jnp.where(rows < nv, x, 0.0).astype(MXU_DTYPE)
        hg = jnp.dot(x, wg_s[...], preferred_element_type=F32)
        hu = jnp.dot(x, wu_s[...], preferred_element_type=F32)
        hb = (hg * _sigmoid(hg) * hu).astype(MXU_DTYPE)
        _store_chunks(o_ref, _pack_rows(jnp.dot(hb, wd_s[...], preferred_element_type=F32)))

    @pl.when(nv == 0)
    def _():
        o_ref[...] = jnp.zeros_like(o_ref)


def expert_ffn(x_sorted, block_e, nvalid, w_gate, w_up, w_down, layer, tmx):
    n_chunks, r, _ = x_sorted.shape
    _, _, d, de = w_gate.shape
    grid_spec = pltpu.PrefetchScalarGridSpec(
        num_scalar_prefetch=2,
        grid=(r // tmx,),
        in_specs=[pl.BlockSpec((n_chunks, tmx, LANES), lambda j, be, nv: (0, j, 0)),
                  pl.BlockSpec((1, 1, d, de), lambda j, be, nv: (layer, be[j], 0, 0)),
                  pl.BlockSpec((1, 1, d, de), lambda j, be, nv: (layer, be[j], 0, 0)),
                  pl.BlockSpec((1, 1, de, d), lambda j, be, nv: (layer, be[j], 0, 0))],
        out_specs=pl.BlockSpec((n_chunks, tmx, LANES), lambda j, be, nv: (0, j, 0)),
        scratch_shapes=[pltpu.VMEM((d, de), MXU_DTYPE), pltpu.VMEM((d, de), MXU_DTYPE),
                        pltpu.VMEM((de, d), MXU_DTYPE)],
    )
    return pl.pallas_call(
        _expert_kernel,
        grid_spec=grid_spec,
        out_shape=jax.ShapeDtypeStruct((n_chunks, r, LANES), jnp.int32),
        compiler_params=_cparams(("arbitrary",)),
        name="expert_ffn",
    )(block_e, nvalid, x_sorted, w_gate, w_up, w_down)


def _combine_final_kernel(h_ref, y1_ref, y2_ref, slab_ref, g2_ref, fn_ref, o_ref):
    y = (_unpack_rows(_load_chunks(y1_ref)) * slab_ref[:, 2:3]
         + _unpack_rows(_load_chunks(y2_ref)) * slab_ref[:, 3:4])
    h_new = h_ref[0] + g2_ref[0] * y
    ms = jnp.mean(h_new * h_new, axis=-1, keepdims=True)
    o_ref[0] = h_new * lax.rsqrt(ms + EPS) * fn_ref[...]


def moe_combine_final(h, y1, y2, slab, g2, final_g):
    bsz, n, d = h.shape
    tm = min(COMBINE_ROWS, n)
    nt = n // tm
    yspec = pl.BlockSpec((y1.shape[0], tm, LANES), lambda b, i: (0, b * nt + i, 0))
    return pl.pallas_call(
        _combine_final_kernel,
        grid=(bsz, nt),
        in_specs=[pl.BlockSpec((1, tm, d), lambda b, i: (b, i, 0)), yspec, yspec,
                  pl.BlockSpec((tm, LANES), lambda b, i: (b * nt + i, 0)),
                  pl.BlockSpec((1, 1, d), lambda b, i: (b, 0, 0)),
                  pl.BlockSpec((1, d), lambda b, i: (0, 0))],
        out_specs=pl.BlockSpec((1, tm, d), lambda b, i: (b, i, 0)),
        out_shape=jax.ShapeDtypeStruct((bsz, n, d), F32),
        compiler_params=_cparams(("parallel", "parallel")),
        name="moe_combine_final",
    )(h, y1, y2, slab, g2, final_g.reshape(1, d))


SC_WINDOW = 128


def _sc_mesh():
    return plsc.VectorSubcoreMesh(core_axis_name="core", subcore_axis_name="subcore")


def sc_gather_rows(table, idx):
    n = idx.shape[0]

    @pl.kernel(out_type=jax.ShapeDtypeStruct((n, LANES), table.dtype), mesh=_sc_mesh())
    def gather_kernel(t_hbm, i_hbm, o_hbm):
        def body(i_vmem, o_vmem):
            pltpu.sync_copy(t_hbm.at[i_vmem.at[0]], o_vmem)

        pltpu.emit_pipeline(
            body,
            grid=(n // SC_WINDOW,),
            in_specs=[pl.BlockSpec((1, SC_WINDOW), lambda i: (0, i))],
            out_specs=[pl.BlockSpec((SC_WINDOW, LANES), lambda i: (i, 0))],
            core_axis_name=("core", "subcore"),
            dimension_semantics=(pltpu.PARALLEL,),
            trace_scopes=False,
        )(i_hbm, o_hbm)

    return gather_kernel(table, idx.reshape(1, n))


def sc_scatter_rows2(rows, idx_a, idx_b, n_out):
    n = rows.shape[0]

    @pl.kernel(out_type=jax.ShapeDtypeStruct((n_out, LANES), rows.dtype), mesh=_sc_mesh())
    def scatter_kernel(r_hbm, ia_hbm, ib_hbm, o_hbm):
        def body(r_vmem, ia_vmem, ib_vmem):
            pltpu.sync_copy(r_vmem, o_hbm.at[ia_vmem.at[0]])
            pltpu.sync_copy(r_vmem, o_hbm.at[ib_vmem.at[0]])

        pltpu.emit_pipeline(
            body,
            grid=(n // SC_WINDOW,),
            in_specs=[pl.BlockSpec((SC_WINDOW, LANES), lambda i: (i, 0)),
                      pl.BlockSpec((1, SC_WINDOW), lambda i: (0, i)),
                      pl.BlockSpec((1, SC_WINDOW), lambda i: (0, i))],
            out_specs=[],
            core_axis_name=("core", "subcore"),
            dimension_semantics=(pltpu.PARALLEL,),
            trace_scopes=False,
        )(r_hbm, ia_hbm, ib_hbm)

    return scatter_kernel(rows, idx_a.reshape(1, n), idx_b.reshape(1, n))


def _plan_kernel(ps_ref, st_ref, ia_ref, ib_ref, *, n_rows):
    e = st_ref[0:2, :].astype(jnp.int32)
    dest = st_ref[4:6, :].astype(jnp.int32)
    for ex in range(N_EXPERTS):
        dest = dest + jnp.where(e == ex, ps_ref[ex], 0)
    for k, ref in enumerate((ia_ref, ib_ref)):
        for j in range(ref.shape[0]):
            ref[j:j + 1, :] = dest[k:k + 1, :] + j * n_rows


def dispatch_indices(slab_t, pad_start, n_chunks, n_rows):
    t = slab_t.shape[1]
    tp = next(c for c in PLAN_TOKENS if t % c == 0)
    grid_spec = pltpu.PrefetchScalarGridSpec(
        num_scalar_prefetch=1,
        grid=(t // tp,),
        in_specs=[pl.BlockSpec((8, tp), lambda i, ps: (0, i))],
        out_specs=[pl.BlockSpec((n_chunks, tp), lambda i, ps: (0, i))] * 2,
    )
    return pl.pallas_call(
        functools.partial(_plan_kernel, n_rows=n_rows),
        grid_spec=grid_spec,
        out_shape=[jax.ShapeDtypeStruct((n_chunks, t), jnp.int32)] * 2,
        compiler_params=_cparams(("parallel",)),
        name="moe_plan",
    )(pad_start, slab_t)


def hier_moe(x2, logits, w_gate, w_up, w_down, layer):
    n_chunks, t, _ = x2.shape
    tmx = min(EXPERT_ROWS, t)
    n_blocks = (t * TOP_K + N_EXPERTS * (tmx - 1)) // tmx + 1
    r = n_blocks * tmx
    slab, slab_t, counts = route(logits)
    cnt = counts[0, N_GROUPS:N_GROUPS + N_EXPERTS].astype(jnp.int32)
    padded = (cnt + tmx - 1) // tmx * tmx
    pad_end = jnp.cumsum(padded)
    pad_start = pad_end - padded
    blk_start = jnp.arange(n_blocks, dtype=jnp.int32) * tmx
    block_e = jnp.minimum(jnp.sum(pad_end[None, :] <= blk_start[:, None], axis=1), N_EXPERTS - 1).astype(jnp.int32)
    nvalid = jnp.clip(pad_start[block_e] + cnt[block_e] - blk_start, 0, tmx).astype(jnp.int32)
    idx1, idx2 = dispatch_indices(slab_t, pad_start.astype(jnp.int32), n_chunks, r)
    idx1, idx2 = idx1.reshape(-1), idx2.reshape(-1)
    x_sorted = sc_scatter_rows2(x2.reshape(n_chunks * t, LANES), idx1, idx2, n_chunks * r)
    y_sorted = expert_ffn(x_sorted.reshape(n_chunks, r, LANES), block_e, nvalid, w_gate, w_up, w_down, layer, tmx)
    y_flat = y_sorted.reshape(n_chunks * r, LANES)
    y1 = sc_gather_rows(y_flat, idx1).reshape(n_chunks, t, LANES)
    y2 = sc_gather_rows(y_flat, idx2).reshape(n_chunks, t, LANES)
    return y1, y2, slab


def _lower_bound_rows(lb_logits):
    lb = jnp.cumsum(jax.nn.softmax(lb_logits.astype(F32), axis=1), axis=1)
    lb = lb - lb[:, :1]
    rows = jnp.stack([jnp.log(lb), jnp.log1p(-lb), 1.0 - lb], axis=2)
    rows = jnp.concatenate([rows, jnp.zeros(rows.shape[:2] + (5, A_WIDTH), F32)], axis=2)
    return rows


def _cast_kernel(w_ref, o_ref):
    o_ref[...] = w_ref[...].astype(o_ref.dtype)


def reorder_cast_w_in(w_in):
    depth, d, n = w_in.shape
    gate_cols = 2 * d
    blk = REORDER_COLS
    nb = n // blk
    shift = (n - gate_cols) // blk
    return pl.pallas_call(
        _cast_kernel,
        grid=(depth, nb),
        in_specs=[pl.BlockSpec((1, d, blk), lambda l, j: (l, 0, (j + shift) % nb))],
        out_specs=pl.BlockSpec((1, d, blk), lambda l, j: (l, 0, j)),
        out_shape=jax.ShapeDtypeStruct((depth, d, n), MXU_DTYPE),
        compiler_params=_cparams(("parallel", "parallel")),
        name="reorder_cast_w_in",
    )(w_in)


def kernel(x, c, ctx, c_ctx, w_ada, b_ada, norm1_g, norm2_g, w_in, lb_logits, hgrn_norm_g, q_norm_g, k_norm_g,
           w_up_a, w_up_b, w_out, w_router_group, b_router_group, w_router_expert, b_router_expert, w_gate, w_up,
           w_down, final_norm_g):
    bsz, n, d = x.shape
    n_ctx = ctx.shape[1]
    depth = w_in.shape[0]
    mx = MXU_DTYPE

    w_in_r = reorder_cast_w_in(w_in)
    w_up_a_c, w_up_b_c, w_out_c = w_up_a.astype(mx), w_up_b.astype(mx), w_out.astype(mx)
    n_r = N_GROUPS + N_EXPERTS
    w_router = jnp.concatenate([w_router_group, w_router_expert, jnp.zeros((depth, d, LANES - n_r), F32)], axis=-1)
    w_router_hi = w_router.astype(jnp.bfloat16)
    w_router_lo = (w_router - w_router_hi.astype(F32)).astype(jnp.bfloat16)
    w_router = jnp.concatenate([w_router_hi, w_router_lo], axis=-1)
    b_router = jnp.concatenate([b_router_group, b_router_expert, jnp.zeros((depth, LANES - n_r), F32)], axis=-1)
    lbp = _lower_bound_rows(lb_logits)
    cos_t, sin_t = rope_tables(n)

    n_rows = -(-(bsz + 1) // 8) * 8
    act = jnp.concatenate([c, c_ctx[None, :], jnp.zeros((n_rows - bsz - 1, d), F32)], axis=0)
    mod = ada_mod(act, w_ada, b_ada)

    h, hc = x, ctx
    pending = None
    zero_state = jnp.zeros((bsz, A_HEADS, LANES, LANES), F32)
    for l in range(depth):
        need_ctx = l < depth - 1
        ml = mod[l, :bsz].reshape(bsz, 1, N_MOD, d)
        mc = jnp.broadcast_to(mod[l, bsz].reshape(1, 1, N_MOD, d), (bsz, 1, N_MOD, d))
        sh1, sc1, g1, sh2, sc2, g2 = (ml[:, :, i] for i in range(N_MOD))
        csh1, csc1, cg1, csh2, csc2, cg2 = (mc[:, :, i] for i in range(N_MOD))

        moe_l = moe_c = None
        if pending is not None:
            y1, y2, slab, g2_prev, cg2_prev = pending
            moe_l = (y1, y2, slab, g2_prev, 0)
            moe_c = (y1, y2, slab, cg2_prev, bsz * n)
        zl, q_l, k_l, v_l, st_l, *h_new = in_proj(h, norm1_g[l], sc1, sh1, w_in_r, l, q_norm_g[l], k_norm_g[l],
                                                  cos_t, sin_t, moe_l)
        zc, q_c, k_c, v_c, st_c, *hc_new = in_proj(hc, norm1_g[l], csc1, csh1, w_in_r, l, q_norm_g[l], k_norm_g[l],
                                                   None, None, moe_c)
        if pending is not None:
            h, hc = h_new[0], hc_new[0]
        ob_c, s_bwd = hgrn_scan(zc, st_c, lbp[1, l], zero_state, True)
        if need_ctx:
            a_c, s_fwd = hgrn_scan(zc, st_c, lbp[0, l], zero_state, False, ob_c, hgrn_norm_g[l])
        else:
            _, s_fwd = hgrn_scan(zc, st_c, lbp[0, l], zero_state, False)
        ob_l, _ = hgrn_scan(zl, st_l, lbp[1, l], s_bwd, True)
        a_l, _ = hgrn_scan(zl, st_l, lbp[0, l], s_fwd, False, ob_l, hgrn_norm_g[l])

        b_l = attention(q_l, jnp.concatenate([k_c, k_l], axis=1), jnp.concatenate([v_c, v_l], axis=1))

        t_total = bsz * (n + n_ctx) if need_ctx else bsz * n
        bufs = (jnp.zeros((d // (2 * LANES), t_total, LANES), jnp.int32),
                jnp.zeros((t_total, LANES), F32)) if need_ctx else None
        h, x2, lg = merge_proj(h, a_l, b_l, zl, w_up_a_c[l], w_up_b_c[l], w_out_c[l], g1, norm2_g[l], sc2, sh2,
                               w_router[l], b_router[l][None, :], t_total, 0, bufs)
        if need_ctx:
            b_c = attention(q_c, k_c, v_c)
            hc, x2, lg = merge_proj(hc, a_c, b_c, zc, w_up_a_c[l], w_up_b_c[l], w_out_c[l], cg1, norm2_g[l],
                                    csc2, csh2, w_router[l], b_router[l][None, :], t_total, bsz * n, (x2, lg))

        y1, y2, slab = hier_moe(x2, lg, w_gate, w_up, w_down, l)
        pending = (y1, y2, slab, g2, cg2)
    y1, y2, slab, g2, _ = pending
    return moe_combine_final(h, y1, y2, slab, g2, final_norm_g)
```

```python
import functools

import jax
import jax.numpy as jnp
from jax import lax
from jax.experimental import pallas as pl
from jax.experimental.pallas import tpu as pltpu
from jax.experimental.pallas import tpu_sc as plsc

F32 = jnp.float32
MXU_DTYPE = jnp.bfloat16

EPS = 1e-6
GRID_W = 64
ROPE_THETA = 10000.0
LANES = 128

A_HEADS = 4
A_DK = 128
A_WIDTH = A_HEADS * A_DK
HGRN_CHUNK = 64
HGRN_SUB = 16
HGRN_STEP_TOKENS = 1024
HGRN_EXP2_CLAMP = 86.0
B_HEADS = 8
B_KV_HEADS = 2
B_HD = 64
B_GROUP = B_HEADS // B_KV_HEADS
B_WIDTH = B_HEADS * B_HD
B_KV_WIDTH = B_KV_HEADS * B_HD
AXIS_DIM = B_HD // 2
N_GROUPS = 4
EXPERTS_PER_GROUP = 8
N_EXPERTS = N_GROUPS * EXPERTS_PER_GROUP
TOP_K = 2
N_MOD = 6

OFF_GATES = 0
OFF_QA = 2048
OFF_FF = OFF_QA + A_WIDTH
OFF_FB = OFF_FF + A_WIDTH
OFF_IA = OFF_FB + A_WIDTH
OFF_OG = OFF_IA + A_WIDTH
OFF_QB = OFF_OG + A_WIDTH
OFF_KB = OFF_QB + B_WIDTH
OFF_VB = OFF_KB + B_KV_WIDTH
D_IN = OFF_VB + B_KV_WIDTH

VMEM_LIMIT = 52 * 1024 * 1024
ADA_COLS = 1024
REORDER_COLS = 256
INPROJ_ROWS = 512
INPROJ_SUB_ROWS = 256
INPROJ_COL_CHUNK = 768
ATTN_Q_ROWS = 128
MERGE_ROWS = 1024
MERGE_SUB_ROWS = 1024
ROUTER_ROWS = (1024, 512, 256, 128)
PLAN_TOKENS = (4096, 2048, 1024, 512, 256, 128)
EXPERT_ROWS = 1024
COMBINE_ROWS = 512


def _cparams(sem):
    return pltpu.CompilerParams(dimension_semantics=sem, vmem_limit_bytes=VMEM_LIMIT)


LOG2E = 1.4426950408889634


def _sigmoid(x):
    return 0.5 * jnp.tanh(0.5 * x) + 0.5


def _nt_dot(a, b):
    return lax.dot_general(a, b, (((1,), (1,)), ((), ())), preferred_element_type=F32)


def _ada_kernel(a_ref, w_ref, b_ref, o_ref):
    a = a_ref[...]
    a = a * _sigmoid(a)
    o_ref[0] = jnp.dot(a.astype(MXU_DTYPE), w_ref[0].astype(MXU_DTYPE), preferred_element_type=F32) + b_ref[0]


def ada_mod(act, w_ada, b_ada):
    depth, d, n = w_ada.shape
    r = act.shape[0]
    tn = ADA_COLS
    return pl.pallas_call(
        _ada_kernel,
        grid=(depth, n // tn),
        in_specs=[
            pl.BlockSpec((r, d), lambda l, j: (0, 0)),
            pl.BlockSpec((1, d, tn), lambda l, j: (l, 0, j)),
            pl.BlockSpec((1, 1, tn), lambda l, j: (l, 0, j)),
        ],
        out_specs=pl.BlockSpec((1, r, tn), lambda l, j: (l, 0, j)),
        out_shape=jax.ShapeDtypeStruct((depth, r, n), F32),
        compiler_params=_cparams(("parallel", "parallel")),
        name="ada_mod",
    )(act, w_ada, b_ada.reshape(depth, 1, n))


def _modulated_norm(x, g, sc, sh):
    ms = jnp.mean(x * x, axis=-1, keepdims=True)
    return (x * lax.rsqrt(ms + EPS) * g) * (1.0 + sc) + sh


def _split3(x):
    hi = x.astype(jnp.bfloat16)
    r1 = x - hi.astype(F32)
    mid = r1.astype(jnp.bfloat16)
    lo = (r1 - mid.astype(F32)).astype(jnp.bfloat16)
    return hi, mid, lo


def _swap16(x):
    lane = lax.broadcasted_iota(jnp.int32, x.shape, 1)
    return jnp.where(lane % 32 < 16, pltpu.roll(x, LANES - 16, 1), pltpu.roll(x, 16, 1))


def _head_norm_rope(x, gsum, g, cos, sin):
    ssum = sum(jnp.dot(t, gsum, preferred_element_type=F32) for t in _split3(x * x))
    y = x * lax.rsqrt(ssum * (1.0 / B_HD) + EPS) * g
    if cos is not None:
        y = y * cos + _swap16(y) * sin
    return y


def _inproj_kernel(*refs, col_chunk, rope, combine):
    refs = list(refs)
    h_ref, g_ref, sc_ref, sh_ref, w_ref, qg_ref, kg_ref = refs[:7]
    del refs[:7]
    cos = sin = None
    if rope:
        cos, sin = refs[0][...], refs[1][...]
        del refs[:2]
    if combine:
        y1_ref, y2_ref, slab_ref, g2_ref = refs[:4]
        del refs[:4]
        ho_ref = refs.pop()
    z_ref, qo_ref, ko_ref, vo_ref, st_ref = refs
    r_i = lax.broadcasted_iota(jnp.int32, (LANES, LANES), 0)
    c_i = lax.broadcasted_iota(jnp.int32, (LANES, LANES), 1)
    gsum = (r_i // B_HD == c_i // B_HD).astype(jnp.bfloat16)
    tm = h_ref.shape[1]
    sub = min(INPROJ_SUB_ROWS, tm)
    lane = lax.broadcasted_iota(jnp.int32, (sub, LANES), 1)
    stats = [None, None]
    for r in range(tm // sub):
        rs = slice(r * sub, (r + 1) * sub)
        h = h_ref[0, rs, :]
        if combine:
            y12 = (_unpack_rows(jnp.concatenate([y1_ref[j, rs, :] for j in range(y1_ref.shape[0])], axis=1))
                   * slab_ref[rs, 2:3]
                   + _unpack_rows(jnp.concatenate([y2_ref[j, rs, :] for j in range(y2_ref.shape[0])], axis=1))
                   * slab_ref[rs, 3:4])
            h = h + g2_ref[0] * y12
            ho_ref[0, rs, :] = h
        y = _modulated_norm(h, g_ref[...], sc_ref[0], sh_ref[0]).astype(MXU_DTYPE)
        gqa = jnp.dot(y, w_ref[0, :, OFF_QB:], preferred_element_type=F32)
        cs_, sn_ = (cos[rs], sin[rs]) if rope else (None, None)
        for cb in range(B_WIDTH // LANES):
            x = gqa[:, cb * LANES:(cb + 1) * LANES]
            q = _head_norm_rope(x, gsum, qg_ref[...], cs_, sn_) * (B_HD ** -0.5 * LOG2E)
            swapped = pltpu.roll(q, B_HD, 1)
            for half in range(2):
                head = 2 * cb + half
                kv = head // B_GROUP
                src = q if half == kv else swapped
                keep = (lane // B_HD) == kv
                qo_ref[0, head, rs, :] = jnp.where(keep, src, 0.0).astype(qo_ref.dtype)
        kn = _head_norm_rope(gqa[:, B_WIDTH:B_WIDTH + B_KV_WIDTH], gsum, kg_ref[...], cs_, sn_)
        ko_ref[0, rs, :] = kn.astype(ko_ref.dtype)
        vo_ref[0, rs, :] = gqa[:, B_WIDTH + B_KV_WIDTH:].astype(vo_ref.dtype)
        for c in range(z_ref.shape[2] // col_chunk):
            cs = slice(c * col_chunk, (c + 1) * col_chunk)
            zc = jnp.dot(y, w_ref[0, :, cs], preferred_element_type=F32).astype(z_ref.dtype)
            z_ref[0, rs, cs] = zc
            for d, off in enumerate((OFF_FF, OFF_FB)):
                lo, hi = max(off, cs.start), min(off + A_WIDTH, cs.stop)
                if lo < hi:
                    zneg = jnp.minimum(zc[:, lo - cs.start:hi - cs.start].astype(F32), 0.0)
                    blocks = zneg.reshape(sub // HGRN_SUB, HGRN_SUB, hi - lo).sum(axis=1)
                    sums = jnp.min(blocks, axis=0, keepdims=True)
                    for lb in range(sums.shape[1] // LANES):
                        piece = sums[:, lb * LANES:(lb + 1) * LANES]
                        stats[d] = piece if stats[d] is None else jnp.minimum(stats[d], piece)
    st_ref[0, 0] = jnp.concatenate([stats[0], stats[1], jnp.zeros((6, LANES), F32)], axis=0)


def in_proj(h, norm_g, sc, sh, w_in, layer, q_norm_g, k_norm_g, cos_t=None, sin_t=None, moe=None):
    bsz, n, d = h.shape
    tm = min(INPROJ_ROWS, n)
    nt = n // tm
    rope = cos_t is not None
    qg = jnp.tile(q_norm_g.astype(F32), LANES // B_HD).reshape(1, LANES)
    kg = jnp.tile(k_norm_g.astype(F32), LANES // B_HD).reshape(1, LANES)
    vec = pl.BlockSpec((1, LANES), lambda b, i: (0, 0))
    hrow = pl.BlockSpec((1, tm, d), lambda b, i: (b, i, 0))
    per_b = pl.BlockSpec((1, 1, d), lambda b, i: (b, 0, 0))
    in_specs = [hrow, pl.BlockSpec((1, d), lambda b, i: (0, 0)), per_b, per_b,
                pl.BlockSpec((1, d, D_IN), lambda b, i: (layer, 0, 0)), vec, vec]
    args = [h, norm_g.reshape(1, d), sc, sh, w_in, qg, kg]
    if rope:
        in_specs += [pl.BlockSpec((tm, LANES), lambda b, i: (i, 0))] * 2
        args += [cos_t, sin_t]
    row = pl.BlockSpec((1, tm, LANES), lambda b, i: (b, i, 0))
    out_specs = [pl.BlockSpec((1, tm, OFF_QB), lambda b, i: (b, i, 0)),
                 pl.BlockSpec((1, B_HEADS, tm, LANES), lambda b, i: (b, 0, i, 0)), row, row,
                 pl.BlockSpec((1, 1, 8, LANES), lambda b, i: (b, i, 0, 0))]
    out_shape = [jax.ShapeDtypeStruct((bsz, n, OFF_QB), MXU_DTYPE),
                 jax.ShapeDtypeStruct((bsz, B_HEADS, n, LANES), MXU_DTYPE),
                 jax.ShapeDtypeStruct((bsz, n, LANES), MXU_DTYPE),
                 jax.ShapeDtypeStruct((bsz, n, LANES), MXU_DTYPE),
                 jax.ShapeDtypeStruct((bsz, nt, 8, LANES), F32)]
    if moe is not None:
        y1, y2, slab, g2, tok_off = moe
        off = tok_off // tm
        yspec = pl.BlockSpec((y1.shape[0], tm, LANES), lambda b, i: (0, off + b * nt + i, 0))
        in_specs += [yspec, yspec, pl.BlockSpec((tm, LANES), lambda b, i: (off + b * nt + i, 0)), per_b]
        args += [y1, y2, slab, g2]
        out_specs.append(hrow)
        out_shape.append(jax.ShapeDtypeStruct((bsz, n, d), F32))
    return pl.pallas_call(
        functools.partial(_inproj_kernel, col_chunk=INPROJ_COL_CHUNK, rope=rope, combine=moe is not None),
        grid=(bsz, nt),
        in_specs=in_specs,
        out_specs=out_specs,
        out_shape=out_shape,
        compiler_params=_cparams(("parallel", "parallel")),
        name="in_proj_rope" if rope else "in_proj",
    )(*args)


def rope_tables(n):
    t = jnp.arange(n)
    r = (t // GRID_W).astype(F32)
    col = (t % GRID_W).astype(F32)
    inv = ROPE_THETA ** (-jnp.arange(0, AXIS_DIM, 2, dtype=F32) / AXIS_DIM)
    ang_r = r[:, None] * inv
    ang_c = col[:, None] * inv
    cos = jnp.concatenate([jnp.cos(ang_r)] * 2 + [jnp.cos(ang_c)] * 2, axis=1)
    sin = jnp.concatenate([-jnp.sin(ang_r), jnp.sin(ang_r), -jnp.sin(ang_c), jnp.sin(ang_c)], axis=1)
    return jnp.tile(cos, (1, 2)), jnp.tile(sin, (1, 2))


def _hgrn_gates(zf, loglb, log1m, onem):
    l1pe = jnp.log(1.0 + jnp.exp(-jnp.abs(zf)))
    c = log1m + (jnp.minimum(zf, 0.0) - l1pe)
    logf = jnp.maximum(loglb, c) + jnp.log(1.0 + jnp.exp(-jnp.abs(loglb - c)))
    k = onem * jnp.exp(jnp.minimum(-zf, 0.0) - l1pe)
    return k, logf * LOG2E


def _cumsum_rows(cum_mat, x):
    out = jnp.dot(cum_mat, jnp.concatenate(_split3(x), axis=1), preferred_element_type=F32)
    w = x.shape[1]
    return out[:, :w] + out[:, w:2 * w] + out[:, 2 * w:]


def _hgrn_kernel(*refs, rev, n_chunks, fuse):
    if fuse:
        safe_ref, q_ref, f_ref, v_ref, lbp_ref, s0_ref, ob_ref, og_ref, ng_ref, o_ref, sT_ref, state = refs
    else:
        safe_ref, q_ref, f_ref, v_ref, lbp_ref, s0_ref, o_ref, sT_ref, state = refs
        ob_ref = og_ref = ng_ref = None
    i = pl.program_id(1)
    nt = pl.num_programs(1)

    @pl.when(i == 0)
    def _():
        state[...] = s0_ref[0]

    loglb = lbp_ref[0:1, :]
    log1m = lbp_ref[1:2, :]
    onem = lbp_ref[2:3, :]
    c = HGRN_CHUNK
    nsub = c // HGRN_SUB
    r_i = lax.broadcasted_iota(jnp.int32, (c, c), 0)
    c_i = lax.broadcasted_iota(jnp.int32, (c, c), 1)
    causal = (c_i >= r_i) if rev else (c_i <= r_i)
    cum_mat = causal.astype(jnp.bfloat16)

    def prepare(sl, exact):
        k, logf = _hgrn_gates(f_ref[0, sl, :].astype(F32), loglb, log1m, onem)
        zq = q_ref[0, sl, :].astype(F32)
        q = zq * _sigmoid(zq) * (A_DK ** -0.5)
        cum = _cumsum_rows(cum_mat, logf)
        tot = cum[0:1] if rev else cum[c - 1:c]
        qd = (q * jnp.exp2(cum)).astype(MXU_DTYPE)
        kd = (k * jnp.exp2(tot - cum)).astype(MXU_DTYPE)
        refs_j = []
        for j in range(nsub):
            if rev:
                r = cum[(j + 1) * HGRN_SUB:(j + 1) * HGRN_SUB + 1] if j < nsub - 1 else jnp.zeros_like(tot)
            else:
                r = cum[j * HGRN_SUB - 1:j * HGRN_SUB] if j > 0 else jnp.zeros_like(tot)
            refs_j.append(r)
        ref_rows = jnp.concatenate([jnp.broadcast_to(r, (HGRN_SUB, r.shape[1])) for r in refs_j], axis=0)
        qt = (q * jnp.exp2(cum - ref_rows)).astype(MXU_DTYPE)
        if exact:
            blk512 = lax.broadcasted_iota(jnp.int32, cum.shape, 0) // HGRN_SUB
            kt = [jnp.where((blk512 > j) if rev else (blk512 < j),
                            k * jnp.exp2(jnp.minimum(r - cum, 0.0)), 0.0).astype(MXU_DTYPE)
                  for j, r in enumerate(refs_j)]
            return qd, kd, jnp.exp2(tot), qt, kt, (q, k, cum)
        kt = [(k * jnp.exp2(jnp.minimum(r - cum, HGRN_EXP2_CLAMP))).astype(MXU_DTYPE) for r in refs_j]
        return qd, kd, jnp.exp2(tot), qt, kt, None

    def exact_diagonal(q, k, cum, hs):
        lane16 = lax.broadcasted_iota(jnp.int32, (HGRN_SUB, HGRN_SUB), 1)
        strips = []
        for j in range(nsub):
            rows = slice(j * HGRN_SUB, (j + 1) * HGRN_SUB)
            qb, kb, cb = q[rows, hs], k[rows, hs], cum[rows, hs]
            blk = jnp.zeros((HGRN_SUB, HGRN_SUB), F32)
            for t in range(HGRN_SUB):
                w = qb * kb[t:t + 1] * jnp.exp2(jnp.minimum(cb - cb[t:t + 1], 0.0))
                blk = jnp.where(lane16 == t, jnp.sum(w, axis=-1, keepdims=True), blk)
            pieces = [blk if jj == j else jnp.zeros((HGRN_SUB, HGRN_SUB), F32) for jj in range(nsub)]
            strips.append(jnp.concatenate(pieces, axis=1))
        return jnp.concatenate(strips, axis=0)

    def finish(sl, prepped, s):
        qd, kd, decay, qt, kt, raw = prepped
        v = v_ref[0, sl, :]
        head = [slice(h * A_DK, (h + 1) * A_DK) for h in range(A_HEADS)]
        score_list = [None] * A_HEADS
        zero_q = jnp.zeros((HGRN_SUB, A_DK), MXU_DTYPE)
        for h0 in range(0, A_HEADS, 2):
            h1 = h0 + 1
            pair = slice(h0 * A_DK, (h1 + 1) * A_DK)
            strips = []
            for j in range(nsub):
                rows = slice(j * HGRN_SUB, (j + 1) * HGRN_SUB)
                lhs = jnp.concatenate([jnp.concatenate([qt[rows, head[h0]], zero_q], axis=1),
                                       jnp.concatenate([zero_q, qt[rows, head[h1]]], axis=1)], axis=0)
                strips.append(_nt_dot(lhs, kt[j][:, pair]))
            for h in (h0, h1):
                off = (h - h0) * HGRN_SUB
                scores = jnp.concatenate([st[off:off + HGRN_SUB] for st in strips], axis=0)
                if raw is not None:
                    scores = scores + exact_diagonal(*raw, head[h])
                score_list[h] = jnp.where(causal, scores, 0.0).astype(MXU_DTYPE)
        s_new = []
        for h0 in range(0, A_HEADS, 2):
            pair = slice(h0 * A_DK, (h0 + 2) * A_DK)
            upd = lax.dot_general(v[:, pair], kd[:, pair], (((0,), (0,)), ((), ())), preferred_element_type=F32)
            for h in (h0, h0 + 1):
                off = (h - h0) * A_DK
                s_new.append(s[h] * decay[:, head[h]] + upd[off:off + A_DK, off:off + A_DK])
        outs = []
        zero_s = jnp.zeros((A_DK, A_DK), MXU_DTYPE)
        zero_v = jnp.zeros((c, A_DK), MXU_DTYPE)
        for h0 in range(0, A_HEADS, 2):
            h1 = h0 + 1
            pair = slice(h0 * A_DK, (h1 + 1) * A_DK)
            s_pair = jnp.concatenate(
                [jnp.concatenate([s[h0].astype(MXU_DTYPE), zero_s], axis=1),
                 jnp.concatenate([zero_s, s[h1].astype(MXU_DTYPE)], axis=1)], axis=0)
            v_pair = jnp.concatenate([jnp.concatenate([v[:, head[h0]], zero_v], axis=1),
                                      jnp.concatenate([zero_v, v[:, head[h1]]], axis=1)], axis=0)
            a_pair = jnp.concatenate([score_list[h0], score_list[h1]], axis=1)
            o_pair = _nt_dot(qd[:, pair], s_pair) + jnp.dot(a_pair, v_pair, preferred_element_type=F32)
            for h in (h0, h1):
                off = (h - h0) * A_DK
                o_h = o_pair[:, off:off + A_DK]
                if fuse:
                    o_h = o_h + ob_ref[0, sl, head[h]]
                    o_h = o_h * lax.rsqrt(jnp.mean(o_h * o_h, axis=-1, keepdims=True) + EPS)
                outs.append(o_h)
        s[:] = s_new
        o = jnp.concatenate(outs, axis=1)
        if fuse:
            og = og_ref[0, sl, :].astype(F32)
            o = o * ng_ref[...] * (og * _sigmoid(og))
        o_ref[0, sl, :] = o.astype(o_ref.dtype)

    safe = safe_ref[pl.program_id(0), (nt - 1 - i) if rev else i] != 0

    @pl.when(safe)
    def _():
        s = [state[h] for h in range(A_HEADS)]
        order = list(range(n_chunks - 1, -1, -1) if rev else range(n_chunks))
        slices = [slice(ci * c, (ci + 1) * c) for ci in order]
        ready = prepare(slices[0], False)
        for n, sl in enumerate(slices):
            prepped = ready
            if n + 1 < len(slices):
                ready = prepare(slices[n + 1], False)
            finish(sl, prepped, s)
        for h in range(A_HEADS):
            state[h] = s[h]

    @pl.when(jnp.logical_not(safe))
    def _():
        def chunk(n, carry):
            ci = (n_chunks - 1 - n) if rev else n
            sl = pl.ds(pl.multiple_of(ci * c, c), c)
            s = [state[h] for h in range(A_HEADS)]
            finish(sl, prepare(sl, True), s)
            for h in range(A_HEADS):
                state[h] = s[h]
            return carry

        lax.fori_loop(0, n_chunks, chunk, 0)

    @pl.when(i == pl.num_programs(1) - 1)
    def _():
        sT_ref[0] = state[...]


def hgrn_safe_flags(stats, direction, n, step_tokens):
    worst = stats[:, :, direction, :].min(axis=-1)
    bsz, nb = worst.shape
    per_step = nb * step_tokens // n
    worst = worst.reshape(bsz, nb // per_step, per_step).min(axis=-1)
    return (worst * LOG2E - HGRN_SUB >= -HGRN_EXP2_CLAMP).astype(jnp.int32)


def hgrn_scan(z, stats, lbp, s0, rev, o_other=None, norm_g=None):
    bsz, n, _ = z.shape
    fuse = o_other is not None
    t = min(HGRN_STEP_TOKENS, n)
    nt = n // t
    f_off = (OFF_FB if rev else OFF_FF) // A_WIDTH
    safe = hgrn_safe_flags(stats, 1 if rev else 0, n, t)

    def tmap(i):
        return nt - 1 - i if rev else i

    def col(blk):
        return pl.BlockSpec((1, t, A_WIDTH), lambda b, i, sf: (b, tmap(i), blk))

    state_spec = pl.BlockSpec((1, A_HEADS, LANES, LANES), lambda b, i, sf: (b, 0, 0, 0))
    in_specs = [col(OFF_QA // A_WIDTH), col(f_off), col(OFF_IA // A_WIDTH),
                pl.BlockSpec((8, A_WIDTH), lambda b, i, sf: (0, 0)), state_spec]
    args = [z, z, z, lbp, s0]
    if fuse:
        in_specs += [col(0), col(OFF_OG // A_WIDTH), pl.BlockSpec((1, A_WIDTH), lambda b, i, sf: (0, 0))]
        args += [o_other, z, norm_g.reshape(1, A_WIDTH)]
    kern = functools.partial(_hgrn_kernel, rev=rev, n_chunks=t // HGRN_CHUNK, fuse=fuse)
    grid_spec = pltpu.PrefetchScalarGridSpec(
        num_scalar_prefetch=1,
        grid=(bsz, nt),
        in_specs=in_specs,
        out_specs=[col(0), state_spec],
        scratch_shapes=[pltpu.VMEM((A_HEADS, LANES, LANES), F32)],
    )
    return pl.pallas_call(
        kern,
        grid_spec=grid_spec,
        out_shape=[jax.ShapeDtypeStruct((bsz, n, A_WIDTH), MXU_DTYPE if fuse else F32),
                   jax.ShapeDtypeStruct((bsz, A_HEADS, LANES, LANES), F32)],
        compiler_params=_cparams(("parallel", "arbitrary")),
        name="hgrn_bwd" if rev else "hgrn_fwd",
    )(safe, *args)


def _attn_kernel(q_ref, k_ref, v_ref, o_ref):
    tq = q_ref.shape[2]
    rows = B_GROUP * tq
    lane = lax.broadcasted_iota(jnp.int32, v_ref.shape[1:], 1)
    lane_o = lax.broadcasted_iota(jnp.int32, (tq, LANES), 1)
    s_all = _nt_dot(q_ref[0].reshape(B_HEADS * tq, LANES), k_ref[0])
    for kv in range(B_KV_HEADS):
        s = s_all[kv * rows:(kv + 1) * rows]
        vs = jnp.where((lane // B_HD) == kv, v_ref[0], jnp.ones_like(v_ref[0]))
        p = jnp.exp2(s - jnp.max(s, axis=-1, keepdims=True))
        a = jnp.dot(p.astype(MXU_DTYPE), vs, preferred_element_type=F32)
        o_full = a / pltpu.roll(a, B_HD, 1)
        o_swapped = pltpu.roll(o_full, B_HD, 1)
        for pair in range(B_GROUP // 2):
            g0, g1 = 2 * pair, 2 * pair + 1
            lo_src = o_full if kv == 0 else o_swapped
            hi_src = o_swapped if kv == 0 else o_full
            blk = jnp.where(lane_o < B_HD, lo_src[g0 * tq:(g0 + 1) * tq], hi_src[g1 * tq:(g1 + 1) * tq])
            cb = kv * (B_GROUP // 2) + pair
            o_ref[0, :, cb * LANES:(cb + 1) * LANES] = blk.astype(o_ref.dtype)


def attention(q, k, v):
    bsz, _, n, _ = q.shape
    nkeys = k.shape[1]
    tq = min(ATTN_Q_ROWS, n)
    return pl.pallas_call(
        _attn_kernel,
        grid=(bsz, n // tq),
        in_specs=[pl.BlockSpec((1, B_HEADS, tq, LANES), lambda b, i: (b, 0, i, 0)),
                  pl.BlockSpec((1, nkeys, LANES), lambda b, i: (b, 0, 0)),
                  pl.BlockSpec((1, nkeys, LANES), lambda b, i: (b, 0, 0))],
        out_specs=pl.BlockSpec((1, tq, B_WIDTH), lambda b, i: (b, i, 0)),
        out_shape=jax.ShapeDtypeStruct((bsz, n, B_WIDTH), MXU_DTYPE),
        compiler_params=_cparams(("parallel", "parallel")),
        name="gqa_attention",
    )(q, k, v)


def _pack_rows(x):
    w = x.shape[1] // 2
    bits = lax.bitcast_convert_type(x.astype(jnp.bfloat16).astype(F32), jnp.uint32)
    packed = (bits[:, :w] & jnp.uint32(0xFFFF0000)) | (bits[:, w:] >> 16)
    return lax.bitcast_convert_type(packed, jnp.int32)


def _unpack_rows(p):
    bits = lax.bitcast_convert_type(p, jnp.uint32)
    hi = lax.bitcast_convert_type(bits & jnp.uint32(0xFFFF0000), F32)
    lo = lax.bitcast_convert_type(bits << 16, F32)
    return jnp.concatenate([hi, lo], axis=1)


def _store_chunks(ref, packed):
    for j in range(ref.shape[0]):
        ref[j] = packed[:, j * LANES:(j + 1) * LANES]


def _load_chunks(ref):
    return jnp.concatenate([ref[j] for j in range(ref.shape[0])], axis=1)


def _merge_kernel(*refs, aliased):
    if aliased:
        refs = refs[:13] + refs[15:]
    (h_ref, a_ref, b_ref, zg_ref, wa_ref, wb_ref, wo_ref, g1_ref, n2_ref, sc_ref, sh_ref, wr_ref, br_ref,
     ho_ref, x2_ref, lg_ref) = refs
    d = h_ref.shape[2]
    tm = h_ref.shape[1]
    sub = min(MERGE_SUB_ROWS, tm)
    for r in range(tm // sub):
        rs = slice(r * sub, (r + 1) * sub)
        ua = jnp.dot(a_ref[0, rs, :], wa_ref[...], preferred_element_type=F32)
        ub = jnp.dot(b_ref[0, rs, :], wb_ref[...], preferred_element_type=F32)
        ga = _sigmoid(zg_ref[0, rs, 0:d].astype(F32))
        gb = _sigmoid(zg_ref[0, rs, d:2 * d].astype(F32))
        mix = (ga * ua + gb * ub).astype(MXU_DTYPE)
        h_new = h_ref[0, rs, :] + g1_ref[0] * jnp.dot(mix, wo_ref[...], preferred_element_type=F32)
        ho_ref[0, rs, :] = h_new
        x2 = _modulated_norm(h_new, n2_ref[...], sc_ref[0], sh_ref[0])
        packed = _pack_rows(x2)
        for j in range(x2_ref.shape[0]):
            x2_ref[j, rs, :] = packed[:, j * LANES:(j + 1) * LANES]
        x_hi = x2.astype(jnp.bfloat16)
        x_lo = (x2 - x_hi.astype(F32)).astype(jnp.bfloat16)
        hi_both = jnp.dot(x_hi, wr_ref[...], preferred_element_type=F32)
        lo_hi = jnp.dot(x_lo, wr_ref[:, 0:LANES], preferred_element_type=F32)
        lg_ref[rs, :] = hi_both[:, 0:LANES] + hi_both[:, LANES:] + lo_hi + br_ref[...]


def merge_proj(h, a, b, z, w_up_a, w_up_b, w_out, g1, norm2_g, sc2, sh2, w_router, b_router, t_total, tok_off,
               bufs=None):
    bsz, n, d = h.shape
    tm = min(MERGE_ROWS, n)
    nt = n // tm
    off = tok_off // tm
    n_chunks = d // (2 * LANES)
    row = lambda w: pl.BlockSpec((1, tm, w), lambda b_, i: (b_, i, 0))
    full = lambda s: pl.BlockSpec(s, lambda b_, i: (0,) * len(s))
    per_b = pl.BlockSpec((1, 1, d), lambda b_, i: (b_, 0, 0))
    in_specs = [row(d), row(A_WIDTH), row(B_WIDTH),
                pl.BlockSpec((1, tm, 2 * d), lambda b_, i: (b_, i, OFF_GATES)),
                full((A_WIDTH, d)), full((B_WIDTH, d)), full((d, d)),
                per_b, full((1, d)), per_b, per_b, full((d, 2 * LANES)), full((1, LANES))]
    args = [h, a, b, z, w_up_a, w_up_b, w_out, g1, norm2_g.reshape(1, d), sc2, sh2, w_router, b_router]
    aliases = {}
    if bufs is not None:
        in_specs += [pl.BlockSpec(memory_space=pl.ANY)] * 2
        args += list(bufs)
        aliases = {13: 1, 14: 2}
    return pl.pallas_call(
        functools.partial(_merge_kernel, aliased=bufs is not None),
        grid=(bsz, nt),
        in_specs=in_specs,
        out_specs=[row(d),
                   pl.BlockSpec((n_chunks, tm, LANES), lambda b_, i: (0, off + b_ * nt + i, 0)),
                   pl.BlockSpec((tm, LANES), lambda b_, i: (off + b_ * nt + i, 0))],
        out_shape=[jax.ShapeDtypeStruct((bsz, n, d), F32),
                   jax.ShapeDtypeStruct((n_chunks, t_total, LANES), jnp.int32),
                   jax.ShapeDtypeStruct((t_total, LANES), F32)],
        input_output_aliases=aliases,
        compiler_params=_cparams(("parallel", "parallel")),
        name="merge_proj",
    )(*args)


def _first_lane_where(cond, lane):
    return jnp.min(jnp.where(cond, lane, LANES), axis=-1, keepdims=True)


def _router_kernel(lg_ref, slab_ref, slab_t_ref, cnt_ref, base):
    @pl.when(pl.program_id(0) == 0)
    def _():
        base[...] = jnp.zeros_like(base)

    lg = lg_ref[...]
    tr = lg.shape[0]
    lane = lax.broadcasted_iota(jnp.int32, lg.shape, 1)
    neg = jnp.float32(-jnp.inf)
    is_g = lane < N_GROUPS
    gl = jnp.where(is_g, lg, neg)
    gmax = jnp.max(gl, axis=-1, keepdims=True)
    gsel = _first_lane_where(gl == gmax, lane)
    p_group = 1.0 / jnp.sum(jnp.where(is_g, jnp.exp(lg - gmax), 0.0), axis=-1, keepdims=True)
    lo = N_GROUPS + gsel * EXPERTS_PER_GROUP
    in_grp = (lane >= lo) & (lane < lo + EXPERTS_PER_GROUP)
    el = jnp.where(in_grp, lg, neg)
    m1 = jnp.max(el, axis=-1, keepdims=True)
    i1 = _first_lane_where(el == m1, lane)
    el2 = jnp.where(lane == i1, neg, el)
    m2 = jnp.max(el2, axis=-1, keepdims=True)
    i2 = _first_lane_where(el2 == m2, lane)
    zsum = jnp.sum(jnp.where(in_grp, jnp.exp(lg - m1), 0.0), axis=-1, keepdims=True)
    p1 = 1.0 / zsum
    p2 = jnp.exp(m2 - m1) / zsum
    w1 = p1 / (p1 + p2) * p_group
    w2 = p2 / (p1 + p2) * p_group
    hit1 = lane == i1
    hit2 = lane == i2
    onehot = (hit1 | hit2).astype(F32)
    r_i = lax.broadcasted_iota(jnp.int32, (tr, tr), 0)
    c_i = lax.broadcasted_iota(jnp.int32, (tr, tr), 1)
    before = (c_i < r_i).astype(MXU_DTYPE)
    prior = jnp.dot(before, onehot.astype(MXU_DTYPE), preferred_element_type=F32) + base[0:1, :]
    r1 = jnp.sum(jnp.where(hit1, prior, 0.0), axis=-1, keepdims=True)
    r2 = jnp.sum(jnp.where(hit2, prior, 0.0), axis=-1, keepdims=True)
    new_base = base[0:1, :] + jnp.sum(onehot, axis=0, keepdims=True)
    base[...] = jnp.broadcast_to(new_base, base.shape)
    cnt_ref[...] = jnp.broadcast_to(new_base, cnt_ref.shape)
    vals = [(i1 - N_GROUPS).astype(F32), (i2 - N_GROUPS).astype(F32), w1, w2, r1, r2]
    slab = jnp.zeros_like(lg)
    for idx, val in enumerate(vals):
        slab = jnp.where(lane == idx, val, slab)
    slab_ref[...] = slab
    slab_t_ref[...] = slab.T[0:8, :]


def route(logits):
    t = logits.shape[0]
    tr = next(c for c in ROUTER_ROWS if t % c == 0)
    return pl.pallas_call(
        _router_kernel,
        grid=(t // tr,),
        in_specs=[pl.BlockSpec((tr, LANES), lambda i: (i, 0))],
        out_specs=[pl.BlockSpec((tr, LANES), lambda i: (i, 0)), pl.BlockSpec((8, tr), lambda i: (0, i)),
                   pl.BlockSpec((8, LANES), lambda i: (0, 0))],
        out_shape=[jax.ShapeDtypeStruct((t, LANES), F32), jax.ShapeDtypeStruct((8, t), F32),
                   jax.ShapeDtypeStruct((8, LANES), F32)],
        scratch_shapes=[pltpu.VMEM((8, LANES), F32)],
        compiler_params=_cparams(("arbitrary",)),
        name="moe_route",
    )(logits)


def _expert_kernel(be_ref, nv_ref, x_ref, wg_ref, wu_ref, wd_ref, o_ref, wg_s, wu_s, wd_s):
    j = pl.program_id(0)
    nv = nv_ref[j]

    @pl.when((j == 0) | (be_ref[j] != be_ref[jnp.maximum(j - 1, 0)]))
    def _():
        wg_s[...] = wg_ref[0, 0].astype(wg_s.dtype)
        wu_s[...] = wu_ref[0, 0].astype(wu_s.dtype)
        wd_s[...] = wd_ref[0, 0].astype(wd_s.dtype)

    @pl.when(nv > 0)
    def _():
        x = _unpack_rows(_load_chunks(x_ref))
        rows = lax.broadcasted_iota(jnp.int32, x.shape, 0)
        x = jnp.where(rows < nv, x, 0.0).astype(MXU_DTYPE)
        hg = jnp.dot(x, wg_s[...], preferred_element_type=F32)
        hu = jnp.dot(x, wu_s[...], preferred_element_type=F32)
        hb = (hg * _sigmoid(hg) * hu).astype(MXU_DTYPE)
        _store_chunks(o_ref, _pack_rows(jnp.dot(hb, wd_s[...], preferred_element_type=F32)))

    @pl.when(nv == 0)
    def _():
        o_ref[...] = jnp.zeros_like(o_ref)


def expert_ffn(x_sorted, block_e, nvalid, w_gate, w_up, w_down, layer, tmx):
    n_chunks, r, _ = x_sorted.shape
    _, _, d, de = w_gate.shape
    grid_spec = pltpu.PrefetchScalarGridSpec(
        num_scalar_prefetch=2,
        grid=(r // tmx,),
        in_specs=[pl.BlockSpec((n_chunks, tmx, LANES), lambda j, be, nv: (0, j, 0)),
                  pl.BlockSpec((1, 1, d, de), lambda j, be, nv: (layer, be[j], 0, 0)),
                  pl.BlockSpec((1, 1, d, de), lambda j, be, nv: (layer, be[j], 0, 0)),
                  pl.BlockSpec((1, 1, de, d), lambda j, be, nv: (layer, be[j], 0, 0))],
        out_specs=pl.BlockSpec((n_chunks, tmx, LANES), lambda j, be, nv: (0, j, 0)),
        scratch_shapes=[pltpu.VMEM((d, de), MXU_DTYPE), pltpu.VMEM((d, de), MXU_DTYPE),
                        pltpu.VMEM((de, d), MXU_DTYPE)],
    )
    return pl.pallas_call(
        _expert_kernel,
        grid_spec=grid_spec,
        out_shape=jax.ShapeDtypeStruct((n_chunks, r, LANES), jnp.int32),
        compiler_params=_cparams(("arbitrary",)),
        name="expert_ffn",
    )(block_e, nvalid, x_sorted, w_gate, w_up, w_down)


def _combine_final_kernel(h_ref, y1_ref, y2_ref, slab_ref, g2_ref, fn_ref, o_ref):
    y = (_unpack_rows(_load_chunks(y1_ref)) * slab_ref[:, 2:3]
         + _unpack_rows(_load_chunks(y2_ref)) * slab_ref[:, 3:4])
    h_new = h_ref[0] + g2_ref[0] * y
    ms = jnp.mean(h_new * h_new, axis=-1, keepdims=True)
    o_ref[0] = h_new * lax.rsqrt(ms + EPS) * fn_ref[...]


def moe_combine_final(h, y1, y2, slab, g2, final_g):
    bsz, n, d = h.shape
    tm = min(COMBINE_ROWS, n)
    nt = n // tm
    yspec = pl.BlockSpec((y1.shape[0], tm, LANES), lambda b, i: (0, b * nt + i, 0))
    return pl.pallas_call(
        _combine_final_kernel,
        grid=(bsz, nt),
        in_specs=[pl.BlockSpec((1, tm, d), lambda b, i: (b, i, 0)), yspec, yspec,
                  pl.BlockSpec((tm, LANES), lambda b, i: (b * nt + i, 0)),
                  pl.BlockSpec((1, 1, d), lambda b, i: (b, 0, 0)),
                  pl.BlockSpec((1, d), lambda b, i: (0, 0))],
        out_specs=pl.BlockSpec((1, tm, d), lambda b, i: (b, i, 0)),
        out_shape=jax.ShapeDtypeStruct((bsz, n, d), F32),
        compiler_params=_cparams(("parallel", "parallel")),
        name="moe_combine_final",
    )(h, y1, y2, slab, g2, final_g.reshape(1, d))


SC_WINDOW = 128


def _sc_mesh():
    return plsc.VectorSubcoreMesh(core_axis_name="core", subcore_axis_name="subcore")


def sc_gather_rows2(table, idx_a, idx_b):
    n = idx_a.shape[0]
    out = jax.ShapeDtypeStruct((n, LANES), table.dtype)

    @pl.kernel(out_type=(out, out), mesh=_sc_mesh())
    def gather_kernel(t_hbm, ia_hbm, ib_hbm, oa_hbm, ob_hbm):
        def body(ia_vmem, ib_vmem, oa_vmem, ob_vmem):
            pltpu.sync_copy(t_hbm.at[ia_vmem.at[0]], oa_vmem)
            pltpu.sync_copy(t_hbm.at[ib_vmem.at[0]], ob_vmem)

        pltpu.emit_pipeline(
            body,
            grid=(n // SC_WINDOW,),
            in_specs=[pl.BlockSpec((1, SC_WINDOW), lambda i: (0, i))] * 2,
            out_specs=[pl.BlockSpec((SC_WINDOW, LANES), lambda i: (i, 0))] * 2,
            core_axis_name=("core", "subcore"),
            dimension_semantics=(pltpu.PARALLEL,),
            trace_scopes=False,
        )(ia_hbm, ib_hbm, oa_hbm, ob_hbm)

    return gather_kernel(table, idx_a.reshape(1, n), idx_b.reshape(1, n))


def sc_scatter_rows2(rows, idx_a, idx_b, n_out):
    n = rows.shape[0]

    @pl.kernel(out_type=jax.ShapeDtypeStruct((n_out, LANES), rows.dtype), mesh=_sc_mesh())
    def scatter_kernel(r_hbm, ia_hbm, ib_hbm, o_hbm):
        def body(r_vmem, ia_vmem, ib_vmem):
            pltpu.sync_copy(r_vmem, o_hbm.at[ia_vmem.at[0]])
            pltpu.sync_copy(r_vmem, o_hbm.at[ib_vmem.at[0]])

        pltpu.emit_pipeline(
            body,
            grid=(n // SC_WINDOW,),
            in_specs=[pl.BlockSpec((SC_WINDOW, LANES), lambda i: (i, 0)),
                      pl.BlockSpec((1, SC_WINDOW), lambda i: (0, i)),
                      pl.BlockSpec((1, SC_WINDOW), lambda i: (0, i))],
            out_specs=[],
            core_axis_name=("core", "subcore"),
            dimension_semantics=(pltpu.PARALLEL,),
            trace_scopes=False,
        )(r_hbm, ia_hbm, ib_hbm)

    return scatter_kernel(rows, idx_a.reshape(1, n), idx_b.reshape(1, n))


def _plan_kernel(ps_ref, st_ref, ia_ref, ib_ref, *, n_rows):
    e = st_ref[0:2, :].astype(jnp.int32)
    dest = st_ref[4:6, :].astype(jnp.int32)
    for ex in range(N_EXPERTS):
        dest = dest + jnp.where(e == ex, ps_ref[ex], 0)
    for k, ref in enumerate((ia_ref, ib_ref)):
        for j in range(ref.shape[0]):
            ref[j:j + 1, :] = dest[k:k + 1, :] + j * n_rows


def dispatch_indices(slab_t, pad_start, n_chunks, n_rows):
    t = slab_t.shape[1]
    tp = next(c for c in PLAN_TOKENS if t % c == 0)
    grid_spec = pltpu.PrefetchScalarGridSpec(
        num_scalar_prefetch=1,
        grid=(t // tp,),
        in_specs=[pl.BlockSpec((8, tp), lambda i, ps: (0, i))],
        out_specs=[pl.BlockSpec((n_chunks, tp), lambda i, ps: (0, i))] * 2,
    )
    return pl.pallas_call(
        functools.partial(_plan_kernel, n_rows=n_rows),
        grid_spec=grid_spec,
        out_shape=[jax.ShapeDtypeStruct((n_chunks, t), jnp.int32)] * 2,
        compiler_params=_cparams(("parallel",)),
        name="moe_plan",
    )(pad_start, slab_t)


def hier_moe(x2, logits, w_gate, w_up, w_down, layer):
    n_chunks, t, _ = x2.shape
    tmx = min(EXPERT_ROWS, t)
    n_blocks = (t * TOP_K + N_EXPERTS * (tmx - 1)) // tmx + 1
    r = n_blocks * tmx
    slab, slab_t, counts = route(logits)
    cnt = counts[0, N_GROUPS:N_GROUPS + N_EXPERTS].astype(jnp.int32)
    padded = (cnt + tmx - 1) // tmx * tmx
    pad_end = jnp.cumsum(padded)
    pad_start = pad_end - padded
    blk_start = jnp.arange(n_blocks, dtype=jnp.int32) * tmx
    block_e = jnp.minimum(jnp.sum(pad_end[None, :] <= blk_start[:, None], axis=1), N_EXPERTS - 1).astype(jnp.int32)
    nvalid = jnp.clip(pad_start[block_e] + cnt[block_e] - blk_start, 0, tmx).astype(jnp.int32)
    idx1, idx2 = dispatch_indices(slab_t, pad_start.astype(jnp.int32), n_chunks, r)
    idx1, idx2 = idx1.reshape(-1), idx2.reshape(-1)
    x_sorted = sc_scatter_rows2(x2.reshape(n_chunks * t, LANES), idx1, idx2, n_chunks * r)
    y_sorted = expert_ffn(x_sorted.reshape(n_chunks, r, LANES), block_e, nvalid, w_gate, w_up, w_down, layer, tmx)
    y_flat = y_sorted.reshape(n_chunks * r, LANES)
    y1, y2 = sc_gather_rows2(y_flat, idx1, idx2)
    y1, y2 = y1.reshape(n_chunks, t, LANES), y2.reshape(n_chunks, t, LANES)
    return y1, y2, slab


def _lower_bound_rows(lb_logits):
    lb = jnp.cumsum(jax.nn.softmax(lb_logits.astype(F32), axis=1), axis=1)
    lb = lb - lb[:, :1]
    rows = jnp.stack([jnp.log(lb), jnp.log1p(-lb), 1.0 - lb], axis=2)
    rows = jnp.concatenate([rows, jnp.zeros(rows.shape[:2] + (5, A_WIDTH), F32)], axis=2)
    return rows


def _cast_kernel(w_ref, o_ref):
    o_ref[...] = w_ref[...].astype(o_ref.dtype)


def reorder_cast_w_in(w_in):
    depth, d, n = w_in.shape
    gate_cols = 2 * d
    blk = REORDER_COLS
    nb = n // blk
    shift = (n - gate_cols) // blk
    return pl.pallas_call(
        _cast_kernel,
        grid=(depth, nb),
        in_specs=[pl.BlockSpec((1, d, blk), lambda l, j: (l, 0, (j + shift) % nb))],
        out_specs=pl.BlockSpec((1, d, blk), lambda l, j: (l, 0, j)),
        out_shape=jax.ShapeDtypeStruct((depth, d, n), MXU_DTYPE),
        compiler_params=_cparams(("parallel", "parallel")),
        name="reorder_cast_w_in",
    )(w_in)


def kernel(x, c, ctx, c_ctx, w_ada, b_ada, norm1_g, norm2_g, w_in, lb_logits, hgrn_norm_g, q_norm_g, k_norm_g,
           w_up_a, w_up_b, w_out, w_router_group, b_router_group, w_router_expert, b_router_expert, w_gate, w_up,
           w_down, final_norm_g):
    bsz, n, d = x.shape
    n_ctx = ctx.shape[1]
    depth = w_in.shape[0]
    mx = MXU_DTYPE

    w_in_r = reorder_cast_w_in(w_in)
    w_up_a_c, w_up_b_c, w_out_c = w_up_a.astype(mx), w_up_b.astype(mx), w_out.astype(mx)
    n_r = N_GROUPS + N_EXPERTS
    w_router = jnp.concatenate([w_router_group, w_router_expert, jnp.zeros((depth, d, LANES - n_r), F32)], axis=-1)
    w_router_hi = w_router.astype(jnp.bfloat16)
    w_router_lo = (w_router - w_router_hi.astype(F32)).astype(jnp.bfloat16)
    w_router = jnp.concatenate([w_router_hi, w_router_lo], axis=-1)
    b_router = jnp.concatenate([b_router_group, b_router_expert, jnp.zeros((depth, LANES - n_r), F32)], axis=-1)
    lbp = _lower_bound_rows(lb_logits)
    cos_t, sin_t = rope_tables(n)

    n_rows = -(-(bsz + 1) // 8) * 8
    act = jnp.concatenate([c, c_ctx[None, :], jnp.zeros((n_rows - bsz - 1, d), F32)], axis=0)
    mod = ada_mod(act, w_ada, b_ada)

    h, hc = x, ctx
    pending = None
    zero_state = jnp.zeros((bsz, A_HEADS, LANES, LANES), F32)
    for l in range(depth):
        need_ctx = l < depth - 1
        ml = mod[l, :bsz].reshape(bsz, 1, N_MOD, d)
        mc = jnp.broadcast_to(mod[l, bsz].reshape(1, 1, N_MOD, d), (bsz, 1, N_MOD, d))
        sh1, sc1, g1, sh2, sc2, g2 = (ml[:, :, i] for i in range(N_MOD))
        csh1, csc1, cg1, csh2, csc2, cg2 = (mc[:, :, i] for i in range(N_MOD))

        moe_l = moe_c = None
        if pending is not None:
            y1, y2, slab, g2_prev, cg2_prev = pending
            moe_l = (y1, y2, slab, g2_prev, 0)
            moe_c = (y1, y2, slab, cg2_prev, bsz * n)
        zl, q_l, k_l, v_l, st_l, *h_new = in_proj(h, norm1_g[l], sc1, sh1, w_in_r, l, q_norm_g[l], k_norm_g[l],
                                                  cos_t, sin_t, moe_l)
        zc, q_c, k_c, v_c, st_c, *hc_new = in_proj(hc, norm1_g[l], csc1, csh1, w_in_r, l, q_norm_g[l], k_norm_g[l],
                                                   None, None, moe_c)
        if pending is not None:
            h, hc = h_new[0], hc_new[0]
        ob_c, s_bwd = hgrn_scan(zc, st_c, lbp[1, l], zero_state, True)
        if need_ctx:
            a_c, s_fwd = hgrn_scan(zc, st_c, lbp[0, l], zero_state, False, ob_c, hgrn_norm_g[l])
        else:
            _, s_fwd = hgrn_scan(zc, st_c, lbp[0, l], zero_state, False)
        ob_l, _ = hgrn_scan(zl, st_l, lbp[1, l], s_bwd, True)
        a_l, _ = hgrn_scan(zl, st_l, lbp[0, l], s_fwd, False, ob_l, hgrn_norm_g[l])

        b_l = attention(q_l, jnp.concatenate([k_c, k_l], axis=1), jnp.concatenate([v_c, v_l], axis=1))

        t_total = bsz * (n + n_ctx) if need_ctx else bsz * n
        bufs = (jnp.zeros((d // (2 * LANES), t_total, LANES), jnp.int32),
                jnp.zeros((t_total, LANES), F32)) if need_ctx else None
        h, x2, lg = merge_proj(h, a_l, b_l, zl, w_up_a_c[l], w_up_b_c[l], w_out_c[l], g1, norm2_g[l], sc2, sh2,
                               w_router[l], b_router[l][None, :], t_total, 0, bufs)
        if need_ctx:
            b_c = attention(q_c, k_c, v_c)
            hc, x2, lg = merge_proj(hc, a_c, b_c, zc, w_up_a_c[l], w_up_b_c[l], w_out_c[l], cg1, norm2_g[l],
                                    csc2, csh2, w_router[l], b_router[l][None, :], t_total, bsz * n, (x2, lg))

        y1, y2, slab = hier_moe(x2, lg, w_gate, w_up, w_down, l)
        pending = (y1, y2, slab, g2, cg2)
    y1, y2, slab, g2, _ = pending
    return moe_combine_final(h, y1, y2, slab, g2, final_norm_g)
```
